```python
import math
import jax, jax.numpy as jnp
from jax import lax
import numpy as np


D_MODEL = 2048
BATCH = 4
SEQ = 2048
DEPTH = 2
DEC_BATCH = 128
DEC_SEQ = 1
PAST_LEN = 16384
PAGE_SIZE = 128

D_A = D_MODEL // 4
S5_GROUP = 16
S5_GROUPS = D_A // S5_GROUP
S5_STATE = 64
D_B = D_MODEL // 2
H_B = 4
DV_B = D_B // H_B
DQK_B = DV_B // 2
MLSTM_CHUNK = 64
D_C = D_MODEL // 4
H_C = 4
DK_C = D_C // H_C
DV_C = D_C // H_C
GDN_CONV = 4
GDN_CHUNK = 64
D_FF = 4 * D_MODEL
FFN_CONV = 3
PLE_DIM = 256
N_BRANCH = 3
EPS = 1e-6
IN_SPLITS = (D_A, H_B * DQK_B, H_B * DQK_B, D_B, H_B, H_B, D_B, 3 * D_C, D_C, H_C, H_C)
IN_WIDTH = sum(IN_SPLITS)

kernel_name = 'hybrid_s5_mlstm_gdn_decoder_step'

F32 = jnp.float32


def rmsnorm(x, w):
    xf = x.astype(F32)
    r = lax.rsqrt(jnp.mean(xf * xf, axis=-1, keepdims=True) + EPS)
    return (xf * r * w.astype(F32)).astype(x.dtype)


def l2norm(x):
    return x * lax.rsqrt(jnp.sum(x * x, axis=-1, keepdims=True) + EPS)


def _split_cols(a, sizes):
    outs = []
    off = 0
    for s in sizes:
        outs.append(a[..., off:off + s])
        off += s
    return outs


def causal_dwconv(x, buf, w):
    k = w.shape[0]
    L = x.shape[1]
    xx = jnp.concatenate([buf.astype(x.dtype), x], axis=1)
    y = w[0] * xx[:, 0:L]
    for j in range(1, k):
        y = y + w[j] * xx[:, j:j + L]
    return y, xx[:, L:]


def _to_chunks(a, cs):
    b, l, hh = a.shape[:3]
    rest = a.shape[3:]
    a = jnp.swapaxes(a, 1, 2).reshape(b, hh, l // cs, cs, *rest)
    return jnp.moveaxis(a, 2, 0)


def _from_chunks(a):
    nc, b, hh, cs, d = a.shape
    return jnp.swapaxes(jnp.moveaxis(a, 0, 2).reshape(b, hh, nc * cs, d), 1, 2)


def _cplx_affine_combine(lhs, rhs):
    ar1, ai1, br1, bi1 = lhs
    ar2, ai2, br2, bi2 = rhs
    return (ar2 * ar1 - ai2 * ai1,
            ar2 * ai1 + ai2 * ar1,
            ar2 * br1 - ai2 * bi1 + br2,
            ar2 * bi1 + ai2 * br1 + bi2)


def s5_mixer(u, st_re, st_im, a_re, a_im, log_dt, b_re, b_im, c_re, c_im, d, w_glu):
    bsz, L, _ = u.shape
    uf = u.astype(F32)
    ug = uf.reshape(bsz, L, S5_GROUPS, S5_GROUP)
    dt = jnp.exp(log_dt.astype(F32))[:, None]
    ar = a_re.astype(F32)
    ai = a_im.astype(F32)
    mag = jnp.exp(dt * ar)
    abar_re = mag * jnp.cos(dt * ai)
    abar_im = mag * jnp.sin(dt * ai)
    den = ar * ar + ai * ai
    zr = abar_re - 1.0
    f_re = (zr * ar + abar_im * ai) / den
    f_im = (abar_im * ar - zr * ai) / den
    bu_re = jnp.einsum('blgc,gcp->blgp', ug, b_re.astype(F32))
    bu_im = jnp.einsum('blgc,gcp->blgp', ug, b_im.astype(F32))
    bbu_re = f_re * bu_re - f_im * bu_im
    bbu_im = f_re * bu_im + f_im * bu_re
    a_seq_re = jnp.broadcast_to(abar_re, bbu_re.shape)
    a_seq_im = jnp.broadcast_to(abar_im, bbu_im.shape)
    p_re, p_im, xr, xi = lax.associative_scan(
        _cplx_affine_combine, (a_seq_re, a_seq_im, bbu_re, bbu_im), axis=1)
    s0r = st_re.astype(F32)[:, None]
    s0i = st_im.astype(F32)[:, None]
    xr = xr + p_re * s0r - p_im * s0i
    xi = xi + p_re * s0i + p_im * s0r
    y = (jnp.einsum('blgp,gpc->blgc', xr, c_re.astype(F32))
         - jnp.einsum('blgp,gpc->blgc', xi, c_im.astype(F32)))
    y = y.reshape(bsz, L, D_A) + d.astype(F32) * uf
    z = jax.nn.gelu(y)
    out = z * jax.nn.sigmoid(z @ w_glu.astype(F32))
    return out, xr[:, -1], xi[:, -1]


def mlstm_mixer(q, k, v, ig, fg, og, c0, n0, m0, b_i, b_f, norm_w):
    bsz, L = q.shape[:2]
    cs = math.gcd(L, MLSTM_CHUNK)
    qf = q.astype(F32) * DQK_B ** -0.5
    kf = k.astype(F32)
    vf = v.astype(F32)
    log_i = ig.astype(F32) + b_i.astype(F32)
    log_f = jax.nn.log_sigmoid(fg.astype(F32) + b_f.astype(F32))
    causal = jnp.tril(jnp.ones((cs, cs), dtype=bool))

    def step(carry, inp):
        c, n, m = carry
        qc, kc, vc, ic, fc = inp
        bcum = jnp.cumsum(fc, axis=-1)
        dmat = jnp.where(causal, bcum[..., :, None] - bcum[..., None, :] + ic[..., None, :], -jnp.inf)
        inter = bcum + m[..., None]
        m_t = jnp.maximum(inter, jnp.max(dmat, axis=-1))
        w_intra = jnp.exp(dmat - m_t[..., None])
        w_inter = jnp.exp(inter - m_t)
        s = jnp.einsum('bhtd,bhsd->bhts', qc, kc) * w_intra
        num = (jnp.einsum('bhts,bhsv->bhtv', s, vc)
               + w_inter[..., None] * jnp.einsum('bhtd,bhdv->bhtv', qc, c))
        nq = jnp.sum(s, axis=-1) + w_inter * jnp.einsum('bhtd,bhd->bht', qc, n)
        h = num / jnp.maximum(jnp.abs(nq), jnp.exp(-m_t))[..., None]
        b_last = bcum[..., -1]
        expo = b_last[..., None] - bcum + ic
        m_new = jnp.maximum(b_last + m, jnp.max(expo, axis=-1))
        w_s = jnp.exp(expo - m_new[..., None])
        decay = jnp.exp(b_last + m - m_new)
        c_new = decay[..., None, None] * c + jnp.einsum('bhs,bhsd,bhsv->bhdv', w_s, kc, vc)
        n_new = decay[..., None] * n + jnp.einsum('bhs,bhsd->bhd', w_s, kc)
        return (c_new, n_new, m_new), h

    (c_fin, n_fin, m_fin), h = lax.scan(
        step, (c0.astype(F32), n0.astype(F32), m0.astype(F32)),
        (_to_chunks(qf, cs), _to_chunks(kf, cs), _to_chunks(vf, cs),
         _to_chunks(log_i, cs), _to_chunks(log_f, cs)))
    h = _from_chunks(h)
    h = rmsnorm(h, norm_w) * jax.nn.sigmoid(og.astype(F32))
    return h.reshape(bsz, L, D_B), c_fin, n_fin, m_fin


def gdn_mixer(qkv, z, beta_in, a_in, s0, conv_buf, conv_w, a_log, dt_bias, norm_w):
    bsz, L, _ = qkv.shape
    qkv_c, new_buf = causal_dwconv(qkv.astype(F32), conv_buf, conv_w.astype(F32))
    qkv_c = jax.nn.silu(qkv_c)
    q, k, v = _split_cols(qkv_c, (D_C, D_C, D_C))
    q = l2norm(q.reshape(bsz, L, H_C, DK_C)) * DK_C ** -0.5
    k = l2norm(k.reshape(bsz, L, H_C, DK_C))
    v = v.reshape(bsz, L, H_C, DV_C)
    beta = jax.nn.sigmoid(beta_in.astype(F32))
    g = -jnp.exp(a_log.astype(F32)) * jax.nn.softplus(a_in.astype(F32) + dt_bias.astype(F32))
    cs = math.gcd(L, GDN_CHUNK)
    incl = jnp.tril(jnp.ones((cs, cs), dtype=bool))
    strict = jnp.tril(jnp.ones((cs, cs), dtype=bool), k=-1)
    eye = jnp.eye(cs, dtype=F32)

    def step(S, inp):
        qc, kc, vc, bc, gc = inp
        gcum = jnp.cumsum(gc, axis=-1)
        gam = jnp.exp(jnp.where(incl, gcum[..., :, None] - gcum[..., None, :], -jnp.inf))
        kk = jnp.einsum('bhtd,bhsd->bhts', kc, kc)
        lmat = jnp.where(strict, bc[..., :, None] * gam * kk, 0.0)
        egc = jnp.exp(gcum)[..., None]
        rhs = bc[..., None] * (vc - egc * jnp.einsum('bhtd,bhdv->bhtv', kc, S))
        u = lax.linalg.triangular_solve(eye + lmat, rhs, left_side=True, lower=True,
                                        unit_diagonal=True)
        qk = jnp.einsum('bhtd,bhsd->bhts', qc, kc) * gam
        o = egc * jnp.einsum('bhtd,bhdv->bhtv', qc, S) + jnp.einsum('bhts,bhsv->bhtv', qk, u)
        g_last = gcum[..., -1]
        w_s = jnp.exp(g_last[..., None] - gcum)
        S_new = jnp.exp(g_last)[..., None, None] * S + jnp.einsum('bhs,bhsd,bhsv->bhdv', w_s, kc, u)
        return S_new, o

    s_fin, o = lax.scan(step, s0.astype(F32),
                        (_to_chunks(q, cs), _to_chunks(k, cs), _to_chunks(v, cs),
                         _to_chunks(beta, cs), _to_chunks(g, cs)))
    o = _from_chunks(o)
    o = rmsnorm(o, norm_w) * jax.nn.silu(z.astype(F32)).reshape(bsz, L, H_C, DV_C)
    return o.reshape(bsz, L, D_C), s_fin, new_buf


def decoder_layer(x, pemb, state, lw):
    ssm_re, ssm_im, m_c, m_n, m_m, g_s, g_conv, f_conv = state
    bsz, L, _ = x.shape
    h = rmsnorm(x, lw['norm_mix_pre'])
    (u_a, q_b, k_b, v_b, i_b, f_b, o_b, qkv_c, z_c, beta_c, a_c) = _split_cols(h @ lw['w_in'], IN_SPLITS)
    y_a, ssm_re_n, ssm_im_n = s5_mixer(
        u_a, ssm_re, ssm_im, lw['s5_a_re'], lw['s5_a_im'], lw['s5_log_dt'], lw['s5_b_re'],
        lw['s5_b_im'], lw['s5_c_re'], lw['s5_c_im'], lw['s5_d'], lw['s5_w_glu'])
    y_b, m_c_n, m_n_n, m_m_n = mlstm_mixer(
        q_b.reshape(bsz, L, H_B, DQK_B), k_b.reshape(bsz, L, H_B, DQK_B),
        v_b.reshape(bsz, L, H_B, DV_B), i_b, f_b, o_b.reshape(bsz, L, H_B, DV_B),
        m_c, m_n, m_m, lw['mlstm_b_i'], lw['mlstm_b_f'], lw['mlstm_norm'])
    y_c, g_s_n, g_conv_n = gdn_mixer(
        qkv_c, z_c, beta_c, a_c, g_s, g_conv, lw['gdn_conv_w'], lw['gdn_a_log'],
        lw['gdn_dt_bias'], lw['gdn_norm'])
    gates = jax.nn.sigmoid((h @ lw['w_gate']).astype(F32)).reshape(bsz, L, N_BRANCH, D_MODEL)
    merged = (gates[:, :, 0] * (y_a @ lw['w_branch_a'])
              + gates[:, :, 1] * (y_b @ lw['w_branch_b'])
              + gates[:, :, 2] * (y_c @ lw['w_branch_c']))
    mix = merged @ lw['w_out']
    x = x + rmsnorm(mix, lw['norm_mix_post']).astype(x.dtype)
    h2 = rmsnorm(x, lw['norm_ffn_pre'])
    a, f_conv_n = causal_dwconv(h2 @ lw['ffn_w_gate'], f_conv, lw['ffn_conv_w'])
    f = (jax.nn.gelu(a) * (h2 @ lw['ffn_w_up'])) @ lw['ffn_w_down']
    x = x + rmsnorm(f, lw['norm_ffn_post']).astype(x.dtype)
    gate_p = jax.nn.sigmoid((x @ lw['ple_w_gate']).astype(F32))
    x = (x + gate_p * (pemb @ lw['ple_w_proj'])).astype(x.dtype)
    return x, (ssm_re_n, ssm_im_n, m_c_n, m_n_n, m_m_n, g_s_n, g_conv_n, f_conv_n)


def _stack(states, j):
    return jnp.stack([s[j] for s in states], axis=0)


def _normal(k, shape, scale=1.0):
    return scale * jax.random.normal(k, shape, dtype=jnp.float32)


def setup_inputs(seed: int = 0) -> dict:
    key = jax.random.key(seed)
    ks = list(jax.random.split(key, 64))
    L_, D = DEPTH, D_MODEL
    gain = lambda k, shape: 1.0 + _normal(k, shape, 0.01)
    s5_a_im = jnp.pi * jnp.arange(S5_STATE, dtype=jnp.float32)[None, None, :] + _normal(ks[30], (L_, S5_GROUPS, S5_STATE), 0.01)
    gdn_dt = jnp.exp(jax.random.uniform(ks[31], (L_, H_C), jnp.float32, math.log(1e-3), math.log(1e-1)))
    return {
        'x_prompt': _normal(ks[0], (BATCH, SEQ, D)),
        'x_sample': _normal(ks[1], (DEC_BATCH, DEC_SEQ, D)),
        'p_prompt': _normal(ks[2], (L_, BATCH, SEQ, PLE_DIM)),
        'p_sample': _normal(ks[3], (L_, DEC_BATCH, DEC_SEQ, PLE_DIM)),
        'state_ssm_re': _normal(ks[4], (L_, DEC_BATCH, S5_GROUPS, S5_STATE), 0.1),
        'state_ssm_im': _normal(ks[5], (L_, DEC_BATCH, S5_GROUPS, S5_STATE), 0.1),
        'state_mlstm_c': _normal(ks[6], (L_, DEC_BATCH, H_B, DQK_B, DV_B), 0.1),
        'state_mlstm_n': _normal(ks[7], (L_, DEC_BATCH, H_B, DQK_B), 0.1),
        'state_mlstm_m': _normal(ks[8], (L_, DEC_BATCH, H_B)),
        'state_gdn_s': _normal(ks[9], (L_, DEC_BATCH, H_C, DK_C, DV_C), 0.1),
        'state_gdn_conv': _normal(ks[10], (L_, DEC_BATCH, GDN_CONV - 1, 3 * D_C)),
        'state_ffn_conv': _normal(ks[11], (L_, DEC_BATCH, FFN_CONV - 1, D_FF)),
        'norm_mix_pre': gain(ks[12], (L_, D)),
        'norm_mix_post': gain(ks[13], (L_, D)),
        'norm_ffn_pre': gain(ks[14], (L_, D)),
        'norm_ffn_post': gain(ks[15], (L_, D)),
        'w_in': _normal(ks[16], (L_, D, IN_WIDTH), D ** -0.5),
        's5_a_re': -0.5 + _normal(ks[17], (L_, S5_GROUPS, S5_STATE), 0.01),
        's5_a_im': s5_a_im,
        's5_log_dt': jax.random.uniform(ks[18], (L_, S5_GROUPS), jnp.float32, math.log(1e-3), math.log(1e-1)),
        's5_b_re': _normal(ks[19], (L_, S5_GROUPS, S5_GROUP, S5_STATE), (2 * S5_GROUP) ** -0.5),
        's5_b_im': _normal(ks[20], (L_, S5_GROUPS, S5_GROUP, S5_STATE), (2 * S5_GROUP) ** -0.5),
        's5_c_re': _normal(ks[21], (L_, S5_GROUPS, S5_STATE, S5_GROUP), (2 * S5_STATE) ** -0.5),
        's5_c_im': _normal(ks[22], (L_, S5_GROUPS, S5_STATE, S5_GROUP), (2 * S5_STATE) ** -0.5),
        's5_d': _normal(ks[23], (L_, D_A)),
        's5_w_glu': _normal(ks[24], (L_, D_A, D_A), D_A ** -0.5),
        'mlstm_b_i': _normal(ks[25], (L_, H_B), 0.1),
        'mlstm_b_f': 3.0 + 3.0 * jax.random.uniform(ks[26], (L_, H_B), jnp.float32),
        'mlstm_norm': gain(ks[27], (L_, H_B, DV_B)),
        'gdn_conv_w': _normal(ks[28], (L_, GDN_CONV, 3 * D_C), GDN_CONV ** -0.5),
        'gdn_a_log': jnp.log(jax.random.uniform(ks[29], (L_, H_C), jnp.float32, 1.0, 16.0)),
        'gdn_dt_bias': gdn_dt + jnp.log(-jnp.expm1(-gdn_dt)),
        'gdn_norm': gain(ks[32], (L_, DV_C)),
        'w_branch_a': _normal(ks[33], (L_, D_A, D), D_A ** -0.5),
        'w_branch_b': _normal(ks[34], (L_, D_B, D), D_B ** -0.5),
        'w_branch_c': _normal(ks[35], (L_, D_C, D), D_C ** -0.5),
        'w_gate': _normal(ks[36], (L_, D, N_BRANCH * D), D ** -0.5),
        'w_out': _normal(ks[37], (L_, D, D), D ** -0.5),
        'ffn_w_gate': _normal(ks[38], (L_, D, D_FF), D ** -0.5),
        'ffn_w_up': _normal(ks[39], (L_, D, D_FF), D ** -0.5),
        'ffn_conv_w': _normal(ks[40], (L_, FFN_CONV, D_FF), FFN_CONV ** -0.5),
        'ffn_w_down': _normal(ks[41], (L_, D_FF, D), D_FF ** -0.5),
        'ple_w_proj': _normal(ks[42], (L_, PLE_DIM, D), PLE_DIM ** -0.5),
        'ple_w_gate': _normal(ks[43], (L_, D, D), D ** -0.5),
    }


def reference(x_prompt, x_sample, p_prompt, p_sample,
              state_ssm_re, state_ssm_im, state_mlstm_c, state_mlstm_n, state_mlstm_m,
              state_gdn_s, state_gdn_conv, state_ffn_conv,
              norm_mix_pre, norm_mix_post, norm_ffn_pre, norm_ffn_post, w_in,
              s5_a_re, s5_a_im, s5_log_dt, s5_b_re, s5_b_im, s5_c_re, s5_c_im, s5_d, s5_w_glu,
              mlstm_b_i, mlstm_b_f, mlstm_norm,
              gdn_conv_w, gdn_a_log, gdn_dt_bias, gdn_norm,
              w_branch_a, w_branch_b, w_branch_c, w_gate, w_out,
              ffn_w_gate, ffn_w_up, ffn_conv_w, ffn_w_down, ple_w_proj, ple_w_gate):
    xp, xs = x_prompt, x_sample
    bp = x_prompt.shape[0]
    sp, ss = [], []
    for i in range(DEPTH):
        lw = {
            'norm_mix_pre': norm_mix_pre[i], 'norm_mix_post': norm_mix_post[i],
            'norm_ffn_pre': norm_ffn_pre[i], 'norm_ffn_post': norm_ffn_post[i],
            'w_in': w_in[i],
            's5_a_re': s5_a_re[i], 's5_a_im': s5_a_im[i], 's5_log_dt': s5_log_dt[i],
            's5_b_re': s5_b_re[i], 's5_b_im': s5_b_im[i], 's5_c_re': s5_c_re[i],
            's5_c_im': s5_c_im[i], 's5_d': s5_d[i], 's5_w_glu': s5_w_glu[i],
            'mlstm_b_i': mlstm_b_i[i], 'mlstm_b_f': mlstm_b_f[i], 'mlstm_norm': mlstm_norm[i],
            'gdn_conv_w': gdn_conv_w[i], 'gdn_a_log': gdn_a_log[i],
            'gdn_dt_bias': gdn_dt_bias[i], 'gdn_norm': gdn_norm[i],
            'w_branch_a': w_branch_a[i], 'w_branch_b': w_branch_b[i], 'w_branch_c': w_branch_c[i],
            'w_gate': w_gate[i], 'w_out': w_out[i],
            'ffn_w_gate': ffn_w_gate[i], 'ffn_w_up': ffn_w_up[i],
            'ffn_conv_w': ffn_conv_w[i], 'ffn_w_down': ffn_w_down[i],
            'ple_w_proj': ple_w_proj[i], 'ple_w_gate': ple_w_gate[i],
        }
        zero_state = (
            jnp.zeros((bp, S5_GROUPS, S5_STATE), F32),
            jnp.zeros((bp, S5_GROUPS, S5_STATE), F32),
            jnp.zeros((bp, H_B, DQK_B, DV_B), F32),
            jnp.zeros((bp, H_B, DQK_B), F32),
            jnp.zeros((bp, H_B), F32),
            jnp.zeros((bp, H_C, DK_C, DV_C), F32),
            jnp.zeros((bp, GDN_CONV - 1, 3 * D_C), F32),
            jnp.zeros((bp, FFN_CONV - 1, D_FF), F32),
        )
        xp, st_p = decoder_layer(xp, p_prompt[i], zero_state, lw)
        xs, st_s = decoder_layer(
            xs, p_sample[i],
            (state_ssm_re[i], state_ssm_im[i], state_mlstm_c[i], state_mlstm_n[i],
             state_mlstm_m[i], state_gdn_s[i], state_gdn_conv[i], state_ffn_conv[i]), lw)
        sp.append(st_p)
        ss.append(st_s)
    return (xp, xs,
            _stack(sp, 0), _stack(sp, 1), _stack(sp, 2), _stack(sp, 3),
            _stack(sp, 4), _stack(sp, 5), _stack(sp, 6), _stack(sp, 7),
            _stack(ss, 0), _stack(ss, 1), _stack(ss, 2), _stack(ss, 3),
            _stack(ss, 4), _stack(ss, 5), _stack(ss, 6), _stack(ss, 7))
```

```python
import functools

import jax
import jax.numpy as jnp
from jax import lax
from jax.experimental import pallas as pl
from jax.experimental.pallas import tpu as pltpu

F32 = jnp.float32
BF16 = jnp.bfloat16

D_MODEL = 2048
DEPTH = 2
D_A = 512
S5_GROUP = 16
S5_GROUPS = 32
S5_STATE = 64
S5_LANES = S5_GROUPS * S5_STATE
S5_CHUNK = 512
S5_NCHUNK = S5_LANES // S5_CHUNK
D_B = 1024
H_B = 4
DV_B = 256
DQK_B = 128
D_C = 512
H_C = 4
DK_C = 128
DV_C = 128
GDN_CONV = 4
D_FF = 8192
FFN_CONV = 3
PLE_DIM = 256
EPS = 1e-6

LANE = 128
SUBLANE = 8
VMEM_LIMIT = 56 * 1024 * 1024

COL_U, COL_QB, COL_KB, COL_VB, COL_OB, COL_QKV, COL_Z, COL_SMALL = 0, 4, 8, 12, 20, 28, 40, 44
N_PROJ = 45 * LANE
SM_I, SM_F, SM_BETA, SM_A = 0, 4, 8, 12

MLSTM_CHUNK = 128
GDN_CHUNK = 64
S5_TILE = 256
SAMPLE_BLOCK = 8

NT = (((1,), (1,)), ((), ()))
TN = (((0,), (0,)), ((), ()))


def _cparams(sem):
    return pltpu.CompilerParams(dimension_semantics=sem, vmem_limit_bytes=VMEM_LIMIT)


def _dot(a, b):
    return jnp.dot(a, b, preferred_element_type=F32)


def _dot_hi(a, b):
    return jnp.dot(a, b, preferred_element_type=F32, precision=lax.Precision.HIGHEST)


def _gelu(x):
    return 0.5 * x * (1.0 + jnp.tanh(0.7978845608028654 * (x + 0.044715 * (x * x * x))))


def _sigmoid(x):
    return 1.0 / (1.0 + jnp.exp(-x))


def _silu(x):
    return x * _sigmoid(x)


def _softplus(x):
    return jnp.maximum(x, 0.0) + jnp.log1p(jnp.exp(-jnp.abs(x)))


def _log_sigmoid(x):
    return -_softplus(-x)


def _rms(x, w):
    return x * lax.rsqrt(jnp.mean(x * x, axis=-1, keepdims=True) + EPS) * w


def _pick(n, cands):
    for c in cands:
        if n % c == 0:
            return c
    return n


def _norm_proj_kernel(x_ref, nw_ref, w_ref, proj_ref, h_ref):
    @pl.when(pl.program_id(1) == 0)
    def _():
        h_ref[...] = _rms(x_ref[...], nw_ref[...]).astype(BF16)

    proj_ref[...] = _dot(h_ref[...], w_ref[...])


def _norm_proj(x, nw, w):
    m, d = x.shape
    n = w.shape[1]
    tm = _pick(m, (832, 640, 512, 256, 128))
    tn = _pick(n, (1152, 640, 384, 128))
    return pl.pallas_call(
        _norm_proj_kernel,
        grid=(m // tm, n // tn),
        in_specs=[pl.BlockSpec((tm, d), lambda i, j: (i, 0)),
                  pl.BlockSpec((1, d), lambda i, j: (0, 0)),
                  pl.BlockSpec((d, tn), lambda i, j: (0, j))],
        out_specs=[pl.BlockSpec((tm, tn), lambda i, j: (i, j)),
                   pl.BlockSpec((tm, d), lambda i, j: (i, 0))],
        out_shape=[jax.ShapeDtypeStruct((m, n), F32), jax.ShapeDtypeStruct((m, d), BF16)],
        compiler_params=_cparams(("arbitrary", "arbitrary")),
        name="norm_proj",
    )(x, nw, w)


def _s5_disc(are, aim, ldt):
    dt = jnp.exp(ldt)
    mag = jnp.exp(dt * are)
    abr = mag * jnp.cos(dt * aim)
    abi = mag * jnp.sin(dt * aim)
    den = are * are + aim * aim
    zr = abr - 1.0
    fre = (zr * are + abi * aim) / den
    fim = (abi * are - zr * aim) / den
    return abr, abi, fre, fim


def _s5_glu(y, u, d_ref, wglu_ref):
    z = _gelu(y + d_ref[...] * u)
    return z * _sigmoid(_dot(z.astype(BF16), wglu_ref[...]))


def _s5_prompt_kernel(u_ref, are_ref, aim_ref, ldt_ref, bre_ref, bim_ref, cre_ref, cim_ref,
                      d_ref, wglu_ref, y_ref, sre_ref, sim_ref, xr_s, xi_s, y_s, car_re, car_im):
    @pl.when(pl.program_id(1) == 0)
    def _():
        car_re[...] = jnp.zeros_like(car_re)
        car_im[...] = jnp.zeros_like(car_im)

    tt = u_ref.shape[0]
    u = u_ref[...]
    ub = u.astype(BF16)
    row = lax.broadcasted_iota(jnp.int32, (SUBLANE, S5_CHUNK), 0)
    for j in range(S5_NCHUNK):
        sl = slice(S5_CHUNK * j, S5_CHUNK * (j + 1))
        abr, abi, fre, fim = _s5_disc(are_ref[:, sl], aim_ref[:, sl], ldt_ref[:, sl])
        uj = ub[:, LANE * j:LANE * (j + 1)]
        bur = _dot(uj, bre_ref[j])
        bui = _dot(uj, bim_ref[j])
        xr_s[...] = fre * bur - fim * bui
        xi_s[...] = fre * bui + fim * bur

        p2r, p2i = abr * abr - abi * abi, 2.0 * abr * abi
        p4r, p4i = p2r * p2r - p2i * p2i, 2.0 * p2r * p2i
        steps = []
        for s, (pr, pi) in ((1, (abr, abi)), (2, (p2r, p2i)), (4, (p4r, p4i))):
            steps.append((s, jnp.where(row >= s, pr, 0.0), jnp.where(row >= s, pi, 0.0)))
        cwr = jnp.zeros((SUBLANE, S5_CHUNK), F32)
        cwi = jnp.zeros((SUBLANE, S5_CHUNK), F32)
        pr, pi = abr, abi
        for r in range(SUBLANE):
            cwr = jnp.where(row == r, pr, cwr)
            cwi = jnp.where(row == r, pi, cwi)
            pr, pi = pr * abr - pi * abi, pr * abi + pi * abr

        def body(g, carry, steps=steps, cwr=cwr, cwi=cwi):
            cr, ci = carry
            r0 = pl.multiple_of(g * SUBLANE, SUBLANE)
            xr = xr_s[pl.ds(r0, SUBLANE), :]
            xi = xi_s[pl.ds(r0, SUBLANE), :]
            for s, mr, mi in steps:
                sr = pltpu.roll(xr, s, axis=0)
                si = pltpu.roll(xi, s, axis=0)
                xr, xi = xr + mr * sr - mi * si, xi + mr * si + mi * sr
            xr, xi = xr + cwr * cr - cwi * ci, xi + cwr * ci + cwi * cr
            xr_s[pl.ds(r0, SUBLANE), :] = xr
            xi_s[pl.ds(r0, SUBLANE), :] = xi
            return xr[SUBLANE - 1:SUBLANE, :], xi[SUBLANE - 1:SUBLANE, :]

        cr, ci = lax.fori_loop(0, tt // SUBLANE, body, (car_re[:, sl], car_im[:, sl]))
        car_re[:, sl] = cr
        car_im[:, sl] = ci
        y_s[:, LANE * j:LANE * (j + 1)] = (_dot(xr_s[...].astype(BF16), cre_ref[j])
                                           - _dot(xi_s[...].astype(BF16), cim_ref[j]))
    y_ref[...] = _s5_glu(y_s[...], u, d_ref, wglu_ref).astype(BF16)
    sre_ref[0] = car_re[...]
    sim_ref[0] = car_im[...]


def _s5_prompt(proj, bsz, seq, sp):
    tt = _pick(seq, (S5_TILE, 128, 64, 32, 16, 8))
    nt = seq // tt
    const2 = lambda b, t: (0, 0)
    const3 = lambda b, t: (0, 0, 0)
    row_spec = pl.BlockSpec((1, S5_LANES), const2)
    return pl.pallas_call(
        _s5_prompt_kernel,
        grid=(bsz, nt),
        in_specs=[pl.BlockSpec((tt, D_A), lambda b, t: (b * nt + t, COL_U)),
                  row_spec, row_spec, row_spec,
                  pl.BlockSpec((S5_NCHUNK, LANE, S5_CHUNK), const3),
                  pl.BlockSpec((S5_NCHUNK, LANE, S5_CHUNK), const3),
                  pl.BlockSpec((S5_NCHUNK, S5_CHUNK, LANE), const3),
                  pl.BlockSpec((S5_NCHUNK, S5_CHUNK, LANE), const3),
                  pl.BlockSpec((1, D_A), const2),
                  pl.BlockSpec((D_A, D_A), const2)],
        out_specs=[pl.BlockSpec((tt, D_A), lambda b, t: (b * nt + t, 0)),
                   pl.BlockSpec((1, 1, S5_LANES), lambda b, t: (b, 0, 0)),
                   pl.BlockSpec((1, 1, S5_LANES), lambda b, t: (b, 0, 0))],
        out_shape=[jax.ShapeDtypeStruct((bsz * seq, D_A), BF16),
                   jax.ShapeDtypeStruct((bsz, 1, S5_LANES), F32),
                   jax.ShapeDtypeStruct((bsz, 1, S5_LANES), F32)],
        scratch_shapes=[pltpu.VMEM((tt, S5_CHUNK), F32), pltpu.VMEM((tt, S5_CHUNK), F32),
                        pltpu.VMEM((tt, D_A), F32),
                        pltpu.VMEM((1, S5_LANES), F32), pltpu.VMEM((1, S5_LANES), F32)],
        compiler_params=_cparams(("arbitrary", "arbitrary")),
        name="s5_prompt",
    )(proj, sp["are"], sp["aim"], sp["ldt"], sp["bre"], sp["bim"], sp["cre"], sp["cim"],
      sp["d"], sp["wglu"])


def _mlstm_prompt_kernel(q_ref, k_ref, v0_ref, v1_ref, o0_ref, o1_ref, sm_ref, bias_ref, nw_ref,
                         y_ref, c_ref, n_ref, m_ref):
    @pl.when(pl.program_id(1) == 0)
    def _():
        c_ref[...] = jnp.zeros_like(c_ref)
        n_ref[...] = jnp.zeros_like(n_ref)
        m_ref[...] = jnp.zeros_like(m_ref)

    c = q_ref.shape[0]
    smb = sm_ref[...] + bias_ref[...]
    lf_all = _log_sigmoid(smb)
    li_t = smb.T
    lf_t = lf_all.T
    rowi = lax.broadcasted_iota(jnp.int32, (c, c), 0)
    coli = lax.broadcasted_iota(jnp.int32, (c, c), 1)
    causal = rowi >= coli
    lane = lax.broadcasted_iota(jnp.int32, (1, LANE), 1)
    m_row = m_ref[0]
    m_out = m_row
    for h in range(H_B):
        q = q_ref[:, DQK_B * h:DQK_B * (h + 1)] * (DQK_B ** -0.5)
        k = k_ref[:, DQK_B * h:DQK_B * (h + 1)]
        v_ref, o_ref = (v0_ref, o0_ref) if h < 2 else (v1_ref, o1_ref)
        v = v_ref[:, DV_B * (h % 2):DV_B * (h % 2 + 1)]
        og = o_ref[:, DV_B * (h % 2):DV_B * (h % 2 + 1)]
        li_c = smb[:, SM_I + h:SM_I + h + 1]
        lf_c = lf_all[:, SM_F + h:SM_F + h + 1]
        li_r = li_t[SM_I + h:SM_I + h + 1, :]
        lf_r = lf_t[SM_F + h:SM_F + h + 1, :]
        bc_c = jnp.sum(jnp.where(causal, lf_r, 0.0), axis=1, keepdims=True)
        bc_r = jnp.sum(jnp.where(rowi <= coli, lf_c, 0.0), axis=0, keepdims=True)
        m_prev = m_row[:, h:h + 1]
        dmat = jnp.where(causal, bc_c - bc_r + li_r, -jnp.inf)
        inter = bc_c + m_prev
        m_t = jnp.maximum(inter, jnp.max(dmat, axis=1, keepdims=True))
        w_intra = jnp.exp(dmat - m_t)
        w_inter = jnp.exp(inter - m_t)
        qb, kb, vb = q.astype(BF16), k.astype(BF16), v.astype(BF16)
        s = lax.dot_general(qb, kb, NT, preferred_element_type=F32) * w_intra
        cst = c_ref[0, h]
        nrow = n_ref[0, h:h + 1, :]
        num = _dot(s.astype(BF16), vb) + w_inter * _dot(qb, cst.astype(BF16))
        nq = jnp.sum(s, axis=1, keepdims=True) + w_inter * jnp.sum(q * nrow, axis=1, keepdims=True)
        hh = num / jnp.maximum(jnp.abs(nq), jnp.exp(-m_t))
        b_last = bc_c[c - 1:c, :]
        expo = b_last - bc_c + li_c
        m_new = jnp.maximum(b_last + m_prev, jnp.max(expo, axis=0, keepdims=True))
        w_s = jnp.exp(expo - m_new)
        decay = jnp.exp(b_last + m_prev - m_new)
        kw = w_s * k
        c_ref[0, h] = decay * cst + lax.dot_general(kw.astype(BF16), vb, TN, preferred_element_type=F32)
        n_ref[0, h:h + 1, :] = decay * nrow + jnp.sum(kw, axis=0, keepdims=True)
        m_out = jnp.where(lane == h, m_new, m_out)
        y_ref[:, DV_B * h:DV_B * (h + 1)] = (_rms(hh, nw_ref[h:h + 1, :]) * _sigmoid(og)).astype(BF16)
    m_ref[0] = m_out


def _mlstm_prompt(proj, bsz, seq, mp):
    c = _pick(seq, (MLSTM_CHUNK,))
    nt = seq // c
    rows = lambda col: (lambda b, t: (b * nt + t, col))
    const2 = lambda b, t: (0, 0)
    return pl.pallas_call(
        _mlstm_prompt_kernel,
        grid=(bsz, nt),
        in_specs=[pl.BlockSpec((c, 512), rows(COL_QB // 4)),
                  pl.BlockSpec((c, 512), rows(COL_KB // 4)),
                  pl.BlockSpec((c, 512), rows(COL_VB // 4)),
                  pl.BlockSpec((c, 512), rows(COL_VB // 4 + 1)),
                  pl.BlockSpec((c, 512), rows(COL_OB // 4)),
                  pl.BlockSpec((c, 512), rows(COL_OB // 4 + 1)),
                  pl.BlockSpec((c, LANE), rows(COL_SMALL)),
                  pl.BlockSpec((1, LANE), const2),
                  pl.BlockSpec((H_B, DV_B), const2)],
        out_specs=[pl.BlockSpec((c, D_B), lambda b, t: (b * nt + t, 0)),
                   pl.BlockSpec((1, H_B, DQK_B, DV_B), lambda b, t: (b, 0, 0, 0)),
                   pl.BlockSpec((1, H_B, DQK_B), lambda b, t: (b, 0, 0)),
                   pl.BlockSpec((1, 1, LANE), lambda b, t: (b, 0, 0))],
        out_shape=[jax.ShapeDtypeStruct((bsz * seq, D_B), BF16),
                   jax.ShapeDtypeStruct((bsz, H_B, DQK_B, DV_B), F32),
                   jax.ShapeDtypeStruct((bsz, H_B, DQK_B), F32),
                   jax.ShapeDtypeStruct((bsz, 1, LANE), F32)],
        compiler_params=_cparams(("arbitrary", "arbitrary")),
        name="mlstm_prompt",
    )(proj, proj, proj, proj, proj, proj, proj, mp["bias"], mp["norm"])


def _unit_lower_inverse(lmat):
    c = lmat.shape[0]
    eye = (lax.broadcasted_iota(jnp.int32, (c, c), 0) == lax.broadcasted_iota(jnp.int32, (c, c), 1)).astype(F32)
    p = -lmat
    t = eye + p
    span = 2
    while span < c:
        p = _dot_hi(p, p)
        t = t + _dot_hi(t, p)
        span *= 2
    return t


def _gdn_prompt_kernel(q_ref, k_ref, v_ref, z_ref, sm_ref, bias_ref, alog_ref, cw_ref, nw_ref,
                       y_ref, s_ref, xb_s):
    c = q_ref.shape[0]

    @pl.when(pl.program_id(1) == 0)
    def _():
        s_ref[...] = jnp.zeros_like(s_ref)
        xb_s[0:SUBLANE, :] = jnp.zeros((SUBLANE, 3 * D_C), F32)

    xb_s[SUBLANE:SUBLANE + c, 0:D_C] = q_ref[...]
    xb_s[SUBLANE:SUBLANE + c, D_C:2 * D_C] = k_ref[...]
    xb_s[SUBLANE:SUBLANE + c, 2 * D_C:3 * D_C] = v_ref[...]
    conv = cw_ref[GDN_CONV - 1:GDN_CONV, :] * xb_s[SUBLANE:SUBLANE + c, :]
    for j in range(GDN_CONV - 1):
        off = SUBLANE - (GDN_CONV - 1) + j
        conv = conv + cw_ref[j:j + 1, :] * xb_s[off:off + c, :]
    xb_s[0:SUBLANE, :] = xb_s[c:c + SUBLANE, :]
    qkv = _silu(conv)

    sm = sm_ref[...]
    beta_all = _sigmoid(sm)
    g_all = -jnp.exp(alog_ref[...]) * _softplus(sm + bias_ref[...])
    g_t = g_all.T
    rowi = lax.broadcasted_iota(jnp.int32, (c, c), 0)
    coli = lax.broadcasted_iota(jnp.int32, (c, c), 1)
    incl = rowi >= coli
    strict = rowi > coli
    for h in range(H_C):
        q = qkv[:, DK_C * h:DK_C * (h + 1)]
        k = qkv[:, D_C + DK_C * h:D_C + DK_C * (h + 1)]
        v = qkv[:, 2 * D_C + DV_C * h:2 * D_C + DV_C * (h + 1)]
        q = q * lax.rsqrt(jnp.sum(q * q, axis=-1, keepdims=True) + EPS) * (DK_C ** -0.5)
        k = k * lax.rsqrt(jnp.sum(k * k, axis=-1, keepdims=True) + EPS)
        beta_c = beta_all[:, SM_BETA + h:SM_BETA + h + 1]
        g_c = g_all[:, SM_A + h:SM_A + h + 1]
        g_r = g_t[SM_A + h:SM_A + h + 1, :]
        gc_c = jnp.sum(jnp.where(incl, g_r, 0.0), axis=1, keepdims=True)
        gc_r = jnp.sum(jnp.where(rowi <= coli, g_c, 0.0), axis=0, keepdims=True)
        gam = jnp.exp(jnp.where(incl, gc_c - gc_r, -jnp.inf))
        qb, kb = q.astype(BF16), k.astype(BF16)
        kk = lax.dot_general(kb, kb, NT, preferred_element_type=F32)
        lmat = jnp.where(strict, beta_c * gam * kk, 0.0)
        egc = jnp.exp(gc_c)
        st = s_ref[0, h]
        stb = st.astype(BF16)
        rhs = beta_c * (v - egc * _dot(kb, stb))
        u = _dot_hi(_unit_lower_inverse(lmat), rhs)
        ub = u.astype(BF16)
        qk = lax.dot_general(qb, kb, NT, preferred_element_type=F32) * gam
        o = egc * _dot(qb, stb) + _dot(qk.astype(BF16), ub)
        g_last = gc_c[c - 1:c, :]
        w_s = jnp.exp(g_last - gc_c)
        s_ref[0, h] = jnp.exp(g_last) * st + lax.dot_general((w_s * k).astype(BF16), ub, TN,
                                                             preferred_element_type=F32)
        zz = z_ref[:, DV_C * h:DV_C * (h + 1)]
        y_ref[:, DV_C * h:DV_C * (h + 1)] = (_rms(o, nw_ref[...]) * _silu(zz)).astype(BF16)


def _gdn_prompt(proj, bsz, seq, gp):
    c = _pick(seq, (GDN_CHUNK,))
    nt = seq // c
    rows = lambda col: (lambda b, t: (b * nt + t, col))
    const2 = lambda b, t: (0, 0)
    return pl.pallas_call(
        _gdn_prompt_kernel,
        grid=(bsz, nt),
        in_specs=[pl.BlockSpec((c, D_C), rows(COL_QKV // 4)),
                  pl.BlockSpec((c, D_C), rows(COL_QKV // 4 + 1)),
                  pl.BlockSpec((c, D_C), rows(COL_QKV // 4 + 2)),
                  pl.BlockSpec((c, D_C), rows(COL_Z // 4)),
                  pl.BlockSpec((c, LANE), rows(COL_SMALL)),
                  pl.BlockSpec((1, LANE), const2),
                  pl.BlockSpec((1, LANE), const2),
                  pl.BlockSpec((GDN_CONV, 3 * D_C), const2),
                  pl.BlockSpec((1, DV_C), const2)],
        out_specs=[pl.BlockSpec((c, D_C), lambda b, t: (b * nt + t, 0)),
                   pl.BlockSpec((1, H_C, DK_C, DV_C), lambda b, t: (b, 0, 0, 0))],
        out_shape=[jax.ShapeDtypeStruct((bsz * seq, D_C), BF16),
                   jax.ShapeDtypeStruct((bsz, H_C, DK_C, DV_C), F32)],
        scratch_shapes=[pltpu.VMEM((SUBLANE + c, 3 * D_C), F32)],
        compiler_params=_cparams(("arbitrary", "arbitrary")),
        name="gdn_prompt",
    )(proj, proj, proj, proj, proj, gp["bias"], gp["alog"], gp["conv_w"], gp["norm"])


def _to_col(row, eye):
    return jnp.sum(jnp.where(eye, row, 0.0), axis=1, keepdims=True)


def _sample_mixers_kernel(
        proj_ref, sre_ref, sim_ref, c_ref, n_ref, m_ref, gs_ref, gbuf_ref,
        are_ref, aim_ref, ldt_ref, bre_ref, bim_ref, cre_ref, cim_ref, d_ref, wglu_ref,
        mbias_ref, mnorm_ref, gbias_ref, alog_ref, gcw_ref, gnorm_ref,
        y_ref, sre_o, sim_o, c_o, n_o, m_o, gs_o, gbuf_o,
        qkv_s, qn_s, kn_s, beta_s, g_s, li_s, lf_s):
    bb = proj_ref.shape[0]

    u = proj_ref[:, COL_U * LANE:COL_U * LANE + D_A]
    ub = u.astype(BF16)
    ys = []
    for j in range(S5_NCHUNK):
        sl = slice(S5_CHUNK * j, S5_CHUNK * (j + 1))
        abr, abi, fre, fim = _s5_disc(are_ref[:, sl], aim_ref[:, sl], ldt_ref[:, sl])
        uj = ub[:, LANE * j:LANE * (j + 1)]
        bur = _dot(uj, bre_ref[j])
        bui = _dot(uj, bim_ref[j])
        s0r = sre_ref[:, sl]
        s0i = sim_ref[:, sl]
        xr = fre * bur - fim * bui + abr * s0r - abi * s0i
        xi = fre * bui + fim * bur + abr * s0i + abi * s0r
        sre_o[:, sl] = xr
        sim_o[:, sl] = xi
        ys.append(_dot(xr.astype(BF16), cre_ref[j]) - _dot(xi.astype(BF16), cim_ref[j]))
    y_a = jnp.concatenate(ys, axis=1)
    y_ref[:, 0:D_A] = _s5_glu(y_a, u, d_ref, wglu_ref)

    sm = proj_ref[:, COL_SMALL * LANE:(COL_SMALL + 1) * LANE]
    smb = sm + mbias_ref[...]
    li_s[...] = smb
    lf_s[...] = _log_sigmoid(smb)
    beta_s[...] = _sigmoid(sm)
    g_s[...] = -jnp.exp(alog_ref[...]) * _softplus(sm + gbias_ref[...])

    xnew = proj_ref[:, COL_QKV * LANE:COL_QKV * LANE + 3 * D_C]
    conv = gcw_ref[GDN_CONV - 1:GDN_CONV, :] * xnew
    for j in range(GDN_CONV - 1):
        conv = conv + gcw_ref[j:j + 1, :] * gbuf_ref[j]
        if j > 0:
            gbuf_o[j - 1] = gbuf_ref[j]
    gbuf_o[GDN_CONV - 2] = xnew
    qkv = _silu(conv)
    qkv_s[...] = qkv
    for h in range(H_C):
        q = qkv[:, DK_C * h:DK_C * (h + 1)]
        k = qkv[:, D_C + DK_C * h:D_C + DK_C * (h + 1)]
        qn_s[:, DK_C * h:DK_C * (h + 1)] = q * lax.rsqrt(jnp.sum(q * q, axis=-1, keepdims=True) + EPS) * (DK_C ** -0.5)
        kn_s[:, DK_C * h:DK_C * (h + 1)] = k * lax.rsqrt(jnp.sum(k * k, axis=-1, keepdims=True) + EPS)

    eye = (lax.broadcasted_iota(jnp.int32, (LANE, LANE), 0)
           == lax.broadcasted_iota(jnp.int32, (LANE, LANE), 1))
    lane = lax.broadcasted_iota(jnp.int32, (1, H_B), 1)

    def per_row(b):
        r = slice(b, b + 1)
        m_row = m_ref[r, :]
        m_out = m_row
        for h in range(H_B):
            q = proj_ref[r, COL_QB * LANE + DQK_B * h:COL_QB * LANE + DQK_B * (h + 1)] * (DQK_B ** -0.5)
            k = proj_ref[r, COL_KB * LANE + DQK_B * h:COL_KB * LANE + DQK_B * (h + 1)]
            v = proj_ref[r, COL_VB * LANE + DV_B * h:COL_VB * LANE + DV_B * (h + 1)]
            og = proj_ref[r, COL_OB * LANE + DV_B * h:COL_OB * LANE + DV_B * (h + 1)]
            li = li_s[r, SM_I + h:SM_I + h + 1]
            lf = lf_s[r, SM_F + h:SM_F + h + 1]
            m_prev = m_row[:, h:h + 1]
            inter = lf + m_prev
            m_t = jnp.maximum(inter, li)
            w_intra = jnp.exp(li - m_t)
            w_inter = jnp.exp(inter - m_t)
            s = jnp.sum(q * k, axis=1, keepdims=True) * w_intra
            cst = c_ref[b, h]
            nrow = n_ref[b, h:h + 1, :]
            qc = _to_col(q, eye)
            kc = _to_col(k, eye)
            num = s * v + w_inter * jnp.sum(qc * cst, axis=0, keepdims=True)
            nq = s + w_inter * jnp.sum(q * nrow, axis=1, keepdims=True)
            hh = num / jnp.maximum(jnp.abs(nq), jnp.exp(-m_t))
            c_o[b, h] = w_inter * cst + (w_intra * kc) * v
            n_o[b, h:h + 1, :] = w_inter * nrow + w_intra * k
            m_out = jnp.where(lane == h, m_t, m_out)
            y_ref[r, D_A + DV_B * h:D_A + DV_B * (h + 1)] = _rms(hh, mnorm_ref[h:h + 1, :]) * _sigmoid(og)
        m_o[r, :] = m_out
        for h in range(H_C):
            q = qn_s[r, DK_C * h:DK_C * (h + 1)]
            k = kn_s[r, DK_C * h:DK_C * (h + 1)]
            v = qkv_s[r, 2 * D_C + DV_C * h:2 * D_C + DV_C * (h + 1)]
            zz = proj_ref[r, COL_Z * LANE + DV_C * h:COL_Z * LANE + DV_C * (h + 1)]
            beta = beta_s[r, SM_BETA + h:SM_BETA + h + 1]
            eg = jnp.exp(g_s[r, SM_A + h:SM_A + h + 1])
            st = gs_ref[b, h]
            qc = _to_col(q, eye)
            kc = _to_col(k, eye)
            un = beta * (v - eg * jnp.sum(kc * st, axis=0, keepdims=True))
            o = eg * jnp.sum(qc * st, axis=0, keepdims=True) + jnp.sum(q * k, axis=1, keepdims=True) * un
            gs_o[b, h] = eg * st + kc * un
            y_ref[r, D_A + D_B + DV_C * h:D_A + D_B + DV_C * (h + 1)] = _rms(o, gnorm_ref[...]) * _silu(zz)

    for b in range(bb):
        per_row(b)


def _sample_mixers(proj, row0, nrows, st, sp, mp, gp):
    bb = SAMPLE_BLOCK
    blk0 = row0 // bb
    const2 = lambda i: (0, 0)
    const3 = lambda i: (0, 0, 0)
    row_spec = pl.BlockSpec((1, S5_LANES), const2)
    in_specs = [
        pl.BlockSpec((bb, N_PROJ), lambda i: (blk0 + i, 0)),
        pl.BlockSpec((bb, S5_LANES), lambda i: (i, 0)),
        pl.BlockSpec((bb, S5_LANES), lambda i: (i, 0)),
        pl.BlockSpec((bb, H_B, DQK_B, DV_B), lambda i: (i, 0, 0, 0)),
        pl.BlockSpec((bb, H_B, DQK_B), lambda i: (i, 0, 0)),
        pl.BlockSpec((bb, H_B), lambda i: (i, 0)),
        pl.BlockSpec((bb, H_C, DK_C, DV_C), lambda i: (i, 0, 0, 0)),
        pl.BlockSpec((GDN_CONV - 1, bb, 3 * D_C), lambda i: (0, i, 0)),
        row_spec, row_spec, row_spec,
        pl.BlockSpec((S5_NCHUNK, LANE, S5_CHUNK), const3),
        pl.BlockSpec((S5_NCHUNK, LANE, S5_CHUNK), const3),
        pl.BlockSpec((S5_NCHUNK, S5_CHUNK, LANE), const3),
        pl.BlockSpec((S5_NCHUNK, S5_CHUNK, LANE), const3),
        pl.BlockSpec((1, D_A), const2),
        pl.BlockSpec((D_A, D_A), const2),
        pl.BlockSpec((1, LANE), const2),
        pl.BlockSpec((H_B, DV_B), const2),
        pl.BlockSpec((1, LANE), const2),
        pl.BlockSpec((1, LANE), const2),
        pl.BlockSpec((GDN_CONV, 3 * D_C), const2),
        pl.BlockSpec((1, DV_C), const2),
    ]
    out_specs = [
        pl.BlockSpec((bb, D_MODEL), lambda i: (i, 0)),
        pl.BlockSpec((bb, S5_LANES), lambda i: (i, 0)),
        pl.BlockSpec((bb, S5_LANES), lambda i: (i, 0)),
        pl.BlockSpec((bb, H_B, DQK_B, DV_B), lambda i: (i, 0, 0, 0)),
        pl.BlockSpec((bb, H_B, DQK_B), lambda i: (i, 0, 0)),
        pl.BlockSpec((bb, H_B), lambda i: (i, 0)),
        pl.BlockSpec((bb, H_C, DK_C, DV_C), lambda i: (i, 0, 0, 0)),
        pl.BlockSpec((GDN_CONV - 1, bb, 3 * D_C), lambda i: (0, i, 0)),
    ]
    out_shape = [
        jax.ShapeDtypeStruct((nrows, D_MODEL), F32),
        jax.ShapeDtypeStruct((nrows, S5_LANES), F32),
        jax.ShapeDtypeStruct((nrows, S5_LANES), F32),
        jax.ShapeDtypeStruct((nrows, H_B, DQK_B, DV_B), F32),
        jax.ShapeDtypeStruct((nrows, H_B, DQK_B), F32),
        jax.ShapeDtypeStruct((nrows, H_B), F32),
        jax.ShapeDtypeStruct((nrows, H_C, DK_C, DV_C), F32),
        jax.ShapeDtypeStruct((GDN_CONV - 1, nrows, 3 * D_C), F32),
    ]
    scratch = [pltpu.VMEM((bb, 3 * D_C), F32), pltpu.VMEM((bb, D_C), F32), pltpu.VMEM((bb, D_C), F32),
               pltpu.VMEM((bb, LANE), F32), pltpu.VMEM((bb, LANE), F32),
               pltpu.VMEM((bb, LANE), F32), pltpu.VMEM((bb, LANE), F32)]
    return pl.pallas_call(
        _sample_mixers_kernel,
        grid=(nrows // bb,),
        in_specs=in_specs, out_specs=out_specs, out_shape=out_shape, scratch_shapes=scratch,
        compiler_params=_cparams(("arbitrary",)),
        name="sample_mixers",
    )(proj, st["sre"], st["sim"], st["c"], st["n"], st["m"], st["gs"], st["gbuf"],
      sp["are"], sp["aim"], sp["ldt"], sp["bre"], sp["bim"], sp["cre"], sp["cim"], sp["d"], sp["wglu"],
      mp["bias"], mp["norm"], gp["bias"], gp["alog"], gp["conv_w"], gp["norm"])


def _merge_kernel(h_ref, ya_ref, yb_ref, yc_ref, wg0_ref, wg1_ref, wg2_ref, wa_ref, wb_ref, wc_ref, o_ref):
    h = h_ref[...]
    acc = _sigmoid(_dot(h, wg0_ref[...])) * _dot(ya_ref[...], wa_ref[...])
    acc = acc + _sigmoid(_dot(h, wg1_ref[...])) * _dot(yb_ref[...], wb_ref[...])
    acc = acc + _sigmoid(_dot(h, wg2_ref[...])) * _dot(yc_ref[...], wc_ref[...])
    o_ref[...] = acc.astype(BF16)


def _merge(h, ya, yb, yc, wg, wa, wb, wc):
    m, d = h.shape
    tm = _pick(m, (640, 512, 256, 128))
    tn = 512
    nb = d // tn
    lhs = lambda w: pl.BlockSpec((tm, w), lambda i, j: (i, 0))
    gate = lambda g: pl.BlockSpec((d, tn), lambda i, j: (0, g * nb + j))
    rhs = lambda w: pl.BlockSpec((w, tn), lambda i, j: (0, j))
    return pl.pallas_call(
        _merge_kernel,
        grid=(m // tm, nb),
        in_specs=[lhs(d), lhs(D_A), lhs(D_B), lhs(D_C), gate(0), gate(1), gate(2),
                  rhs(D_A), rhs(D_B), rhs(D_C)],
        out_specs=pl.BlockSpec((tm, tn), lambda i, j: (i, j)),
        out_shape=jax.ShapeDtypeStruct((m, d), BF16),
        compiler_params=_cparams(("arbitrary", "arbitrary")),
        name="merge",
    )(h, ya, yb, yc, wg, wg, wg, wa, wb, wc)


def _out_proj_kernel(a_ref, x_ref, w_ref, nw_ref, o_ref):
    o_ref[...] = x_ref[...] + _rms(_dot(a_ref[...], w_ref[...]), nw_ref[...])


def _out_proj(a, x, w, nw):
    m, d = x.shape
    tm = _pick(m, (320, 256, 128))
    return pl.pallas_call(
        _out_proj_kernel,
        grid=(m // tm,),
        in_specs=[pl.BlockSpec((tm, d), lambda i: (i, 0)),
                  pl.BlockSpec((tm, d), lambda i: (i, 0)),
                  pl.BlockSpec((d, d), lambda i: (0, 0)),
                  pl.BlockSpec((1, d), lambda i: (0, 0))],
        out_specs=pl.BlockSpec((tm, d), lambda i: (i, 0)),
        out_shape=jax.ShapeDtypeStruct((m, d), F32),
        compiler_params=_cparams(("arbitrary",)),
        name="out_proj",
    )(a, x, w, nw)


def _ffn_kernel(*refs, sample, tiles_per_seq):
    if sample:
        (x_ref, nw_ref, wg_ref, wu_ref, cw_ref, wd_ref, pnw_ref, b0_ref, b1_ref,
         o_ref, g_ref, h2_s, acc_s) = refs
    else:
        (x_ref, nw_ref, wg_ref, wu_ref, cw_ref, wd_ref, pnw_ref,
         o_ref, g_ref, h2_s, acc_s, gb_s, carry_s) = refs
    i = pl.program_id(0)
    j = pl.program_id(1)
    tm = x_ref.shape[0]

    @pl.when(j == 0)
    def _():
        h2_s[...] = _rms(x_ref[...], nw_ref[...]).astype(BF16)
        acc_s[...] = jnp.zeros_like(acc_s)

    h2 = h2_s[...]
    g = _dot(h2, wg_ref[...])
    up = _dot(h2, wu_ref[...])
    if sample:
        a = cw_ref[0:1, :] * b0_ref[...] + cw_ref[1:2, :] * b1_ref[...] + cw_ref[2:3, :] * g
        g_ref[...] = g
    else:
        prev = jnp.where(i % tiles_per_seq == 0, 0.0, carry_s[j])
        gb_s[0:SUBLANE, :] = prev
        gb_s[SUBLANE:SUBLANE + tm, :] = g
        a = (cw_ref[0:1, :] * gb_s[SUBLANE - 2:SUBLANE - 2 + tm, :]
             + cw_ref[1:2, :] * gb_s[SUBLANE - 1:SUBLANE - 1 + tm, :]
             + cw_ref[2:3, :] * g)
        tail = g[tm - SUBLANE:tm, :]
        carry_s[j] = tail
        g_ref[0] = tail
    act = (_gelu(a) * up).astype(BF16)
    acc_s[...] += _dot(act, wd_ref[...])

    @pl.when(j == pl.num_programs(1) - 1)
    def _():
        o_ref[...] = x_ref[...] + _rms(acc_s[...], pnw_ref[...])


def _ffn(x, row0, nrows, seq, nw, wg, wu, cw, wd, pnw, bufs=None):
    d = x.shape[1]
    f = wg.shape[1]
    sample = bufs is not None
    tm = nrows if sample else _pick(seq, (512, 256, 128, 64, 32, 16, 8))
    tn = _pick(f, (512, 256, 128))
    mt, ft = nrows // tm, f // tn
    blk0 = row0 // tm
    in_specs = [pl.BlockSpec((tm, d), lambda i, j: (blk0 + i, 0)),
                pl.BlockSpec((1, d), lambda i, j: (0, 0)),
                pl.BlockSpec((d, tn), lambda i, j: (0, j)),
                pl.BlockSpec((d, tn), lambda i, j: (0, j)),
                pl.BlockSpec((FFN_CONV, tn), lambda i, j: (0, j)),
                pl.BlockSpec((tn, d), lambda i, j: (j, 0)),
                pl.BlockSpec((1, d), lambda i, j: (0, 0))]
    args = [x, nw, wg, wu, cw, wd, pnw]
    scratch = [pltpu.VMEM((tm, d), BF16), pltpu.VMEM((tm, d), F32)]
    if sample:
        in_specs += [pl.BlockSpec((tm, tn), lambda i, j: (i, j))] * 2
        args += list(bufs)
        g_spec = pl.BlockSpec((tm, tn), lambda i, j: (i, j))
        g_shape = jax.ShapeDtypeStruct((nrows, f), F32)
    else:
        scratch += [pltpu.VMEM((SUBLANE + tm, tn), F32), pltpu.VMEM((ft, SUBLANE, tn), F32)]
        g_spec = pl.BlockSpec((1, SUBLANE, tn), lambda i, j: (i, 0, j))
        g_shape = jax.ShapeDtypeStruct((mt, SUBLANE, f), F32)
    return pl.pallas_call(
        functools.partial(_ffn_kernel, sample=sample, tiles_per_seq=max(seq // tm, 1)),
        grid=(mt, ft),
        in_specs=in_specs,
        out_specs=[pl.BlockSpec((tm, d), lambda i, j: (i, 0)), g_spec],
        out_shape=[jax.ShapeDtypeStruct((nrows, d), F32), g_shape],
        scratch_shapes=scratch,
        compiler_params=_cparams(("arbitrary", "arbitrary")),
        name="ffn_sample" if sample else "ffn_prompt",
    )(*args)


def _ple_kernel(x_ref, p_ref, wg_ref, wp_ref, o_ref):
    x = x_ref[...]
    gate = _sigmoid(_dot(x.astype(BF16), wg_ref[...]))
    o_ref[...] = x + gate * _dot(p_ref[...].astype(BF16), wp_ref[...])


def _ple(x, p, wg, wp):
    m, d = x.shape
    tm = _pick(m, (320, 256, 128))
    return pl.pallas_call(
        _ple_kernel,
        grid=(m // tm,),
        in_specs=[pl.BlockSpec((tm, d), lambda i: (i, 0)),
                  pl.BlockSpec((tm, p.shape[1]), lambda i: (i, 0)),
                  pl.BlockSpec((d, d), lambda i: (0, 0)),
                  pl.BlockSpec((p.shape[1], d), lambda i: (0, 0))],
        out_specs=pl.BlockSpec((tm, d), lambda i: (i, 0)),
        out_shape=jax.ShapeDtypeStruct((m, d), F32),
        compiler_params=_cparams(("arbitrary",)),
        name="ple",
    )(x, p, wg, wp)


def _permute_w_in(w):
    d = w.shape[0]
    u, q, k, v = w[:, 0:512], w[:, 512:1024], w[:, 1024:1536], w[:, 1536:2560]
    ig, fg, o = w[:, 2560:2564], w[:, 2564:2568], w[:, 2568:3592]
    qkv, z, beta, a = w[:, 3592:5128], w[:, 5128:5640], w[:, 5640:5644], w[:, 5644:5648]
    pad = jnp.zeros((d, LANE - 16), w.dtype)
    return jnp.concatenate([u, q, k, v, o, qkv, z, ig, fg, beta, a, pad], axis=1).astype(BF16)


def _small_row(entries):
    row = jnp.zeros((LANE,), F32)
    for off, val in entries:
        row = row.at[off:off + val.shape[0]].set(val.astype(F32))
    return row.reshape(1, LANE)


def _s5_params(a_re, a_im, log_dt, b_re, b_im, c_re, c_im, d, w_glu):
    gpc = S5_CHUNK // S5_STATE
    eye = jnp.eye(gpc, dtype=F32)

    def bmat(b):
        b4 = b.reshape(S5_NCHUNK, gpc, S5_GROUP, S5_STATE)
        return jnp.einsum('jgcp,gh->jgchp', b4, eye).reshape(S5_NCHUNK, gpc * S5_GROUP, S5_CHUNK).astype(BF16)

    def cmat(c):
        c4 = c.reshape(S5_NCHUNK, gpc, S5_STATE, S5_GROUP)
        return jnp.einsum('jgpc,gh->jgphc', c4, eye).reshape(S5_NCHUNK, S5_CHUNK, gpc * S5_GROUP).astype(BF16)

    return dict(are=a_re.reshape(1, S5_LANES), aim=a_im.reshape(1, S5_LANES),
                ldt=jnp.broadcast_to(log_dt[:, None], (S5_GROUPS, S5_STATE)).reshape(1, S5_LANES),
                bre=bmat(b_re), bim=bmat(b_im), cre=cmat(c_re), cim=cmat(c_im),
                d=d.reshape(1, D_A), wglu=w_glu.astype(BF16))


def _layer(x, pemb, bsz, seq, nsamp, state, lw):
    np_rows = bsz * seq
    proj, h = _norm_proj(x, lw['norm_mix_pre'].reshape(1, -1), _permute_w_in(lw['w_in']))

    sp = _s5_params(lw['s5_a_re'], lw['s5_a_im'], lw['s5_log_dt'], lw['s5_b_re'], lw['s5_b_im'],
                    lw['s5_c_re'], lw['s5_c_im'], lw['s5_d'], lw['s5_w_glu'])
    mp = dict(bias=_small_row([(SM_I, lw['mlstm_b_i']), (SM_F, lw['mlstm_b_f'])]), norm=lw['mlstm_norm'])
    gp = dict(bias=_small_row([(SM_A, lw['gdn_dt_bias'])]), alog=_small_row([(SM_A, lw['gdn_a_log'])]),
              conv_w=lw['gdn_conv_w'], norm=lw['gdn_norm'].reshape(1, DV_C))

    ya_p, sre_p, sim_p = _s5_prompt(proj, bsz, seq, sp)
    yb_p, c_p, n_p, m_p = _mlstm_prompt(proj, bsz, seq, mp)
    yc_p, gs_p = _gdn_prompt(proj, bsz, seq, gp)

    ssm_re, ssm_im, m_c, m_n, m_m, g_s, g_conv, f_conv = state
    st = dict(sre=ssm_re.reshape(nsamp, S5_LANES), sim=ssm_im.reshape(nsamp, S5_LANES),
              c=m_c, n=m_n, m=m_m, gs=g_s, gbuf=jnp.swapaxes(g_conv, 0, 1))
    y_s, sre_s, sim_s, c_s, n_s, m_s, gs_s, gbuf_s = _sample_mixers(proj, np_rows, nsamp, st, sp, mp, gp)
    y_s = y_s.astype(BF16)

    ya = jnp.concatenate([ya_p, y_s[:, 0:D_A]], axis=0)
    yb = jnp.concatenate([yb_p, y_s[:, D_A:D_A + D_B]], axis=0)
    yc = jnp.concatenate([yc_p, y_s[:, D_A + D_B:]], axis=0)
    merged = _merge(h, ya, yb, yc, lw['w_gate'].astype(BF16), lw['w_branch_a'].astype(BF16),
                    lw['w_branch_b'].astype(BF16), lw['w_branch_c'].astype(BF16))
    x1 = _out_proj(merged, x, lw['w_out'].astype(BF16), lw['norm_mix_post'].reshape(1, -1))

    wg, wu, wd = lw['ffn_w_gate'].astype(BF16), lw['ffn_w_up'].astype(BF16), lw['ffn_w_down'].astype(BF16)
    nw2, pnw2 = lw['norm_ffn_pre'].reshape(1, -1), lw['norm_ffn_post'].reshape(1, -1)
    x2_p, gtail = _ffn(x1, 0, np_rows, seq, nw2, wg, wu, lw['ffn_conv_w'], wd, pnw2)
    x2_s, g_new = _ffn(x1, np_rows, nsamp, 1, nw2, wg, wu, lw['ffn_conv_w'], wd, pnw2,
                       bufs=(f_conv[:, 0], f_conv[:, 1]))
    x2 = jnp.concatenate([x2_p, x2_s], axis=0)
    x3 = _ple(x2, pemb, lw['ple_w_gate'].astype(BF16), lw['ple_w_proj'].astype(BF16))

    tiles_per_seq = gtail.shape[0] // bsz
    qkv_pre = proj[:np_rows, COL_QKV * LANE:COL_QKV * LANE + 3 * D_C].reshape(bsz, seq, 3 * D_C)
    st_p = (sre_p.reshape(bsz, S5_GROUPS, S5_STATE), sim_p.reshape(bsz, S5_GROUPS, S5_STATE),
            c_p, n_p, m_p[:, 0, :H_B], gs_p,
            qkv_pre[:, seq - (GDN_CONV - 1):, :],
            gtail.reshape(bsz, tiles_per_seq, SUBLANE, D_FF)[:, -1, SUBLANE - (FFN_CONV - 1):, :])
    st_s = (sre_s.reshape(nsamp, S5_GROUPS, S5_STATE), sim_s.reshape(nsamp, S5_GROUPS, S5_STATE),
            c_s, n_s, m_s, gs_s, jnp.swapaxes(gbuf_s, 0, 1),
            jnp.stack([f_conv[:, 1], g_new], axis=1))
    return x3, st_p, st_s


def kernel(x_prompt, x_sample, p_prompt, p_sample, state_ssm_re, state_ssm_im, state_mlstm_c, state_mlstm_n, state_mlstm_m, state_gdn_s, state_gdn_conv, state_ffn_conv, norm_mix_pre, norm_mix_post, norm_ffn_pre, norm_ffn_post, w_in, s5_a_re, s5_a_im, s5_log_dt, s5_b_re, s5_b_im, s5_c_re, s5_c_im, s5_d, s5_w_glu, mlstm_b_i, mlstm_b_f, mlstm_norm, gdn_conv_w, gdn_a_log, gdn_dt_bias, gdn_norm, w_branch_a, w_branch_b, w_branch_c, w_gate, w_out, ffn_w_gate, ffn_w_up, ffn_conv_w, ffn_w_down, ple_w_proj, ple_w_gate):
    bsz, seq, d = x_prompt.shape
    nsamp = x_sample.shape[0]
    depth = w_in.shape[0]
    weights = dict(
        norm_mix_pre=norm_mix_pre, norm_mix_post=norm_mix_post, norm_ffn_pre=norm_ffn_pre,
        norm_ffn_post=norm_ffn_post, w_in=w_in, s5_a_re=s5_a_re, s5_a_im=s5_a_im, s5_log_dt=s5_log_dt,
        s5_b_re=s5_b_re, s5_b_im=s5_b_im, s5_c_re=s5_c_re, s5_c_im=s5_c_im, s5_d=s5_d, s5_w_glu=s5_w_glu,
        mlstm_b_i=mlstm_b_i, mlstm_b_f=mlstm_b_f, mlstm_norm=mlstm_norm, gdn_conv_w=gdn_conv_w,
        gdn_a_log=gdn_a_log, gdn_dt_bias=gdn_dt_bias, gdn_norm=gdn_norm, w_branch_a=w_branch_a,
        w_branch_b=w_branch_b, w_branch_c=w_branch_c, w_gate=w_gate, w_out=w_out, ffn_w_gate=ffn_w_gate,
        ffn_w_up=ffn_w_up, ffn_conv_w=ffn_conv_w, ffn_w_down=ffn_w_down, ple_w_proj=ple_w_proj,
        ple_w_gate=ple_w_gate)
    states = (state_ssm_re, state_ssm_im, state_mlstm_c, state_mlstm_n, state_mlstm_m,
              state_gdn_s, state_gdn_conv, state_ffn_conv)
    x = jnp.concatenate([x_prompt.reshape(bsz * seq, d), x_sample.reshape(nsamp, d)], axis=0)
    sp_all, ss_all = [], []
    for i in range(depth):
        lw = {k: v[i] for k, v in weights.items()}
        pemb = jnp.concatenate([p_prompt[i].reshape(bsz * seq, -1), p_sample[i].reshape(nsamp, -1)], axis=0)
        x, st_p, st_s = _layer(x, pemb, bsz, seq, nsamp, tuple(s[i] for s in states), lw)
        sp_all.append(st_p)
        ss_all.append(st_s)
    stack = lambda sts, j: jnp.stack([s[j] for s in sts], axis=0)
    return ((x[:bsz * seq].reshape(bsz, seq, d), x[bsz * seq:].reshape(nsamp, 1, d))
            + tuple(stack(sp_all, j) for j in range(8))
            + tuple(stack(ss_all, j) for j in range(8)))
```

```python
import functools

import jax
import jax.numpy as jnp
from jax import lax
from jax.experimental import pallas as pl
from jax.experimental.pallas import tpu as pltpu

F32 = jnp.float32
BF16 = jnp.bfloat16

D_MODEL = 2048
DEPTH = 2
D_A = 512
S5_GROUP = 16
S5_GROUPS = 32
S5_STATE = 64
S5_LANES = S5_GROUPS * S5_STATE
S5_CHUNK = 512
S5_NCHUNK = S5_LANES // S5_CHUNK
D_B = 1024
H_B = 4
DV_B = 256
DQK_B = 128
D_C = 512
H_C = 4
DK_C = 128
DV_C = 128
GDN_CONV = 4
D_FF = 8192
FFN_CONV = 3
PLE_DIM = 256
EPS = 1e-6

LANE = 128
SUBLANE = 8
VMEM_LIMIT = 56 * 1024 * 1024

COL_U, COL_QB, COL_KB, COL_VB, COL_OB, COL_QKV, COL_Z, COL_SMALL = 0, 4, 8, 12, 20, 28, 40, 44
N_PROJ = 45 * LANE
SM_I, SM_F, SM_BETA, SM_A = 0, 4, 8, 12

MLSTM_CHUNK = 128
GDN_CHUNK = 64
GDN_TILE = 256
S5_TILE = 256
SAMPLE_BLOCK = 8
Y_B, Y_A, Y_C = 0, D_B, D_B + D_A

NT = (((1,), (1,)), ((), ()))
TN = (((0,), (0,)), ((), ()))


def _cparams(sem):
    return pltpu.CompilerParams(dimension_semantics=sem, vmem_limit_bytes=VMEM_LIMIT)


def _dot(a, b):
    return jnp.dot(a, b, preferred_element_type=F32)


def _dot_hi(a, b):
    return jnp.dot(a, b, preferred_element_type=F32, precision=lax.Precision.HIGHEST)


def _gelu(x):
    return 0.5 * x * (1.0 + jnp.tanh(0.7978845608028654 * (x + 0.044715 * (x * x * x))))


def _sigmoid(x):
    return 1.0 / (1.0 + jnp.exp(-x))


def _silu(x):
    return x * _sigmoid(x)


def _softplus(x):
    return jnp.maximum(x, 0.0) + jnp.log1p(jnp.exp(-jnp.abs(x)))


def _log_sigmoid(x):
    return -_softplus(-x)


def _rms(x, w):
    return x * lax.rsqrt(jnp.mean(x * x, axis=-1, keepdims=True) + EPS) * w


def _pick(n, cands):
    for c in cands:
        if n % c == 0:
            return c
    return n


def _norm_proj_kernel(x_ref, nw_ref, w_ref, proj_ref, h_ref):
    @pl.when(pl.program_id(1) == 0)
    def _():
        h_ref[...] = _rms(x_ref[...], nw_ref[...]).astype(BF16)

    proj_ref[...] = _dot(h_ref[...], w_ref[...])


def _norm_proj(x, nw, w):
    m, d = x.shape
    n = w.shape[1]
    tm = _pick(m, (1024, 512, 256, 128))
    tn = _pick(n, (1152, 640, 384, 128))
    return pl.pallas_call(
        _norm_proj_kernel,
        grid=(m // tm, n // tn),
        in_specs=[pl.BlockSpec((tm, d), lambda i, j: (i, 0)),
                  pl.BlockSpec((1, d), lambda i, j: (0, 0)),
                  pl.BlockSpec((d, tn), lambda i, j: (0, j))],
        out_specs=[pl.BlockSpec((tm, tn), lambda i, j: (i, j)),
                   pl.BlockSpec((tm, d), lambda i, j: (i, 0))],
        out_shape=[jax.ShapeDtypeStruct((m, n), F32), jax.ShapeDtypeStruct((m, d), BF16)],
        compiler_params=_cparams(("arbitrary", "arbitrary")),
        name="norm_proj",
    )(x, nw, w)


def _s5_disc(are, aim, ldt):
    dt = jnp.exp(ldt)
    mag = jnp.exp(dt * are)
    abr = mag * jnp.cos(dt * aim)
    abi = mag * jnp.sin(dt * aim)
    den = are * are + aim * aim
    zr = abr - 1.0
    fre = (zr * are + abi * aim) / den
    fim = (abi * are - zr * aim) / den
    return abr, abi, fre, fim


def _s5_glu(y, u, d_ref, wglu_ref):
    z = _gelu(y + d_ref[...] * u)
    return z * _sigmoid(_dot(z.astype(BF16), wglu_ref[...]))


def _s5_prompt_kernel(u_ref, are_ref, aim_ref, ldt_ref, bre_ref, bim_ref, cre_ref, cim_ref,
                      d_ref, wglu_ref, y_ref, sre_ref, sim_ref, xr_s, xi_s, y_s, car_re, car_im):
    @pl.when(pl.program_id(1) == 0)
    def _():
        car_re[...] = jnp.zeros_like(car_re)
        car_im[...] = jnp.zeros_like(car_im)

    tt = u_ref.shape[0]
    u = u_ref[...]
    ub = u.astype(BF16)
    row = lax.broadcasted_iota(jnp.int32, (SUBLANE, S5_CHUNK), 0)
    for j in range(S5_NCHUNK):
        sl = slice(S5_CHUNK * j, S5_CHUNK * (j + 1))
        abr, abi, fre, fim = _s5_disc(are_ref[:, sl], aim_ref[:, sl], ldt_ref[:, sl])
        uj = ub[:, LANE * j:LANE * (j + 1)]
        bur = _dot(uj, bre_ref[j])
        bui = _dot(uj, bim_ref[j])
        xr_s[...] = fre * bur - fim * bui
        xi_s[...] = fre * bui + fim * bur

        p2r, p2i = abr * abr - abi * abi, 2.0 * abr * abi
        p4r, p4i = p2r * p2r - p2i * p2i, 2.0 * p2r * p2i
        steps = []
        for s, (pr, pi) in ((1, (abr, abi)), (2, (p2r, p2i)), (4, (p4r, p4i))):
            steps.append((s, jnp.where(row >= s, pr, 0.0), jnp.where(row >= s, pi, 0.0)))
        cwr = jnp.zeros((SUBLANE, S5_CHUNK), F32)
        cwi = jnp.zeros((SUBLANE, S5_CHUNK), F32)
        pr, pi = abr, abi
        for r in range(SUBLANE):
            cwr = jnp.where(row == r, pr, cwr)
            cwi = jnp.where(row == r, pi, cwi)
            pr, pi = pr * abr - pi * abi, pr * abi + pi * abr

        def body(g, carry, steps=steps, cwr=cwr, cwi=cwi):
            cr, ci = carry
            r0 = pl.multiple_of(g * SUBLANE, SUBLANE)
            xr = xr_s[pl.ds(r0, SUBLANE), :]
            xi = xi_s[pl.ds(r0, SUBLANE), :]
            for s, mr, mi in steps:
                sr = pltpu.roll(xr, s, axis=0)
                si = pltpu.roll(xi, s, axis=0)
                xr, xi = xr + mr * sr - mi * si, xi + mr * si + mi * sr
            xr, xi = xr + cwr * cr - cwi * ci, xi + cwr * ci + cwi * cr
            xr_s[pl.ds(r0, SUBLANE), :] = xr
            xi_s[pl.ds(r0, SUBLANE), :] = xi
            return xr[SUBLANE - 1:SUBLANE, :], xi[SUBLANE - 1:SUBLANE, :]

        cr, ci = lax.fori_loop(0, tt // SUBLANE, body, (car_re[:, sl], car_im[:, sl]))
        car_re[:, sl] = cr
        car_im[:, sl] = ci
        y_s[:, LANE * j:LANE * (j + 1)] = (_dot(xr_s[...].astype(BF16), cre_ref[j])
                                           - _dot(xi_s[...].astype(BF16), cim_ref[j]))
    y_ref[...] = _s5_glu(y_s[...], u, d_ref, wglu_ref).astype(BF16)
    sre_ref[0] = car_re[...]
    sim_ref[0] = car_im[...]


def _s5_prompt(proj, bsz, seq, sp):
    tt = _pick(seq, (S5_TILE, 128, 64, 32, 16, 8))
    nt = seq // tt
    const2 = lambda b, t: (0, 0)
    const3 = lambda b, t: (0, 0, 0)
    row_spec = pl.BlockSpec((1, S5_LANES), const2)
    return pl.pallas_call(
        _s5_prompt_kernel,
        grid=(bsz, nt),
        in_specs=[pl.BlockSpec((tt, D_A), lambda b, t: (b * nt + t, COL_U)),
                  row_spec, row_spec, row_spec,
                  pl.BlockSpec((S5_NCHUNK, LANE, S5_CHUNK), const3),
                  pl.BlockSpec((S5_NCHUNK, LANE, S5_CHUNK), const3),
                  pl.BlockSpec((S5_NCHUNK, S5_CHUNK, LANE), const3),
                  pl.BlockSpec((S5_NCHUNK, S5_CHUNK, LANE), const3),
                  pl.BlockSpec((1, D_A), const2),
                  pl.BlockSpec((D_A, D_A), const2)],
        out_specs=[pl.BlockSpec((tt, D_A), lambda b, t: (b * nt + t, 0)),
                   pl.BlockSpec((1, 1, S5_LANES), lambda b, t: (b, 0, 0)),
                   pl.BlockSpec((1, 1, S5_LANES), lambda b, t: (b, 0, 0))],
        out_shape=[jax.ShapeDtypeStruct((bsz * seq, D_A), BF16),
                   jax.ShapeDtypeStruct((bsz, 1, S5_LANES), F32),
                   jax.ShapeDtypeStruct((bsz, 1, S5_LANES), F32)],
        scratch_shapes=[pltpu.VMEM((tt, S5_CHUNK), F32), pltpu.VMEM((tt, S5_CHUNK), F32),
                        pltpu.VMEM((tt, D_A), F32),
                        pltpu.VMEM((1, S5_LANES), F32), pltpu.VMEM((1, S5_LANES), F32)],
        compiler_params=_cparams(("arbitrary", "arbitrary")),
        name="s5_prompt",
    )(proj, sp["are"], sp["aim"], sp["ldt"], sp["bre"], sp["bim"], sp["cre"], sp["cim"],
      sp["d"], sp["wglu"])


def _mlstm_prompt_kernel(q_ref, k_ref, v0_ref, v1_ref, o0_ref, o1_ref, sm_ref, bias_ref, nw_ref,
                         y_ref, c_ref, n_ref, m_ref):
    @pl.when(pl.program_id(1) == 0)
    def _():
        c_ref[...] = jnp.zeros_like(c_ref)
        n_ref[...] = jnp.zeros_like(n_ref)
        m_ref[...] = jnp.zeros_like(m_ref)

    c = q_ref.shape[0]
    smb = sm_ref[...] + bias_ref[...]
    lf_all = _log_sigmoid(smb)
    li_t = smb.T
    lf_t = lf_all.T
    rowi = lax.broadcasted_iota(jnp.int32, (c, c), 0)
    coli = lax.broadcasted_iota(jnp.int32, (c, c), 1)
    causal = rowi >= coli
    lane = lax.broadcasted_iota(jnp.int32, (1, LANE), 1)
    m_row = m_ref[0]
    hs = range(H_B)
    vo_refs = [(v0_ref, o0_ref) if h < 2 else (v1_ref, o1_ref) for h in hs]
    q = [q_ref[:, DQK_B * h:DQK_B * (h + 1)] * (DQK_B ** -0.5) for h in hs]
    k = [k_ref[:, DQK_B * h:DQK_B * (h + 1)] for h in hs]
    qb = [x.astype(BF16) for x in q]
    kb = [x.astype(BF16) for x in k]
    vb = [vo_refs[h][0][:, DV_B * (h % 2):DV_B * (h % 2 + 1)].astype(BF16) for h in hs]
    qkt = [lax.dot_general(a, b, NT, preferred_element_type=F32) for a, b in zip(qb, kb)]
    cst = [c_ref[0, h] for h in hs]
    qc = [_dot(a, x.astype(BF16)) for a, x in zip(qb, cst)]
    li_c = [smb[:, SM_I + h:SM_I + h + 1] for h in hs]
    li_r = [li_t[SM_I + h:SM_I + h + 1, :] for h in hs]
    bc_c = [jnp.sum(jnp.where(causal, lf_t[SM_F + h:SM_F + h + 1, :], 0.0), axis=1, keepdims=True) for h in hs]
    bc_r = [jnp.sum(jnp.where(rowi <= coli, lf_all[:, SM_F + h:SM_F + h + 1], 0.0), axis=0, keepdims=True)
            for h in hs]
    m_prev = [m_row[:, h:h + 1] for h in hs]
    dmat = [jnp.where(causal, bc_c[h] - bc_r[h] + li_r[h], -jnp.inf) for h in hs]
    inter = [bc_c[h] + m_prev[h] for h in hs]
    m_t = [jnp.maximum(inter[h], jnp.max(dmat[h], axis=1, keepdims=True)) for h in hs]
    w_inter = [jnp.exp(inter[h] - m_t[h]) for h in hs]
    s = [qkt[h] * jnp.exp(dmat[h] - m_t[h]) for h in hs]
    nrow = [n_ref[0, h:h + 1, :] for h in hs]
    num = [_dot(s[h].astype(BF16), vb[h]) + w_inter[h] * qc[h] for h in hs]
    nq = [jnp.sum(s[h], axis=1, keepdims=True) + w_inter[h] * jnp.sum(q[h] * nrow[h], axis=1, keepdims=True)
          for h in hs]
    hh = [num[h] / jnp.maximum(jnp.abs(nq[h]), jnp.exp(-m_t[h])) for h in hs]
    b_last = [x[c - 1:c, :] for x in bc_c]
    expo = [b_last[h] - bc_c[h] + li_c[h] for h in hs]
    m_new = [jnp.maximum(b_last[h] + m_prev[h], jnp.max(expo[h], axis=0, keepdims=True)) for h in hs]
    decay = [jnp.exp(b_last[h] + m_prev[h] - m_new[h]) for h in hs]
    kw = [jnp.exp(expo[h] - m_new[h]) * k[h] for h in hs]
    kv = [lax.dot_general(kw[h].astype(BF16), vb[h], TN, preferred_element_type=F32) for h in hs]
    m_out = m_row
    for h in hs:
        c_ref[0, h] = decay[h] * cst[h] + kv[h]
        n_ref[0, h:h + 1, :] = decay[h] * nrow[h] + jnp.sum(kw[h], axis=0, keepdims=True)
        m_out = jnp.where(lane == h, m_new[h], m_out)
        og = vo_refs[h][1][:, DV_B * (h % 2):DV_B * (h % 2 + 1)]
        y_ref[:, DV_B * h:DV_B * (h + 1)] = (_rms(hh[h], nw_ref[h:h + 1, :]) * _sigmoid(og)).astype(BF16)
    m_ref[0] = m_out


def _mlstm_prompt(proj, bsz, seq, mp):
    c = _pick(seq, (MLSTM_CHUNK,))
    nt = seq // c
    rows = lambda col: (lambda b, t: (b * nt + t, col))
    const2 = lambda b, t: (0, 0)
    return pl.pallas_call(
        _mlstm_prompt_kernel,
        grid=(bsz, nt),
        in_specs=[pl.BlockSpec((c, 512), rows(COL_QB // 4)),
                  pl.BlockSpec((c, 512), rows(COL_KB // 4)),
                  pl.BlockSpec((c, 512), rows(COL_VB // 4)),
                  pl.BlockSpec((c, 512), rows(COL_VB // 4 + 1)),
                  pl.BlockSpec((c, 512), rows(COL_OB // 4)),
                  pl.BlockSpec((c, 512), rows(COL_OB // 4 + 1)),
                  pl.BlockSpec((c, LANE), rows(COL_SMALL)),
                  pl.BlockSpec((1, LANE), const2),
                  pl.BlockSpec((H_B, DV_B), const2)],
        out_specs=[pl.BlockSpec((c, D_B), lambda b, t: (b * nt + t, 0)),
                   pl.BlockSpec((1, H_B, DQK_B, DV_B), lambda b, t: (b, 0, 0, 0)),
                   pl.BlockSpec((1, H_B, DQK_B), lambda b, t: (b, 0, 0)),
                   pl.BlockSpec((1, 1, LANE), lambda b, t: (b, 0, 0))],
        out_shape=[jax.ShapeDtypeStruct((bsz * seq, D_B), BF16),
                   jax.ShapeDtypeStruct((bsz, H_B, DQK_B, DV_B), F32),
                   jax.ShapeDtypeStruct((bsz, H_B, DQK_B), F32),
                   jax.ShapeDtypeStruct((bsz, 1, LANE), F32)],
        compiler_params=_cparams(("arbitrary", "arbitrary")),
        name="mlstm_prompt",
    )(proj, proj, proj, proj, proj, proj, proj, mp["bias"], mp["norm"])


def _split_bf16(a):
    hi = a.astype(BF16)
    return hi, (a - hi.astype(F32)).astype(BF16)


def _dot_x3(a, b):
    ah, al = _split_bf16(a)
    bh, bl = _split_bf16(b)
    return _dot(ah, bh) + _dot(ah, bl) + _dot(al, bh)


def _unit_lower_inverses(lmats):
    c = lmats[0].shape[0]
    eye = (lax.broadcasted_iota(jnp.int32, (c, c), 0) == lax.broadcasted_iota(jnp.int32, (c, c), 1)).astype(F32)
    hi_half = lax.broadcasted_iota(jnp.int32, (c, 2 * c), 1) >= c
    ms = [jnp.concatenate([-l, eye], axis=1) for l in lmats]
    span = 1
    while span < c:
        ms = [_dot_x3(m[:, 0:c], m) + jnp.where(hi_half, m, 0.0) for m in ms]
        span *= 2
    return [m[:, c:2 * c] for m in ms]


def _gdn_prompt_kernel(q_ref, k_ref, v_ref, z_ref, sm_ref, bias_ref, alog_ref, cw_ref, nw_ref,
                       y_ref, s_ref, xb_s, *, chunk):
    tt = q_ref.shape[0]
    c = chunk

    @pl.when(pl.program_id(1) == 0)
    def _():
        s_ref[...] = jnp.zeros_like(s_ref)
        xb_s[0:SUBLANE, :] = jnp.zeros((SUBLANE, 3 * D_C), F32)

    xb_s[SUBLANE:SUBLANE + tt, 0:D_C] = q_ref[...]
    xb_s[SUBLANE:SUBLANE + tt, D_C:2 * D_C] = k_ref[...]
    xb_s[SUBLANE:SUBLANE + tt, 2 * D_C:3 * D_C] = v_ref[...]
    conv = cw_ref[GDN_CONV - 1:GDN_CONV, :] * xb_s[SUBLANE:SUBLANE + tt, :]
    for j in range(GDN_CONV - 1):
        off = SUBLANE - (GDN_CONV - 1) + j
        conv = conv + cw_ref[j:j + 1, :] * xb_s[off:off + tt, :]
    xb_s[0:SUBLANE, :] = xb_s[tt:tt + SUBLANE, :]
    qkv = _silu(conv)

    sm = sm_ref[...]
    beta_all = _sigmoid(sm)
    g_all = -jnp.exp(alog_ref[...]) * _softplus(sm + bias_ref[...])
    g_t = g_all.T
    rowi = lax.broadcasted_iota(jnp.int32, (c, c), 0)
    coli = lax.broadcasted_iota(jnp.int32, (c, c), 1)
    incl = rowi >= coli
    strict = rowi > coli

    nchunk = tt // c
    probs = [(ci, h) for ci in range(nchunk) for h in range(H_C)]
    rows = lambda ci: slice(c * ci, c * (ci + 1))
    l2 = lambda x: x * lax.rsqrt(jnp.sum(x * x, axis=-1, keepdims=True) + EPS)
    q = [l2(qkv[rows(ci), DK_C * h:DK_C * (h + 1)]) * (DK_C ** -0.5) for ci, h in probs]
    k = [l2(qkv[rows(ci), D_C + DK_C * h:D_C + DK_C * (h + 1)]) for ci, h in probs]
    v = [qkv[rows(ci), 2 * D_C + DV_C * h:2 * D_C + DV_C * (h + 1)] for ci, h in probs]
    beta_c = [beta_all[rows(ci), SM_BETA + h:SM_BETA + h + 1] for ci, h in probs]
    gc_c = [jnp.sum(jnp.where(incl, g_t[SM_A + h:SM_A + h + 1, rows(ci)], 0.0), axis=1, keepdims=True)
            for ci, h in probs]
    gc_r = [jnp.sum(jnp.where(rowi <= coli, g_all[rows(ci), SM_A + h:SM_A + h + 1], 0.0), axis=0, keepdims=True)
            for ci, h in probs]
    gam = [jnp.exp(jnp.where(incl, a - b, -jnp.inf)) for a, b in zip(gc_c, gc_r)]
    qb = [x.astype(BF16) for x in q]
    kb = [x.astype(BF16) for x in k]
    kk = [lax.dot_general(x, x, NT, preferred_element_type=F32) for x in kb]
    qk = [(lax.dot_general(a, b, NT, preferred_element_type=F32) * g).astype(BF16) for a, b, g in zip(qb, kb, gam)]
    egc = [jnp.exp(x) for x in gc_c]
    tinv = _unit_lower_inverses([jnp.where(strict, b * g * x, 0.0) for b, g, x in zip(beta_c, gam, kk)])
    tr = [_dot_x3(t, jnp.concatenate([b * vv, (b * e) * kx], axis=1))
          for t, b, vv, e, kx in zip(tinv, beta_c, v, egc, k)]
    u0 = [x[:, 0:DV_C] for x in tr]
    wb = [x[:, DV_C:].astype(BF16) for x in tr]
    g_last = [x[c - 1:c, :] for x in gc_c]
    kw = [(jnp.exp(gl - gc) * kx).astype(BF16) for gl, gc, kx in zip(g_last, gc_c, k)]
    e_last = [jnp.exp(x) for x in g_last]

    st = [s_ref[0, h] for h in range(H_C)]
    for ci in range(nchunk):
        ids = [ci * H_C + h for h in range(H_C)]
        stb = [x.astype(BF16) for x in st]
        ub = [(u0[i] - _dot(wb[i], stb[h])).astype(BF16) for h, i in enumerate(ids)]
        qs = [_dot(qb[i], stb[h]) for h, i in enumerate(ids)]
        st = [e_last[i] * st[h] + lax.dot_general(kw[i], ub[h], TN, preferred_element_type=F32)
              for h, i in enumerate(ids)]
        o = [egc[i] * qs[h] + _dot(qk[i], ub[h]) for h, i in enumerate(ids)]
        for h in range(H_C):
            zz = z_ref[rows(ci), DV_C * h:DV_C * (h + 1)]
            y_ref[rows(ci), DV_C * h:DV_C * (h + 1)] = (_rms(o[h], nw_ref[...]) * _silu(zz)).astype(BF16)
    for h in range(H_C):
        s_ref[0, h] = st[h]


def _gdn_prompt(proj, bsz, seq, gp):
    c = _pick(seq, (GDN_CHUNK,))
    tt = _pick(seq, (GDN_TILE,))
    nt = seq // tt
    rows = lambda col: (lambda b, t: (b * nt + t, col))
    const2 = lambda b, t: (0, 0)
    return pl.pallas_call(
        functools.partial(_gdn_prompt_kernel, chunk=c),
        grid=(bsz, nt),
        in_specs=[pl.BlockSpec((tt, D_C), rows(COL_QKV // 4)),
                  pl.BlockSpec((tt, D_C), rows(COL_QKV // 4 + 1)),
                  pl.BlockSpec((tt, D_C), rows(COL_QKV // 4 + 2)),
                  pl.BlockSpec((tt, D_C), rows(COL_Z // 4)),
                  pl.BlockSpec((tt, LANE), rows(COL_SMALL)),
                  pl.BlockSpec((1, LANE), const2),
                  pl.BlockSpec((1, LANE), const2),
                  pl.BlockSpec((GDN_CONV, 3 * D_C), const2),
                  pl.BlockSpec((1, DV_C), const2)],
        out_specs=[pl.BlockSpec((tt, D_C), lambda b, t: (b * nt + t, 0)),
                   pl.BlockSpec((1, H_C, DK_C, DV_C), lambda b, t: (b, 0, 0, 0))],
        out_shape=[jax.ShapeDtypeStruct((bsz * seq, D_C), BF16),
                   jax.ShapeDtypeStruct((bsz, H_C, DK_C, DV_C), F32)],
        scratch_shapes=[pltpu.VMEM((SUBLANE + tt, 3 * D_C), F32)],
        compiler_params=_cparams(("arbitrary", "arbitrary")),
        name="gdn_prompt",
    )(proj, proj, proj, proj, proj, gp["bias"], gp["alog"], gp["conv_w"], gp["norm"])


def _to_col(row, eye):
    return jnp.sum(jnp.where(eye, row, 0.0), axis=1, keepdims=True)


def _sample_mixers_kernel(
        proj_ref, sre_ref, sim_ref, c_ref, n_ref, m_ref, gs_ref, gbuf_ref,
        are_ref, aim_ref, ldt_ref, bre_ref, bim_ref, cre_ref, cim_ref, d_ref, wglu_ref,
        mbias_ref, mnorm_ref, gbias_ref, alog_ref, gcw_ref, gnorm_ref,
        y_ref, sre_o, sim_o, c_o, n_o, m_o, gs_o, gbuf_o,
        qkv_s, qn_s, kn_s, beta_s, g_s, li_s, lf_s):
    bb = proj_ref.shape[0]

    u = proj_ref[:, COL_U * LANE:COL_U * LANE + D_A]
    ub = u.astype(BF16)
    ys = []
    for j in range(S5_NCHUNK):
        sl = slice(S5_CHUNK * j, S5_CHUNK * (j + 1))
        abr, abi, fre, fim = _s5_disc(are_ref[:, sl], aim_ref[:, sl], ldt_ref[:, sl])
        uj = ub[:, LANE * j:LANE * (j + 1)]
        bur = _dot(uj, bre_ref[j])
        bui = _dot(uj, bim_ref[j])
        s0r = sre_ref[:, sl]
        s0i = sim_ref[:, sl]
        xr = fre * bur - fim * bui + abr * s0r - abi * s0i
        xi = fre * bui + fim * bur + abr * s0i + abi * s0r
        sre_o[:, sl] = xr
        sim_o[:, sl] = xi
        ys.append(_dot(xr.astype(BF16), cre_ref[j]) - _dot(xi.astype(BF16), cim_ref[j]))
    y_a = jnp.concatenate(ys, axis=1)
    y_ref[:, Y_A:Y_A + D_A] = _s5_glu(y_a, u, d_ref, wglu_ref)

    sm = proj_ref[:, COL_SMALL * LANE:(COL_SMALL + 1) * LANE]
    smb = sm + mbias_ref[...]
    li_s[...] = smb
    lf_s[...] = _log_sigmoid(smb)
    beta_s[...] = _sigmoid(sm)
    g_s[...] = -jnp.exp(alog_ref[...]) * _softplus(sm + gbias_ref[...])

    xnew = proj_ref[:, COL_QKV * LANE:COL_QKV * LANE + 3 * D_C]
    conv = gcw_ref[GDN_CONV - 1:GDN_CONV, :] * xnew
    for j in range(GDN_CONV - 1):
        conv = conv + gcw_ref[j:j + 1, :] * gbuf_ref[j]
        if j > 0:
            gbuf_o[j - 1] = gbuf_ref[j]
    gbuf_o[GDN_CONV - 2] = xnew
    qkv = _silu(conv)
    qkv_s[...] = qkv
    for h in range(H_C):
        q = qkv[:, DK_C * h:DK_C * (h + 1)]
        k = qkv[:, D_C + DK_C * h:D_C + DK_C * (h + 1)]
        qn_s[:, DK_C * h:DK_C * (h + 1)] = q * lax.rsqrt(jnp.sum(q * q, axis=-1, keepdims=True) + EPS) * (DK_C ** -0.5)
        kn_s[:, DK_C * h:DK_C * (h + 1)] = k * lax.rsqrt(jnp.sum(k * k, axis=-1, keepdims=True) + EPS)

    eye = (lax.broadcasted_iota(jnp.int32, (LANE, LANE), 0)
           == lax.broadcasted_iota(jnp.int32, (LANE, LANE), 1))
    lane = lax.broadcasted_iota(jnp.int32, (1, H_B), 1)

    probs = [(b, h) for b in range(bb) for h in range(H_B)]
    rsl = lambda b: slice(b, b + 1)

    q = [proj_ref[rsl(b), COL_QB * LANE + DQK_B * h:COL_QB * LANE + DQK_B * (h + 1)] * (DQK_B ** -0.5)
         for b, h in probs]
    k = [proj_ref[rsl(b), COL_KB * LANE + DQK_B * h:COL_KB * LANE + DQK_B * (h + 1)] for b, h in probs]
    v = [proj_ref[rsl(b), COL_VB * LANE + DV_B * h:COL_VB * LANE + DV_B * (h + 1)] for b, h in probs]
    li = [li_s[rsl(b), SM_I + h:SM_I + h + 1] for b, h in probs]
    inter = [lf_s[rsl(b), SM_F + h:SM_F + h + 1] + m_ref[rsl(b), h:h + 1] for b, h in probs]
    m_t = [jnp.maximum(a, c) for a, c in zip(inter, li)]
    w_intra = [jnp.exp(a - c) for a, c in zip(li, m_t)]
    w_inter = [jnp.exp(a - c) for a, c in zip(inter, m_t)]
    qcol = [_to_col(x, eye) for x in q]
    kcol = [_to_col(x, eye) for x in k]
    s = [jnp.sum(a * c, axis=1, keepdims=True) * w for a, c, w in zip(q, k, w_intra)]
    cst = [c_ref[b, h] for b, h in probs]
    nrow = [n_ref[b, h:h + 1, :] for b, h in probs]
    qc = [jnp.sum(a * c, axis=0, keepdims=True) for a, c in zip(qcol, cst)]
    for i, (b, h) in enumerate(probs):
        c_o[b, h] = w_inter[i] * cst[i] + (w_intra[i] * kcol[i]) * v[i]
        n_o[b, h:h + 1, :] = w_inter[i] * nrow[i] + w_intra[i] * k[i]
    num = [s[i] * v[i] + w_inter[i] * qc[i] for i in range(len(probs))]
    nq = [s[i] + w_inter[i] * jnp.sum(q[i] * nrow[i], axis=1, keepdims=True) for i in range(len(probs))]
    hh = [num[i] / jnp.maximum(jnp.abs(nq[i]), jnp.exp(-m_t[i])) for i in range(len(probs))]
    for i, (b, h) in enumerate(probs):
        og = proj_ref[rsl(b), COL_OB * LANE + DV_B * h:COL_OB * LANE + DV_B * (h + 1)]
        y_ref[rsl(b), Y_B + DV_B * h:Y_B + DV_B * (h + 1)] = _rms(hh[i], mnorm_ref[h:h + 1, :]) * _sigmoid(og)
    for b in range(bb):
        m_out = m_ref[rsl(b), :]
        for h in range(H_B):
            m_out = jnp.where(lane == h, m_t[b * H_B + h], m_out)
        m_o[rsl(b), :] = m_out

    q = [qn_s[rsl(b), DK_C * h:DK_C * (h + 1)] for b, h in probs]
    k = [kn_s[rsl(b), DK_C * h:DK_C * (h + 1)] for b, h in probs]
    v = [qkv_s[rsl(b), 2 * D_C + DV_C * h:2 * D_C + DV_C * (h + 1)] for b, h in probs]
    beta = [beta_s[rsl(b), SM_BETA + h:SM_BETA + h + 1] for b, h in probs]
    eg = [jnp.exp(g_s[rsl(b), SM_A + h:SM_A + h + 1]) for b, h in probs]
    qcol = [_to_col(x, eye) for x in q]
    kcol = [_to_col(x, eye) for x in k]
    st = [gs_ref[b, h] for b, h in probs]
    ks = [jnp.sum(a * c, axis=0, keepdims=True) for a, c in zip(kcol, st)]
    qs = [jnp.sum(a * c, axis=0, keepdims=True) for a, c in zip(qcol, st)]
    un = [beta[i] * (v[i] - eg[i] * ks[i]) for i in range(len(probs))]
    for i, (b, h) in enumerate(probs):
        gs_o[b, h] = eg[i] * st[i] + kcol[i] * un[i]
    o = [eg[i] * qs[i] + jnp.sum(q[i] * k[i], axis=1, keepdims=True) * un[i] for i in range(len(probs))]
    for i, (b, h) in enumerate(probs):
        zz = proj_ref[rsl(b), COL_Z * LANE + DV_C * h:COL_Z * LANE + DV_C * (h + 1)]
        y_ref[rsl(b), Y_C + DV_C * h:Y_C + DV_C * (h + 1)] = _rms(o[i], gnorm_ref[...]) * _silu(zz)


def _sample_mixers(proj, row0, nrows, st, sp, mp, gp):
    bb = SAMPLE_BLOCK
    blk0 = row0 // bb
    const2 = lambda i: (0, 0)
    const3 = lambda i: (0, 0, 0)
    row_spec = pl.BlockSpec((1, S5_LANES), const2)
    in_specs = [
        pl.BlockSpec((bb, N_PROJ), lambda i: (blk0 + i, 0)),
        pl.BlockSpec((bb, S5_LANES), lambda i: (i, 0)),
        pl.BlockSpec((bb, S5_LANES), lambda i: (i, 0)),
        pl.BlockSpec((bb, H_B, DQK_B, DV_B), lambda i: (i, 0, 0, 0)),
        pl.BlockSpec((bb, H_B, DQK_B), lambda i: (i, 0, 0)),
        pl.BlockSpec((bb, H_B), lambda i: (i, 0)),
        pl.BlockSpec((bb, H_C, DK_C, DV_C), lambda i: (i, 0, 0, 0)),
        pl.BlockSpec((GDN_CONV - 1, bb, 3 * D_C), lambda i: (0, i, 0)),
        row_spec, row_spec, row_spec,
        pl.BlockSpec((S5_NCHUNK, LANE, S5_CHUNK), const3),
        pl.BlockSpec((S5_NCHUNK, LANE, S5_CHUNK), const3),
        pl.BlockSpec((S5_NCHUNK, S5_CHUNK, LANE), const3),
        pl.BlockSpec((S5_NCHUNK, S5_CHUNK, LANE), const3),
        pl.BlockSpec((1, D_A), const2),
        pl.BlockSpec((D_A, D_A), const2),
        pl.BlockSpec((1, LANE), const2),
        pl.BlockSpec((H_B, DV_B), const2),
        pl.BlockSpec((1, LANE), const2),
        pl.BlockSpec((1, LANE), const2),
        pl.BlockSpec((GDN_CONV, 3 * D_C), const2),
        pl.BlockSpec((1, DV_C), const2),
    ]
    out_specs = [
        pl.BlockSpec((bb, D_MODEL), lambda i: (i, 0)),
        pl.BlockSpec((bb, S5_LANES), lambda i: (i, 0)),
        pl.BlockSpec((bb, S5_LANES), lambda i: (i, 0)),
        pl.BlockSpec((bb, H_B, DQK_B, DV_B), lambda i: (i, 0, 0, 0)),
        pl.BlockSpec((bb, H_B, DQK_B), lambda i: (i, 0, 0)),
        pl.BlockSpec((bb, H_B), lambda i: (i, 0)),
        pl.BlockSpec((bb, H_C, DK_C, DV_C), lambda i: (i, 0, 0, 0)),
        pl.BlockSpec((GDN_CONV - 1, bb, 3 * D_C), lambda i: (0, i, 0)),
    ]
    out_shape = [
        jax.ShapeDtypeStruct((nrows, D_MODEL), F32),
        jax.ShapeDtypeStruct((nrows, S5_LANES), F32),
        jax.ShapeDtypeStruct((nrows, S5_LANES), F32),
        jax.ShapeDtypeStruct((nrows, H_B, DQK_B, DV_B), F32),
        jax.ShapeDtypeStruct((nrows, H_B, DQK_B), F32),
        jax.ShapeDtypeStruct((nrows, H_B), F32),
        jax.ShapeDtypeStruct((nrows, H_C, DK_C, DV_C), F32),
        jax.ShapeDtypeStruct((GDN_CONV - 1, nrows, 3 * D_C), F32),
    ]
    scratch = [pltpu.VMEM((bb, 3 * D_C), F32), pltpu.VMEM((bb, D_C), F32), pltpu.VMEM((bb, D_C), F32),
               pltpu.VMEM((bb, LANE), F32), pltpu.VMEM((bb, LANE), F32),
               pltpu.VMEM((bb, LANE), F32), pltpu.VMEM((bb, LANE), F32)]
    return pl.pallas_call(
        _sample_mixers_kernel,
        grid=(nrows // bb,),
        in_specs=in_specs, out_specs=out_specs, out_shape=out_shape, scratch_shapes=scratch,
        compiler_params=_cparams(("arbitrary",)),
        name="sample_mixers",
    )(proj, st["sre"], st["sim"], st["c"], st["n"], st["m"], st["gs"], st["gbuf"],
      sp["are"], sp["aim"], sp["ldt"], sp["bre"], sp["bim"], sp["cre"], sp["cim"], sp["d"], sp["wglu"],
      mp["bias"], mp["norm"], gp["bias"], gp["alog"], gp["conv_w"], gp["norm"])


def _merge_kernel(h_ref, ya_ref, yb_ref, yc_ref, wg0_ref, wg1_ref, wg2_ref, wa_ref, wb_ref, wc_ref, o_ref):
    h = h_ref[...]
    acc = _sigmoid(_dot(h, wg0_ref[...])) * _dot(ya_ref[...].astype(BF16), wa_ref[...])
    acc = acc + _sigmoid(_dot(h, wg1_ref[...])) * _dot(yb_ref[...].astype(BF16), wb_ref[...])
    acc = acc + _sigmoid(_dot(h, wg2_ref[...])) * _dot(yc_ref[...].astype(BF16), wc_ref[...])
    o_ref[...] = acc.astype(BF16)


def _merge(h, ya, yb, yc, wg, wa, wb, wc):
    m, d = h.shape
    tm = _pick(m, (1024, 512, 256, 128))
    tn = 512
    nb = d // tn
    lhs = lambda w, cb: pl.BlockSpec((tm, w), lambda i, j: (i, cb))
    gate = lambda g: pl.BlockSpec((d, tn), lambda i, j: (0, g * nb + j))
    rhs = lambda w: pl.BlockSpec((w, tn), lambda i, j: (0, j))
    return pl.pallas_call(
        _merge_kernel,
        grid=(m // tm, nb),
        in_specs=[lhs(d, 0), lhs(D_A, ya[1]), lhs(D_B, yb[1]), lhs(D_C, yc[1]), gate(0), gate(1), gate(2),
                  rhs(D_A), rhs(D_B), rhs(D_C)],
        out_specs=pl.BlockSpec((tm, tn), lambda i, j: (i, j)),
        out_shape=jax.ShapeDtypeStruct((m, d), BF16),
        compiler_params=_cparams(("arbitrary", "arbitrary")),
        name="merge",
    )(h, ya[0], yb[0], yc[0], wg, wg, wg, wa, wb, wc)


def _out_proj_kernel(a_ref, x_ref, w_ref, nw_ref, o_ref):
    o_ref[...] = x_ref[...] + _rms(_dot(a_ref[...], w_ref[...]), nw_ref[...])


def _out_proj(a, x, w, nw):
    m, d = x.shape
    tm = _pick(m, (512, 256, 128))
    return pl.pallas_call(
        _out_proj_kernel,
        grid=(m // tm,),
        in_specs=[pl.BlockSpec((tm, d), lambda i: (i, 0)),
                  pl.BlockSpec((tm, d), lambda i: (i, 0)),
                  pl.BlockSpec((d, d), lambda i: (0, 0)),
                  pl.BlockSpec((1, d), lambda i: (0, 0))],
        out_specs=pl.BlockSpec((tm, d), lambda i: (i, 0)),
        out_shape=jax.ShapeDtypeStruct((m, d), F32),
        compiler_params=_cparams(("arbitrary",)),
        name="out_proj",
    )(a, x, w, nw)


def _ffn_kernel(*refs, sample, tiles_per_seq):
    if sample:
        (x_ref, nw_ref, wg_ref, wu_ref, cw_ref, wd_ref, pnw_ref, b0_ref, b1_ref,
         o_ref, g_ref, h2_s, acc_s) = refs
    else:
        (x_ref, nw_ref, wg_ref, wu_ref, cw_ref, wd_ref, pnw_ref,
         o_ref, g_ref, h2_s, acc_s, gb_s, carry_s) = refs
    i = pl.program_id(0)
    j = pl.program_id(1)
    tm = x_ref.shape[0]

    @pl.when(j == 0)
    def _():
        h2_s[...] = _rms(x_ref[...], nw_ref[...]).astype(BF16)
        acc_s[...] = jnp.zeros_like(acc_s)

    h2 = h2_s[...]
    g = _dot(h2, wg_ref[...])
    up = _dot(h2, wu_ref[...])
    if sample:
        a = cw_ref[0:1, :] * b0_ref[...] + cw_ref[1:2, :] * b1_ref[...] + cw_ref[2:3, :] * g
        g_ref[...] = g
    else:
        prev = jnp.where(i % tiles_per_seq == 0, 0.0, carry_s[j])
        gb_s[0:SUBLANE, :] = prev
        gb_s[SUBLANE:SUBLANE + tm, :] = g
        a = (cw_ref[0:1, :] * gb_s[SUBLANE - 2:SUBLANE - 2 + tm, :]
             + cw_ref[1:2, :] * gb_s[SUBLANE - 1:SUBLANE - 1 + tm, :]
             + cw_ref[2:3, :] * g)
        tail = g[tm - SUBLANE:tm, :]
        carry_s[j] = tail
        g_ref[0] = tail
    act = (_gelu(a) * up).astype(BF16)
    acc_s[...] += _dot(act, wd_ref[...])

    @pl.when(j == pl.num_programs(1) - 1)
    def _():
        o_ref[...] = x_ref[...] + _rms(acc_s[...], pnw_ref[...])


def _ffn(x, row0, nrows, seq, nw, wg, wu, cw, wd, pnw, bufs=None):
    d = x.shape[1]
    f = wg.shape[1]
    sample = bufs is not None
    tm = nrows if sample else _pick(seq, (512, 256, 128, 64, 32, 16, 8))
    tn = _pick(f, (512, 256, 128))
    mt, ft = nrows // tm, f // tn
    blk0 = row0 // tm
    in_specs = [pl.BlockSpec((tm, d), lambda i, j: (blk0 + i, 0)),
                pl.BlockSpec((1, d), lambda i, j: (0, 0)),
                pl.BlockSpec((d, tn), lambda i, j: (0, j)),
                pl.BlockSpec((d, tn), lambda i, j: (0, j)),
                pl.BlockSpec((FFN_CONV, tn), lambda i, j: (0, j)),
                pl.BlockSpec((tn, d), lambda i, j: (j, 0)),
                pl.BlockSpec((1, d), lambda i, j: (0, 0))]
    args = [x, nw, wg, wu, cw, wd, pnw]
    scratch = [pltpu.VMEM((tm, d), BF16), pltpu.VMEM((tm, d), F32)]
    if sample:
        in_specs += [pl.BlockSpec((tm, tn), lambda i, j: (i, j))] * 2
        args += list(bufs)
        g_spec = pl.BlockSpec((tm, tn), lambda i, j: (i, j))
        g_shape = jax.ShapeDtypeStruct((nrows, f), F32)
    else:
        scratch += [pltpu.VMEM((SUBLANE + tm, tn), F32), pltpu.VMEM((ft, SUBLANE, tn), F32)]
        g_spec = pl.BlockSpec((1, SUBLANE, tn), lambda i, j: (i, 0, j))
        g_shape = jax.ShapeDtypeStruct((mt, SUBLANE, f), F32)
    return pl.pallas_call(
        functools.partial(_ffn_kernel, sample=sample, tiles_per_seq=max(seq // tm, 1)),
        grid=(mt, ft),
        in_specs=in_specs,
        out_specs=[pl.BlockSpec((tm, d), lambda i, j: (i, 0)), g_spec],
        out_shape=[jax.ShapeDtypeStruct((nrows, d), F32), g_shape],
        scratch_shapes=scratch,
        compiler_params=_cparams(("arbitrary", "arbitrary")),
        name="ffn_sample" if sample else "ffn_prompt",
    )(*args)


def _ple_kernel(x_ref, p_ref, wg_ref, wp_ref, o_ref):
    x = x_ref[...]
    gate = _sigmoid(_dot(x.astype(BF16), wg_ref[...]))
    o_ref[...] = x + gate * _dot(p_ref[...].astype(BF16), wp_ref[...])


def _ple(x, p, wg, wp):
    m, d = x.shape
    tm = _pick(m, (512, 256, 128))
    return pl.pallas_call(
        _ple_kernel,
        grid=(m // tm,),
        in_specs=[pl.BlockSpec((tm, d), lambda i: (i, 0)),
                  pl.BlockSpec((tm, p.shape[1]), lambda i: (i, 0)),
                  pl.BlockSpec((d, d), lambda i: (0, 0)),
                  pl.BlockSpec((p.shape[1], d), lambda i: (0, 0))],
        out_specs=pl.BlockSpec((tm, d), lambda i: (i, 0)),
        out_shape=jax.ShapeDtypeStruct((m, d), F32),
        compiler_params=_cparams(("arbitrary",)),
        name="ple",
    )(x, p, wg, wp)


def _permute_w_in(w):
    d = w.shape[0]
    u, q, k, v = w[:, 0:512], w[:, 512:1024], w[:, 1024:1536], w[:, 1536:2560]
    ig, fg, o = w[:, 2560:2564], w[:, 2564:2568], w[:, 2568:3592]
    qkv, z, beta, a = w[:, 3592:5128], w[:, 5128:5640], w[:, 5640:5644], w[:, 5644:5648]
    pad = jnp.zeros((d, LANE - 16), w.dtype)
    return jnp.concatenate([u, q, k, v, o, qkv, z, ig, fg, beta, a, pad], axis=1).astype(BF16)


def _small_row(entries):
    row = jnp.zeros((LANE,), F32)
    for off, val in entries:
        row = row.at[off:off + val.shape[0]].set(val.astype(F32))
    return row.reshape(1, LANE)


def _s5_params(a_re, a_im, log_dt, b_re, b_im, c_re, c_im, d, w_glu):
    gpc = S5_CHUNK // S5_STATE
    eye = jnp.eye(gpc, dtype=F32)

    def bmat(b):
        b4 = b.reshape(S5_NCHUNK, gpc, S5_GROUP, S5_STATE)
        return jnp.einsum('jgcp,gh->jgchp', b4, eye).reshape(S5_NCHUNK, gpc * S5_GROUP, S5_CHUNK).astype(BF16)

    def cmat(c):
        c4 = c.reshape(S5_NCHUNK, gpc, S5_STATE, S5_GROUP)
        return jnp.einsum('jgpc,gh->jgphc', c4, eye).reshape(S5_NCHUNK, S5_CHUNK, gpc * S5_GROUP).astype(BF16)

    return dict(are=a_re.reshape(1, S5_LANES), aim=a_im.reshape(1, S5_LANES),
                ldt=jnp.broadcast_to(log_dt[:, None], (S5_GROUPS, S5_STATE)).reshape(1, S5_LANES),
                bre=bmat(b_re), bim=bmat(b_im), cre=cmat(c_re), cim=cmat(c_im),
                d=d.reshape(1, D_A), wglu=w_glu.astype(BF16))


def _layer(xp, xs, pemb_p, pemb_s, bsz, seq, state, lw):
    np_rows = bsz * seq
    nsamp = xs.shape[0]
    nw1 = lw['norm_mix_pre'].reshape(1, -1)
    w_in = _permute_w_in(lw['w_in'])
    proj, h_p = _norm_proj(xp, nw1, w_in)
    proj_s, h_s = _norm_proj(xs, nw1, w_in)

    sp = _s5_params(lw['s5_a_re'], lw['s5_a_im'], lw['s5_log_dt'], lw['s5_b_re'], lw['s5_b_im'],
                    lw['s5_c_re'], lw['s5_c_im'], lw['s5_d'], lw['s5_w_glu'])
    mp = dict(bias=_small_row([(SM_I, lw['mlstm_b_i']), (SM_F, lw['mlstm_b_f'])]), norm=lw['mlstm_norm'])
    gp = dict(bias=_small_row([(SM_A, lw['gdn_dt_bias'])]), alog=_small_row([(SM_A, lw['gdn_a_log'])]),
              conv_w=lw['gdn_conv_w'], norm=lw['gdn_norm'].reshape(1, DV_C))

    ya_p, sre_p, sim_p = _s5_prompt(proj, bsz, seq, sp)
    yb_p, c_p, n_p, m_p = _mlstm_prompt(proj, bsz, seq, mp)
    yc_p, gs_p = _gdn_prompt(proj, bsz, seq, gp)

    ssm_re, ssm_im, m_c, m_n, m_m, g_s, g_conv, f_conv = state
    st = dict(sre=ssm_re.reshape(nsamp, S5_LANES), sim=ssm_im.reshape(nsamp, S5_LANES),
              c=m_c, n=m_n, m=m_m, gs=g_s, gbuf=jnp.swapaxes(g_conv, 0, 1))
    y_s, sre_s, sim_s, c_s, n_s, m_s, gs_s, gbuf_s = _sample_mixers(proj_s, 0, nsamp, st, sp, mp, gp)

    wgate, wa, wb, wc = (lw['w_gate'].astype(BF16), lw['w_branch_a'].astype(BF16),
                         lw['w_branch_b'].astype(BF16), lw['w_branch_c'].astype(BF16))
    merged_p = _merge(h_p, (ya_p, 0), (yb_p, 0), (yc_p, 0), wgate, wa, wb, wc)
    merged_s = _merge(h_s, (y_s, Y_A // D_A), (y_s, Y_B // D_B), (y_s, Y_C // D_C), wgate, wa, wb, wc)
    w_out, nw_post = lw['w_out'].astype(BF16), lw['norm_mix_post'].reshape(1, -1)
    x1_p = _out_proj(merged_p, xp, w_out, nw_post)
    x1_s = _out_proj(merged_s, xs, w_out, nw_post)

    wg, wu, wd = lw['ffn_w_gate'].astype(BF16), lw['ffn_w_up'].astype(BF16), lw['ffn_w_down'].astype(BF16)
    nw2, pnw2 = lw['norm_ffn_pre'].reshape(1, -1), lw['norm_ffn_post'].reshape(1, -1)
    x2_p, gtail = _ffn(x1_p, 0, np_rows, seq, nw2, wg, wu, lw['ffn_conv_w'], wd, pnw2)
    x2_s, g_new = _ffn(x1_s, 0, nsamp, 1, nw2, wg, wu, lw['ffn_conv_w'], wd, pnw2,
                       bufs=(f_conv[:, 0], f_conv[:, 1]))
    ple_g, ple_p = lw['ple_w_gate'].astype(BF16), lw['ple_w_proj'].astype(BF16)
    x3_p = _ple(x2_p, pemb_p, ple_g, ple_p)
    x3_s = _ple(x2_s, pemb_s, ple_g, ple_p)

    tiles_per_seq = gtail.shape[0] // bsz
    qkv_pre = proj[:, COL_QKV * LANE:COL_QKV * LANE + 3 * D_C].reshape(bsz, seq, 3 * D_C)
    st_p = (sre_p.reshape(bsz, S5_GROUPS, S5_STATE), sim_p.reshape(bsz, S5_GROUPS, S5_STATE),
            c_p, n_p, m_p[:, 0, :H_B], gs_p,
            qkv_pre[:, seq - (GDN_CONV - 1):, :],
            gtail.reshape(bsz, tiles_per_seq, SUBLANE, D_FF)[:, -1, SUBLANE - (FFN_CONV - 1):, :])
    st_s = (sre_s.reshape(nsamp, S5_GROUPS, S5_STATE), sim_s.reshape(nsamp, S5_GROUPS, S5_STATE),
            c_s, n_s, m_s, gs_s, jnp.swapaxes(gbuf_s, 0, 1),
            jnp.stack([f_conv[:, 1], g_new], axis=1))
    return x3_p, x3_s, st_p, st_s


def kernel(x_prompt, x_sample, p_prompt, p_sample, state_ssm_re, state_ssm_im, state_mlstm_c, state_mlstm_n, state_mlstm_m, state_gdn_s, state_gdn_conv, state_ffn_conv, norm_mix_pre, norm_mix_post, norm_ffn_pre, norm_ffn_post, w_in, s5_a_re, s5_a_im, s5_log_dt, s5_b_re, s5_b_im, s5_c_re, s5_c_im, s5_d, s5_w_glu, mlstm_b_i, mlstm_b_f, mlstm_norm, gdn_conv_w, gdn_a_log, gdn_dt_bias, gdn_norm, w_branch_a, w_branch_b, w_branch_c, w_gate, w_out, ffn_w_gate, ffn_w_up, ffn_conv_w, ffn_w_down, ple_w_proj, ple_w_gate):
    bsz, seq, d = x_prompt.shape
    nsamp = x_sample.shape[0]
    depth = w_in.shape[0]
    weights = dict(
        norm_mix_pre=norm_mix_pre, norm_mix_post=norm_mix_post, norm_ffn_pre=norm_ffn_pre,
        norm_ffn_post=norm_ffn_post, w_in=w_in, s5_a_re=s5_a_re, s5_a_im=s5_a_im, s5_log_dt=s5_log_dt,
        s5_b_re=s5_b_re, s5_b_im=s5_b_im, s5_c_re=s5_c_re, s5_c_im=s5_c_im, s5_d=s5_d, s5_w_glu=s5_w_glu,
        mlstm_b_i=mlstm_b_i, mlstm_b_f=mlstm_b_f, mlstm_norm=mlstm_norm, gdn_conv_w=gdn_conv_w,
        gdn_a_log=gdn_a_log, gdn_dt_bias=gdn_dt_bias, gdn_norm=gdn_norm, w_branch_a=w_branch_a,
        w_branch_b=w_branch_b, w_branch_c=w_branch_c, w_gate=w_gate, w_out=w_out, ffn_w_gate=ffn_w_gate,
        ffn_w_up=ffn_w_up, ffn_conv_w=ffn_conv_w, ffn_w_down=ffn_w_down, ple_w_proj=ple_w_proj,
        ple_w_gate=ple_w_gate)
    states = (state_ssm_re, state_ssm_im, state_mlstm_c, state_mlstm_n, state_mlstm_m,
              state_gdn_s, state_gdn_conv, state_ffn_conv)
    xp = x_prompt.reshape(bsz * seq, d)
    xs = x_sample.reshape(nsamp, d)
    sp_all, ss_all = [], []
    for i in range(depth):
        lw = {k: v[i] for k, v in weights.items()}
        xp, xs, st_p, st_s = _layer(xp, xs, p_prompt[i].reshape(bsz * seq, -1), p_sample[i].reshape(nsamp, -1),
                                    bsz, seq, tuple(s[i] for s in states), lw)
        sp_all.append(st_p)
        ss_all.append(st_s)
    stack = lambda sts, j: jnp.stack([s[j] for s in sts], axis=0)
    return ((xp.reshape(bsz, seq, d), xs.reshape(nsamp, 1, d))
            + tuple(stack(sp_all, j) for j in range(8))
            + tuple(stack(ss_all, j) for j in range(8)))
```

```python
import functools

import jax
import jax.numpy as jnp
from jax import lax
from jax.experimental import pallas as pl
from jax.experimental.pallas import tpu as pltpu

F32 = jnp.float32
BF16 = jnp.bfloat16

D_MODEL = 2048
DEPTH = 2
D_A = 512
S5_GROUP = 16
S5_GROUPS = 32
S5_STATE = 64
S5_LANES = S5_GROUPS * S5_STATE
S5_CHUNK = 512
S5_NCHUNK = S5_LANES // S5_CHUNK
D_B = 1024
H_B = 4
DV_B = 256
DQK_B = 128
D_C = 512
H_C = 4
DK_C = 128
DV_C = 128
GDN_CONV = 4
D_FF = 8192
FFN_CONV = 3
PLE_DIM = 256
EPS = 1e-6

LANE = 128
SUBLANE = 8
VMEM_LIMIT = 56 * 1024 * 1024

COL_U, COL_QB, COL_KB, COL_VB, COL_OB, COL_QKV, COL_Z, COL_SMALL = 0, 4, 8, 12, 20, 28, 40, 44
N_PROJ = 45 * LANE
SM_I, SM_F, SM_BETA, SM_A = 0, 4, 8, 12

MLSTM_CHUNK = 128
GDN_CHUNK = 64
GDN_TILE = 256
S5_TILE = 256
SAMPLE_BLOCK = 8
Y_B, Y_A, Y_C = 0, D_B, D_B + D_A

NT = (((1,), (1,)), ((), ()))
TN = (((0,), (0,)), ((), ()))


def _cparams(sem):
    return pltpu.CompilerParams(dimension_semantics=sem, vmem_limit_bytes=VMEM_LIMIT)


def _dot(a, b):
    return jnp.dot(a, b, preferred_element_type=F32)


def _dot_hi(a, b):
    return jnp.dot(a, b, preferred_element_type=F32, precision=lax.Precision.HIGHEST)


def _gelu(x):
    return 0.5 * x * (1.0 + jnp.tanh(0.7978845608028654 * (x + 0.044715 * (x * x * x))))


def _sigmoid(x):
    return 1.0 / (1.0 + jnp.exp(-x))


def _silu(x):
    return x * _sigmoid(x)


def _softplus(x):
    return jnp.maximum(x, 0.0) + jnp.log1p(jnp.exp(-jnp.abs(x)))


def _log_sigmoid(x):
    return -_softplus(-x)


def _rms(x, w):
    return x * lax.rsqrt(jnp.mean(x * x, axis=-1, keepdims=True) + EPS) * w


def _layered(layer, shape, imap):
    return pl.BlockSpec((None,) + shape, lambda *g: (layer,) + imap(*g))


def _pick(n, cands):
    for c in cands:
        if n % c == 0:
            return c
    return n


def _norm_proj_kernel(x_ref, nw_ref, w_ref, proj_ref, h_ref):
    @pl.when(pl.program_id(1) == 0)
    def _():
        h_ref[...] = _rms(x_ref[...], nw_ref[...]).astype(BF16)

    proj_ref[...] = _dot(h_ref[...], w_ref[...])


def _norm_proj(x, nw, w, layer):
    m, d = x.shape
    n = w.shape[-1]
    tm = _pick(m, (1024, 512, 256, 128))
    tn = _pick(n, (1152, 640, 384, 128))
    return pl.pallas_call(
        _norm_proj_kernel,
        grid=(m // tm, n // tn),
        in_specs=[pl.BlockSpec((tm, d), lambda i, j: (i, 0)),
                  _layered(layer, (1, d), lambda i, j: (0, 0)),
                  _layered(layer, (d, tn), lambda i, j: (0, j))],
        out_specs=[pl.BlockSpec((tm, tn), lambda i, j: (i, j)),
                   pl.BlockSpec((tm, d), lambda i, j: (i, 0))],
        out_shape=[jax.ShapeDtypeStruct((m, n), F32), jax.ShapeDtypeStruct((m, d), BF16)],
        compiler_params=_cparams(("arbitrary", "arbitrary")),
        name="norm_proj",
    )(x, nw, w)


def _s5_disc(are, aim, ldt):
    dt = jnp.exp(ldt)
    mag = jnp.exp(dt * are)
    abr = mag * jnp.cos(dt * aim)
    abi = mag * jnp.sin(dt * aim)
    den = are * are + aim * aim
    zr = abr - 1.0
    fre = (zr * are + abi * aim) / den
    fim = (abi * are - zr * aim) / den
    return abr, abi, fre, fim


def _s5_glu(y, u, d_ref, wglu_ref):
    z = _gelu(y + d_ref[...] * u)
    return z * _sigmoid(_dot(z.astype(BF16), wglu_ref[...]))


def _s5_prompt_kernel(u_ref, are_ref, aim_ref, ldt_ref, bre_ref, bim_ref, cre_ref, cim_ref,
                      d_ref, wglu_ref, y_ref, sre_ref, sim_ref, xr_s, xi_s, y_s, car_re, car_im,
                      f_s, tab_s):
    @pl.when(pl.program_id(1) == 0)
    def _():
        car_re[...] = jnp.zeros_like(car_re)
        car_im[...] = jnp.zeros_like(car_im)
        row = lax.broadcasted_iota(jnp.int32, (SUBLANE, S5_LANES), 0)
        abr, abi, fre, fim = _s5_disc(are_ref[...], aim_ref[...], ldt_ref[...])
        f_s[0:1, :] = fre
        f_s[1:2, :] = fim
        pr, pi = abr, abi
        for lvl, s in enumerate((1, 2, 4)):
            tab_s[2 * lvl] = jnp.where(row >= s, pr, 0.0)
            tab_s[2 * lvl + 1] = jnp.where(row >= s, pi, 0.0)
            pr, pi = pr * pr - pi * pi, 2.0 * pr * pi
        cwr = jnp.zeros((SUBLANE, S5_LANES), F32)
        cwi = jnp.zeros((SUBLANE, S5_LANES), F32)
        pr, pi = abr, abi
        for r in range(SUBLANE):
            cwr = jnp.where(row == r, pr, cwr)
            cwi = jnp.where(row == r, pi, cwi)
            pr, pi = pr * abr - pi * abi, pr * abi + pi * abr
        tab_s[6] = cwr
        tab_s[7] = cwi

    tt = u_ref.shape[0]
    u = u_ref[...]
    ub = u.astype(BF16)
    for j in range(S5_NCHUNK):
        sl = slice(S5_CHUNK * j, S5_CHUNK * (j + 1))
        fre, fim = f_s[0:1, sl], f_s[1:2, sl]
        uj = ub[:, LANE * j:LANE * (j + 1)]
        bur = _dot(uj, bre_ref[j])
        bui = _dot(uj, bim_ref[j])
        xr = (fre * bur - fim * bui).reshape(tt // SUBLANE, SUBLANE, S5_CHUNK)
        xi = (fre * bui + fim * bur).reshape(tt // SUBLANE, SUBLANE, S5_CHUNK)
        for lvl, s in enumerate((1, 2, 4)):
            mr = tab_s[2 * lvl, :, sl]
            mi = tab_s[2 * lvl + 1, :, sl]
            sr = pltpu.roll(xr, s, axis=1)
            si = pltpu.roll(xi, s, axis=1)
            xr, xi = xr + mr * sr - mi * si, xi + mr * si + mi * sr
        xr_s[...] = xr.reshape(tt, S5_CHUNK)
        xi_s[...] = xi.reshape(tt, S5_CHUNK)
        cwr = tab_s[6, :, sl]
        cwi = tab_s[7, :, sl]

        def body(g, carry, cwr=cwr, cwi=cwi):
            cr, ci = carry
            r0 = pl.multiple_of(g * SUBLANE, SUBLANE)
            gr = xr_s[pl.ds(r0, SUBLANE), :] + cwr * cr - cwi * ci
            gi = xi_s[pl.ds(r0, SUBLANE), :] + cwr * ci + cwi * cr
            xr_s[pl.ds(r0, SUBLANE), :] = gr
            xi_s[pl.ds(r0, SUBLANE), :] = gi
            return gr[SUBLANE - 1:SUBLANE, :], gi[SUBLANE - 1:SUBLANE, :]

        cr, ci = lax.fori_loop(0, tt // SUBLANE, body, (car_re[:, sl], car_im[:, sl]), unroll=4)
        car_re[:, sl] = cr
        car_im[:, sl] = ci
        y_s[:, LANE * j:LANE * (j + 1)] = (_dot(xr_s[...].astype(BF16), cre_ref[j])
                                           - _dot(xi_s[...].astype(BF16), cim_ref[j]))
    y_ref[...] = _s5_glu(y_s[...], u, d_ref, wglu_ref).astype(BF16)
    sre_ref[0] = car_re[...]
    sim_ref[0] = car_im[...]


def _s5_prompt(proj, bsz, seq, sp):
    tt = _pick(seq, (S5_TILE, 128, 64, 32, 16, 8))
    nt = seq // tt
    const2 = lambda b, t: (0, 0)
    const3 = lambda b, t: (0, 0, 0)
    row_spec = pl.BlockSpec((1, S5_LANES), const2)
    return pl.pallas_call(
        _s5_prompt_kernel,
        grid=(bsz, nt),
        in_specs=[pl.BlockSpec((tt, D_A), lambda b, t: (b * nt + t, COL_U)),
                  row_spec, row_spec, row_spec,
                  pl.BlockSpec((S5_NCHUNK, LANE, S5_CHUNK), const3),
                  pl.BlockSpec((S5_NCHUNK, LANE, S5_CHUNK), const3),
                  pl.BlockSpec((S5_NCHUNK, S5_CHUNK, LANE), const3),
                  pl.BlockSpec((S5_NCHUNK, S5_CHUNK, LANE), const3),
                  pl.BlockSpec((1, D_A), const2),
                  pl.BlockSpec((D_A, D_A), const2)],
        out_specs=[pl.BlockSpec((tt, D_A), lambda b, t: (b * nt + t, 0)),
                   pl.BlockSpec((1, 1, S5_LANES), lambda b, t: (b, 0, 0)),
                   pl.BlockSpec((1, 1, S5_LANES), lambda b, t: (b, 0, 0))],
        out_shape=[jax.ShapeDtypeStruct((bsz * seq, D_A), BF16),
                   jax.ShapeDtypeStruct((bsz, 1, S5_LANES), F32),
                   jax.ShapeDtypeStruct((bsz, 1, S5_LANES), F32)],
        scratch_shapes=[pltpu.VMEM((tt, S5_CHUNK), F32), pltpu.VMEM((tt, S5_CHUNK), F32),
                        pltpu.VMEM((tt, D_A), F32),
                        pltpu.VMEM((1, S5_LANES), F32), pltpu.VMEM((1, S5_LANES), F32),
                        pltpu.VMEM((2, S5_LANES), F32), pltpu.VMEM((8, SUBLANE, S5_LANES), F32)],
        compiler_params=_cparams(("arbitrary", "arbitrary")),
        name="s5_prompt",
    )(proj, sp["are"], sp["aim"], sp["ldt"], sp["bre"], sp["bim"], sp["cre"], sp["cim"],
      sp["d"], sp["wglu"])


def _mlstm_prompt_kernel(q_ref, k_ref, v0_ref, v1_ref, o0_ref, o1_ref, sm_ref, bias_ref, nw_ref,
                         y_ref, c_ref, n_ref, m_ref):
    @pl.when(pl.program_id(1) == 0)
    def _():
        c_ref[...] = jnp.zeros_like(c_ref)
        n_ref[...] = jnp.zeros_like(n_ref)
        m_ref[...] = jnp.zeros_like(m_ref)

    c = q_ref.shape[0]
    smb = sm_ref[...] + bias_ref[...]
    lf_all = _log_sigmoid(smb)
    li_t = smb.T
    lf_t = lf_all.T
    rowi = lax.broadcasted_iota(jnp.int32, (c, c), 0)
    coli = lax.broadcasted_iota(jnp.int32, (c, c), 1)
    causal = rowi >= coli
    lane = lax.broadcasted_iota(jnp.int32, (1, LANE), 1)
    m_row = m_ref[0]
    hs = range(H_B)
    vo_refs = [(v0_ref, o0_ref) if h < 2 else (v1_ref, o1_ref) for h in hs]
    q = [q_ref[:, DQK_B * h:DQK_B * (h + 1)] * (DQK_B ** -0.5) for h in hs]
    k = [k_ref[:, DQK_B * h:DQK_B * (h + 1)] for h in hs]
    qb = [x.astype(BF16) for x in q]
    kb = [x.astype(BF16) for x in k]
    vb = [vo_refs[h][0][:, DV_B * (h % 2):DV_B * (h % 2 + 1)].astype(BF16) for h in hs]
    qkt = [lax.dot_general(a, b, NT, preferred_element_type=F32) for a, b in zip(qb, kb)]
    cst = [c_ref[0, h] for h in hs]
    qc = [_dot(a, x.astype(BF16)) for a, x in zip(qb, cst)]
    li_c = [smb[:, SM_I + h:SM_I + h + 1] for h in hs]
    li_r = [li_t[SM_I + h:SM_I + h + 1, :] for h in hs]
    bc_c = [jnp.sum(jnp.where(causal, lf_t[SM_F + h:SM_F + h + 1, :], 0.0), axis=1, keepdims=True) for h in hs]
    bc_r = [jnp.sum(jnp.where(rowi <= coli, lf_all[:, SM_F + h:SM_F + h + 1], 0.0), axis=0, keepdims=True)
            for h in hs]
    m_prev = [m_row[:, h:h + 1] for h in hs]
    dmat = [jnp.where(causal, bc_c[h] - bc_r[h] + li_r[h], -jnp.inf) for h in hs]
    inter = [bc_c[h] + m_prev[h] for h in hs]
    m_t = [jnp.maximum(inter[h], jnp.max(dmat[h], axis=1, keepdims=True)) for h in hs]
    w_inter = [jnp.exp(inter[h] - m_t[h]) for h in hs]
    s = [qkt[h] * jnp.exp(dmat[h] - m_t[h]) for h in hs]
    nrow = [n_ref[0, h:h + 1, :] for h in hs]
    num = [_dot(s[h].astype(BF16), vb[h]) + w_inter[h] * qc[h] for h in hs]
    nq = [jnp.sum(s[h], axis=1, keepdims=True) + w_inter[h] * jnp.sum(q[h] * nrow[h], axis=1, keepdims=True)
          for h in hs]
    hh = [num[h] / jnp.maximum(jnp.abs(nq[h]), jnp.exp(-m_t[h])) for h in hs]
    b_last = [x[c - 1:c, :] for x in bc_c]
    expo = [b_last[h] - bc_c[h] + li_c[h] for h in hs]
    m_new = [jnp.maximum(b_last[h] + m_prev[h], jnp.max(expo[h], axis=0, keepdims=True)) for h in hs]
    decay = [jnp.exp(b_last[h] + m_prev[h] - m_new[h]) for h in hs]
    kw = [jnp.exp(expo[h] - m_new[h]) * k[h] for h in hs]
    kv = [lax.dot_general(kw[h].astype(BF16), vb[h], TN, preferred_element_type=F32) for h in hs]
    m_out = m_row
    for h in hs:
        c_ref[0, h] = decay[h] * cst[h] + kv[h]
        n_ref[0, h:h + 1, :] = decay[h] * nrow[h] + jnp.sum(kw[h], axis=0, keepdims=True)
        m_out = jnp.where(lane == h, m_new[h], m_out)
        og = vo_refs[h][1][:, DV_B * (h % 2):DV_B * (h % 2 + 1)]
        y_ref[:, DV_B * h:DV_B * (h + 1)] = (_rms(hh[h], nw_ref[h:h + 1, :]) * _sigmoid(og)).astype(BF16)
    m_ref[0] = m_out


def _mlstm_prompt(proj, bsz, seq, mp):
    c = _pick(seq, (MLSTM_CHUNK,))
    nt = seq // c
    rows = lambda col: (lambda b, t: (b * nt + t, col))
    const2 = lambda b, t: (0, 0)
    return pl.pallas_call(
        _mlstm_prompt_kernel,
        grid=(bsz, nt),
        in_specs=[pl.BlockSpec((c, 512), rows(COL_QB // 4)),
                  pl.BlockSpec((c, 512), rows(COL_KB // 4)),
                  pl.BlockSpec((c, 512), rows(COL_VB // 4)),
                  pl.BlockSpec((c, 512), rows(COL_VB // 4 + 1)),
                  pl.BlockSpec((c, 512), rows(COL_OB // 4)),
                  pl.BlockSpec((c, 512), rows(COL_OB // 4 + 1)),
                  pl.BlockSpec((c, LANE), rows(COL_SMALL)),
                  pl.BlockSpec((1, LANE), const2),
                  pl.BlockSpec((H_B, DV_B), const2)],
        out_specs=[pl.BlockSpec((c, D_B), lambda b, t: (b * nt + t, 0)),
                   pl.BlockSpec((1, H_B, DQK_B, DV_B), lambda b, t: (b, 0, 0, 0)),
                   pl.BlockSpec((1, H_B, DQK_B), lambda b, t: (b, 0, 0)),
                   pl.BlockSpec((1, 1, LANE), lambda b, t: (b, 0, 0))],
        out_shape=[jax.ShapeDtypeStruct((bsz * seq, D_B), BF16),
                   jax.ShapeDtypeStruct((bsz, H_B, DQK_B, DV_B), F32),
                   jax.ShapeDtypeStruct((bsz, H_B, DQK_B), F32),
                   jax.ShapeDtypeStruct((bsz, 1, LANE), F32)],
        compiler_params=_cparams(("arbitrary", "arbitrary")),
        name="mlstm_prompt",
    )(proj, proj, proj, proj, proj, proj, proj, mp["bias"], mp["norm"])


def _split_bf16(a):
    hi = a.astype(BF16)
    return hi, (a - hi.astype(F32)).astype(BF16)


def _dot_x3(a, b):
    ah, al = _split_bf16(a)
    bh, bl = _split_bf16(b)
    return _dot(ah, bh) + _dot(ah, bl) + _dot(al, bh)


def _unit_lower_inverses(lmats):
    c = lmats[0].shape[0]
    eye = (lax.broadcasted_iota(jnp.int32, (c, c), 0) == lax.broadcasted_iota(jnp.int32, (c, c), 1)).astype(F32)
    hi_half = lax.broadcasted_iota(jnp.int32, (c, 2 * c), 1) >= c
    ms = [jnp.concatenate([-l, eye], axis=1) for l in lmats]
    span = 1
    while span < c:
        ms = [_dot_x3(m[:, 0:c], m) + jnp.where(hi_half, m, 0.0) for m in ms]
        span *= 2
    return [m[:, c:2 * c] for m in ms]


def _gdn_prompt_kernel(q_ref, k_ref, v_ref, z_ref, sm_ref, bias_ref, alog_ref, cw_ref, nw_ref,
                       y_ref, s_ref, xb_s, *, chunk):
    tt = q_ref.shape[0]
    c = chunk

    @pl.when(pl.program_id(1) == 0)
    def _():
        s_ref[...] = jnp.zeros_like(s_ref)
        xb_s[0:SUBLANE, :] = jnp.zeros((SUBLANE, 3 * D_C), F32)

    xb_s[SUBLANE:SUBLANE + tt, 0:D_C] = q_ref[...]
    xb_s[SUBLANE:SUBLANE + tt, D_C:2 * D_C] = k_ref[...]
    xb_s[SUBLANE:SUBLANE + tt, 2 * D_C:3 * D_C] = v_ref[...]
    conv = cw_ref[GDN_CONV - 1:GDN_CONV, :] * xb_s[SUBLANE:SUBLANE + tt, :]
    for j in range(GDN_CONV - 1):
        off = SUBLANE - (GDN_CONV - 1) + j
        conv = conv + cw_ref[j:j + 1, :] * xb_s[off:off + tt, :]
    xb_s[0:SUBLANE, :] = xb_s[tt:tt + SUBLANE, :]
    qkv = _silu(conv)

    sm = sm_ref[...]
    beta_all = _sigmoid(sm)
    g_all = -jnp.exp(alog_ref[...]) * _softplus(sm + bias_ref[...])
    g_t = g_all.T
    rowi = lax.broadcasted_iota(jnp.int32, (c, c), 0)
    coli = lax.broadcasted_iota(jnp.int32, (c, c), 1)
    incl = rowi >= coli
    strict = rowi > coli

    nchunk = tt // c
    probs = [(ci, h) for ci in range(nchunk) for h in range(H_C)]
    rows = lambda ci: slice(c * ci, c * (ci + 1))
    l2 = lambda x: x * lax.rsqrt(jnp.sum(x * x, axis=-1, keepdims=True) + EPS)
    q = [l2(qkv[rows(ci), DK_C * h:DK_C * (h + 1)]) * (DK_C ** -0.5) for ci, h in probs]
    k = [l2(qkv[rows(ci), D_C + DK_C * h:D_C + DK_C * (h + 1)]) for ci, h in probs]
    v = [qkv[rows(ci), 2 * D_C + DV_C * h:2 * D_C + DV_C * (h + 1)] for ci, h in probs]
    beta_c = [beta_all[rows(ci), SM_BETA + h:SM_BETA + h + 1] for ci, h in probs]
    gc_c = [jnp.sum(jnp.where(incl, g_t[SM_A + h:SM_A + h + 1, rows(ci)], 0.0), axis=1, keepdims=True)
            for ci, h in probs]
    gc_r = [jnp.sum(jnp.where(rowi <= coli, g_all[rows(ci), SM_A + h:SM_A + h + 1], 0.0), axis=0, keepdims=True)
            for ci, h in probs]
    gam = [jnp.exp(jnp.where(incl, a - b, -jnp.inf)) for a, b in zip(gc_c, gc_r)]
    qb = [x.astype(BF16) for x in q]
    kb = [x.astype(BF16) for x in k]
    kk = [lax.dot_general(x, x, NT, preferred_element_type=F32) for x in kb]
    qk = [(lax.dot_general(a, b, NT, preferred_element_type=F32) * g).astype(BF16) for a, b, g in zip(qb, kb, gam)]
    egc = [jnp.exp(x) for x in gc_c]
    tinv = _unit_lower_inverses([jnp.where(strict, b * g * x, 0.0) for b, g, x in zip(beta_c, gam, kk)])
    tr = [_dot_x3(t, jnp.concatenate([b * vv, (b * e) * kx], axis=1))
          for t, b, vv, e, kx in zip(tinv, beta_c, v, egc, k)]
    u0 = [x[:, 0:DV_C] for x in tr]
    wb = [x[:, DV_C:].astype(BF16) for x in tr]
    g_last = [x[c - 1:c, :] for x in gc_c]
    kw = [(jnp.exp(gl - gc) * kx).astype(BF16) for gl, gc, kx in zip(g_last, gc_c, k)]
    e_last = [jnp.exp(x) for x in g_last]

    st = [s_ref[0, h] for h in range(H_C)]
    for ci in range(nchunk):
        ids = [ci * H_C + h for h in range(H_C)]
        stb = [x.astype(BF16) for x in st]
        ub = [(u0[i] - _dot(wb[i], stb[h])).astype(BF16) for h, i in enumerate(ids)]
        qs = [_dot(qb[i], stb[h]) for h, i in enumerate(ids)]
        st = [e_last[i] * st[h] + lax.dot_general(kw[i], ub[h], TN, preferred_element_type=F32)
              for h, i in enumerate(ids)]
        o = [egc[i] * qs[h] + _dot(qk[i], ub[h]) for h, i in enumerate(ids)]
        for h in range(H_C):
            zz = z_ref[rows(ci), DV_C * h:DV_C * (h + 1)]
            y_ref[rows(ci), DV_C * h:DV_C * (h + 1)] = (_rms(o[h], nw_ref[...]) * _silu(zz)).astype(BF16)
    for h in range(H_C):
        s_ref[0, h] = st[h]


def _gdn_prompt(proj, bsz, seq, gp):
    c = _pick(seq, (GDN_CHUNK,))
    tt = _pick(seq, (GDN_TILE,))
    nt = seq // tt
    rows = lambda col: (lambda b, t: (b * nt + t, col))
    const2 = lambda b, t: (0, 0)
    return pl.pallas_call(
        functools.partial(_gdn_prompt_kernel, chunk=c),
        grid=(bsz, nt),
        in_specs=[pl.BlockSpec((tt, D_C), rows(COL_QKV // 4)),
                  pl.BlockSpec((tt, D_C), rows(COL_QKV // 4 + 1)),
                  pl.BlockSpec((tt, D_C), rows(COL_QKV // 4 + 2)),
                  pl.BlockSpec((tt, D_C), rows(COL_Z // 4)),
                  pl.BlockSpec((tt, LANE), rows(COL_SMALL)),
                  pl.BlockSpec((1, LANE), const2),
                  pl.BlockSpec((1, LANE), const2),
                  pl.BlockSpec((GDN_CONV, 3 * D_C), const2),
                  pl.BlockSpec((1, DV_C), const2)],
        out_specs=[pl.BlockSpec((tt, D_C), lambda b, t: (b * nt + t, 0)),
                   pl.BlockSpec((1, H_C, DK_C, DV_C), lambda b, t: (b, 0, 0, 0))],
        out_shape=[jax.ShapeDtypeStruct((bsz * seq, D_C), BF16),
                   jax.ShapeDtypeStruct((bsz, H_C, DK_C, DV_C), F32)],
        scratch_shapes=[pltpu.VMEM((SUBLANE + tt, 3 * D_C), F32)],
        compiler_params=_cparams(("arbitrary", "arbitrary")),
        name="gdn_prompt",
    )(proj, proj, proj, proj, proj, gp["bias"], gp["alog"], gp["conv_w"], gp["norm"])


def _to_col(row, eye):
    return jnp.sum(jnp.where(eye, row, 0.0), axis=1, keepdims=True)


N_SAMPLE_INPUTS = 23


def _sample_mixers_kernel(*refs):
    (proj_ref, sre_ref, sim_ref, c_ref, n_ref, m_ref, gs_ref, gbuf_ref,
     are_ref, aim_ref, ldt_ref, bre_ref, bim_ref, cre_ref, cim_ref, d_ref, wglu_ref,
     mbias_ref, mnorm_ref, gbias_ref, alog_ref, gcw_ref, gnorm_ref) = refs[:N_SAMPLE_INPUTS]
    (y_ref, sre_o, sim_o, c_o, n_o, m_o, gs_o, gbuf_o,
     qkv_s, qn_s, kn_s, beta_s, g_s, li_s, lf_s) = refs[-15:]
    bb = proj_ref.shape[0]

    u = proj_ref[:, COL_U * LANE:COL_U * LANE + D_A]
    ub = u.astype(BF16)
    ys = []
    for j in range(S5_NCHUNK):
        sl = slice(S5_CHUNK * j, S5_CHUNK * (j + 1))
        abr, abi, fre, fim = _s5_disc(are_ref[:, sl], aim_ref[:, sl], ldt_ref[:, sl])
        uj = ub[:, LANE * j:LANE * (j + 1)]
        bur = _dot(uj, bre_ref[j])
        bui = _dot(uj, bim_ref[j])
        s0r = sre_ref[:, sl]
        s0i = sim_ref[:, sl]
        xr = fre * bur - fim * bui + abr * s0r - abi * s0i
        xi = fre * bui + fim * bur + abr * s0i + abi * s0r
        sre_o[:, sl] = xr
        sim_o[:, sl] = xi
        ys.append(_dot(xr.astype(BF16), cre_ref[j]) - _dot(xi.astype(BF16), cim_ref[j]))
    y_a = jnp.concatenate(ys, axis=1)
    y_ref[:, Y_A:Y_A + D_A] = _s5_glu(y_a, u, d_ref, wglu_ref)

    sm = proj_ref[:, COL_SMALL * LANE:(COL_SMALL + 1) * LANE]
    smb = sm + mbias_ref[...]
    li_s[...] = smb
    lf_s[...] = _log_sigmoid(smb)
    beta_s[...] = _sigmoid(sm)
    g_s[...] = -jnp.exp(alog_ref[...]) * _softplus(sm + gbias_ref[...])

    xnew = proj_ref[:, COL_QKV * LANE:COL_QKV * LANE + 3 * D_C]
    conv = gcw_ref[GDN_CONV - 1:GDN_CONV, :] * xnew
    for j in range(GDN_CONV - 1):
        conv = conv + gcw_ref[j:j + 1, :] * gbuf_ref[j]
        if j > 0:
            gbuf_o[j - 1] = gbuf_ref[j]
    gbuf_o[GDN_CONV - 2] = xnew
    qkv = _silu(conv)
    qkv_s[...] = qkv
    for h in range(H_C):
        q = qkv[:, DK_C * h:DK_C * (h + 1)]
        k = qkv[:, D_C + DK_C * h:D_C + DK_C * (h + 1)]
        qn_s[:, DK_C * h:DK_C * (h + 1)] = q * lax.rsqrt(jnp.sum(q * q, axis=-1, keepdims=True) + EPS) * (DK_C ** -0.5)
        kn_s[:, DK_C * h:DK_C * (h + 1)] = k * lax.rsqrt(jnp.sum(k * k, axis=-1, keepdims=True) + EPS)

    eye = (lax.broadcasted_iota(jnp.int32, (LANE, LANE), 0)
           == lax.broadcasted_iota(jnp.int32, (LANE, LANE), 1))
    lane = lax.broadcasted_iota(jnp.int32, (1, H_B), 1)

    probs = [(b, h) for b in range(bb) for h in range(H_B)]
    rsl = lambda b: slice(b, b + 1)

    q = [proj_ref[rsl(b), COL_QB * LANE + DQK_B * h:COL_QB * LANE + DQK_B * (h + 1)] * (DQK_B ** -0.5)
         for b, h in probs]
    k = [proj_ref[rsl(b), COL_KB * LANE + DQK_B * h:COL_KB * LANE + DQK_B * (h + 1)] for b, h in probs]
    v = [proj_ref[rsl(b), COL_VB * LANE + DV_B * h:COL_VB * LANE + DV_B * (h + 1)] for b, h in probs]
    li = [li_s[rsl(b), SM_I + h:SM_I + h + 1] for b, h in probs]
    inter = [lf_s[rsl(b), SM_F + h:SM_F + h + 1] + m_ref[rsl(b), h:h + 1] for b, h in probs]
    m_t = [jnp.maximum(a, c) for a, c in zip(inter, li)]
    w_intra = [jnp.exp(a - c) for a, c in zip(li, m_t)]
    w_inter = [jnp.exp(a - c) for a, c in zip(inter, m_t)]
    qcol = [_to_col(x, eye) for x in q]
    kcol = [_to_col(x, eye) for x in k]
    s = [jnp.sum(a * c, axis=1, keepdims=True) * w for a, c, w in zip(q, k, w_intra)]
    cst = [c_ref[b, h] for b, h in probs]
    nrow = [n_ref[b, h:h + 1, :] for b, h in probs]
    qc = [jnp.sum(a * c, axis=0, keepdims=True) for a, c in zip(qcol, cst)]
    for i, (b, h) in enumerate(probs):
        c_o[b, h] = w_inter[i] * cst[i] + (w_intra[i] * kcol[i]) * v[i]
        n_o[b, h:h + 1, :] = w_inter[i] * nrow[i] + w_intra[i] * k[i]
    num = [s[i] * v[i] + w_inter[i] * qc[i] for i in range(len(probs))]
    nq = [s[i] + w_inter[i] * jnp.sum(q[i] * nrow[i], axis=1, keepdims=True) for i in range(len(probs))]
    hh = [num[i] / jnp.maximum(jnp.abs(nq[i]), jnp.exp(-m_t[i])) for i in range(len(probs))]
    for i, (b, h) in enumerate(probs):
        og = proj_ref[rsl(b), COL_OB * LANE + DV_B * h:COL_OB * LANE + DV_B * (h + 1)]
        y_ref[rsl(b), Y_B + DV_B * h:Y_B + DV_B * (h + 1)] = _rms(hh[i], mnorm_ref[h:h + 1, :]) * _sigmoid(og)
    for b in range(bb):
        m_out = m_ref[rsl(b), :]
        for h in range(H_B):
            m_out = jnp.where(lane == h, m_t[b * H_B + h], m_out)
        m_o[rsl(b), :] = m_out

    q = [qn_s[rsl(b), DK_C * h:DK_C * (h + 1)] for b, h in probs]
    k = [kn_s[rsl(b), DK_C * h:DK_C * (h + 1)] for b, h in probs]
    v = [qkv_s[rsl(b), 2 * D_C + DV_C * h:2 * D_C + DV_C * (h + 1)] for b, h in probs]
    beta = [beta_s[rsl(b), SM_BETA + h:SM_BETA + h + 1] for b, h in probs]
    eg = [jnp.exp(g_s[rsl(b), SM_A + h:SM_A + h + 1]) for b, h in probs]
    qcol = [_to_col(x, eye) for x in q]
    kcol = [_to_col(x, eye) for x in k]
    st = [gs_ref[b, h] for b, h in probs]
    ks = [jnp.sum(a * c, axis=0, keepdims=True) for a, c in zip(kcol, st)]
    qs = [jnp.sum(a * c, axis=0, keepdims=True) for a, c in zip(qcol, st)]
    un = [beta[i] * (v[i] - eg[i] * ks[i]) for i in range(len(probs))]
    for i, (b, h) in enumerate(probs):
        gs_o[b, h] = eg[i] * st[i] + kcol[i] * un[i]
    o = [eg[i] * qs[i] + jnp.sum(q[i] * k[i], axis=1, keepdims=True) * un[i] for i in range(len(probs))]
    for i, (b, h) in enumerate(probs):
        zz = proj_ref[rsl(b), COL_Z * LANE + DV_C * h:COL_Z * LANE + DV_C * (h + 1)]
        y_ref[rsl(b), Y_C + DV_C * h:Y_C + DV_C * (h + 1)] = _rms(o[i], gnorm_ref[...]) * _silu(zz)


def _sample_mixers(proj, row0, nrows, st, sp, mp, gp, layer, prev):
    bb = SAMPLE_BLOCK
    blk0 = row0 // bb
    depth = st["c"].shape[0]
    const2 = lambda i: (0, 0)
    const3 = lambda i: (0, 0, 0)
    row_spec = pl.BlockSpec((1, S5_LANES), const2)
    c_spec = _layered(layer, (bb, H_B, DQK_B, DV_B), lambda i: (i, 0, 0, 0))
    gs_spec = _layered(layer, (bb, H_C, DK_C, DV_C), lambda i: (i, 0, 0, 0))
    in_specs = [
        pl.BlockSpec((bb, N_PROJ), lambda i: (blk0 + i, 0)),
        pl.BlockSpec((bb, S5_LANES), lambda i: (i, 0)),
        pl.BlockSpec((bb, S5_LANES), lambda i: (i, 0)),
        c_spec,
        pl.BlockSpec((bb, H_B, DQK_B), lambda i: (i, 0, 0)),
        pl.BlockSpec((bb, H_B), lambda i: (i, 0)),
        gs_spec,
        pl.BlockSpec((GDN_CONV - 1, bb, 3 * D_C), lambda i: (0, i, 0)),
        row_spec, row_spec, row_spec,
        pl.BlockSpec((S5_NCHUNK, LANE, S5_CHUNK), const3),
        pl.BlockSpec((S5_NCHUNK, LANE, S5_CHUNK), const3),
        pl.BlockSpec((S5_NCHUNK, S5_CHUNK, LANE), const3),
        pl.BlockSpec((S5_NCHUNK, S5_CHUNK, LANE), const3),
        pl.BlockSpec((1, D_A), const2),
        pl.BlockSpec((D_A, D_A), const2),
        pl.BlockSpec((1, LANE), const2),
        pl.BlockSpec((H_B, DV_B), const2),
        pl.BlockSpec((1, LANE), const2),
        pl.BlockSpec((1, LANE), const2),
        pl.BlockSpec((GDN_CONV, 3 * D_C), const2),
        pl.BlockSpec((1, DV_C), const2),
    ]
    out_specs = [
        pl.BlockSpec((bb, D_MODEL), lambda i: (i, 0)),
        pl.BlockSpec((bb, S5_LANES), lambda i: (i, 0)),
        pl.BlockSpec((bb, S5_LANES), lambda i: (i, 0)),
        c_spec,
        pl.BlockSpec((bb, H_B, DQK_B), lambda i: (i, 0, 0)),
        pl.BlockSpec((bb, H_B), lambda i: (i, 0)),
        gs_spec,
        pl.BlockSpec((GDN_CONV - 1, bb, 3 * D_C), lambda i: (0, i, 0)),
    ]
    out_shape = [
        jax.ShapeDtypeStruct((nrows, D_MODEL), F32),
        jax.ShapeDtypeStruct((nrows, S5_LANES), F32),
        jax.ShapeDtypeStruct((nrows, S5_LANES), F32),
        jax.ShapeDtypeStruct((depth, nrows, H_B, DQK_B, DV_B), F32),
        jax.ShapeDtypeStruct((nrows, H_B, DQK_B), F32),
        jax.ShapeDtypeStruct((nrows, H_B), F32),
        jax.ShapeDtypeStruct((depth, nrows, H_C, DK_C, DV_C), F32),
        jax.ShapeDtypeStruct((GDN_CONV - 1, nrows, 3 * D_C), F32),
    ]
    scratch = [pltpu.VMEM((bb, 3 * D_C), F32), pltpu.VMEM((bb, D_C), F32), pltpu.VMEM((bb, D_C), F32),
               pltpu.VMEM((bb, LANE), F32), pltpu.VMEM((bb, LANE), F32),
               pltpu.VMEM((bb, LANE), F32), pltpu.VMEM((bb, LANE), F32)]
    args = [proj, st["sre"], st["sim"], st["c"], st["n"], st["m"], st["gs"], st["gbuf"],
            sp["are"], sp["aim"], sp["ldt"], sp["bre"], sp["bim"], sp["cre"], sp["cim"], sp["d"], sp["wglu"],
            mp["bias"], mp["norm"], gp["bias"], gp["alog"], gp["conv_w"], gp["norm"]]
    assert len(args) == N_SAMPLE_INPUTS
    aliases = {}
    if prev is not None:
        in_specs += [pl.BlockSpec(memory_space=pl.ANY)] * 2
        args += list(prev)
        aliases = {N_SAMPLE_INPUTS: 3, N_SAMPLE_INPUTS + 1: 6}
    return pl.pallas_call(
        _sample_mixers_kernel,
        grid=(nrows // bb,),
        in_specs=in_specs, out_specs=out_specs, out_shape=out_shape, scratch_shapes=scratch,
        input_output_aliases=aliases,
        compiler_params=_cparams(("arbitrary",)),
        name="sample_mixers",
    )(*args)


def _merge_kernel(h_ref, ya_ref, yb_ref, yc_ref, wg0_ref, wg1_ref, wg2_ref, wa_ref, wb_ref, wc_ref, o_ref):
    h = h_ref[...]
    acc = _sigmoid(_dot(h, wg0_ref[...])) * _dot(ya_ref[...].astype(BF16), wa_ref[...])
    acc = acc + _sigmoid(_dot(h, wg1_ref[...])) * _dot(yb_ref[...].astype(BF16), wb_ref[...])
    acc = acc + _sigmoid(_dot(h, wg2_ref[...])) * _dot(yc_ref[...].astype(BF16), wc_ref[...])
    o_ref[...] = acc.astype(BF16)


def _merge(h, ya, yb, yc, wg, wa, wb, wc, layer):
    m, d = h.shape
    tm = _pick(m, (1024, 512, 256, 128))
    tn = 512
    nb = d // tn
    lhs = lambda w, cb: pl.BlockSpec((tm, w), lambda i, j: (i, cb))
    gate = lambda g: _layered(layer, (d, tn), lambda i, j: (0, g * nb + j))
    rhs = lambda w: _layered(layer, (w, tn), lambda i, j: (0, j))
    return pl.pallas_call(
        _merge_kernel,
        grid=(m // tm, nb),
        in_specs=[lhs(d, 0), lhs(D_A, ya[1]), lhs(D_B, yb[1]), lhs(D_C, yc[1]), gate(0), gate(1), gate(2),
                  rhs(D_A), rhs(D_B), rhs(D_C)],
        out_specs=pl.BlockSpec((tm, tn), lambda i, j: (i, j)),
        out_shape=jax.ShapeDtypeStruct((m, d), BF16),
        compiler_params=_cparams(("arbitrary", "arbitrary")),
        name="merge",
    )(h, ya[0], yb[0], yc[0], wg, wg, wg, wa, wb, wc)


def _out_proj_kernel(a_ref, x_ref, w_ref, nw_ref, o_ref):
    o_ref[...] = x_ref[...] + _rms(_dot(a_ref[...], w_ref[...]), nw_ref[...])


def _out_proj(a, x, w, nw, layer):
    m, d = x.shape
    tm = _pick(m, (512, 256, 128))
    return pl.pallas_call(
        _out_proj_kernel,
        grid=(m // tm,),
        in_specs=[pl.BlockSpec((tm, d), lambda i: (i, 0)),
                  pl.BlockSpec((tm, d), lambda i: (i, 0)),
                  _layered(layer, (d, d), lambda i: (0, 0)),
                  _layered(layer, (1, d), lambda i: (0, 0))],
        out_specs=pl.BlockSpec((tm, d), lambda i: (i, 0)),
        out_shape=jax.ShapeDtypeStruct((m, d), F32),
        compiler_params=_cparams(("arbitrary",)),
        name="out_proj",
    )(a, x, w, nw)


def _ffn_kernel(*refs, sample, tiles_per_seq):
    if sample:
        (x_ref, nw_ref, wg_ref, wu_ref, cw_ref, wd_ref, pnw_ref, b0_ref, b1_ref,
         o_ref, g_ref, h2_s, acc_s) = refs
    else:
        (x_ref, nw_ref, wg_ref, wu_ref, cw_ref, wd_ref, pnw_ref,
         o_ref, g_ref, h2_s, acc_s, gb_s, carry_s) = refs
    i = pl.program_id(0)
    j = pl.program_id(1)
    tm = x_ref.shape[0]

    @pl.when(j == 0)
    def _():
        h2_s[...] = _rms(x_ref[...], nw_ref[...]).astype(BF16)
        acc_s[...] = jnp.zeros_like(acc_s)

    h2 = h2_s[...]
    g = _dot(h2, wg_ref[...])
    up = _dot(h2, wu_ref[...])
    if sample:
        a = cw_ref[0:1, :] * b0_ref[...] + cw_ref[1:2, :] * b1_ref[...] + cw_ref[2:3, :] * g
        g_ref[...] = g
    else:
        prev = jnp.where(i % tiles_per_seq == 0, 0.0, carry_s[j])
        gb_s[0:SUBLANE, :] = prev
        gb_s[SUBLANE:SUBLANE + tm, :] = g
        a = (cw_ref[0:1, :] * gb_s[SUBLANE - 2:SUBLANE - 2 + tm, :]
             + cw_ref[1:2, :] * gb_s[SUBLANE - 1:SUBLANE - 1 + tm, :]
             + cw_ref[2:3, :] * g)
        tail = g[tm - SUBLANE:tm, :]
        carry_s[j] = tail
        g_ref[0] = tail
    act = (_gelu(a) * up).astype(BF16)
    acc_s[...] += _dot(act, wd_ref[...])

    @pl.when(j == pl.num_programs(1) - 1)
    def _():
        o_ref[...] = x_ref[...] + _rms(acc_s[...], pnw_ref[...])


def _ffn(x, row0, nrows, seq, nw, wg, wu, cw, wd, pnw, layer, bufs=None):
    d = x.shape[1]
    f = wg.shape[-1]
    sample = bufs is not None
    tm = nrows if sample else _pick(seq, (512, 256, 128, 64, 32, 16, 8))
    tn = _pick(f, (512, 256, 128))
    mt, ft = nrows // tm, f // tn
    blk0 = row0 // tm
    in_specs = [pl.BlockSpec((tm, d), lambda i, j: (blk0 + i, 0)),
                _layered(layer, (1, d), lambda i, j: (0, 0)),
                _layered(layer, (d, tn), lambda i, j: (0, j)),
                _layered(layer, (d, tn), lambda i, j: (0, j)),
                _layered(layer, (FFN_CONV, tn), lambda i, j: (0, j)),
                _layered(layer, (tn, d), lambda i, j: (j, 0)),
                _layered(layer, (1, d), lambda i, j: (0, 0))]
    args = [x, nw, wg, wu, cw, wd, pnw]
    scratch = [pltpu.VMEM((tm, d), BF16), pltpu.VMEM((tm, d), F32)]
    if sample:
        in_specs += [pl.BlockSpec((tm, tn), lambda i, j: (i, j))] * 2
        args += list(bufs)
        g_spec = pl.BlockSpec((tm, tn), lambda i, j: (i, j))
        g_shape = jax.ShapeDtypeStruct((nrows, f), F32)
    else:
        scratch += [pltpu.VMEM((SUBLANE + tm, tn), F32), pltpu.VMEM((ft, SUBLANE, tn), F32)]
        g_spec = pl.BlockSpec((1, SUBLANE, tn), lambda i, j: (i, 0, j))
        g_shape = jax.ShapeDtypeStruct((mt, SUBLANE, f), F32)
    return pl.pallas_call(
        functools.partial(_ffn_kernel, sample=sample, tiles_per_seq=max(seq // tm, 1)),
        grid=(mt, ft),
        in_specs=in_specs,
        out_specs=[pl.BlockSpec((tm, d), lambda i, j: (i, 0)), g_spec],
        out_shape=[jax.ShapeDtypeStruct((nrows, d), F32), g_shape],
        scratch_shapes=scratch,
        compiler_params=_cparams(("arbitrary", "arbitrary")),
        name="ffn_sample" if sample else "ffn_prompt",
    )(*args)


def _ple_kernel(x_ref, p_ref, wg_ref, wp_ref, o_ref):
    x = x_ref[...]
    gate = _sigmoid(_dot(x.astype(BF16), wg_ref[...]))
    o_ref[...] = x + gate * _dot(p_ref[...].astype(BF16), wp_ref[...])


def _ple(x, p, wg, wp, layer):
    m, d = x.shape
    pd = p.shape[-1]
    tm = _pick(m, (512, 256, 128))
    return pl.pallas_call(
        _ple_kernel,
        grid=(m // tm,),
        in_specs=[pl.BlockSpec((tm, d), lambda i: (i, 0)),
                  _layered(layer, (tm, pd), lambda i: (i, 0)),
                  _layered(layer, (d, d), lambda i: (0, 0)),
                  _layered(layer, (pd, d), lambda i: (0, 0))],
        out_specs=pl.BlockSpec((tm, d), lambda i: (i, 0)),
        out_shape=jax.ShapeDtypeStruct((m, d), F32),
        compiler_params=_cparams(("arbitrary",)),
        name="ple",
    )(x, p, wg, wp)


def _permute_w_in(w):
    u, q, k, v = w[..., 0:512], w[..., 512:1024], w[..., 1024:1536], w[..., 1536:2560]
    ig, fg, o = w[..., 2560:2564], w[..., 2564:2568], w[..., 2568:3592]
    qkv, z, beta, a = w[..., 3592:5128], w[..., 5128:5640], w[..., 5640:5644], w[..., 5644:5648]
    pad = jnp.zeros(w.shape[:-1] + (LANE - 16,), w.dtype)
    return jnp.concatenate([u, q, k, v, o, qkv, z, ig, fg, beta, a, pad], axis=-1).astype(BF16)


def _small_row(entries):
    row = jnp.zeros((LANE,), F32)
    for off, val in entries:
        row = row.at[off:off + val.shape[0]].set(val.astype(F32))
    return row.reshape(1, LANE)


def _s5_params(a_re, a_im, log_dt, b_re, b_im, c_re, c_im, d, w_glu):
    gpc = S5_CHUNK // S5_STATE
    eye = jnp.eye(gpc, dtype=F32)

    def bmat(b):
        b4 = b.reshape(S5_NCHUNK, gpc, S5_GROUP, S5_STATE)
        return jnp.einsum('jgcp,gh->jgchp', b4, eye).reshape(S5_NCHUNK, gpc * S5_GROUP, S5_CHUNK).astype(BF16)

    def cmat(c):
        c4 = c.reshape(S5_NCHUNK, gpc, S5_STATE, S5_GROUP)
        return jnp.einsum('jgpc,gh->jgphc', c4, eye).reshape(S5_NCHUNK, S5_CHUNK, gpc * S5_GROUP).astype(BF16)

    return dict(are=a_re.reshape(1, S5_LANES), aim=a_im.reshape(1, S5_LANES),
                ldt=jnp.broadcast_to(log_dt[:, None], (S5_GROUPS, S5_STATE)).reshape(1, S5_LANES),
                bre=bmat(b_re), bim=bmat(b_im), cre=cmat(c_re), cim=cmat(c_im),
                d=d.reshape(1, D_A), wglu=w_glu.astype(BF16))


def _layer(layer, xp, xs, bsz, seq, dense, lw, state, prev):
    np_rows = bsz * seq
    nsamp = xs.shape[0]
    proj, h_p = _norm_proj(xp, dense['norm_mix_pre'], dense['w_in'], layer)
    proj_s, h_s = _norm_proj(xs, dense['norm_mix_pre'], dense['w_in'], layer)

    sp = _s5_params(lw['s5_a_re'], lw['s5_a_im'], lw['s5_log_dt'], lw['s5_b_re'], lw['s5_b_im'],
                    lw['s5_c_re'], lw['s5_c_im'], lw['s5_d'], lw['s5_w_glu'])
    mp = dict(bias=_small_row([(SM_I, lw['mlstm_b_i']), (SM_F, lw['mlstm_b_f'])]), norm=lw['mlstm_norm'])
    gp = dict(bias=_small_row([(SM_A, lw['gdn_dt_bias'])]), alog=_small_row([(SM_A, lw['gdn_a_log'])]),
              conv_w=lw['gdn_conv_w'], norm=lw['gdn_norm'].reshape(1, DV_C))

    ya_p, sre_p, sim_p = _s5_prompt(proj, bsz, seq, sp)
    yb_p, c_p, n_p, m_p = _mlstm_prompt(proj, bsz, seq, mp)
    yc_p, gs_p = _gdn_prompt(proj, bsz, seq, gp)

    ssm_re, ssm_im, m_c, m_n, m_m, g_s, g_conv, f_conv = state
    st = dict(sre=ssm_re.reshape(nsamp, S5_LANES), sim=ssm_im.reshape(nsamp, S5_LANES),
              c=m_c, n=m_n, m=m_m, gs=g_s, gbuf=jnp.swapaxes(g_conv, 0, 1))
    y_s, sre_s, sim_s, c_s, n_s, m_s, gs_s, gbuf_s = _sample_mixers(proj_s, 0, nsamp, st, sp, mp, gp, layer, prev)

    merge_w = (dense['w_gate'], dense['w_branch_a'], dense['w_branch_b'], dense['w_branch_c'], layer)
    merged_p = _merge(h_p, (ya_p, 0), (yb_p, 0), (yc_p, 0), *merge_w)
    merged_s = _merge(h_s, (y_s, Y_A // D_A), (y_s, Y_B // D_B), (y_s, Y_C // D_C), *merge_w)
    x1_p = _out_proj(merged_p, xp, dense['w_out'], dense['norm_mix_post'], layer)
    x1_s = _out_proj(merged_s, xs, dense['w_out'], dense['norm_mix_post'], layer)

    ffn_w = (dense['norm_ffn_pre'], dense['ffn_w_gate'], dense['ffn_w_up'], dense['ffn_conv_w'],
             dense['ffn_w_down'], dense['norm_ffn_post'], layer)
    x2_p, gtail = _ffn(x1_p, 0, np_rows, seq, *ffn_w)
    x2_s, g_new = _ffn(x1_s, 0, nsamp, 1, *ffn_w, bufs=(f_conv[:, 0], f_conv[:, 1]))
    x3_p = _ple(x2_p, dense['p_prompt'], dense['ple_w_gate'], dense['ple_w_proj'], layer)
    x3_s = _ple(x2_s, dense['p_sample'], dense['ple_w_gate'], dense['ple_w_proj'], layer)

    tiles_per_seq = gtail.shape[0] // bsz
    qkv_tail = proj.reshape(bsz, seq, N_PROJ)[:, seq - (GDN_CONV - 1):, COL_QKV * LANE:COL_QKV * LANE + 3 * D_C]
    st_p = (sre_p.reshape(bsz, S5_GROUPS, S5_STATE), sim_p.reshape(bsz, S5_GROUPS, S5_STATE),
            c_p, n_p, m_p[:, 0, :H_B], gs_p, qkv_tail,
            gtail.reshape(bsz, tiles_per_seq, SUBLANE, D_FF)[:, -1, SUBLANE - (FFN_CONV - 1):, :])
    st_s = (sre_s.reshape(nsamp, S5_GROUPS, S5_STATE), sim_s.reshape(nsamp, S5_GROUPS, S5_STATE),
            None, n_s, m_s, None, jnp.swapaxes(gbuf_s, 0, 1),
            jnp.stack([f_conv[:, 1], g_new], axis=1))
    return x3_p, x3_s, st_p, st_s, (c_s, gs_s)


def kernel(x_prompt, x_sample, p_prompt, p_sample, state_ssm_re, state_ssm_im, state_mlstm_c, state_mlstm_n, state_mlstm_m, state_gdn_s, state_gdn_conv, state_ffn_conv, norm_mix_pre, norm_mix_post, norm_ffn_pre, norm_ffn_post, w_in, s5_a_re, s5_a_im, s5_log_dt, s5_b_re, s5_b_im, s5_c_re, s5_c_im, s5_d, s5_w_glu, mlstm_b_i, mlstm_b_f, mlstm_norm, gdn_conv_w, gdn_a_log, gdn_dt_bias, gdn_norm, w_branch_a, w_branch_b, w_branch_c, w_gate, w_out, ffn_w_gate, ffn_w_up, ffn_conv_w, ffn_w_down, ple_w_proj, ple_w_gate):
    bsz, seq, d = x_prompt.shape
    nsamp = x_sample.shape[0]
    depth = w_in.shape[0]
    small = dict(
        s5_a_re=s5_a_re, s5_a_im=s5_a_im, s5_log_dt=s5_log_dt, s5_b_re=s5_b_re, s5_b_im=s5_b_im,
        s5_c_re=s5_c_re, s5_c_im=s5_c_im, s5_d=s5_d, s5_w_glu=s5_w_glu, mlstm_b_i=mlstm_b_i,
        mlstm_b_f=mlstm_b_f, mlstm_norm=mlstm_norm, gdn_conv_w=gdn_conv_w, gdn_a_log=gdn_a_log,
        gdn_dt_bias=gdn_dt_bias, gdn_norm=gdn_norm)
    row = lambda w: w.reshape(depth, 1, -1)
    dense = dict(
        norm_mix_pre=row(norm_mix_pre), norm_mix_post=row(norm_mix_post), norm_ffn_pre=row(norm_ffn_pre),
        norm_ffn_post=row(norm_ffn_post), w_in=_permute_w_in(w_in), w_gate=w_gate.astype(BF16),
        w_branch_a=w_branch_a.astype(BF16), w_branch_b=w_branch_b.astype(BF16),
        w_branch_c=w_branch_c.astype(BF16), w_out=w_out.astype(BF16), ffn_w_gate=ffn_w_gate.astype(BF16),
        ffn_w_up=ffn_w_up.astype(BF16), ffn_conv_w=ffn_conv_w, ffn_w_down=ffn_w_down.astype(BF16),
        ple_w_gate=ple_w_gate.astype(BF16), ple_w_proj=ple_w_proj.astype(BF16),
        p_prompt=p_prompt.reshape(depth, bsz * seq, -1), p_sample=p_sample.reshape(depth, nsamp, -1))
    xp = x_prompt.reshape(bsz * seq, d)
    xs = x_sample.reshape(nsamp, d)
    sp_all, ss_all = [], []
    big = None
    for i in range(depth):
        lw = {k: v[i] for k, v in small.items()}
        state = (state_ssm_re[i], state_ssm_im[i], state_mlstm_c, state_mlstm_n[i], state_mlstm_m[i],
                 state_gdn_s, state_gdn_conv[i], state_ffn_conv[i])
        xp, xs, st_p, st_s, big = _layer(i, xp, xs, bsz, seq, dense, lw, state, big)
        sp_all.append(st_p)
        ss_all.append(st_s)
    stack = lambda sts, j: jnp.stack([s[j] for s in sts], axis=0)
    sample_states = [big[0] if j == 2 else big[1] if j == 5 else stack(ss_all, j) for j in range(8)]
    return ((xp.reshape(bsz, seq, d), xs.reshape(nsamp, 1, d))
            + tuple(stack(sp_all, j) for j in range(8))
            + tuple(sample_states))
```

```python
import functools

import jax
import jax.numpy as jnp
from jax import lax
from jax.experimental import pallas as pl
from jax.experimental.pallas import tpu as pltpu

F32 = jnp.float32
BF16 = jnp.bfloat16

D_MODEL = 2048
DEPTH = 2
D_A = 512
S5_GROUP = 16
S5_GROUPS = 32
S5_STATE = 64
S5_LANES = S5_GROUPS * S5_STATE
S5_CHUNK = 512
S5_NCHUNK = S5_LANES // S5_CHUNK
D_B = 1024
H_B = 4
DV_B = 256
DQK_B = 128
D_C = 512
H_C = 4
DK_C = 128
DV_C = 128
GDN_CONV = 4
D_FF = 8192
FFN_CONV = 3
PLE_DIM = 256
EPS = 1e-6

LANE = 128
SUBLANE = 8
VMEM_LIMIT = 56 * 1024 * 1024

COL_U, COL_QB, COL_KB, COL_VB, COL_OB, COL_QKV, COL_Z, COL_SMALL = 0, 4, 8, 12, 20, 28, 40, 44
N_HEAD = COL_OB * LANE
N_MID = (COL_SMALL - COL_OB) * LANE
PROJ_TILE = 512
N_PROJ = N_HEAD + N_MID + PROJ_TILE
SM_I, SM_F, SM_BETA, SM_A = 0, 4, 8, 12

MLSTM_CHUNK = 128
MLSTM_TILE = 256
GDN_CHUNK = 64
GDN_TILE = 256
S5_TILE = 256
SAMPLE_BLOCK = 8
Y_B, Y_A, Y_C = 0, D_B, D_B + D_A

NT = (((1,), (1,)), ((), ()))
TN = (((0,), (0,)), ((), ()))


def _cparams(sem):
    return pltpu.CompilerParams(dimension_semantics=sem, vmem_limit_bytes=VMEM_LIMIT)


def _dot(a, b):
    return jnp.dot(a, b, preferred_element_type=F32)


def _dot_hi(a, b):
    return jnp.dot(a, b, preferred_element_type=F32, precision=lax.Precision.HIGHEST)


def _gelu(x):
    return 0.5 * x * (1.0 + jnp.tanh(0.7978845608028654 * (x + 0.044715 * (x * x * x))))


def _sigmoid(x):
    return 1.0 / (1.0 + jnp.exp(-x))


def _silu(x):
    return x * _sigmoid(x)


def _softplus(x):
    return jnp.maximum(x, 0.0) + jnp.log1p(jnp.exp(-jnp.abs(x)))


def _log_sigmoid(x):
    return -_softplus(-x)


def _rms(x, w):
    return x * lax.rsqrt(jnp.mean(x * x, axis=-1, keepdims=True) + EPS) * w


def _layered(layer, shape, imap):
    return pl.BlockSpec((None,) + shape, lambda *g: (layer,) + imap(*g))


def _pick(n, cands):
    for c in cands:
        if n % c == 0:
            return c
    return n


def _norm_proj_kernel(x_ref, nw_ref, wa_ref, wb_ref, ws_ref, proj_ref, h_ref, *, na, nb):
    j = pl.program_id(1)

    @pl.when(j == 0)
    def _():
        h_ref[...] = _rms(x_ref[...], nw_ref[...]).astype(BF16)

    @pl.when(j < na)
    def _():
        proj_ref[...] = _dot(h_ref[...], wa_ref[...].astype(BF16))

    @pl.when(jnp.logical_and(j >= na, j < na + nb))
    def _():
        proj_ref[...] = _dot(h_ref[...], wb_ref[...].astype(BF16))

    @pl.when(j >= na + nb)
    def _():
        proj_ref[...] = jnp.zeros_like(proj_ref)
        proj_ref[:, 0:LANE] = _dot(h_ref[...], ws_ref[...].astype(BF16))


def _norm_proj(x, nw, w_in, w_mid, w_small, layer):
    m, d = x.shape
    tm = _pick(m, (1024, 512, 256, 128))
    tn = PROJ_TILE
    na, nb = N_HEAD // tn, N_MID // tn
    return pl.pallas_call(
        functools.partial(_norm_proj_kernel, na=na, nb=nb),
        grid=(m // tm, na + nb + 1),
        in_specs=[pl.BlockSpec((tm, d), lambda i, j: (i, 0)),
                  _layered(layer, (1, d), lambda i, j: (0, 0)),
                  _layered(layer, (d, tn), lambda i, j: (0, jnp.minimum(j, na - 1))),
                  _layered(layer, (d, tn), lambda i, j: (0, jnp.clip(j - na, 0, nb - 1))),
                  _layered(layer, (d, LANE), lambda i, j: (0, 0))],
        out_specs=[pl.BlockSpec((tm, tn), lambda i, j: (i, j)),
                   pl.BlockSpec((tm, d), lambda i, j: (i, 0))],
        out_shape=[jax.ShapeDtypeStruct((m, N_PROJ), F32), jax.ShapeDtypeStruct((m, d), BF16)],
        compiler_params=_cparams(("arbitrary", "arbitrary")),
        name="norm_proj",
    )(x, nw, w_in, w_mid, w_small)


def _s5_disc(are, aim, ldt):
    dt = jnp.exp(ldt)
    mag = jnp.exp(dt * are)
    abr = mag * jnp.cos(dt * aim)
    abi = mag * jnp.sin(dt * aim)
    den = are * are + aim * aim
    zr = abr - 1.0
    fre = (zr * are + abi * aim) / den
    fim = (abi * are - zr * aim) / den
    return abr, abi, fre, fim


def _s5_glu(y, u, d_ref, wglu_ref):
    z = _gelu(y + d_ref[...] * u)
    return z * _sigmoid(_dot(z.astype(BF16), wglu_ref[...]))


def _s5_prompt_kernel(u_ref, are_ref, aim_ref, ldt_ref, bre_ref, bim_ref, cre_ref, cim_ref,
                      d_ref, wglu_ref, y_ref, sre_ref, sim_ref, xr_s, xi_s, y_s, car_re, car_im,
                      f_s, tab_s):
    @pl.when(pl.program_id(1) == 0)
    def _():
        car_re[...] = jnp.zeros_like(car_re)
        car_im[...] = jnp.zeros_like(car_im)
        row = lax.broadcasted_iota(jnp.int32, (SUBLANE, S5_LANES), 0)
        abr, abi, fre, fim = _s5_disc(are_ref[...], aim_ref[...], ldt_ref[...])
        f_s[0:1, :] = fre
        f_s[1:2, :] = fim
        pr, pi = abr, abi
        for lvl, s in enumerate((1, 2, 4)):
            tab_s[2 * lvl] = jnp.where(row >= s, pr, 0.0)
            tab_s[2 * lvl + 1] = jnp.where(row >= s, pi, 0.0)
            pr, pi = pr * pr - pi * pi, 2.0 * pr * pi
        cwr = jnp.zeros((SUBLANE, S5_LANES), F32)
        cwi = jnp.zeros((SUBLANE, S5_LANES), F32)
        pr, pi = abr, abi
        for r in range(SUBLANE):
            cwr = jnp.where(row == r, pr, cwr)
            cwi = jnp.where(row == r, pi, cwi)
            pr, pi = pr * abr - pi * abi, pr * abi + pi * abr
        tab_s[6] = cwr
        tab_s[7] = cwi

    tt = u_ref.shape[0]
    u = u_ref[...]
    ub = u.astype(BF16)
    for j in range(S5_NCHUNK):
        sl = slice(S5_CHUNK * j, S5_CHUNK * (j + 1))
        fre, fim = f_s[0:1, sl], f_s[1:2, sl]
        uj = ub[:, LANE * j:LANE * (j + 1)]
        bur = _dot(uj, bre_ref[j])
        bui = _dot(uj, bim_ref[j])
        xr = (fre * bur - fim * bui).reshape(tt // SUBLANE, SUBLANE, S5_CHUNK)
        xi = (fre * bui + fim * bur).reshape(tt // SUBLANE, SUBLANE, S5_CHUNK)
        for lvl, s in enumerate((1, 2, 4)):
            mr = tab_s[2 * lvl, :, sl]
            mi = tab_s[2 * lvl + 1, :, sl]
            sr = pltpu.roll(xr, s, axis=1)
            si = pltpu.roll(xi, s, axis=1)
            xr, xi = xr + mr * sr - mi * si, xi + mr * si + mi * sr
        xr_s[...] = xr.reshape(tt, S5_CHUNK)
        xi_s[...] = xi.reshape(tt, S5_CHUNK)
        cwr = tab_s[6, :, sl]
        cwi = tab_s[7, :, sl]

        def body(g, carry, cwr=cwr, cwi=cwi):
            cr, ci = carry
            r0 = pl.multiple_of(g * SUBLANE, SUBLANE)
            gr = xr_s[pl.ds(r0, SUBLANE), :] + cwr * cr - cwi * ci
            gi = xi_s[pl.ds(r0, SUBLANE), :] + cwr * ci + cwi * cr
            xr_s[pl.ds(r0, SUBLANE), :] = gr
            xi_s[pl.ds(r0, SUBLANE), :] = gi
            return gr[SUBLANE - 1:SUBLANE, :], gi[SUBLANE - 1:SUBLANE, :]

        cr, ci = lax.fori_loop(0, tt // SUBLANE, body, (car_re[:, sl], car_im[:, sl]), unroll=4)
        car_re[:, sl] = cr
        car_im[:, sl] = ci
        y_s[:, LANE * j:LANE * (j + 1)] = (_dot(xr_s[...].astype(BF16), cre_ref[j])
                                           - _dot(xi_s[...].astype(BF16), cim_ref[j]))
    y_ref[...] = _s5_glu(y_s[...], u, d_ref, wglu_ref).astype(BF16)
    sre_ref[0] = car_re[...]
    sim_ref[0] = car_im[...]


def _s5_prompt(proj, bsz, seq, sp):
    tt = _pick(seq, (S5_TILE, 128, 64, 32, 16, 8))
    nt = seq // tt
    const2 = lambda b, t: (0, 0)
    const3 = lambda b, t: (0, 0, 0)
    row_spec = pl.BlockSpec((1, S5_LANES), const2)
    return pl.pallas_call(
        _s5_prompt_kernel,
        grid=(bsz, nt),
        in_specs=[pl.BlockSpec((tt, D_A), lambda b, t: (b * nt + t, COL_U)),
                  row_spec, row_spec, row_spec,
                  pl.BlockSpec((S5_NCHUNK, LANE, S5_CHUNK), const3),
                  pl.BlockSpec((S5_NCHUNK, LANE, S5_CHUNK), const3),
                  pl.BlockSpec((S5_NCHUNK, S5_CHUNK, LANE), const3),
                  pl.BlockSpec((S5_NCHUNK, S5_CHUNK, LANE), const3),
                  pl.BlockSpec((1, D_A), const2),
                  pl.BlockSpec((D_A, D_A), const2)],
        out_specs=[pl.BlockSpec((tt, D_A), lambda b, t: (b * nt + t, 0)),
                   pl.BlockSpec((1, 1, S5_LANES), lambda b, t: (b, 0, 0)),
                   pl.BlockSpec((1, 1, S5_LANES), lambda b, t: (b, 0, 0))],
        out_shape=[jax.ShapeDtypeStruct((bsz * seq, D_A), BF16),
                   jax.ShapeDtypeStruct((bsz, 1, S5_LANES), F32),
                   jax.ShapeDtypeStruct((bsz, 1, S5_LANES), F32)],
        scratch_shapes=[pltpu.VMEM((tt, S5_CHUNK), F32), pltpu.VMEM((tt, S5_CHUNK), F32),
                        pltpu.VMEM((tt, D_A), F32),
                        pltpu.VMEM((1, S5_LANES), F32), pltpu.VMEM((1, S5_LANES), F32),
                        pltpu.VMEM((2, S5_LANES), F32), pltpu.VMEM((8, SUBLANE, S5_LANES), F32)],
        compiler_params=_cparams(("arbitrary", "arbitrary")),
        name="s5_prompt",
    )(proj, sp["are"], sp["aim"], sp["ldt"], sp["bre"], sp["bim"], sp["cre"], sp["cim"],
      sp["d"], sp["wglu"])


def _mlstm_prompt_kernel(q_ref, k_ref, v0_ref, v1_ref, o0_ref, o1_ref, sm_ref, bias_ref, nw_ref,
                         y_ref, c_ref, n_ref, m_ref, *, chunk):
    @pl.when(pl.program_id(1) == 0)
    def _():
        c_ref[...] = jnp.zeros_like(c_ref)
        n_ref[...] = jnp.zeros_like(n_ref)
        m_ref[...] = jnp.zeros_like(m_ref)

    tt = q_ref.shape[0]
    c = chunk
    nchunk = tt // c
    smb = sm_ref[...] + bias_ref[...]
    lf_all = _log_sigmoid(smb)
    li_t = smb.T
    lf_t = lf_all.T
    rowi = lax.broadcasted_iota(jnp.int32, (c, c), 0)
    coli = lax.broadcasted_iota(jnp.int32, (c, c), 1)
    causal = rowi >= coli
    lane = lax.broadcasted_iota(jnp.int32, (1, LANE), 1)
    m_row = m_ref[0]
    probs = [(ci, h) for ci in range(nchunk) for h in range(H_B)]
    idx = {p: i for i, p in enumerate(probs)}
    rows = lambda ci: slice(c * ci, c * (ci + 1))
    vo_refs = [(v0_ref, o0_ref) if h < 2 else (v1_ref, o1_ref) for h in range(H_B)]
    q = [q_ref[rows(ci), DQK_B * h:DQK_B * (h + 1)] * (DQK_B ** -0.5) for ci, h in probs]
    k = [k_ref[rows(ci), DQK_B * h:DQK_B * (h + 1)] for ci, h in probs]
    qb = [x.astype(BF16) for x in q]
    kb = [x.astype(BF16) for x in k]
    vb = [vo_refs[h][0][rows(ci), DV_B * (h % 2):DV_B * (h % 2 + 1)].astype(BF16) for ci, h in probs]
    qkt = [lax.dot_general(a, b, NT, preferred_element_type=F32) for a, b in zip(qb, kb)]
    li_c = [smb[rows(ci), SM_I + h:SM_I + h + 1] for ci, h in probs]
    li_r = [li_t[SM_I + h:SM_I + h + 1, rows(ci)] for ci, h in probs]
    bc_c = [jnp.sum(jnp.where(causal, lf_t[SM_F + h:SM_F + h + 1, rows(ci)], 0.0), axis=1, keepdims=True)
            for ci, h in probs]
    bc_r = [jnp.sum(jnp.where(rowi <= coli, lf_all[rows(ci), SM_F + h:SM_F + h + 1], 0.0), axis=0, keepdims=True)
            for ci, h in probs]
    dmat = [jnp.where(causal, a - b + r, -jnp.inf) for a, b, r in zip(bc_c, bc_r, li_r)]
    dmax = [jnp.max(x, axis=1, keepdims=True) for x in dmat]
    b_last = [x[c - 1:c, :] for x in bc_c]
    expo = [bl - a + l for bl, a, l in zip(b_last, bc_c, li_c)]
    emax = [jnp.max(x, axis=0, keepdims=True) for x in expo]
    m_prev, m_new = [None] * len(probs), [None] * len(probs)
    for h in range(H_B):
        m = m_row[:, h:h + 1]
        for ci in range(nchunk):
            i = idx[ci, h]
            m_prev[i] = m
            m = jnp.maximum(b_last[i] + m, emax[i])
            m_new[i] = m
    inter = [a + m for a, m in zip(bc_c, m_prev)]
    m_t = [jnp.maximum(a, b) for a, b in zip(inter, dmax)]
    w_inter = [jnp.exp(a - b) for a, b in zip(inter, m_t)]
    s = [x * jnp.exp(d - m) for x, d, m in zip(qkt, dmat, m_t)]
    sv = [_dot(x.astype(BF16), v) for x, v in zip(s, vb)]
    ssum = [jnp.sum(x, axis=1, keepdims=True) for x in s]
    emt = [jnp.exp(-x) for x in m_t]
    decay = [jnp.exp(bl + mp - mn) for bl, mp, mn in zip(b_last, m_prev, m_new)]
    kw = [jnp.exp(e - mn) * kx for e, mn, kx in zip(expo, m_new, k)]
    kv = [lax.dot_general(x.astype(BF16), v, TN, preferred_element_type=F32) for x, v in zip(kw, vb)]
    ksum = [jnp.sum(x, axis=0, keepdims=True) for x in kw]

    cst = [c_ref[0, h] for h in range(H_B)]
    nrow = [n_ref[0, h:h + 1, :] for h in range(H_B)]
    for ci in range(nchunk):
        ids = [idx[ci, h] for h in range(H_B)]
        qc = [_dot(qb[i], cst[h].astype(BF16)) for h, i in enumerate(ids)]
        num = [sv[i] + w_inter[i] * qc[h] for h, i in enumerate(ids)]
        nq = [ssum[i] + w_inter[i] * jnp.sum(q[i] * nrow[h], axis=1, keepdims=True) for h, i in enumerate(ids)]
        hh = [num[h] / jnp.maximum(jnp.abs(nq[h]), emt[i]) for h, i in enumerate(ids)]
        cst = [decay[i] * cst[h] + kv[i] for h, i in enumerate(ids)]
        nrow = [decay[i] * nrow[h] + ksum[i] for h, i in enumerate(ids)]
        for h in range(H_B):
            og = vo_refs[h][1][rows(ci), DV_B * (h % 2):DV_B * (h % 2 + 1)]
            y_ref[rows(ci), DV_B * h:DV_B * (h + 1)] = (_rms(hh[h], nw_ref[h:h + 1, :]) * _sigmoid(og)).astype(BF16)
    m_out = m_row
    for h in range(H_B):
        c_ref[0, h] = cst[h]
        n_ref[0, h:h + 1, :] = nrow[h]
        m_out = jnp.where(lane == h, m_new[idx[nchunk - 1, h]], m_out)
    m_ref[0] = m_out


def _mlstm_prompt(proj, bsz, seq, mp):
    chunk = _pick(seq, (MLSTM_CHUNK,))
    c = _pick(seq, (MLSTM_TILE, MLSTM_CHUNK))
    nt = seq // c
    rows = lambda col: (lambda b, t: (b * nt + t, col))
    const2 = lambda b, t: (0, 0)
    return pl.pallas_call(
        functools.partial(_mlstm_prompt_kernel, chunk=chunk),
        grid=(bsz, nt),
        in_specs=[pl.BlockSpec((c, 512), rows(COL_QB // 4)),
                  pl.BlockSpec((c, 512), rows(COL_KB // 4)),
                  pl.BlockSpec((c, 512), rows(COL_VB // 4)),
                  pl.BlockSpec((c, 512), rows(COL_VB // 4 + 1)),
                  pl.BlockSpec((c, 512), rows(COL_OB // 4)),
                  pl.BlockSpec((c, 512), rows(COL_OB // 4 + 1)),
                  pl.BlockSpec((c, LANE), rows(COL_SMALL)),
                  pl.BlockSpec((1, LANE), const2),
                  pl.BlockSpec((H_B, DV_B), const2)],
        out_specs=[pl.BlockSpec((c, D_B), lambda b, t: (b * nt + t, 0)),
                   pl.BlockSpec((1, H_B, DQK_B, DV_B), lambda b, t: (b, 0, 0, 0)),
                   pl.BlockSpec((1, H_B, DQK_B), lambda b, t: (b, 0, 0)),
                   pl.BlockSpec((1, 1, LANE), lambda b, t: (b, 0, 0))],
        out_shape=[jax.ShapeDtypeStruct((bsz * seq, D_B), BF16),
                   jax.ShapeDtypeStruct((bsz, H_B, DQK_B, DV_B), F32),
                   jax.ShapeDtypeStruct((bsz, H_B, DQK_B), F32),
                   jax.ShapeDtypeStruct((bsz, 1, LANE), F32)],
        compiler_params=_cparams(("arbitrary", "arbitrary")),
        name="mlstm_prompt",
    )(proj, proj, proj, proj, proj, proj, proj, mp["bias"], mp["norm"])


def _split_bf16(a):
    hi = a.astype(BF16)
    return hi, (a - hi.astype(F32)).astype(BF16)


def _dot_x3(a, b):
    ah, al = _split_bf16(a)
    bh, bl = _split_bf16(b)
    return _dot(ah, bh) + _dot(ah, bl) + _dot(al, bh)


def _unit_lower_inverses(lmats):
    c = lmats[0].shape[0]
    eye = (lax.broadcasted_iota(jnp.int32, (c, c), 0) == lax.broadcasted_iota(jnp.int32, (c, c), 1)).astype(F32)
    hi_half = lax.broadcasted_iota(jnp.int32, (c, 2 * c), 1) >= c
    ms = [jnp.concatenate([-l, eye], axis=1) for l in lmats]
    span = 1
    while span < c:
        ms = [_dot_x3(m[:, 0:c], m) + jnp.where(hi_half, m, 0.0) for m in ms]
        span *= 2
    return [m[:, c:2 * c] for m in ms]


def _gdn_prompt_kernel(q_ref, k_ref, v_ref, z_ref, sm_ref, bias_ref, alog_ref, cw_ref, nw_ref,
                       y_ref, s_ref, xb_s, *, chunk):
    tt = q_ref.shape[0]
    c = chunk

    @pl.when(pl.program_id(1) == 0)
    def _():
        s_ref[...] = jnp.zeros_like(s_ref)
        xb_s[0:SUBLANE, :] = jnp.zeros((SUBLANE, 3 * D_C), F32)

    xb_s[SUBLANE:SUBLANE + tt, 0:D_C] = q_ref[...]
    xb_s[SUBLANE:SUBLANE + tt, D_C:2 * D_C] = k_ref[...]
    xb_s[SUBLANE:SUBLANE + tt, 2 * D_C:3 * D_C] = v_ref[...]
    conv = cw_ref[GDN_CONV - 1:GDN_CONV, :] * xb_s[SUBLANE:SUBLANE + tt, :]
    for j in range(GDN_CONV - 1):
        off = SUBLANE - (GDN_CONV - 1) + j
        conv = conv + cw_ref[j:j + 1, :] * xb_s[off:off + tt, :]
    xb_s[0:SUBLANE, :] = xb_s[tt:tt + SUBLANE, :]
    qkv = _silu(conv)

    sm = sm_ref[...]
    beta_all = _sigmoid(sm)
    g_all = -jnp.exp(alog_ref[...]) * _softplus(sm + bias_ref[...])
    g_t = g_all.T
    rowi = lax.broadcasted_iota(jnp.int32, (c, c), 0)
    coli = lax.broadcasted_iota(jnp.int32, (c, c), 1)
    incl = rowi >= coli
    strict = rowi > coli

    nchunk = tt // c
    probs = [(ci, h) for ci in range(nchunk) for h in range(H_C)]
    rows = lambda ci: slice(c * ci, c * (ci + 1))
    l2 = lambda x: x * lax.rsqrt(jnp.sum(x * x, axis=-1, keepdims=True) + EPS)
    q = [l2(qkv[rows(ci), DK_C * h:DK_C * (h + 1)]) * (DK_C ** -0.5) for ci, h in probs]
    k = [l2(qkv[rows(ci), D_C + DK_C * h:D_C + DK_C * (h + 1)]) for ci, h in probs]
    v = [qkv[rows(ci), 2 * D_C + DV_C * h:2 * D_C + DV_C * (h + 1)] for ci, h in probs]
    beta_c = [beta_all[rows(ci), SM_BETA + h:SM_BETA + h + 1] for ci, h in probs]
    gc_c = [jnp.sum(jnp.where(incl, g_t[SM_A + h:SM_A + h + 1, rows(ci)], 0.0), axis=1, keepdims=True)
            for ci, h in probs]
    gc_r = [jnp.sum(jnp.where(rowi <= coli, g_all[rows(ci), SM_A + h:SM_A + h + 1], 0.0), axis=0, keepdims=True)
            for ci, h in probs]
    gam = [jnp.exp(jnp.where(incl, a - b, -jnp.inf)) for a, b in zip(gc_c, gc_r)]
    qb = [x.astype(BF16) for x in q]
    kb = [x.astype(BF16) for x in k]
    kk = [lax.dot_general(x, x, NT, preferred_element_type=F32) for x in kb]
    qk = [(lax.dot_general(a, b, NT, preferred_element_type=F32) * g).astype(BF16) for a, b, g in zip(qb, kb, gam)]
    egc = [jnp.exp(x) for x in gc_c]
    tinv = _unit_lower_inverses([jnp.where(strict, b * g * x, 0.0) for b, g, x in zip(beta_c, gam, kk)])
    tr = [_dot_x3(t, jnp.concatenate([b * vv, (b * e) * kx], axis=1))
          for t, b, vv, e, kx in zip(tinv, beta_c, v, egc, k)]
    u0 = [x[:, 0:DV_C] for x in tr]
    wb = [x[:, DV_C:].astype(BF16) for x in tr]
    g_last = [x[c - 1:c, :] for x in gc_c]
    kw = [(jnp.exp(gl - gc) * kx).astype(BF16) for gl, gc, kx in zip(g_last, gc_c, k)]
    e_last = [jnp.exp(x) for x in g_last]

    st = [s_ref[0, h] for h in range(H_C)]
    for ci in range(nchunk):
        ids = [ci * H_C + h for h in range(H_C)]
        stb = [x.astype(BF16) for x in st]
        ub = [(u0[i] - _dot(wb[i], stb[h])).astype(BF16) for h, i in enumerate(ids)]
        qs = [_dot(qb[i], stb[h]) for h, i in enumerate(ids)]
        st = [e_last[i] * st[h] + lax.dot_general(kw[i], ub[h], TN, preferred_element_type=F32)
              for h, i in enumerate(ids)]
        o = [egc[i] * qs[h] + _dot(qk[i], ub[h]) for h, i in enumerate(ids)]
        for h in range(H_C):
            zz = z_ref[rows(ci), DV_C * h:DV_C * (h + 1)]
            y_ref[rows(ci), DV_C * h:DV_C * (h + 1)] = (_rms(o[h], nw_ref[...]) * _silu(zz)).astype(BF16)
    for h in range(H_C):
        s_ref[0, h] = st[h]


def _gdn_prompt(proj, bsz, seq, gp):
    c = _pick(seq, (GDN_CHUNK,))
    tt = _pick(seq, (GDN_TILE,))
    nt = seq // tt
    rows = lambda col: (lambda b, t: (b * nt + t, col))
    const2 = lambda b, t: (0, 0)
    return pl.pallas_call(
        functools.partial(_gdn_prompt_kernel, chunk=c),
        grid=(bsz, nt),
        in_specs=[pl.BlockSpec((tt, D_C), rows(COL_QKV // 4)),
                  pl.BlockSpec((tt, D_C), rows(COL_QKV // 4 + 1)),
                  pl.BlockSpec((tt, D_C), rows(COL_QKV // 4 + 2)),
                  pl.BlockSpec((tt, D_C), rows(COL_Z // 4)),
                  pl.BlockSpec((tt, LANE), rows(COL_SMALL)),
                  pl.BlockSpec((1, LANE), const2),
                  pl.BlockSpec((1, LANE), const2),
                  pl.BlockSpec((GDN_CONV, 3 * D_C), const2),
                  pl.BlockSpec((1, DV_C), const2)],
        out_specs=[pl.BlockSpec((tt, D_C), lambda b, t: (b * nt + t, 0)),
                   pl.BlockSpec((1, H_C, DK_C, DV_C), lambda b, t: (b, 0, 0, 0))],
        out_shape=[jax.ShapeDtypeStruct((bsz * seq, D_C), BF16),
                   jax.ShapeDtypeStruct((bsz, H_C, DK_C, DV_C), F32)],
        scratch_shapes=[pltpu.VMEM((SUBLANE + tt, 3 * D_C), F32)],
        compiler_params=_cparams(("arbitrary", "arbitrary")),
        name="gdn_prompt",
    )(proj, proj, proj, proj, proj, gp["bias"], gp["alog"], gp["conv_w"], gp["norm"])


def _to_col(row, eye):
    return jnp.sum(jnp.where(eye, row, 0.0), axis=1, keepdims=True)


N_SAMPLE_INPUTS = 23


def _sample_mixers_kernel(*refs):
    (proj_ref, sre_ref, sim_ref, c_ref, n_ref, m_ref, gs_ref, gbuf_ref,
     are_ref, aim_ref, ldt_ref, bre_ref, bim_ref, cre_ref, cim_ref, d_ref, wglu_ref,
     mbias_ref, mnorm_ref, gbias_ref, alog_ref, gcw_ref, gnorm_ref) = refs[:N_SAMPLE_INPUTS]
    (y_ref, sre_o, sim_o, c_o, n_o, m_o, gs_o, gbuf_o,
     qkv_s, qn_s, kn_s, beta_s, g_s, li_s, lf_s) = refs[-15:]
    bb = proj_ref.shape[0]

    u = proj_ref[:, COL_U * LANE:COL_U * LANE + D_A]
    ub = u.astype(BF16)
    ys = []
    for j in range(S5_NCHUNK):
        sl = slice(S5_CHUNK * j, S5_CHUNK * (j + 1))
        abr, abi, fre, fim = _s5_disc(are_ref[:, sl], aim_ref[:, sl], ldt_ref[:, sl])
        uj = ub[:, LANE * j:LANE * (j + 1)]
        bur = _dot(uj, bre_ref[j])
        bui = _dot(uj, bim_ref[j])
        s0r = sre_ref[:, sl]
        s0i = sim_ref[:, sl]
        xr = fre * bur - fim * bui + abr * s0r - abi * s0i
        xi = fre * bui + fim * bur + abr * s0i + abi * s0r
        sre_o[:, sl] = xr
        sim_o[:, sl] = xi
        ys.append(_dot(xr.astype(BF16), cre_ref[j]) - _dot(xi.astype(BF16), cim_ref[j]))
    y_a = jnp.concatenate(ys, axis=1)
    y_ref[:, Y_A:Y_A + D_A] = _s5_glu(y_a, u, d_ref, wglu_ref)

    sm = proj_ref[:, COL_SMALL * LANE:(COL_SMALL + 1) * LANE]
    smb = sm + mbias_ref[...]
    li_s[...] = smb
    lf_s[...] = _log_sigmoid(smb)
    beta_s[...] = _sigmoid(sm)
    g_s[...] = -jnp.exp(alog_ref[...]) * _softplus(sm + gbias_ref[...])

    xnew = proj_ref[:, COL_QKV * LANE:COL_QKV * LANE + 3 * D_C]
    conv = gcw_ref[GDN_CONV - 1:GDN_CONV, :] * xnew
    for j in range(GDN_CONV - 1):
        conv = conv + gcw_ref[j:j + 1, :] * gbuf_ref[j]
        if j > 0:
            gbuf_o[j - 1] = gbuf_ref[j]
    gbuf_o[GDN_CONV - 2] = xnew
    qkv = _silu(conv)
    qkv_s[...] = qkv
    for h in range(H_C):
        q = qkv[:, DK_C * h:DK_C * (h + 1)]
        k = qkv[:, D_C + DK_C * h:D_C + DK_C * (h + 1)]
        qn_s[:, DK_C * h:DK_C * (h + 1)] = q * lax.rsqrt(jnp.sum(q * q, axis=-1, keepdims=True) + EPS) * (DK_C ** -0.5)
        kn_s[:, DK_C * h:DK_C * (h + 1)] = k * lax.rsqrt(jnp.sum(k * k, axis=-1, keepdims=True) + EPS)

    eye = (lax.broadcasted_iota(jnp.int32, (LANE, LANE), 0)
           == lax.broadcasted_iota(jnp.int32, (LANE, LANE), 1))
    lane = lax.broadcasted_iota(jnp.int32, (1, H_B), 1)

    probs = [(b, h) for b in range(bb) for h in range(H_B)]
    rsl = lambda b: slice(b, b + 1)

    q = [proj_ref[rsl(b), COL_QB * LANE + DQK_B * h:COL_QB * LANE + DQK_B * (h + 1)] * (DQK_B ** -0.5)
         for b, h in probs]
    k = [proj_ref[rsl(b), COL_KB * LANE + DQK_B * h:COL_KB * LANE + DQK_B * (h + 1)] for b, h in probs]
    v = [proj_ref[rsl(b), COL_VB * LANE + DV_B * h:COL_VB * LANE + DV_B * (h + 1)] for b, h in probs]
    li = [li_s[rsl(b), SM_I + h:SM_I + h + 1] for b, h in probs]
    inter = [lf_s[rsl(b), SM_F + h:SM_F + h + 1] + m_ref[rsl(b), h:h + 1] for b, h in probs]
    m_t = [jnp.maximum(a, c) for a, c in zip(inter, li)]
    w_intra = [jnp.exp(a - c) for a, c in zip(li, m_t)]
    w_inter = [jnp.exp(a - c) for a, c in zip(inter, m_t)]
    qcol = [_to_col(x, eye) for x in q]
    kcol = [_to_col(x, eye) for x in k]
    s = [jnp.sum(a * c, axis=1, keepdims=True) * w for a, c, w in zip(q, k, w_intra)]
    cst = [c_ref[b, h] for b, h in probs]
    nrow = [n_ref[b, h:h + 1, :] for b, h in probs]
    qc = [jnp.sum(a * c, axis=0, keepdims=True) for a, c in zip(qcol, cst)]
    for i, (b, h) in enumerate(probs):
        c_o[b, h] = w_inter[i] * cst[i] + (w_intra[i] * kcol[i]) * v[i]
        n_o[b, h:h + 1, :] = w_inter[i] * nrow[i] + w_intra[i] * k[i]
    num = [s[i] * v[i] + w_inter[i] * qc[i] for i in range(len(probs))]
    nq = [s[i] + w_inter[i] * jnp.sum(q[i] * nrow[i], axis=1, keepdims=True) for i in range(len(probs))]
    hh = [num[i] / jnp.maximum(jnp.abs(nq[i]), jnp.exp(-m_t[i])) for i in range(len(probs))]
    for i, (b, h) in enumerate(probs):
        og = proj_ref[rsl(b), COL_OB * LANE + DV_B * h:COL_OB * LANE + DV_B * (h + 1)]
        y_ref[rsl(b), Y_B + DV_B * h:Y_B + DV_B * (h + 1)] = _rms(hh[i], mnorm_ref[h:h + 1, :]) * _sigmoid(og)
    for b in range(bb):
        m_out = m_ref[rsl(b), :]
        for h in range(H_B):
            m_out = jnp.where(lane == h, m_t[b * H_B + h], m_out)
        m_o[rsl(b), :] = m_out

    q = [qn_s[rsl(b), DK_C * h:DK_C * (h + 1)] for b, h in probs]
    k = [kn_s[rsl(b), DK_C * h:DK_C * (h + 1)] for b, h in probs]
    v = [qkv_s[rsl(b), 2 * D_C + DV_C * h:2 * D_C + DV_C * (h + 1)] for b, h in probs]
    beta = [beta_s[rsl(b), SM_BETA + h:SM_BETA + h + 1] for b, h in probs]
    eg = [jnp.exp(g_s[rsl(b), SM_A + h:SM_A + h + 1]) for b, h in probs]
    qcol = [_to_col(x, eye) for x in q]
    kcol = [_to_col(x, eye) for x in k]
    st = [gs_ref[b, h] for b, h in probs]
    ks = [jnp.sum(a * c, axis=0, keepdims=True) for a, c in zip(kcol, st)]
    qs = [jnp.sum(a * c, axis=0, keepdims=True) for a, c in zip(qcol, st)]
    un = [beta[i] * (v[i] - eg[i] * ks[i]) for i in range(len(probs))]
    for i, (b, h) in enumerate(probs):
        gs_o[b, h] = eg[i] * st[i] + kcol[i] * un[i]
    o = [eg[i] * qs[i] + jnp.sum(q[i] * k[i], axis=1, keepdims=True) * un[i] for i in range(len(probs))]
    for i, (b, h) in enumerate(probs):
        zz = proj_ref[rsl(b), COL_Z * LANE + DV_C * h:COL_Z * LANE + DV_C * (h + 1)]
        y_ref[rsl(b), Y_C + DV_C * h:Y_C + DV_C * (h + 1)] = _rms(o[i], gnorm_ref[...]) * _silu(zz)


def _sample_mixers(proj, row0, nrows, st, sp, mp, gp, layer, prev):
    bb = SAMPLE_BLOCK
    blk0 = row0 // bb
    depth = st["c"].shape[0]
    const2 = lambda i: (0, 0)
    const3 = lambda i: (0, 0, 0)
    row_spec = pl.BlockSpec((1, S5_LANES), const2)
    c_spec = _layered(layer, (bb, H_B, DQK_B, DV_B), lambda i: (i, 0, 0, 0))
    gs_spec = _layered(layer, (bb, H_C, DK_C, DV_C), lambda i: (i, 0, 0, 0))
    in_specs = [
        pl.BlockSpec((bb, N_PROJ), lambda i: (blk0 + i, 0)),
        pl.BlockSpec((bb, S5_LANES), lambda i: (i, 0)),
        pl.BlockSpec((bb, S5_LANES), lambda i: (i, 0)),
        c_spec,
        pl.BlockSpec((bb, H_B, DQK_B), lambda i: (i, 0, 0)),
        pl.BlockSpec((bb, H_B), lambda i: (i, 0)),
        gs_spec,
        pl.BlockSpec((GDN_CONV - 1, bb, 3 * D_C), lambda i: (0, i, 0)),
        row_spec, row_spec, row_spec,
        pl.BlockSpec((S5_NCHUNK, LANE, S5_CHUNK), const3),
        pl.BlockSpec((S5_NCHUNK, LANE, S5_CHUNK), const3),
        pl.BlockSpec((S5_NCHUNK, S5_CHUNK, LANE), const3),
        pl.BlockSpec((S5_NCHUNK, S5_CHUNK, LANE), const3),
        pl.BlockSpec((1, D_A), const2),
        pl.BlockSpec((D_A, D_A), const2),
        pl.BlockSpec((1, LANE), const2),
        pl.BlockSpec((H_B, DV_B), const2),
        pl.BlockSpec((1, LANE), const2),
        pl.BlockSpec((1, LANE), const2),
        pl.BlockSpec((GDN_CONV, 3 * D_C), const2),
        pl.BlockSpec((1, DV_C), const2),
    ]
    out_specs = [
        pl.BlockSpec((bb, D_MODEL), lambda i: (i, 0)),
        pl.BlockSpec((bb, S5_LANES), lambda i: (i, 0)),
        pl.BlockSpec((bb, S5_LANES), lambda i: (i, 0)),
        c_spec,
        pl.BlockSpec((bb, H_B, DQK_B), lambda i: (i, 0, 0)),
        pl.BlockSpec((bb, H_B), lambda i: (i, 0)),
        gs_spec,
        pl.BlockSpec((GDN_CONV - 1, bb, 3 * D_C), lambda i: (0, i, 0)),
    ]
    out_shape = [
        jax.ShapeDtypeStruct((nrows, D_MODEL), F32),
        jax.ShapeDtypeStruct((nrows, S5_LANES), F32),
        jax.ShapeDtypeStruct((nrows, S5_LANES), F32),
        jax.ShapeDtypeStruct((depth, nrows, H_B, DQK_B, DV_B), F32),
        jax.ShapeDtypeStruct((nrows, H_B, DQK_B), F32),
        jax.ShapeDtypeStruct((nrows, H_B), F32),
        jax.ShapeDtypeStruct((depth, nrows, H_C, DK_C, DV_C), F32),
        jax.ShapeDtypeStruct((GDN_CONV - 1, nrows, 3 * D_C), F32),
    ]
    scratch = [pltpu.VMEM((bb, 3 * D_C), F32), pltpu.VMEM((bb, D_C), F32), pltpu.VMEM((bb, D_C), F32),
               pltpu.VMEM((bb, LANE), F32), pltpu.VMEM((bb, LANE), F32),
               pltpu.VMEM((bb, LANE), F32), pltpu.VMEM((bb, LANE), F32)]
    args = [proj, st["sre"], st["sim"], st["c"], st["n"], st["m"], st["gs"], st["gbuf"],
            sp["are"], sp["aim"], sp["ldt"], sp["bre"], sp["bim"], sp["cre"], sp["cim"], sp["d"], sp["wglu"],
            mp["bias"], mp["norm"], gp["bias"], gp["alog"], gp["conv_w"], gp["norm"]]
    assert len(args) == N_SAMPLE_INPUTS
    aliases = {}
    if prev is not None:
        in_specs += [pl.BlockSpec(memory_space=pl.ANY)] * 2
        args += list(prev)
        aliases = {N_SAMPLE_INPUTS: 3, N_SAMPLE_INPUTS + 1: 6}
    return pl.pallas_call(
        _sample_mixers_kernel,
        grid=(nrows // bb,),
        in_specs=in_specs, out_specs=out_specs, out_shape=out_shape, scratch_shapes=scratch,
        input_output_aliases=aliases,
        compiler_params=_cparams(("arbitrary",)),
        name="sample_mixers",
    )(*args)


def _merge_kernel(h_ref, ya_ref, yb_ref, yc_ref, wg0_ref, wg1_ref, wg2_ref, wa_ref, wb_ref, wc_ref, o_ref):
    h = h_ref[...]
    bf = lambda ref: ref[...].astype(BF16)
    acc = _sigmoid(_dot(h, bf(wg0_ref))) * _dot(bf(ya_ref), bf(wa_ref))
    acc = acc + _sigmoid(_dot(h, bf(wg1_ref))) * _dot(bf(yb_ref), bf(wb_ref))
    acc = acc + _sigmoid(_dot(h, bf(wg2_ref))) * _dot(bf(yc_ref), bf(wc_ref))
    o_ref[...] = acc.astype(BF16)


def _merge(h, ya, yb, yc, wg, wa, wb, wc, layer):
    m, d = h.shape
    tm = _pick(m, (1024, 512, 256, 128))
    tn = 256
    nb = d // tn
    lhs = lambda w, cb: pl.BlockSpec((tm, w), lambda i, j: (i, cb))
    gate = lambda g: _layered(layer, (d, tn), lambda i, j: (0, g * nb + j))
    rhs = lambda w: _layered(layer, (w, tn), lambda i, j: (0, j))
    return pl.pallas_call(
        _merge_kernel,
        grid=(m // tm, nb),
        in_specs=[lhs(d, 0), lhs(D_A, ya[1]), lhs(D_B, yb[1]), lhs(D_C, yc[1]), gate(0), gate(1), gate(2),
                  rhs(D_A), rhs(D_B), rhs(D_C)],
        out_specs=pl.BlockSpec((tm, tn), lambda i, j: (i, j)),
        out_shape=jax.ShapeDtypeStruct((m, d), BF16),
        compiler_params=_cparams(("arbitrary", "arbitrary")),
        name="merge",
    )(h, ya[0], yb[0], yc[0], wg, wg, wg, wa, wb, wc)


def _out_proj_kernel(a_ref, x_ref, w_ref, nw_ref, o_ref):
    o_ref[...] = x_ref[...] + _rms(_dot(a_ref[...], w_ref[...]), nw_ref[...])


def _out_proj(a, x, w, nw, layer):
    m, d = x.shape
    tm = _pick(m, (512, 256, 128))
    return pl.pallas_call(
        _out_proj_kernel,
        grid=(m // tm,),
        in_specs=[pl.BlockSpec((tm, d), lambda i: (i, 0)),
                  pl.BlockSpec((tm, d), lambda i: (i, 0)),
                  _layered(layer, (d, d), lambda i: (0, 0)),
                  _layered(layer, (1, d), lambda i: (0, 0))],
        out_specs=pl.BlockSpec((tm, d), lambda i: (i, 0)),
        out_shape=jax.ShapeDtypeStruct((m, d), F32),
        compiler_params=_cparams(("arbitrary",)),
        name="out_proj",
    )(a, x, w, nw)


def _ffn_kernel(*refs, sample, tiles_per_seq):
    if sample:
        (x_ref, nw_ref, wg_ref, wu_ref, cw_ref, wd_ref, pnw_ref, b0_ref, b1_ref,
         o_ref, g_ref, h2_s) = refs
    else:
        (x_ref, nw_ref, wg_ref, wu_ref, cw_ref, wd_ref, pnw_ref,
         o_ref, g_ref, h2_s, gb_s, carry_s) = refs
    i = pl.program_id(0)
    j = pl.program_id(1)
    tm = x_ref.shape[0]

    @pl.when(j == 0)
    def _():
        h2_s[...] = _rms(x_ref[...], nw_ref[...]).astype(BF16)
        o_ref[...] = jnp.zeros_like(o_ref)

    h2 = h2_s[...]
    g = _dot(h2, wg_ref[...])
    up = _dot(h2, wu_ref[...])
    if sample:
        a = cw_ref[0:1, :] * b0_ref[...] + cw_ref[1:2, :] * b1_ref[...] + cw_ref[2:3, :] * g
        g_ref[...] = g
    else:
        prev = jnp.where(i % tiles_per_seq == 0, 0.0, carry_s[j])
        gb_s[0:SUBLANE, :] = prev
        gb_s[SUBLANE:SUBLANE + tm, :] = g
        a = (cw_ref[0:1, :] * gb_s[SUBLANE - 2:SUBLANE - 2 + tm, :]
             + cw_ref[1:2, :] * gb_s[SUBLANE - 1:SUBLANE - 1 + tm, :]
             + cw_ref[2:3, :] * g)
        tail = g[tm - SUBLANE:tm, :]
        carry_s[j] = tail
        g_ref[0] = tail
    act = (_gelu(a) * up).astype(BF16)
    o_ref[...] += _dot(act, wd_ref[...])

    @pl.when(j == pl.num_programs(1) - 1)
    def _():
        o_ref[...] = x_ref[...] + _rms(o_ref[...], pnw_ref[...])


def _ffn(x, row0, nrows, seq, nw, wg, wu, cw, wd, pnw, layer, bufs=None):
    d = x.shape[1]
    f = wg.shape[-1]
    sample = bufs is not None
    tm = nrows if sample else _pick(seq, (512, 256, 128, 64, 32, 16, 8))
    tn = _pick(f, (1024, 512, 256, 128))
    mt, ft = nrows // tm, f // tn
    blk0 = row0 // tm
    in_specs = [pl.BlockSpec((tm, d), lambda i, j: (blk0 + i, 0)),
                _layered(layer, (1, d), lambda i, j: (0, 0)),
                _layered(layer, (d, tn), lambda i, j: (0, j)),
                _layered(layer, (d, tn), lambda i, j: (0, j)),
                _layered(layer, (FFN_CONV, tn), lambda i, j: (0, j)),
                _layered(layer, (tn, d), lambda i, j: (j, 0)),
                _layered(layer, (1, d), lambda i, j: (0, 0))]
    args = [x, nw, wg, wu, cw, wd, pnw]
    scratch = [pltpu.VMEM((tm, d), BF16)]
    if sample:
        in_specs += [pl.BlockSpec((tm, tn), lambda i, j: (i, j))] * 2
        args += list(bufs)
        g_spec = pl.BlockSpec((tm, tn), lambda i, j: (i, j))
        g_shape = jax.ShapeDtypeStruct((nrows, f), F32)
    else:
        scratch += [pltpu.VMEM((SUBLANE + tm, tn), F32), pltpu.VMEM((ft, SUBLANE, tn), F32)]
        g_spec = pl.BlockSpec((1, SUBLANE, tn), lambda i, j: (i, 0, j))
        g_shape = jax.ShapeDtypeStruct((mt, SUBLANE, f), F32)
    return pl.pallas_call(
        functools.partial(_ffn_kernel, sample=sample, tiles_per_seq=max(seq // tm, 1)),
        grid=(mt, ft),
        in_specs=in_specs,
        out_specs=[pl.BlockSpec((tm, d), lambda i, j: (i, 0)), g_spec],
        out_shape=[jax.ShapeDtypeStruct((nrows, d), F32), g_shape],
        scratch_shapes=scratch,
        compiler_params=_cparams(("arbitrary", "arbitrary")),
        name="ffn_sample" if sample else "ffn_prompt",
    )(*args)


def _ple_kernel(x_ref, p_ref, wg_ref, wp_ref, o_ref):
    x = x_ref[...]
    gate = _sigmoid(_dot(x.astype(BF16), wg_ref[...]))
    o_ref[...] = x + gate * _dot(p_ref[...].astype(BF16), wp_ref[...])


def _ple(x, p, wg, wp, layer):
    m, d = x.shape
    pd = p.shape[-1]
    tm = _pick(m, (512, 256, 128))
    return pl.pallas_call(
        _ple_kernel,
        grid=(m // tm,),
        in_specs=[pl.BlockSpec((tm, d), lambda i: (i, 0)),
                  _layered(layer, (tm, pd), lambda i: (i, 0)),
                  _layered(layer, (d, d), lambda i: (0, 0)),
                  _layered(layer, (pd, d), lambda i: (0, 0))],
        out_specs=pl.BlockSpec((tm, d), lambda i: (i, 0)),
        out_shape=jax.ShapeDtypeStruct((m, d), F32),
        compiler_params=_cparams(("arbitrary",)),
        name="ple",
    )(x, p, wg, wp)


def _split_w_in(w):
    n_if = 2 * H_B
    w_mid = w[..., N_HEAD + n_if:N_HEAD + n_if + N_MID]
    gates = jnp.concatenate([w[..., N_HEAD:N_HEAD + n_if], w[..., N_HEAD + n_if + N_MID:]], axis=-1)
    pad = jnp.zeros(w.shape[:-1] + (LANE - gates.shape[-1],), w.dtype)
    return w_mid, jnp.concatenate([gates, pad], axis=-1)


def _small_row(entries):
    row = jnp.zeros((LANE,), F32)
    for off, val in entries:
        row = row.at[off:off + val.shape[0]].set(val.astype(F32))
    return row.reshape(1, LANE)


def _s5_params(a_re, a_im, log_dt, b_re, b_im, c_re, c_im, d, w_glu):
    gpc = S5_CHUNK // S5_STATE
    eye = jnp.eye(gpc, dtype=F32)

    def bmat(b):
        b4 = b.reshape(S5_NCHUNK, gpc, S5_GROUP, S5_STATE)
        return jnp.einsum('jgcp,gh->jgchp', b4, eye).reshape(S5_NCHUNK, gpc * S5_GROUP, S5_CHUNK).astype(BF16)

    def cmat(c):
        c4 = c.reshape(S5_NCHUNK, gpc, S5_STATE, S5_GROUP)
        return jnp.einsum('jgpc,gh->jgphc', c4, eye).reshape(S5_NCHUNK, S5_CHUNK, gpc * S5_GROUP).astype(BF16)

    return dict(are=a_re.reshape(1, S5_LANES), aim=a_im.reshape(1, S5_LANES),
                ldt=jnp.broadcast_to(log_dt[:, None], (S5_GROUPS, S5_STATE)).reshape(1, S5_LANES),
                bre=bmat(b_re), bim=bmat(b_im), cre=cmat(c_re), cim=cmat(c_im),
                d=d.reshape(1, D_A), wglu=w_glu.astype(BF16))


def _layer(layer, xp, xs, bsz, seq, dense, lw, state, prev):
    np_rows = bsz * seq
    nsamp = xs.shape[0]
    in_w = (dense['norm_mix_pre'], dense['w_in'], dense['w_in_mid'], dense['w_in_small'], layer)
    proj, h_p = _norm_proj(xp, *in_w)
    proj_s, h_s = _norm_proj(xs, *in_w)

    sp = _s5_params(lw['s5_a_re'], lw['s5_a_im'], lw['s5_log_dt'], lw['s5_b_re'], lw['s5_b_im'],
                    lw['s5_c_re'], lw['s5_c_im'], lw['s5_d'], lw['s5_w_glu'])
    mp = dict(bias=_small_row([(SM_I, lw['mlstm_b_i']), (SM_F, lw['mlstm_b_f'])]), norm=lw['mlstm_norm'])
    gp = dict(bias=_small_row([(SM_A, lw['gdn_dt_bias'])]), alog=_small_row([(SM_A, lw['gdn_a_log'])]),
              conv_w=lw['gdn_conv_w'], norm=lw['gdn_norm'].reshape(1, DV_C))

    ya_p, sre_p, sim_p = _s5_prompt(proj, bsz, seq, sp)
    yb_p, c_p, n_p, m_p = _mlstm_prompt(proj, bsz, seq, mp)
    yc_p, gs_p = _gdn_prompt(proj, bsz, seq, gp)

    ssm_re, ssm_im, m_c, m_n, m_m, g_s, g_conv, f_conv = state
    st = dict(sre=ssm_re.reshape(nsamp, S5_LANES), sim=ssm_im.reshape(nsamp, S5_LANES),
              c=m_c, n=m_n, m=m_m, gs=g_s, gbuf=jnp.swapaxes(g_conv, 0, 1))
    y_s, sre_s, sim_s, c_s, n_s, m_s, gs_s, gbuf_s = _sample_mixers(proj_s, 0, nsamp, st, sp, mp, gp, layer, prev)

    merge_w = (dense['w_gate'], dense['w_branch_a'], dense['w_branch_b'], dense['w_branch_c'], layer)
    merged_p = _merge(h_p, (ya_p, 0), (yb_p, 0), (yc_p, 0), *merge_w)
    merged_s = _merge(h_s, (y_s, Y_A // D_A), (y_s, Y_B // D_B), (y_s, Y_C // D_C), *merge_w)
    x1_p = _out_proj(merged_p, xp, dense['w_out'], dense['norm_mix_post'], layer)
    x1_s = _out_proj(merged_s, xs, dense['w_out'], dense['norm_mix_post'], layer)

    ffn_w = (dense['norm_ffn_pre'], dense['ffn_w_gate'], dense['ffn_w_up'], dense['ffn_conv_w'],
             dense['ffn_w_down'], dense['norm_ffn_post'], layer)
    x2_p, gtail = _ffn(x1_p, 0, np_rows, seq, *ffn_w)
    x2_s, g_new = _ffn(x1_s, 0, nsamp, 1, *ffn_w, bufs=(f_conv[:, 0], f_conv[:, 1]))
    x3_p = _ple(x2_p, dense['p_prompt'], dense['ple_w_gate'], dense['ple_w_proj'], layer)
    x3_s = _ple(x2_s, dense['p_sample'], dense['ple_w_gate'], dense['ple_w_proj'], layer)

    tiles_per_seq = gtail.shape[0] // bsz
    qkv_tail = proj.reshape(bsz, seq, N_PROJ)[:, seq - (GDN_CONV - 1):, COL_QKV * LANE:COL_QKV * LANE + 3 * D_C]
    st_p = (sre_p.reshape(bsz, S5_GROUPS, S5_STATE), sim_p.reshape(bsz, S5_GROUPS, S5_STATE),
            c_p, n_p, m_p[:, 0, :H_B], gs_p, qkv_tail,
            gtail.reshape(bsz, tiles_per_seq, SUBLANE, D_FF)[:, -1, SUBLANE - (FFN_CONV - 1):, :])
    st_s = (sre_s.reshape(nsamp, S5_GROUPS, S5_STATE), sim_s.reshape(nsamp, S5_GROUPS, S5_STATE),
            None, n_s, m_s, None, jnp.swapaxes(gbuf_s, 0, 1),
            jnp.stack([f_conv[:, 1], g_new], axis=1))
    return x3_p, x3_s, st_p, st_s, (c_s, gs_s)


def kernel(x_prompt, x_sample, p_prompt, p_sample, state_ssm_re, state_ssm_im, state_mlstm_c, state_mlstm_n, state_mlstm_m, state_gdn_s, state_gdn_conv, state_ffn_conv, norm_mix_pre, norm_mix_post, norm_ffn_pre, norm_ffn_post, w_in, s5_a_re, s5_a_im, s5_log_dt, s5_b_re, s5_b_im, s5_c_re, s5_c_im, s5_d, s5_w_glu, mlstm_b_i, mlstm_b_f, mlstm_norm, gdn_conv_w, gdn_a_log, gdn_dt_bias, gdn_norm, w_branch_a, w_branch_b, w_branch_c, w_gate, w_out, ffn_w_gate, ffn_w_up, ffn_conv_w, ffn_w_down, ple_w_proj, ple_w_gate):
    bsz, seq, d = x_prompt.shape
    nsamp = x_sample.shape[0]
    depth = w_in.shape[0]
    small = dict(
        s5_a_re=s5_a_re, s5_a_im=s5_a_im, s5_log_dt=s5_log_dt, s5_b_re=s5_b_re, s5_b_im=s5_b_im,
        s5_c_re=s5_c_re, s5_c_im=s5_c_im, s5_d=s5_d, s5_w_glu=s5_w_glu, mlstm_b_i=mlstm_b_i,
        mlstm_b_f=mlstm_b_f, mlstm_norm=mlstm_norm, gdn_conv_w=gdn_conv_w, gdn_a_log=gdn_a_log,
        gdn_dt_bias=gdn_dt_bias, gdn_norm=gdn_norm)
    row = lambda w: w.reshape(depth, 1, -1)
    w_in_mid, w_in_small = _split_w_in(w_in)
    dense = dict(
        norm_mix_pre=row(norm_mix_pre), norm_mix_post=row(norm_mix_post), norm_ffn_pre=row(norm_ffn_pre),
        norm_ffn_post=row(norm_ffn_post), w_in=w_in, w_in_mid=w_in_mid, w_in_small=w_in_small,
        w_gate=w_gate, w_branch_a=w_branch_a, w_branch_b=w_branch_b, w_branch_c=w_branch_c, w_out=w_out.astype(BF16), ffn_w_gate=ffn_w_gate.astype(BF16),
        ffn_w_up=ffn_w_up.astype(BF16), ffn_conv_w=ffn_conv_w, ffn_w_down=ffn_w_down.astype(BF16),
        ple_w_gate=ple_w_gate.astype(BF16), ple_w_proj=ple_w_proj.astype(BF16),
        p_prompt=p_prompt.reshape(depth, bsz * seq, -1), p_sample=p_sample.reshape(depth, nsamp, -1))
    xp = x_prompt.reshape(bsz * seq, d)
    xs = x_sample.reshape(nsamp, d)
    sp_all, ss_all = [], []
    big = None
    for i in range(depth):
        lw = {k: v[i] for k, v in small.items()}
        state = (state_ssm_re[i], state_ssm_im[i], state_mlstm_c, state_mlstm_n[i], state_mlstm_m[i],
                 state_gdn_s, state_gdn_conv[i], state_ffn_conv[i])
        xp, xs, st_p, st_s, big = _layer(i, xp, xs, bsz, seq, dense, lw, state, big)
        sp_all.append(st_p)
        ss_all.append(st_s)
    stack = lambda sts, j: jnp.stack([s[j] for s in sts], axis=0)
    sample_states = [big[0] if j == 2 else big[1] if j == 5 else stack(ss_all, j) for j in range(8)]
    return ((xp.reshape(bsz, seq, d), xs.reshape(nsamp, 1, d))
            + tuple(stack(sp_all, j) for j in range(8))
            + tuple(sample_states))
```

```python
import functools

import jax
import jax.numpy as jnp
from jax import lax
from jax.experimental import pallas as pl
from jax.experimental.pallas import tpu as pltpu

F32 = jnp.float32
BF16 = jnp.bfloat16

D_MODEL = 2048
DEPTH = 2
D_A = 512
S5_GROUP = 16
S5_GROUPS = 32
S5_STATE = 64
S5_LANES = S5_GROUPS * S5_STATE
S5_CHUNK = 512
S5_NCHUNK = S5_LANES // S5_CHUNK
D_B = 1024
H_B = 4
DV_B = 256
DQK_B = 128
D_C = 512
H_C = 4
DK_C = 128
DV_C = 128
GDN_CONV = 4
D_FF = 8192
FFN_CONV = 3
PLE_DIM = 256
EPS = 1e-6

LANE = 128
SUBLANE = 8
VMEM_LIMIT = 56 * 1024 * 1024

COL_U, COL_QB, COL_KB, COL_VB, COL_OB, COL_QKV, COL_Z, COL_SMALL = 0, 4, 8, 12, 20, 28, 40, 44
N_HEAD = COL_OB * LANE
N_MID = (COL_SMALL - COL_OB) * LANE
PROJ_TILE = 512
N_PROJ = N_HEAD + N_MID + PROJ_TILE
SM_I, SM_F, SM_BETA, SM_A = 0, 4, 8, 12

MLSTM_CHUNK = 128
MLSTM_TILE = 256
GDN_CHUNK = 64
GDN_TILE = 256
S5_TILE = 256
SAMPLE_BLOCK = 8
Y_B, Y_A, Y_C = 0, D_B, D_B + D_A

NT = (((1,), (1,)), ((), ()))
TN = (((0,), (0,)), ((), ()))


def _cparams(sem):
    return pltpu.CompilerParams(dimension_semantics=sem, vmem_limit_bytes=VMEM_LIMIT)


def _dot(a, b):
    return jnp.dot(a, b, preferred_element_type=F32)


def _dot_hi(a, b):
    return jnp.dot(a, b, preferred_element_type=F32, precision=lax.Precision.HIGHEST)


def _gelu(x):
    return 0.5 * x * (1.0 + jnp.tanh(0.7978845608028654 * (x + 0.044715 * (x * x * x))))


def _sigmoid(x):
    return 1.0 / (1.0 + jnp.exp(-x))


def _silu(x):
    return x * _sigmoid(x)


def _softplus(x):
    return jnp.maximum(x, 0.0) + jnp.log1p(jnp.exp(-jnp.abs(x)))


def _log_sigmoid(x):
    return -_softplus(-x)


def _rms(x, w):
    return x * lax.rsqrt(jnp.mean(x * x, axis=-1, keepdims=True) + EPS) * w


def _layered(layer, shape, imap):
    return pl.BlockSpec((None,) + shape, lambda *g: (layer,) + imap(*g))


def _pick(n, cands):
    for c in cands:
        if n % c == 0:
            return c
    return n


def _norm_proj_kernel(x_ref, nw_ref, wa_ref, wb_ref, ws_ref, proj_ref, h_ref, *, na, nb):
    j = pl.program_id(1)

    @pl.when(j == 0)
    def _():
        h_ref[...] = _rms(x_ref[...], nw_ref[...]).astype(BF16)

    @pl.when(j < na)
    def _():
        proj_ref[...] = _dot(h_ref[...], wa_ref[...].astype(BF16))

    @pl.when(jnp.logical_and(j >= na, j < na + nb))
    def _():
        proj_ref[...] = _dot(h_ref[...], wb_ref[...].astype(BF16))

    @pl.when(j >= na + nb)
    def _():
        proj_ref[...] = jnp.zeros_like(proj_ref)
        proj_ref[:, 0:LANE] = _dot(h_ref[...], ws_ref[...].astype(BF16))


def _norm_proj(x, nw, w_in, w_mid, w_small, layer):
    m, d = x.shape
    tm = _pick(m, (1024, 512, 256, 128))
    tn = PROJ_TILE
    na, nb = N_HEAD // tn, N_MID // tn
    return pl.pallas_call(
        functools.partial(_norm_proj_kernel, na=na, nb=nb),
        grid=(m // tm, na + nb + 1),
        in_specs=[pl.BlockSpec((tm, d), lambda i, j: (i, 0)),
                  _layered(layer, (1, d), lambda i, j: (0, 0)),
                  _layered(layer, (d, tn), lambda i, j: (0, jnp.minimum(j, na - 1))),
                  _layered(layer, (d, tn), lambda i, j: (0, jnp.clip(j - na, 0, nb - 1))),
                  _layered(layer, (d, LANE), lambda i, j: (0, 0))],
        out_specs=[pl.BlockSpec((tm, tn), lambda i, j: (i, j)),
                   pl.BlockSpec((tm, d), lambda i, j: (i, 0))],
        out_shape=[jax.ShapeDtypeStruct((m, N_PROJ), F32), jax.ShapeDtypeStruct((m, d), BF16)],
        compiler_params=_cparams(("arbitrary", "arbitrary")),
        name="norm_proj",
    )(x, nw, w_in, w_mid, w_small)


def _s5_disc(are, aim, ldt):
    dt = jnp.exp(ldt)
    mag = jnp.exp(dt * are)
    abr = mag * jnp.cos(dt * aim)
    abi = mag * jnp.sin(dt * aim)
    den = are * are + aim * aim
    zr = abr - 1.0
    fre = (zr * are + abi * aim) / den
    fim = (abi * are - zr * aim) / den
    return abr, abi, fre, fim


def _s5_glu(y, u, d_ref, wglu_ref):
    z = _gelu(y + d_ref[...] * u)
    return z * _sigmoid(_dot(z.astype(BF16), wglu_ref[...]))


def _s5_prompt_kernel(u_ref, are_ref, aim_ref, ldt_ref, bre_ref, bim_ref, cre_ref, cim_ref,
                      d_ref, wglu_ref, y_ref, sre_ref, sim_ref, xr_s, xi_s, y_s, car_re, car_im,
                      f_s, tab_s):
    @pl.when(pl.program_id(1) == 0)
    def _():
        car_re[...] = jnp.zeros_like(car_re)
        car_im[...] = jnp.zeros_like(car_im)
        row = lax.broadcasted_iota(jnp.int32, (SUBLANE, S5_LANES), 0)
        abr, abi, fre, fim = _s5_disc(are_ref[...], aim_ref[...], ldt_ref[...])
        f_s[0:1, :] = fre
        f_s[1:2, :] = fim
        pr, pi = abr, abi
        for lvl, s in enumerate((1, 2, 4)):
            tab_s[2 * lvl] = jnp.where(row >= s, pr, 0.0)
            tab_s[2 * lvl + 1] = jnp.where(row >= s, pi, 0.0)
            pr, pi = pr * pr - pi * pi, 2.0 * pr * pi
        cwr = jnp.zeros((SUBLANE, S5_LANES), F32)
        cwi = jnp.zeros((SUBLANE, S5_LANES), F32)
        pr, pi = abr, abi
        for r in range(SUBLANE):
            cwr = jnp.where(row == r, pr, cwr)
            cwi = jnp.where(row == r, pi, cwi)
            pr, pi = pr * abr - pi * abi, pr * abi + pi * abr
        tab_s[6] = cwr
        tab_s[7] = cwi

    tt = u_ref.shape[0]
    u = u_ref[...]
    ub = u.astype(BF16)
    for j in range(S5_NCHUNK):
        sl = slice(S5_CHUNK * j, S5_CHUNK * (j + 1))
        fre, fim = f_s[0:1, sl], f_s[1:2, sl]
        uj = ub[:, LANE * j:LANE * (j + 1)]
        bur = _dot(uj, bre_ref[j])
        bui = _dot(uj, bim_ref[j])
        xr = (fre * bur - fim * bui).reshape(tt // SUBLANE, SUBLANE, S5_CHUNK)
        xi = (fre * bui + fim * bur).reshape(tt // SUBLANE, SUBLANE, S5_CHUNK)
        for lvl, s in enumerate((1, 2, 4)):
            mr = tab_s[2 * lvl, :, sl]
            mi = tab_s[2 * lvl + 1, :, sl]
            sr = pltpu.roll(xr, s, axis=1)
            si = pltpu.roll(xi, s, axis=1)
            xr, xi = xr + mr * sr - mi * si, xi + mr * si + mi * sr
        xr_s[...] = xr.reshape(tt, S5_CHUNK)
        xi_s[...] = xi.reshape(tt, S5_CHUNK)
        cwr = tab_s[6, :, sl]
        cwi = tab_s[7, :, sl]

        def body(g, carry, cwr=cwr, cwi=cwi):
            cr, ci = carry
            r0 = pl.multiple_of(g * SUBLANE, SUBLANE)
            gr = xr_s[pl.ds(r0, SUBLANE), :] + cwr * cr - cwi * ci
            gi = xi_s[pl.ds(r0, SUBLANE), :] + cwr * ci + cwi * cr
            xr_s[pl.ds(r0, SUBLANE), :] = gr
            xi_s[pl.ds(r0, SUBLANE), :] = gi
            return gr[SUBLANE - 1:SUBLANE, :], gi[SUBLANE - 1:SUBLANE, :]

        cr, ci = lax.fori_loop(0, tt // SUBLANE, body, (car_re[:, sl], car_im[:, sl]), unroll=4)
        car_re[:, sl] = cr
        car_im[:, sl] = ci
        y_s[:, LANE * j:LANE * (j + 1)] = (_dot(xr_s[...].astype(BF16), cre_ref[j])
                                           - _dot(xi_s[...].astype(BF16), cim_ref[j]))
    y_ref[...] = _s5_glu(y_s[...], u, d_ref, wglu_ref).astype(BF16)
    sre_ref[0] = car_re[...]
    sim_ref[0] = car_im[...]


def _s5_prompt(proj, bsz, seq, sp):
    tt = _pick(seq, (S5_TILE, 128, 64, 32, 16, 8))
    nt = seq // tt
    const2 = lambda b, t: (0, 0)
    const3 = lambda b, t: (0, 0, 0)
    row_spec = pl.BlockSpec((1, S5_LANES), const2)
    return pl.pallas_call(
        _s5_prompt_kernel,
        grid=(bsz, nt),
        in_specs=[pl.BlockSpec((tt, D_A), lambda b, t: (b * nt + t, COL_U)),
                  row_spec, row_spec, row_spec,
                  pl.BlockSpec((S5_NCHUNK, LANE, S5_CHUNK), const3),
                  pl.BlockSpec((S5_NCHUNK, LANE, S5_CHUNK), const3),
                  pl.BlockSpec((S5_NCHUNK, S5_CHUNK, LANE), const3),
                  pl.BlockSpec((S5_NCHUNK, S5_CHUNK, LANE), const3),
                  pl.BlockSpec((1, D_A), const2),
                  pl.BlockSpec((D_A, D_A), const2)],
        out_specs=[pl.BlockSpec((tt, D_A), lambda b, t: (b * nt + t, 0)),
                   pl.BlockSpec((1, 1, S5_LANES), lambda b, t: (b, 0, 0)),
                   pl.BlockSpec((1, 1, S5_LANES), lambda b, t: (b, 0, 0))],
        out_shape=[jax.ShapeDtypeStruct((bsz * seq, D_A), BF16),
                   jax.ShapeDtypeStruct((bsz, 1, S5_LANES), F32),
                   jax.ShapeDtypeStruct((bsz, 1, S5_LANES), F32)],
        scratch_shapes=[pltpu.VMEM((tt, S5_CHUNK), F32), pltpu.VMEM((tt, S5_CHUNK), F32),
                        pltpu.VMEM((tt, D_A), F32),
                        pltpu.VMEM((1, S5_LANES), F32), pltpu.VMEM((1, S5_LANES), F32),
                        pltpu.VMEM((2, S5_LANES), F32), pltpu.VMEM((8, SUBLANE, S5_LANES), F32)],
        compiler_params=_cparams(("arbitrary", "arbitrary")),
        name="s5_prompt",
    )(proj, sp["are"], sp["aim"], sp["ldt"], sp["bre"], sp["bim"], sp["cre"], sp["cim"],
      sp["d"], sp["wglu"])


def _mlstm_prompt_kernel(q_ref, k_ref, v0_ref, v1_ref, o0_ref, o1_ref, sm_ref, bias_ref, nw_ref,
                         y_ref, c_ref, n_ref, m_ref, *, chunk):
    @pl.when(pl.program_id(1) == 0)
    def _():
        c_ref[...] = jnp.zeros_like(c_ref)
        n_ref[...] = jnp.zeros_like(n_ref)
        m_ref[...] = jnp.zeros_like(m_ref)

    tt = q_ref.shape[0]
    c = chunk
    nchunk = tt // c
    smb = sm_ref[...] + bias_ref[...]
    lf_all = _log_sigmoid(smb)
    li_t = smb.T
    lf_t = lf_all.T
    rowi = lax.broadcasted_iota(jnp.int32, (c, c), 0)
    coli = lax.broadcasted_iota(jnp.int32, (c, c), 1)
    causal = rowi >= coli
    lane = lax.broadcasted_iota(jnp.int32, (1, LANE), 1)
    m_row = m_ref[0]
    probs = [(ci, h) for ci in range(nchunk) for h in range(H_B)]
    idx = {p: i for i, p in enumerate(probs)}
    rows = lambda ci: slice(c * ci, c * (ci + 1))
    vo_refs = [(v0_ref, o0_ref) if h < 2 else (v1_ref, o1_ref) for h in range(H_B)]
    q = [q_ref[rows(ci), DQK_B * h:DQK_B * (h + 1)] * (DQK_B ** -0.5) for ci, h in probs]
    k = [k_ref[rows(ci), DQK_B * h:DQK_B * (h + 1)] for ci, h in probs]
    qb = [x.astype(BF16) for x in q]
    kb = [x.astype(BF16) for x in k]
    vb = [vo_refs[h][0][rows(ci), DV_B * (h % 2):DV_B * (h % 2 + 1)].astype(BF16) for ci, h in probs]
    qkt = [lax.dot_general(a, b, NT, preferred_element_type=F32) for a, b in zip(qb, kb)]
    li_c = [smb[rows(ci), SM_I + h:SM_I + h + 1] for ci, h in probs]
    li_r = [li_t[SM_I + h:SM_I + h + 1, rows(ci)] for ci, h in probs]
    bc_c = [jnp.sum(jnp.where(causal, lf_t[SM_F + h:SM_F + h + 1, rows(ci)], 0.0), axis=1, keepdims=True)
            for ci, h in probs]
    bc_r = [jnp.sum(jnp.where(rowi <= coli, lf_all[rows(ci), SM_F + h:SM_F + h + 1], 0.0), axis=0, keepdims=True)
            for ci, h in probs]
    dmat = [jnp.where(causal, a - b + r, -jnp.inf) for a, b, r in zip(bc_c, bc_r, li_r)]
    dmax = [jnp.max(x, axis=1, keepdims=True) for x in dmat]
    b_last = [x[c - 1:c, :] for x in bc_c]
    expo = [bl - a + l for bl, a, l in zip(b_last, bc_c, li_c)]
    emax = [jnp.max(x, axis=0, keepdims=True) for x in expo]
    m_prev, m_new = [None] * len(probs), [None] * len(probs)
    for h in range(H_B):
        m = m_row[:, h:h + 1]
        for ci in range(nchunk):
            i = idx[ci, h]
            m_prev[i] = m
            m = jnp.maximum(b_last[i] + m, emax[i])
            m_new[i] = m
    inter = [a + m for a, m in zip(bc_c, m_prev)]
    m_t = [jnp.maximum(a, b) for a, b in zip(inter, dmax)]
    w_inter = [jnp.exp(a - b) for a, b in zip(inter, m_t)]
    s = [x * jnp.exp(d - m) for x, d, m in zip(qkt, dmat, m_t)]
    sv = [_dot(x.astype(BF16), v) for x, v in zip(s, vb)]
    ssum = [jnp.sum(x, axis=1, keepdims=True) for x in s]
    emt = [jnp.exp(-x) for x in m_t]
    decay = [jnp.exp(bl + mp - mn) for bl, mp, mn in zip(b_last, m_prev, m_new)]
    kw = [jnp.exp(e - mn) * kx for e, mn, kx in zip(expo, m_new, k)]
    kv = [lax.dot_general(x.astype(BF16), v, TN, preferred_element_type=F32) for x, v in zip(kw, vb)]
    ksum = [jnp.sum(x, axis=0, keepdims=True) for x in kw]

    cst = [c_ref[0, h] for h in range(H_B)]
    nrow = [n_ref[0, h:h + 1, :] for h in range(H_B)]
    for ci in range(nchunk):
        ids = [idx[ci, h] for h in range(H_B)]
        qc = [_dot(qb[i], cst[h].astype(BF16)) for h, i in enumerate(ids)]
        num = [sv[i] + w_inter[i] * qc[h] for h, i in enumerate(ids)]
        nq = [ssum[i] + w_inter[i] * jnp.sum(q[i] * nrow[h], axis=1, keepdims=True) for h, i in enumerate(ids)]
        hh = [num[h] / jnp.maximum(jnp.abs(nq[h]), emt[i]) for h, i in enumerate(ids)]
        cst = [decay[i] * cst[h] + kv[i] for h, i in enumerate(ids)]
        nrow = [decay[i] * nrow[h] + ksum[i] for h, i in enumerate(ids)]
        for h in range(H_B):
            og = vo_refs[h][1][rows(ci), DV_B * (h % 2):DV_B * (h % 2 + 1)]
            y_ref[rows(ci), DV_B * h:DV_B * (h + 1)] = (_rms(hh[h], nw_ref[h:h + 1, :]) * _sigmoid(og)).astype(BF16)
    m_out = m_row
    for h in range(H_B):
        c_ref[0, h] = cst[h]
        n_ref[0, h:h + 1, :] = nrow[h]
        m_out = jnp.where(lane == h, m_new[idx[nchunk - 1, h]], m_out)
    m_ref[0] = m_out


def _mlstm_prompt(proj, bsz, seq, mp):
    chunk = _pick(seq, (MLSTM_CHUNK,))
    c = _pick(seq, (MLSTM_TILE, MLSTM_CHUNK))
    nt = seq // c
    rows = lambda col: (lambda b, t: (b * nt + t, col))
    const2 = lambda b, t: (0, 0)
    return pl.pallas_call(
        functools.partial(_mlstm_prompt_kernel, chunk=chunk),
        grid=(bsz, nt),
        in_specs=[pl.BlockSpec((c, 512), rows(COL_QB // 4)),
                  pl.BlockSpec((c, 512), rows(COL_KB // 4)),
                  pl.BlockSpec((c, 512), rows(COL_VB // 4)),
                  pl.BlockSpec((c, 512), rows(COL_VB // 4 + 1)),
                  pl.BlockSpec((c, 512), rows(COL_OB // 4)),
                  pl.BlockSpec((c, 512), rows(COL_OB // 4 + 1)),
                  pl.BlockSpec((c, LANE), rows(COL_SMALL)),
                  pl.BlockSpec((1, LANE), const2),
                  pl.BlockSpec((H_B, DV_B), const2)],
        out_specs=[pl.BlockSpec((c, D_B), lambda b, t: (b * nt + t, 0)),
                   pl.BlockSpec((1, H_B, DQK_B, DV_B), lambda b, t: (b, 0, 0, 0)),
                   pl.BlockSpec((1, H_B, DQK_B), lambda b, t: (b, 0, 0)),
                   pl.BlockSpec((1, 1, LANE), lambda b, t: (b, 0, 0))],
        out_shape=[jax.ShapeDtypeStruct((bsz * seq, D_B), BF16),
                   jax.ShapeDtypeStruct((bsz, H_B, DQK_B, DV_B), F32),
                   jax.ShapeDtypeStruct((bsz, H_B, DQK_B), F32),
                   jax.ShapeDtypeStruct((bsz, 1, LANE), F32)],
        compiler_params=_cparams(("arbitrary", "arbitrary")),
        name="mlstm_prompt",
    )(proj, proj, proj, proj, proj, proj, proj, mp["bias"], mp["norm"])


def _split_bf16(a):
    hi = a.astype(BF16)
    return hi, (a - hi.astype(F32)).astype(BF16)


def _dot_x3(a, b):
    ah, al = _split_bf16(a)
    bh, bl = _split_bf16(b)
    return _dot(ah, bh) + _dot(ah, bl) + _dot(al, bh)


def _unit_lower_inverses(lmats):
    c = lmats[0].shape[0]
    eye = (lax.broadcasted_iota(jnp.int32, (c, c), 0) == lax.broadcasted_iota(jnp.int32, (c, c), 1)).astype(F32)
    hi_half = lax.broadcasted_iota(jnp.int32, (c, 2 * c), 1) >= c
    ms = [jnp.concatenate([-l, eye], axis=1) for l in lmats]
    span = 1
    while span < c:
        ms = [_dot_x3(m[:, 0:c], m) + jnp.where(hi_half, m, 0.0) for m in ms]
        span *= 2
    return [m[:, c:2 * c] for m in ms]


def _gdn_prompt_kernel(q_ref, k_ref, v_ref, z_ref, sm_ref, bias_ref, alog_ref, cw_ref, nw_ref,
                       y_ref, s_ref, xb_s, *, chunk):
    tt = q_ref.shape[0]
    c = chunk

    @pl.when(pl.program_id(1) == 0)
    def _():
        s_ref[...] = jnp.zeros_like(s_ref)
        xb_s[0:SUBLANE, :] = jnp.zeros((SUBLANE, 3 * D_C), F32)

    xb_s[SUBLANE:SUBLANE + tt, 0:D_C] = q_ref[...]
    xb_s[SUBLANE:SUBLANE + tt, D_C:2 * D_C] = k_ref[...]
    xb_s[SUBLANE:SUBLANE + tt, 2 * D_C:3 * D_C] = v_ref[...]
    conv = cw_ref[GDN_CONV - 1:GDN_CONV, :] * xb_s[SUBLANE:SUBLANE + tt, :]
    for j in range(GDN_CONV - 1):
        off = SUBLANE - (GDN_CONV - 1) + j
        conv = conv + cw_ref[j:j + 1, :] * xb_s[off:off + tt, :]
    xb_s[0:SUBLANE, :] = xb_s[tt:tt + SUBLANE, :]
    qkv = _silu(conv)

    sm = sm_ref[...]
    beta_all = _sigmoid(sm)
    g_all = -jnp.exp(alog_ref[...]) * _softplus(sm + bias_ref[...])
    g_t = g_all.T
    rowi = lax.broadcasted_iota(jnp.int32, (c, c), 0)
    coli = lax.broadcasted_iota(jnp.int32, (c, c), 1)
    incl = rowi >= coli
    strict = rowi > coli

    nchunk = tt // c
    probs = [(ci, h) for ci in range(nchunk) for h in range(H_C)]
    rows = lambda ci: slice(c * ci, c * (ci + 1))
    l2 = lambda x: x * lax.rsqrt(jnp.sum(x * x, axis=-1, keepdims=True) + EPS)
    q = [l2(qkv[rows(ci), DK_C * h:DK_C * (h + 1)]) * (DK_C ** -0.5) for ci, h in probs]
    k = [l2(qkv[rows(ci), D_C + DK_C * h:D_C + DK_C * (h + 1)]) for ci, h in probs]
    v = [qkv[rows(ci), 2 * D_C + DV_C * h:2 * D_C + DV_C * (h + 1)] for ci, h in probs]
    beta_c = [beta_all[rows(ci), SM_BETA + h:SM_BETA + h + 1] for ci, h in probs]
    gc_c = [jnp.sum(jnp.where(incl, g_t[SM_A + h:SM_A + h + 1, rows(ci)], 0.0), axis=1, keepdims=True)
            for ci, h in probs]
    gc_r = [jnp.sum(jnp.where(rowi <= coli, g_all[rows(ci), SM_A + h:SM_A + h + 1], 0.0), axis=0, keepdims=True)
            for ci, h in probs]
    gam = [jnp.exp(jnp.where(incl, a - b, -jnp.inf)) for a, b in zip(gc_c, gc_r)]
    qb = [x.astype(BF16) for x in q]
    kb = [x.astype(BF16) for x in k]
    kk = [lax.dot_general(x, x, NT, preferred_element_type=F32) for x in kb]
    qk = [(lax.dot_general(a, b, NT, preferred_element_type=F32) * g).astype(BF16) for a, b, g in zip(qb, kb, gam)]
    egc = [jnp.exp(x) for x in gc_c]
    tinv = _unit_lower_inverses([jnp.where(strict, b * g * x, 0.0) for b, g, x in zip(beta_c, gam, kk)])
    tr = [_dot_x3(t, jnp.concatenate([b * vv, (b * e) * kx], axis=1))
          for t, b, vv, e, kx in zip(tinv, beta_c, v, egc, k)]
    u0 = [x[:, 0:DV_C] for x in tr]
    wb = [x[:, DV_C:].astype(BF16) for x in tr]
    g_last = [x[c - 1:c, :] for x in gc_c]
    kw = [(jnp.exp(gl - gc) * kx).astype(BF16) for gl, gc, kx in zip(g_last, gc_c, k)]
    e_last = [jnp.exp(x) for x in g_last]

    st = [s_ref[0, h] for h in range(H_C)]
    for ci in range(nchunk):
        ids = [ci * H_C + h for h in range(H_C)]
        stb = [x.astype(BF16) for x in st]
        ub = [(u0[i] - _dot(wb[i], stb[h])).astype(BF16) for h, i in enumerate(ids)]
        qs = [_dot(qb[i], stb[h]) for h, i in enumerate(ids)]
        st = [e_last[i] * st[h] + lax.dot_general(kw[i], ub[h], TN, preferred_element_type=F32)
              for h, i in enumerate(ids)]
        o = [egc[i] * qs[h] + _dot(qk[i], ub[h]) for h, i in enumerate(ids)]
        for h in range(H_C):
            zz = z_ref[rows(ci), DV_C * h:DV_C * (h + 1)]
            y_ref[rows(ci), DV_C * h:DV_C * (h + 1)] = (_rms(o[h], nw_ref[...]) * _silu(zz)).astype(BF16)
    for h in range(H_C):
        s_ref[0, h] = st[h]


def _gdn_prompt(proj, bsz, seq, gp):
    c = _pick(seq, (GDN_CHUNK,))
    tt = _pick(seq, (GDN_TILE,))
    nt = seq // tt
    rows = lambda col: (lambda b, t: (b * nt + t, col))
    const2 = lambda b, t: (0, 0)
    return pl.pallas_call(
        functools.partial(_gdn_prompt_kernel, chunk=c),
        grid=(bsz, nt),
        in_specs=[pl.BlockSpec((tt, D_C), rows(COL_QKV // 4)),
                  pl.BlockSpec((tt, D_C), rows(COL_QKV // 4 + 1)),
                  pl.BlockSpec((tt, D_C), rows(COL_QKV // 4 + 2)),
                  pl.BlockSpec((tt, D_C), rows(COL_Z // 4)),
                  pl.BlockSpec((tt, LANE), rows(COL_SMALL)),
                  pl.BlockSpec((1, LANE), const2),
                  pl.BlockSpec((1, LANE), const2),
                  pl.BlockSpec((GDN_CONV, 3 * D_C), const2),
                  pl.BlockSpec((1, DV_C), const2)],
        out_specs=[pl.BlockSpec((tt, D_C), lambda b, t: (b * nt + t, 0)),
                   pl.BlockSpec((1, H_C, DK_C, DV_C), lambda b, t: (b, 0, 0, 0))],
        out_shape=[jax.ShapeDtypeStruct((bsz * seq, D_C), BF16),
                   jax.ShapeDtypeStruct((bsz, H_C, DK_C, DV_C), F32)],
        scratch_shapes=[pltpu.VMEM((SUBLANE + tt, 3 * D_C), F32)],
        compiler_params=_cparams(("arbitrary", "arbitrary")),
        name="gdn_prompt",
    )(proj, proj, proj, proj, proj, gp["bias"], gp["alog"], gp["conv_w"], gp["norm"])


def _to_col(row, eye):
    return jnp.sum(jnp.where(eye, row, 0.0), axis=1, keepdims=True)


N_SAMPLE_INPUTS = 23


def _sample_mixers_kernel(*refs):
    (proj_ref, sre_ref, sim_ref, c_ref, n_ref, m_ref, gs_ref, gbuf_ref,
     are_ref, aim_ref, ldt_ref, bre_ref, bim_ref, cre_ref, cim_ref, d_ref, wglu_ref,
     mbias_ref, mnorm_ref, gbias_ref, alog_ref, gcw_ref, gnorm_ref) = refs[:N_SAMPLE_INPUTS]
    (y_ref, sre_o, sim_o, c_o, n_o, m_o, gs_o, gbuf_o,
     qkv_s, qn_s, kn_s, beta_s, g_s, li_s, lf_s) = refs[-15:]
    bb = proj_ref.shape[0]

    u = proj_ref[:, COL_U * LANE:COL_U * LANE + D_A]
    ub = u.astype(BF16)
    ys = []
    for j in range(S5_NCHUNK):
        sl = slice(S5_CHUNK * j, S5_CHUNK * (j + 1))
        abr, abi, fre, fim = _s5_disc(are_ref[:, sl], aim_ref[:, sl], ldt_ref[:, sl])
        uj = ub[:, LANE * j:LANE * (j + 1)]
        bur = _dot(uj, bre_ref[j])
        bui = _dot(uj, bim_ref[j])
        s0r = sre_ref[:, sl]
        s0i = sim_ref[:, sl]
        xr = fre * bur - fim * bui + abr * s0r - abi * s0i
        xi = fre * bui + fim * bur + abr * s0i + abi * s0r
        sre_o[:, sl] = xr
        sim_o[:, sl] = xi
        ys.append(_dot(xr.astype(BF16), cre_ref[j]) - _dot(xi.astype(BF16), cim_ref[j]))
    y_a = jnp.concatenate(ys, axis=1)
    y_ref[:, Y_A:Y_A + D_A] = _s5_glu(y_a, u, d_ref, wglu_ref)

    sm = proj_ref[:, COL_SMALL * LANE:(COL_SMALL + 1) * LANE]
    smb = sm + mbias_ref[...]
    li_s[...] = smb
    lf_s[...] = _log_sigmoid(smb)
    beta_s[...] = _sigmoid(sm)
    g_s[...] = -jnp.exp(alog_ref[...]) * _softplus(sm + gbias_ref[...])

    xnew = proj_ref[:, COL_QKV * LANE:COL_QKV * LANE + 3 * D_C]
    conv = gcw_ref[GDN_CONV - 1:GDN_CONV, :] * xnew
    for j in range(GDN_CONV - 1):
        conv = conv + gcw_ref[j:j + 1, :] * gbuf_ref[j]
        if j > 0:
            gbuf_o[j - 1] = gbuf_ref[j]
    gbuf_o[GDN_CONV - 2] = xnew
    qkv = _silu(conv)
    qkv_s[...] = qkv
    for h in range(H_C):
        q = qkv[:, DK_C * h:DK_C * (h + 1)]
        k = qkv[:, D_C + DK_C * h:D_C + DK_C * (h + 1)]
        qn_s[:, DK_C * h:DK_C * (h + 1)] = q * lax.rsqrt(jnp.sum(q * q, axis=-1, keepdims=True) + EPS) * (DK_C ** -0.5)
        kn_s[:, DK_C * h:DK_C * (h + 1)] = k * lax.rsqrt(jnp.sum(k * k, axis=-1, keepdims=True) + EPS)

    assert 4 * H_B * bb == LANE and DQK_B == LANE and DK_C == LANE and H_B == H_C
    tiles = ([proj_ref[:, COL_QB * LANE + DQK_B * h:COL_QB * LANE + DQK_B * (h + 1)] * (DQK_B ** -0.5)
              for h in range(H_B)]
             + [proj_ref[:, COL_KB * LANE + DQK_B * h:COL_KB * LANE + DQK_B * (h + 1)] for h in range(H_B)]
             + [qn_s[:, DK_C * h:DK_C * (h + 1)] for h in range(H_C)]
             + [kn_s[:, DK_C * h:DK_C * (h + 1)] for h in range(H_C)])
    stacked_t = jnp.concatenate(tiles, axis=0).T

    def col_of(kind, b, h):
        j = (kind * H_B + h) * bb + b
        return stacked_t[:, j:j + 1]

    lane = lax.broadcasted_iota(jnp.int32, (1, H_B), 1)

    probs = [(b, h) for b in range(bb) for h in range(H_B)]
    rsl = lambda b: slice(b, b + 1)

    q = [proj_ref[rsl(b), COL_QB * LANE + DQK_B * h:COL_QB * LANE + DQK_B * (h + 1)] * (DQK_B ** -0.5)
         for b, h in probs]
    k = [proj_ref[rsl(b), COL_KB * LANE + DQK_B * h:COL_KB * LANE + DQK_B * (h + 1)] for b, h in probs]
    v = [proj_ref[rsl(b), COL_VB * LANE + DV_B * h:COL_VB * LANE + DV_B * (h + 1)] for b, h in probs]
    li = [li_s[rsl(b), SM_I + h:SM_I + h + 1] for b, h in probs]
    inter = [lf_s[rsl(b), SM_F + h:SM_F + h + 1] + m_ref[rsl(b), h:h + 1] for b, h in probs]
    m_t = [jnp.maximum(a, c) for a, c in zip(inter, li)]
    w_intra = [jnp.exp(a - c) for a, c in zip(li, m_t)]
    w_inter = [jnp.exp(a - c) for a, c in zip(inter, m_t)]
    qcol = [col_of(0, b, h) for b, h in probs]
    kcol = [col_of(1, b, h) for b, h in probs]
    s = [jnp.sum(a * c, axis=1, keepdims=True) * w for a, c, w in zip(q, k, w_intra)]
    cst = [c_ref[b, h] for b, h in probs]
    nrow = [n_ref[b, h:h + 1, :] for b, h in probs]
    qc = [jnp.sum(a * c, axis=0, keepdims=True) for a, c in zip(qcol, cst)]
    for i, (b, h) in enumerate(probs):
        c_o[b, h] = w_inter[i] * cst[i] + (w_intra[i] * kcol[i]) * v[i]
        n_o[b, h:h + 1, :] = w_inter[i] * nrow[i] + w_intra[i] * k[i]
    num = [s[i] * v[i] + w_inter[i] * qc[i] for i in range(len(probs))]
    nq = [s[i] + w_inter[i] * jnp.sum(q[i] * nrow[i], axis=1, keepdims=True) for i in range(len(probs))]
    hh = [num[i] / jnp.maximum(jnp.abs(nq[i]), jnp.exp(-m_t[i])) for i in range(len(probs))]
    for i, (b, h) in enumerate(probs):
        og = proj_ref[rsl(b), COL_OB * LANE + DV_B * h:COL_OB * LANE + DV_B * (h + 1)]
        y_ref[rsl(b), Y_B + DV_B * h:Y_B + DV_B * (h + 1)] = _rms(hh[i], mnorm_ref[h:h + 1, :]) * _sigmoid(og)
    for b in range(bb):
        m_out = m_ref[rsl(b), :]
        for h in range(H_B):
            m_out = jnp.where(lane == h, m_t[b * H_B + h], m_out)
        m_o[rsl(b), :] = m_out

    q = [qn_s[rsl(b), DK_C * h:DK_C * (h + 1)] for b, h in probs]
    k = [kn_s[rsl(b), DK_C * h:DK_C * (h + 1)] for b, h in probs]
    v = [qkv_s[rsl(b), 2 * D_C + DV_C * h:2 * D_C + DV_C * (h + 1)] for b, h in probs]
    beta = [beta_s[rsl(b), SM_BETA + h:SM_BETA + h + 1] for b, h in probs]
    eg = [jnp.exp(g_s[rsl(b), SM_A + h:SM_A + h + 1]) for b, h in probs]
    qcol = [col_of(2, b, h) for b, h in probs]
    kcol = [col_of(3, b, h) for b, h in probs]
    st = [gs_ref[b, h] for b, h in probs]
    ks = [jnp.sum(a * c, axis=0, keepdims=True) for a, c in zip(kcol, st)]
    qs = [jnp.sum(a * c, axis=0, keepdims=True) for a, c in zip(qcol, st)]
    un = [beta[i] * (v[i] - eg[i] * ks[i]) for i in range(len(probs))]
    for i, (b, h) in enumerate(probs):
        gs_o[b, h] = eg[i] * st[i] + kcol[i] * un[i]
    o = [eg[i] * qs[i] + jnp.sum(q[i] * k[i], axis=1, keepdims=True) * un[i] for i in range(len(probs))]
    for i, (b, h) in enumerate(probs):
        zz = proj_ref[rsl(b), COL_Z * LANE + DV_C * h:COL_Z * LANE + DV_C * (h + 1)]
        y_ref[rsl(b), Y_C + DV_C * h:Y_C + DV_C * (h + 1)] = _rms(o[i], gnorm_ref[...]) * _silu(zz)


def _sample_mixers(proj, row0, nrows, st, sp, mp, gp, layer, prev):
    bb = SAMPLE_BLOCK
    blk0 = row0 // bb
    depth = st["c"].shape[0]
    const2 = lambda i: (0, 0)
    const3 = lambda i: (0, 0, 0)
    row_spec = pl.BlockSpec((1, S5_LANES), const2)
    c_spec = _layered(layer, (bb, H_B, DQK_B, DV_B), lambda i: (i, 0, 0, 0))
    gs_spec = _layered(layer, (bb, H_C, DK_C, DV_C), lambda i: (i, 0, 0, 0))
    in_specs = [
        pl.BlockSpec((bb, N_PROJ), lambda i: (blk0 + i, 0)),
        pl.BlockSpec((bb, S5_LANES), lambda i: (i, 0)),
        pl.BlockSpec((bb, S5_LANES), lambda i: (i, 0)),
        c_spec,
        pl.BlockSpec((bb, H_B, DQK_B), lambda i: (i, 0, 0)),
        pl.BlockSpec((bb, H_B), lambda i: (i, 0)),
        gs_spec,
        pl.BlockSpec((GDN_CONV - 1, bb, 3 * D_C), lambda i: (0, i, 0)),
        row_spec, row_spec, row_spec,
        pl.BlockSpec((S5_NCHUNK, LANE, S5_CHUNK), const3),
        pl.BlockSpec((S5_NCHUNK, LANE, S5_CHUNK), const3),
        pl.BlockSpec((S5_NCHUNK, S5_CHUNK, LANE), const3),
        pl.BlockSpec((S5_NCHUNK, S5_CHUNK, LANE), const3),
        pl.BlockSpec((1, D_A), const2),
        pl.BlockSpec((D_A, D_A), const2),
        pl.BlockSpec((1, LANE), const2),
        pl.BlockSpec((H_B, DV_B), const2),
        pl.BlockSpec((1, LANE), const2),
        pl.BlockSpec((1, LANE), const2),
        pl.BlockSpec((GDN_CONV, 3 * D_C), const2),
        pl.BlockSpec((1, DV_C), const2),
    ]
    out_specs = [
        pl.BlockSpec((bb, D_MODEL), lambda i: (i, 0)),
        pl.BlockSpec((bb, S5_LANES), lambda i: (i, 0)),
        pl.BlockSpec((bb, S5_LANES), lambda i: (i, 0)),
        c_spec,
        pl.BlockSpec((bb, H_B, DQK_B), lambda i: (i, 0, 0)),
        pl.BlockSpec((bb, H_B), lambda i: (i, 0)),
        gs_spec,
        pl.BlockSpec((GDN_CONV - 1, bb, 3 * D_C), lambda i: (0, i, 0)),
    ]
    out_shape = [
        jax.ShapeDtypeStruct((nrows, D_MODEL), F32),
        jax.ShapeDtypeStruct((nrows, S5_LANES), F32),
        jax.ShapeDtypeStruct((nrows, S5_LANES), F32),
        jax.ShapeDtypeStruct((depth, nrows, H_B, DQK_B, DV_B), F32),
        jax.ShapeDtypeStruct((nrows, H_B, DQK_B), F32),
        jax.ShapeDtypeStruct((nrows, H_B), F32),
        jax.ShapeDtypeStruct((depth, nrows, H_C, DK_C, DV_C), F32),
        jax.ShapeDtypeStruct((GDN_CONV - 1, nrows, 3 * D_C), F32),
    ]
    scratch = [pltpu.VMEM((bb, 3 * D_C), F32), pltpu.VMEM((bb, D_C), F32), pltpu.VMEM((bb, D_C), F32),
               pltpu.VMEM((bb, LANE), F32), pltpu.VMEM((bb, LANE), F32),
               pltpu.VMEM((bb, LANE), F32), pltpu.VMEM((bb, LANE), F32)]
    args = [proj, st["sre"], st["sim"], st["c"], st["n"], st["m"], st["gs"], st["gbuf"],
            sp["are"], sp["aim"], sp["ldt"], sp["bre"], sp["bim"], sp["cre"], sp["cim"], sp["d"], sp["wglu"],
            mp["bias"], mp["norm"], gp["bias"], gp["alog"], gp["conv_w"], gp["norm"]]
    assert len(args) == N_SAMPLE_INPUTS
    aliases = {}
    if prev is not None:
        in_specs += [pl.BlockSpec(memory_space=pl.ANY)] * 2
        args += list(prev)
        aliases = {N_SAMPLE_INPUTS: 3, N_SAMPLE_INPUTS + 1: 6}
    return pl.pallas_call(
        _sample_mixers_kernel,
        grid=(nrows // bb,),
        in_specs=in_specs, out_specs=out_specs, out_shape=out_shape, scratch_shapes=scratch,
        input_output_aliases=aliases,
        compiler_params=_cparams(("arbitrary",)),
        name="sample_mixers",
    )(*args)


def _merge_kernel(h_ref, ya_ref, yb_ref, yc_ref, wg0_ref, wg1_ref, wg2_ref, wa_ref, wb_ref, wc_ref, o_ref):
    h = h_ref[...]
    bf = lambda ref: ref[...].astype(BF16)
    acc = _sigmoid(_dot(h, bf(wg0_ref))) * _dot(bf(ya_ref), bf(wa_ref))
    acc = acc + _sigmoid(_dot(h, bf(wg1_ref))) * _dot(bf(yb_ref), bf(wb_ref))
    acc = acc + _sigmoid(_dot(h, bf(wg2_ref))) * _dot(bf(yc_ref), bf(wc_ref))
    o_ref[...] = acc.astype(BF16)


def _merge(h, ya, yb, yc, wg, wa, wb, wc, layer):
    m, d = h.shape
    tm = _pick(m, (1024, 512, 256, 128))
    tn = 256
    nb = d // tn
    lhs = lambda w, cb: pl.BlockSpec((tm, w), lambda i, j: (i, cb))
    gate = lambda g: _layered(layer, (d, tn), lambda i, j: (0, g * nb + j))
    rhs = lambda w: _layered(layer, (w, tn), lambda i, j: (0, j))
    return pl.pallas_call(
        _merge_kernel,
        grid=(m // tm, nb),
        in_specs=[lhs(d, 0), lhs(D_A, ya[1]), lhs(D_B, yb[1]), lhs(D_C, yc[1]), gate(0), gate(1), gate(2),
                  rhs(D_A), rhs(D_B), rhs(D_C)],
        out_specs=pl.BlockSpec((tm, tn), lambda i, j: (i, j)),
        out_shape=jax.ShapeDtypeStruct((m, d), BF16),
        compiler_params=_cparams(("arbitrary", "arbitrary")),
        name="merge",
    )(h, ya[0], yb[0], yc[0], wg, wg, wg, wa, wb, wc)


def _out_proj_kernel(a_ref, x_ref, w_ref, nw_ref, o_ref):
    o_ref[...] = x_ref[...] + _rms(_dot(a_ref[...], w_ref[...]), nw_ref[...])


def _out_proj(a, x, w, nw, layer):
    m, d = x.shape
    tm = _pick(m, (512, 256, 128))
    return pl.pallas_call(
        _out_proj_kernel,
        grid=(m // tm,),
        in_specs=[pl.BlockSpec((tm, d), lambda i: (i, 0)),
                  pl.BlockSpec((tm, d), lambda i: (i, 0)),
                  _layered(layer, (d, d), lambda i: (0, 0)),
                  _layered(layer, (1, d), lambda i: (0, 0))],
        out_specs=pl.BlockSpec((tm, d), lambda i: (i, 0)),
        out_shape=jax.ShapeDtypeStruct((m, d), F32),
        compiler_params=_cparams(("arbitrary",)),
        name="out_proj",
    )(a, x, w, nw)


def _ffn_kernel(*refs, sample, tiles_per_seq):
    if sample:
        (x_ref, nw_ref, wg_ref, wu_ref, cw_ref, wd_ref, pnw_ref, b0_ref, b1_ref,
         o_ref, g_ref, wg_o, wu_o, wd_o, h2_s) = refs
    else:
        (x_ref, nw_ref, wg_ref, wu_ref, cw_ref, wd_ref, pnw_ref,
         o_ref, g_ref, h2_s, gb_s, carry_s) = refs
    i = pl.program_id(0)
    j = pl.program_id(1)
    tm = x_ref.shape[0]

    @pl.when(j == 0)
    def _():
        h2_s[...] = _rms(x_ref[...], nw_ref[...]).astype(BF16)
        o_ref[...] = jnp.zeros_like(o_ref)

    h2 = h2_s[...]
    wg, wu, wd = wg_ref[...], wu_ref[...], wd_ref[...]
    if sample:
        wg, wu, wd = wg.astype(BF16), wu.astype(BF16), wd.astype(BF16)
        wg_o[0] = wg
        wu_o[0] = wu
        wd_o[0] = wd
    g = _dot(h2, wg)
    up = _dot(h2, wu)
    if sample:
        a = cw_ref[0:1, :] * b0_ref[...] + cw_ref[1:2, :] * b1_ref[...] + cw_ref[2:3, :] * g
        g_ref[...] = g
    else:
        prev = jnp.where(i % tiles_per_seq == 0, 0.0, carry_s[j])
        gb_s[0:SUBLANE, :] = prev
        gb_s[SUBLANE:SUBLANE + tm, :] = g
        a = (cw_ref[0:1, :] * gb_s[SUBLANE - 2:SUBLANE - 2 + tm, :]
             + cw_ref[1:2, :] * gb_s[SUBLANE - 1:SUBLANE - 1 + tm, :]
             + cw_ref[2:3, :] * g)
        tail = g[tm - SUBLANE:tm, :]
        carry_s[j] = tail
        g_ref[0] = tail
    act = (_gelu(a) * up).astype(BF16)
    o_ref[...] += _dot(act, wd)

    @pl.when(j == pl.num_programs(1) - 1)
    def _():
        o_ref[...] = x_ref[...] + _rms(o_ref[...], pnw_ref[...])


def _ffn(x, row0, nrows, seq, nw, wg, wu, cw, wd, pnw, layer, bufs=None):
    d = x.shape[1]
    f = wg.shape[-1]
    sample = bufs is not None
    tm = nrows if sample else _pick(seq, (512, 256, 128, 64, 32, 16, 8))
    tn = _pick(f, (512, 256, 128) if sample else (1024, 512, 256, 128))
    mt, ft = nrows // tm, f // tn
    blk0 = row0 // tm
    in_specs = [pl.BlockSpec((tm, d), lambda i, j: (blk0 + i, 0)),
                _layered(layer, (1, d), lambda i, j: (0, 0)),
                _layered(layer, (d, tn), lambda i, j: (0, j)),
                _layered(layer, (d, tn), lambda i, j: (0, j)),
                _layered(layer, (FFN_CONV, tn), lambda i, j: (0, j)),
                _layered(layer, (tn, d), lambda i, j: (j, 0)),
                _layered(layer, (1, d), lambda i, j: (0, 0))]
    args = [x, nw, wg, wu, cw, wd, pnw]
    scratch = [pltpu.VMEM((tm, d), BF16)]
    if sample:
        in_specs += [pl.BlockSpec((tm, tn), lambda i, j: (i, j))] * 2
        args += list(bufs)
        assert mt == 1, "the sample call must visit every weight tile exactly once"
        g_spec = pl.BlockSpec((tm, tn), lambda i, j: (i, j))
        g_shape = jax.ShapeDtypeStruct((nrows, f), F32)
        extra_specs = [pl.BlockSpec((1, d, tn), lambda i, j: (0, 0, j)),
                       pl.BlockSpec((1, d, tn), lambda i, j: (0, 0, j)),
                       pl.BlockSpec((1, tn, d), lambda i, j: (0, j, 0))]
        extra_shapes = [jax.ShapeDtypeStruct((1, d, f), BF16), jax.ShapeDtypeStruct((1, d, f), BF16),
                        jax.ShapeDtypeStruct((1, f, d), BF16)]
    else:
        extra_specs, extra_shapes = [], []
        scratch += [pltpu.VMEM((SUBLANE + tm, tn), F32), pltpu.VMEM((ft, SUBLANE, tn), F32)]
        g_spec = pl.BlockSpec((1, SUBLANE, tn), lambda i, j: (i, 0, j))
        g_shape = jax.ShapeDtypeStruct((mt, SUBLANE, f), F32)
    return pl.pallas_call(
        functools.partial(_ffn_kernel, sample=sample, tiles_per_seq=max(seq // tm, 1)),
        grid=(mt, ft),
        in_specs=in_specs,
        out_specs=[pl.BlockSpec((tm, d), lambda i, j: (i, 0)), g_spec] + extra_specs,
        out_shape=[jax.ShapeDtypeStruct((nrows, d), F32), g_shape] + extra_shapes,
        scratch_shapes=scratch,
        compiler_params=_cparams(("arbitrary", "arbitrary")),
        name="ffn_sample" if sample else "ffn_prompt",
    )(*args)


def _ple_kernel(x_ref, p_ref, wg_ref, wp_ref, o_ref):
    x = x_ref[...]
    gate = _sigmoid(_dot(x.astype(BF16), wg_ref[...]))
    o_ref[...] = x + gate * _dot(p_ref[...].astype(BF16), wp_ref[...])


def _ple(x, p, wg, wp, layer):
    m, d = x.shape
    pd = p.shape[-1]
    tm = _pick(m, (512, 256, 128))
    return pl.pallas_call(
        _ple_kernel,
        grid=(m // tm,),
        in_specs=[pl.BlockSpec((tm, d), lambda i: (i, 0)),
                  _layered(layer, (tm, pd), lambda i: (i, 0)),
                  _layered(layer, (d, d), lambda i: (0, 0)),
                  _layered(layer, (pd, d), lambda i: (0, 0))],
        out_specs=pl.BlockSpec((tm, d), lambda i: (i, 0)),
        out_shape=jax.ShapeDtypeStruct((m, d), F32),
        compiler_params=_cparams(("arbitrary",)),
        name="ple",
    )(x, p, wg, wp)


def _split_w_in(w):
    n_if = 2 * H_B
    w_head = w[..., 0:N_HEAD]
    w_mid = w[..., N_HEAD + n_if:N_HEAD + n_if + N_MID]
    gates = jnp.concatenate([w[..., N_HEAD:N_HEAD + n_if], w[..., N_HEAD + n_if + N_MID:]], axis=-1)
    pad = jnp.zeros(w.shape[:-1] + (LANE - gates.shape[-1],), w.dtype)
    return w_head.astype(BF16), w_mid.astype(BF16), jnp.concatenate([gates, pad], axis=-1).astype(BF16)


def _small_row(entries):
    row = jnp.zeros((LANE,), F32)
    for off, val in entries:
        row = row.at[off:off + val.shape[0]].set(val.astype(F32))
    return row.reshape(1, LANE)


def _s5_params(a_re, a_im, log_dt, b_re, b_im, c_re, c_im, d, w_glu):
    gpc = S5_CHUNK // S5_STATE
    eye = jnp.eye(gpc, dtype=F32)

    def bmat(b):
        b4 = b.reshape(S5_NCHUNK, gpc, S5_GROUP, S5_STATE)
        return jnp.einsum('jgcp,gh->jgchp', b4, eye).reshape(S5_NCHUNK, gpc * S5_GROUP, S5_CHUNK).astype(BF16)

    def cmat(c):
        c4 = c.reshape(S5_NCHUNK, gpc, S5_STATE, S5_GROUP)
        return jnp.einsum('jgpc,gh->jgphc', c4, eye).reshape(S5_NCHUNK, S5_CHUNK, gpc * S5_GROUP).astype(BF16)

    return dict(are=a_re.reshape(1, S5_LANES), aim=a_im.reshape(1, S5_LANES),
                ldt=jnp.broadcast_to(log_dt[:, None], (S5_GROUPS, S5_STATE)).reshape(1, S5_LANES),
                bre=bmat(b_re), bim=bmat(b_im), cre=cmat(c_re), cim=cmat(c_im),
                d=d.reshape(1, D_A), wglu=w_glu.astype(BF16))


def _layer(layer, xp, xs, bsz, seq, dense, lw, state, prev):
    np_rows = bsz * seq
    nsamp = xs.shape[0]
    in_w = (dense['norm_mix_pre'], dense['w_in'], dense['w_in_mid'], dense['w_in_small'], layer)
    proj, h_p = _norm_proj(xp, *in_w)
    proj_s, h_s = _norm_proj(xs, *in_w)

    sp = _s5_params(lw['s5_a_re'], lw['s5_a_im'], lw['s5_log_dt'], lw['s5_b_re'], lw['s5_b_im'],
                    lw['s5_c_re'], lw['s5_c_im'], lw['s5_d'], lw['s5_w_glu'])
    mp = dict(bias=_small_row([(SM_I, lw['mlstm_b_i']), (SM_F, lw['mlstm_b_f'])]), norm=lw['mlstm_norm'])
    gp = dict(bias=_small_row([(SM_A, lw['gdn_dt_bias'])]), alog=_small_row([(SM_A, lw['gdn_a_log'])]),
              conv_w=lw['gdn_conv_w'], norm=lw['gdn_norm'].reshape(1, DV_C))

    ya_p, sre_p, sim_p = _s5_prompt(proj, bsz, seq, sp)
    yb_p, c_p, n_p, m_p = _mlstm_prompt(proj, bsz, seq, mp)
    yc_p, gs_p = _gdn_prompt(proj, bsz, seq, gp)

    ssm_re, ssm_im, m_c, m_n, m_m, g_s, g_conv, f_conv = state
    st = dict(sre=ssm_re.reshape(nsamp, S5_LANES), sim=ssm_im.reshape(nsamp, S5_LANES),
              c=m_c, n=m_n, m=m_m, gs=g_s, gbuf=jnp.swapaxes(g_conv, 0, 1))
    y_s, sre_s, sim_s, c_s, n_s, m_s, gs_s, gbuf_s = _sample_mixers(proj_s, 0, nsamp, st, sp, mp, gp, layer, prev)

    merge_w = (dense['w_gate'], dense['w_branch_a'], dense['w_branch_b'], dense['w_branch_c'], layer)
    merged_p = _merge(h_p, (ya_p, 0), (yb_p, 0), (yc_p, 0), *merge_w)
    merged_s = _merge(h_s, (y_s, Y_A // D_A), (y_s, Y_B // D_B), (y_s, Y_C // D_C), *merge_w)
    x1_p = _out_proj(merged_p, xp, dense['w_out'], dense['norm_mix_post'], layer)
    x1_s = _out_proj(merged_s, xs, dense['w_out'], dense['norm_mix_post'], layer)

    x2_s, g_new, wg_b, wu_b, wd_b = _ffn(
        x1_s, 0, nsamp, 1, dense['norm_ffn_pre'], dense['ffn_w_gate'], dense['ffn_w_up'], dense['ffn_conv_w'],
        dense['ffn_w_down'], dense['norm_ffn_post'], layer, bufs=(f_conv[:, 0], f_conv[:, 1]))
    cw_l, nw_pre_l, nw_post_l = (dense[k][layer:layer + 1] for k in ('ffn_conv_w', 'norm_ffn_pre', 'norm_ffn_post'))
    x2_p, gtail = _ffn(x1_p, 0, np_rows, seq, nw_pre_l, wg_b, wu_b, cw_l, wd_b, nw_post_l, 0)
    x3_p = _ple(x2_p, dense['p_prompt'], dense['ple_w_gate'], dense['ple_w_proj'], layer)
    x3_s = _ple(x2_s, dense['p_sample'], dense['ple_w_gate'], dense['ple_w_proj'], layer)

    tiles_per_seq = gtail.shape[0] // bsz
    qkv_tail = proj.reshape(bsz, seq, N_PROJ)[:, seq - (GDN_CONV - 1):, COL_QKV * LANE:COL_QKV * LANE + 3 * D_C]
    st_p = (sre_p.reshape(bsz, S5_GROUPS, S5_STATE), sim_p.reshape(bsz, S5_GROUPS, S5_STATE),
            c_p, n_p, m_p[:, 0, :H_B], gs_p, qkv_tail,
            gtail.reshape(bsz, tiles_per_seq, SUBLANE, D_FF)[:, -1, SUBLANE - (FFN_CONV - 1):, :])
    st_s = (sre_s.reshape(nsamp, S5_GROUPS, S5_STATE), sim_s.reshape(nsamp, S5_GROUPS, S5_STATE),
            None, n_s, m_s, None, jnp.swapaxes(gbuf_s, 0, 1),
            jnp.stack([f_conv[:, 1], g_new], axis=1))
    return x3_p, x3_s, st_p, st_s, (c_s, gs_s)


def kernel(x_prompt, x_sample, p_prompt, p_sample, state_ssm_re, state_ssm_im, state_mlstm_c, state_mlstm_n, state_mlstm_m, state_gdn_s, state_gdn_conv, state_ffn_conv, norm_mix_pre, norm_mix_post, norm_ffn_pre, norm_ffn_post, w_in, s5_a_re, s5_a_im, s5_log_dt, s5_b_re, s5_b_im, s5_c_re, s5_c_im, s5_d, s5_w_glu, mlstm_b_i, mlstm_b_f, mlstm_norm, gdn_conv_w, gdn_a_log, gdn_dt_bias, gdn_norm, w_branch_a, w_branch_b, w_branch_c, w_gate, w_out, ffn_w_gate, ffn_w_up, ffn_conv_w, ffn_w_down, ple_w_proj, ple_w_gate):
    bsz, seq, d = x_prompt.shape
    nsamp = x_sample.shape[0]
    depth = w_in.shape[0]
    small = dict(
        s5_a_re=s5_a_re, s5_a_im=s5_a_im, s5_log_dt=s5_log_dt, s5_b_re=s5_b_re, s5_b_im=s5_b_im,
        s5_c_re=s5_c_re, s5_c_im=s5_c_im, s5_d=s5_d, s5_w_glu=s5_w_glu, mlstm_b_i=mlstm_b_i,
        mlstm_b_f=mlstm_b_f, mlstm_norm=mlstm_norm, gdn_conv_w=gdn_conv_w, gdn_a_log=gdn_a_log,
        gdn_dt_bias=gdn_dt_bias, gdn_norm=gdn_norm)
    row = lambda w: w.reshape(depth, 1, -1)
    w_in_head, w_in_mid, w_in_small = _split_w_in(w_in)
    dense = dict(
        norm_mix_pre=row(norm_mix_pre), norm_mix_post=row(norm_mix_post), norm_ffn_pre=row(norm_ffn_pre),
        norm_ffn_post=row(norm_ffn_post), w_in=w_in_head, w_in_mid=w_in_mid, w_in_small=w_in_small,
        w_gate=w_gate, w_branch_a=w_branch_a, w_branch_b=w_branch_b, w_branch_c=w_branch_c,
        w_out=w_out.astype(BF16), ffn_w_gate=ffn_w_gate, ffn_w_up=ffn_w_up, ffn_conv_w=ffn_conv_w,
        ffn_w_down=ffn_w_down,
        ple_w_gate=ple_w_gate.astype(BF16), ple_w_proj=ple_w_proj.astype(BF16),
        p_prompt=p_prompt.reshape(depth, bsz * seq, -1), p_sample=p_sample.reshape(depth, nsamp, -1))
    xp = x_prompt.reshape(bsz * seq, d)
    xs = x_sample.reshape(nsamp, d)
    sp_all, ss_all = [], []
    big = None
    for i in range(depth):
        lw = {k: v[i] for k, v in small.items()}
        state = (state_ssm_re[i], state_ssm_im[i], state_mlstm_c, state_mlstm_n[i], state_mlstm_m[i],
                 state_gdn_s, state_gdn_conv[i], state_ffn_conv[i])
        xp, xs, st_p, st_s, big = _layer(i, xp, xs, bsz, seq, dense, lw, state, big)
        sp_all.append(st_p)
        ss_all.append(st_s)
    stack = lambda sts, j: jnp.stack([s[j] for s in sts], axis=0)
    sample_states = [big[0] if j == 2 else big[1] if j == 5 else stack(ss_all, j) for j in range(8)]
    return ((xp.reshape(bsz, seq, d), xs.reshape(nsamp, 1, d))
            + tuple(stack(sp_all, j) for j in range(8))
            + tuple(sample_states))
```

```python
import functools

import jax
import jax.numpy as jnp
from jax import lax
from jax.experimental import pallas as pl
from jax.experimental.pallas import tpu as pltpu

F32 = jnp.float32
BF16 = jnp.bfloat16

D_MODEL = 2048
DEPTH = 2
D_A = 512
S5_GROUP = 16
S5_GROUPS = 32
S5_STATE = 64
S5_LANES = S5_GROUPS * S5_STATE
S5_CHUNK = 512
S5_NCHUNK = S5_LANES // S5_CHUNK
D_B = 1024
H_B = 4
DV_B = 256
DQK_B = 128
D_C = 512
H_C = 4
DK_C = 128
DV_C = 128
GDN_CONV = 4
D_FF = 8192
FFN_CONV = 3
PLE_DIM = 256
EPS = 1e-6

LANE = 128
SUBLANE = 8
VMEM_LIMIT = 56 * 1024 * 1024

COL_U, COL_QB, COL_KB, COL_VB, COL_OB, COL_QKV, COL_Z, COL_SMALL = 0, 4, 8, 12, 20, 28, 40, 44
N_HEAD = COL_OB * LANE
N_MID = (COL_SMALL - COL_OB) * LANE
PROJ_TILE = 512
N_PROJ = N_HEAD + N_MID + PROJ_TILE
SM_I, SM_F, SM_BETA, SM_A = 0, 4, 8, 12

MLSTM_CHUNK = 128
MLSTM_TILE = 256
GDN_CHUNK = 64
GDN_TILE = 256
S5_TILE = 256
SAMPLE_BLOCK = 8
Y_B, Y_A, Y_C = 0, D_B, D_B + D_A

NT = (((1,), (1,)), ((), ()))
TN = (((0,), (0,)), ((), ()))


def _cparams(sem):
    return pltpu.CompilerParams(dimension_semantics=sem, vmem_limit_bytes=VMEM_LIMIT)


def _dot(a, b):
    return jnp.dot(a, b, preferred_element_type=F32)


def _dot_hi(a, b):
    return jnp.dot(a, b, preferred_element_type=F32, precision=lax.Precision.HIGHEST)


def _gelu(x):
    return 0.5 * x * (1.0 + jnp.tanh(0.7978845608028654 * (x + 0.044715 * (x * x * x))))


def _sigmoid(x):
    return 1.0 / (1.0 + jnp.exp(-x))


def _silu(x):
    return x * _sigmoid(x)


def _softplus(x):
    return jnp.maximum(x, 0.0) + jnp.log1p(jnp.exp(-jnp.abs(x)))


def _log_sigmoid(x):
    return -_softplus(-x)


def _rms(x, w):
    return x * lax.rsqrt(jnp.mean(x * x, axis=-1, keepdims=True) + EPS) * w


def _layered(layer, shape, imap):
    return pl.BlockSpec((None,) + shape, lambda *g: (layer,) + imap(*g))


def _pick(n, cands):
    for c in cands:
        if n % c == 0:
            return c
    return n


def _norm_proj_kernel(x_ref, nw_ref, wa_ref, wb_ref, ws_ref, proj_ref, h_ref, *, na, nb):
    j = pl.program_id(1)

    @pl.when(j == 0)
    def _():
        h_ref[...] = _rms(x_ref[...], nw_ref[...]).astype(BF16)

    @pl.when(j < na)
    def _():
        proj_ref[...] = _dot(h_ref[...], wa_ref[...].astype(BF16))

    @pl.when(jnp.logical_and(j >= na, j < na + nb))
    def _():
        proj_ref[...] = _dot(h_ref[...], wb_ref[...].astype(BF16))

    @pl.when(j >= na + nb)
    def _():
        proj_ref[...] = jnp.zeros_like(proj_ref)
        proj_ref[:, 0:LANE] = _dot(h_ref[...], ws_ref[...].astype(BF16))


def _norm_proj(x, nw, w_in, w_mid, w_small, layer):
    m, d = x.shape
    tm = _pick(m, (1024, 512, 256, 128))
    tn = PROJ_TILE
    na, nb = N_HEAD // tn, N_MID // tn
    return pl.pallas_call(
        functools.partial(_norm_proj_kernel, na=na, nb=nb),
        grid=(m // tm, na + nb + 1),
        in_specs=[pl.BlockSpec((tm, d), lambda i, j: (i, 0)),
                  _layered(layer, (1, d), lambda i, j: (0, 0)),
                  _layered(layer, (d, tn), lambda i, j: (0, jnp.minimum(j, na - 1))),
                  _layered(layer, (d, tn), lambda i, j: (0, jnp.clip(j - na, 0, nb - 1))),
                  _layered(layer, (d, LANE), lambda i, j: (0, 0))],
        out_specs=[pl.BlockSpec((tm, tn), lambda i, j: (i, j)),
                   pl.BlockSpec((tm, d), lambda i, j: (i, 0))],
        out_shape=[jax.ShapeDtypeStruct((m, N_PROJ), F32), jax.ShapeDtypeStruct((m, d), BF16)],
        compiler_params=_cparams(("arbitrary", "arbitrary")),
        name="norm_proj",
    )(x, nw, w_in, w_mid, w_small)


def _s5_disc(are, aim, ldt):
    dt = jnp.exp(ldt)
    mag = jnp.exp(dt * are)
    abr = mag * jnp.cos(dt * aim)
    abi = mag * jnp.sin(dt * aim)
    den = are * are + aim * aim
    zr = abr - 1.0
    fre = (zr * are + abi * aim) / den
    fim = (abi * are - zr * aim) / den
    return abr, abi, fre, fim


def _s5_glu(y, u, d_ref, wglu_ref):
    z = _gelu(y + d_ref[...] * u)
    return z * _sigmoid(_dot(z.astype(BF16), wglu_ref[...]))


def _s5_prompt_body(step, u_ref, are_ref, aim_ref, ldt_ref, bre_ref, bim_ref, cre_ref, cim_ref,
                    d_ref, wglu_ref, y_ref, sre_ref, sim_ref, xr_s, xi_s, y_s, car_re, car_im,
                    f_s, tab_s, *, nt):
    @pl.when(step % nt == 0)
    def _():
        car_re[...] = jnp.zeros_like(car_re)
        car_im[...] = jnp.zeros_like(car_im)
        row = lax.broadcasted_iota(jnp.int32, (SUBLANE, S5_LANES), 0)
        abr, abi, fre, fim = _s5_disc(are_ref[...], aim_ref[...], ldt_ref[...])
        f_s[0:1, :] = fre
        f_s[1:2, :] = fim
        pr, pi = abr, abi
        for lvl, s in enumerate((1, 2, 4)):
            tab_s[2 * lvl] = jnp.where(row >= s, pr, 0.0)
            tab_s[2 * lvl + 1] = jnp.where(row >= s, pi, 0.0)
            pr, pi = pr * pr - pi * pi, 2.0 * pr * pi
        cwr = jnp.zeros((SUBLANE, S5_LANES), F32)
        cwi = jnp.zeros((SUBLANE, S5_LANES), F32)
        pr, pi = abr, abi
        for r in range(SUBLANE):
            cwr = jnp.where(row == r, pr, cwr)
            cwi = jnp.where(row == r, pi, cwi)
            pr, pi = pr * abr - pi * abi, pr * abi + pi * abr
        tab_s[6] = cwr
        tab_s[7] = cwi

    tt = u_ref.shape[0]
    u = u_ref[...]
    ub = u.astype(BF16)
    for j in range(S5_NCHUNK):
        sl = slice(S5_CHUNK * j, S5_CHUNK * (j + 1))
        fre, fim = f_s[0:1, sl], f_s[1:2, sl]
        uj = ub[:, LANE * j:LANE * (j + 1)]
        bur = _dot(uj, bre_ref[j])
        bui = _dot(uj, bim_ref[j])
        xr = (fre * bur - fim * bui).reshape(tt // SUBLANE, SUBLANE, S5_CHUNK)
        xi = (fre * bui + fim * bur).reshape(tt // SUBLANE, SUBLANE, S5_CHUNK)
        for lvl, s in enumerate((1, 2, 4)):
            mr = tab_s[2 * lvl, :, sl]
            mi = tab_s[2 * lvl + 1, :, sl]
            sr = pltpu.roll(xr, s, axis=1)
            si = pltpu.roll(xi, s, axis=1)
            xr, xi = xr + mr * sr - mi * si, xi + mr * si + mi * sr
        xr_s[...] = xr.reshape(tt, S5_CHUNK)
        xi_s[...] = xi.reshape(tt, S5_CHUNK)
        cwr = tab_s[6, :, sl]
        cwi = tab_s[7, :, sl]

        def body(g, carry, cwr=cwr, cwi=cwi):
            cr, ci = carry
            r0 = pl.multiple_of(g * SUBLANE, SUBLANE)
            gr = xr_s[pl.ds(r0, SUBLANE), :] + cwr * cr - cwi * ci
            gi = xi_s[pl.ds(r0, SUBLANE), :] + cwr * ci + cwi * cr
            xr_s[pl.ds(r0, SUBLANE), :] = gr
            xi_s[pl.ds(r0, SUBLANE), :] = gi
            return gr[SUBLANE - 1:SUBLANE, :], gi[SUBLANE - 1:SUBLANE, :]

        cr, ci = lax.fori_loop(0, tt // SUBLANE, body, (car_re[:, sl], car_im[:, sl]), unroll=4)
        car_re[:, sl] = cr
        car_im[:, sl] = ci
        y_s[:, LANE * j:LANE * (j + 1)] = (_dot(xr_s[...].astype(BF16), cre_ref[j])
                                           - _dot(xi_s[...].astype(BF16), cim_ref[j]))
    y_ref[...] = _s5_glu(y_s[...], u, d_ref, wglu_ref).astype(BF16)
    sre_ref[0] = car_re[...]
    sim_ref[0] = car_im[...]


def _run_parts(name, parts):
    n = parts[0]["n"]
    assert all(p["n"] == n for p in parts)
    n_in = [len(p["args"]) for p in parts]
    n_out = [len(p["out_shape"]) for p in parts]
    n_scr = [len(p["scratch"]) for p in parts]

    def kern(*refs):
        step = pl.program_id(0)
        ins, outs, scr = refs[:sum(n_in)], refs[sum(n_in):sum(n_in) + sum(n_out)], refs[sum(n_in) + sum(n_out):]
        a = b = c = 0
        for p, na, nb, nc in zip(parts, n_in, n_out, n_scr):
            p["body"](step, *ins[a:a + na], *outs[b:b + nb], *scr[c:c + nc])
            a, b, c = a + na, b + nb, c + nc

    flat = lambda key: [x for p in parts for x in p[key]]
    outs = pl.pallas_call(
        kern, grid=(n,), in_specs=flat("in_specs"), out_specs=flat("out_specs"), out_shape=flat("out_shape"),
        scratch_shapes=flat("scratch"), compiler_params=_cparams(("arbitrary",)), name=name,
    )(*flat("args"))
    res, b = [], 0
    for nb in n_out:
        res.append(list(outs[b:b + nb]))
        b += nb
    return res


def _s5_prompt_part(proj, bsz, seq, sp):
    tt = _pick(seq, (S5_TILE, 128, 64, 32, 16, 8))
    nt = seq // tt
    const2 = lambda i: (0, 0)
    const3 = lambda i: (0, 0, 0)
    row_spec = pl.BlockSpec((1, S5_LANES), const2)
    return dict(
        n=bsz * nt,
        body=functools.partial(_s5_prompt_body, nt=nt),
        in_specs=[pl.BlockSpec((tt, D_A), lambda i: (i, COL_U)),
                  row_spec, row_spec, row_spec,
                  pl.BlockSpec((S5_NCHUNK, LANE, S5_CHUNK), const3),
                  pl.BlockSpec((S5_NCHUNK, LANE, S5_CHUNK), const3),
                  pl.BlockSpec((S5_NCHUNK, S5_CHUNK, LANE), const3),
                  pl.BlockSpec((S5_NCHUNK, S5_CHUNK, LANE), const3),
                  pl.BlockSpec((1, D_A), const2),
                  pl.BlockSpec((D_A, D_A), const2)],
        args=[proj, sp["are"], sp["aim"], sp["ldt"], sp["bre"], sp["bim"], sp["cre"], sp["cim"],
              sp["d"], sp["wglu"]],
        out_specs=[pl.BlockSpec((tt, D_A), lambda i: (i, 0)),
                   pl.BlockSpec((1, 1, S5_LANES), lambda i: (i // nt, 0, 0)),
                   pl.BlockSpec((1, 1, S5_LANES), lambda i: (i // nt, 0, 0))],
        out_shape=[jax.ShapeDtypeStruct((bsz * seq, D_A), BF16),
                   jax.ShapeDtypeStruct((bsz, 1, S5_LANES), F32),
                   jax.ShapeDtypeStruct((bsz, 1, S5_LANES), F32)],
        scratch=[pltpu.VMEM((tt, S5_CHUNK), F32), pltpu.VMEM((tt, S5_CHUNK), F32),
                 pltpu.VMEM((tt, D_A), F32),
                 pltpu.VMEM((1, S5_LANES), F32), pltpu.VMEM((1, S5_LANES), F32),
                 pltpu.VMEM((2, S5_LANES), F32), pltpu.VMEM((8, SUBLANE, S5_LANES), F32)])


def _s5_prompt(proj, bsz, seq, sp):
    return _run_parts("s5_prompt", [_s5_prompt_part(proj, bsz, seq, sp)])[0]


def _mlstm_prompt_kernel(q_ref, k_ref, v0_ref, v1_ref, o0_ref, o1_ref, sm_ref, bias_ref, nw_ref,
                         y_ref, c_ref, n_ref, m_ref, *, chunk):
    @pl.when(pl.program_id(1) == 0)
    def _():
        c_ref[...] = jnp.zeros_like(c_ref)
        n_ref[...] = jnp.zeros_like(n_ref)
        m_ref[...] = jnp.zeros_like(m_ref)

    tt = q_ref.shape[0]
    c = chunk
    nchunk = tt // c
    smb = sm_ref[...] + bias_ref[...]
    lf_all = _log_sigmoid(smb)
    li_t = smb.T
    lf_t = lf_all.T
    rowi = lax.broadcasted_iota(jnp.int32, (c, c), 0)
    coli = lax.broadcasted_iota(jnp.int32, (c, c), 1)
    causal = rowi >= coli
    lane = lax.broadcasted_iota(jnp.int32, (1, LANE), 1)
    m_row = m_ref[0]
    probs = [(ci, h) for ci in range(nchunk) for h in range(H_B)]
    idx = {p: i for i, p in enumerate(probs)}
    rows = lambda ci: slice(c * ci, c * (ci + 1))
    vo_refs = [(v0_ref, o0_ref) if h < 2 else (v1_ref, o1_ref) for h in range(H_B)]
    q = [q_ref[rows(ci), DQK_B * h:DQK_B * (h + 1)] * (DQK_B ** -0.5) for ci, h in probs]
    k = [k_ref[rows(ci), DQK_B * h:DQK_B * (h + 1)] for ci, h in probs]
    qb = [x.astype(BF16) for x in q]
    kb = [x.astype(BF16) for x in k]
    vb = [vo_refs[h][0][rows(ci), DV_B * (h % 2):DV_B * (h % 2 + 1)].astype(BF16) for ci, h in probs]
    qkt = [lax.dot_general(a, b, NT, preferred_element_type=F32) for a, b in zip(qb, kb)]
    li_c = [smb[rows(ci), SM_I + h:SM_I + h + 1] for ci, h in probs]
    li_r = [li_t[SM_I + h:SM_I + h + 1, rows(ci)] for ci, h in probs]
    bc_c = [jnp.sum(jnp.where(causal, lf_t[SM_F + h:SM_F + h + 1, rows(ci)], 0.0), axis=1, keepdims=True)
            for ci, h in probs]
    bc_r = [jnp.sum(jnp.where(rowi <= coli, lf_all[rows(ci), SM_F + h:SM_F + h + 1], 0.0), axis=0, keepdims=True)
            for ci, h in probs]
    dmat = [jnp.where(causal, a - b + r, -jnp.inf) for a, b, r in zip(bc_c, bc_r, li_r)]
    dmax = [jnp.max(x, axis=1, keepdims=True) for x in dmat]
    b_last = [x[c - 1:c, :] for x in bc_c]
    expo = [bl - a + l for bl, a, l in zip(b_last, bc_c, li_c)]
    emax = [jnp.max(x, axis=0, keepdims=True) for x in expo]
    m_prev, m_new = [None] * len(probs), [None] * len(probs)
    for h in range(H_B):
        m = m_row[:, h:h + 1]
        for ci in range(nchunk):
            i = idx[ci, h]
            m_prev[i] = m
            m = jnp.maximum(b_last[i] + m, emax[i])
            m_new[i] = m
    inter = [a + m for a, m in zip(bc_c, m_prev)]
    m_t = [jnp.maximum(a, b) for a, b in zip(inter, dmax)]
    w_inter = [jnp.exp(a - b) for a, b in zip(inter, m_t)]
    s = [x * jnp.exp(d - m) for x, d, m in zip(qkt, dmat, m_t)]
    sv = [_dot(x.astype(BF16), v) for x, v in zip(s, vb)]
    ssum = [jnp.sum(x, axis=1, keepdims=True) for x in s]
    emt = [jnp.exp(-x) for x in m_t]
    decay = [jnp.exp(bl + mp - mn) for bl, mp, mn in zip(b_last, m_prev, m_new)]
    kw = [jnp.exp(e - mn) * kx for e, mn, kx in zip(expo, m_new, k)]
    kv = [lax.dot_general(x.astype(BF16), v, TN, preferred_element_type=F32) for x, v in zip(kw, vb)]
    ksum = [jnp.sum(x, axis=0, keepdims=True) for x in kw]

    cst = [c_ref[0, h] for h in range(H_B)]
    nrow = [n_ref[0, h:h + 1, :] for h in range(H_B)]
    for ci in range(nchunk):
        ids = [idx[ci, h] for h in range(H_B)]
        qc = [_dot(qb[i], cst[h].astype(BF16)) for h, i in enumerate(ids)]
        num = [sv[i] + w_inter[i] * qc[h] for h, i in enumerate(ids)]
        nq = [ssum[i] + w_inter[i] * jnp.sum(q[i] * nrow[h], axis=1, keepdims=True) for h, i in enumerate(ids)]
        hh = [num[h] / jnp.maximum(jnp.abs(nq[h]), emt[i]) for h, i in enumerate(ids)]
        cst = [decay[i] * cst[h] + kv[i] for h, i in enumerate(ids)]
        nrow = [decay[i] * nrow[h] + ksum[i] for h, i in enumerate(ids)]
        for h in range(H_B):
            og = vo_refs[h][1][rows(ci), DV_B * (h % 2):DV_B * (h % 2 + 1)]
            y_ref[rows(ci), DV_B * h:DV_B * (h + 1)] = (_rms(hh[h], nw_ref[h:h + 1, :]) * _sigmoid(og)).astype(BF16)
    m_out = m_row
    for h in range(H_B):
        c_ref[0, h] = cst[h]
        n_ref[0, h:h + 1, :] = nrow[h]
        m_out = jnp.where(lane == h, m_new[idx[nchunk - 1, h]], m_out)
    m_ref[0] = m_out


def _mlstm_prompt(proj, bsz, seq, mp):
    chunk = _pick(seq, (MLSTM_CHUNK,))
    c = _pick(seq, (MLSTM_TILE, MLSTM_CHUNK))
    nt = seq // c
    rows = lambda col: (lambda b, t: (b * nt + t, col))
    const2 = lambda b, t: (0, 0)
    return pl.pallas_call(
        functools.partial(_mlstm_prompt_kernel, chunk=chunk),
        grid=(bsz, nt),
        in_specs=[pl.BlockSpec((c, 512), rows(COL_QB // 4)),
                  pl.BlockSpec((c, 512), rows(COL_KB // 4)),
                  pl.BlockSpec((c, 512), rows(COL_VB // 4)),
                  pl.BlockSpec((c, 512), rows(COL_VB // 4 + 1)),
                  pl.BlockSpec((c, 512), rows(COL_OB // 4)),
                  pl.BlockSpec((c, 512), rows(COL_OB // 4 + 1)),
                  pl.BlockSpec((c, LANE), rows(COL_SMALL)),
                  pl.BlockSpec((1, LANE), const2),
                  pl.BlockSpec((H_B, DV_B), const2)],
        out_specs=[pl.BlockSpec((c, D_B), lambda b, t: (b * nt + t, 0)),
                   pl.BlockSpec((1, H_B, DQK_B, DV_B), lambda b, t: (b, 0, 0, 0)),
                   pl.BlockSpec((1, H_B, DQK_B), lambda b, t: (b, 0, 0)),
                   pl.BlockSpec((1, 1, LANE), lambda b, t: (b, 0, 0))],
        out_shape=[jax.ShapeDtypeStruct((bsz * seq, D_B), BF16),
                   jax.ShapeDtypeStruct((bsz, H_B, DQK_B, DV_B), F32),
                   jax.ShapeDtypeStruct((bsz, H_B, DQK_B), F32),
                   jax.ShapeDtypeStruct((bsz, 1, LANE), F32)],
        compiler_params=_cparams(("arbitrary", "arbitrary")),
        name="mlstm_prompt",
    )(proj, proj, proj, proj, proj, proj, proj, mp["bias"], mp["norm"])


def _split_bf16(a):
    hi = a.astype(BF16)
    return hi, (a - hi.astype(F32)).astype(BF16)


def _dot_x3(a, b):
    ah, al = _split_bf16(a)
    bh, bl = _split_bf16(b)
    return _dot(ah, bh) + _dot(ah, bl) + _dot(al, bh)


def _unit_lower_inverses(lmats):
    c = lmats[0].shape[0]
    eye = (lax.broadcasted_iota(jnp.int32, (c, c), 0) == lax.broadcasted_iota(jnp.int32, (c, c), 1)).astype(F32)
    hi_half = lax.broadcasted_iota(jnp.int32, (c, 2 * c), 1) >= c
    ms = [jnp.concatenate([-l, eye], axis=1) for l in lmats]
    span = 1
    while span < c:
        ms = [_dot_x3(m[:, 0:c], m) + jnp.where(hi_half, m, 0.0) for m in ms]
        span *= 2
    return [m[:, c:2 * c] for m in ms]


def _gdn_prompt_kernel(q_ref, k_ref, v_ref, z_ref, sm_ref, bias_ref, alog_ref, cw_ref, nw_ref,
                       y_ref, s_ref, xb_s, *, chunk):
    tt = q_ref.shape[0]
    c = chunk

    @pl.when(pl.program_id(1) == 0)
    def _():
        s_ref[...] = jnp.zeros_like(s_ref)
        xb_s[0:SUBLANE, :] = jnp.zeros((SUBLANE, 3 * D_C), F32)

    xb_s[SUBLANE:SUBLANE + tt, 0:D_C] = q_ref[...]
    xb_s[SUBLANE:SUBLANE + tt, D_C:2 * D_C] = k_ref[...]
    xb_s[SUBLANE:SUBLANE + tt, 2 * D_C:3 * D_C] = v_ref[...]
    conv = cw_ref[GDN_CONV - 1:GDN_CONV, :] * xb_s[SUBLANE:SUBLANE + tt, :]
    for j in range(GDN_CONV - 1):
        off = SUBLANE - (GDN_CONV - 1) + j
        conv = conv + cw_ref[j:j + 1, :] * xb_s[off:off + tt, :]
    xb_s[0:SUBLANE, :] = xb_s[tt:tt + SUBLANE, :]
    qkv = _silu(conv)

    sm = sm_ref[...]
    beta_all = _sigmoid(sm)
    g_all = -jnp.exp(alog_ref[...]) * _softplus(sm + bias_ref[...])
    g_t = g_all.T
    rowi = lax.broadcasted_iota(jnp.int32, (c, c), 0)
    coli = lax.broadcasted_iota(jnp.int32, (c, c), 1)
    incl = rowi >= coli
    strict = rowi > coli

    nchunk = tt // c
    probs = [(ci, h) for ci in range(nchunk) for h in range(H_C)]
    rows = lambda ci: slice(c * ci, c * (ci + 1))
    l2 = lambda x: x * lax.rsqrt(jnp.sum(x * x, axis=-1, keepdims=True) + EPS)
    q = [l2(qkv[rows(ci), DK_C * h:DK_C * (h + 1)]) * (DK_C ** -0.5) for ci, h in probs]
    k = [l2(qkv[rows(ci), D_C + DK_C * h:D_C + DK_C * (h + 1)]) for ci, h in probs]
    v = [qkv[rows(ci), 2 * D_C + DV_C * h:2 * D_C + DV_C * (h + 1)] for ci, h in probs]
    beta_c = [beta_all[rows(ci), SM_BETA + h:SM_BETA + h + 1] for ci, h in probs]
    gc_c = [jnp.sum(jnp.where(incl, g_t[SM_A + h:SM_A + h + 1, rows(ci)], 0.0), axis=1, keepdims=True)
            for ci, h in probs]
    gc_r = [jnp.sum(jnp.where(rowi <= coli, g_all[rows(ci), SM_A + h:SM_A + h + 1], 0.0), axis=0, keepdims=True)
            for ci, h in probs]
    gam = [jnp.exp(jnp.where(incl, a - b, -jnp.inf)) for a, b in zip(gc_c, gc_r)]
    qb = [x.astype(BF16) for x in q]
    kb = [x.astype(BF16) for x in k]
    kk = [lax.dot_general(x, x, NT, preferred_element_type=F32) for x in kb]
    qk = [(lax.dot_general(a, b, NT, preferred_element_type=F32) * g).astype(BF16) for a, b, g in zip(qb, kb, gam)]
    egc = [jnp.exp(x) for x in gc_c]
    tinv = _unit_lower_inverses([jnp.where(strict, b * g * x, 0.0) for b, g, x in zip(beta_c, gam, kk)])
    tr = [_dot_x3(t, jnp.concatenate([b * vv, (b * e) * kx], axis=1))
          for t, b, vv, e, kx in zip(tinv, beta_c, v, egc, k)]
    u0 = [x[:, 0:DV_C] for x in tr]
    wb = [x[:, DV_C:].astype(BF16) for x in tr]
    g_last = [x[c - 1:c, :] for x in gc_c]
    kw = [(jnp.exp(gl - gc) * kx).astype(BF16) for gl, gc, kx in zip(g_last, gc_c, k)]
    e_last = [jnp.exp(x) for x in g_last]

    st = [s_ref[0, h] for h in range(H_C)]
    for ci in range(nchunk):
        ids = [ci * H_C + h for h in range(H_C)]
        stb = [x.astype(BF16) for x in st]
        ub = [(u0[i] - _dot(wb[i], stb[h])).astype(BF16) for h, i in enumerate(ids)]
        qs = [_dot(qb[i], stb[h]) for h, i in enumerate(ids)]
        st = [e_last[i] * st[h] + lax.dot_general(kw[i], ub[h], TN, preferred_element_type=F32)
              for h, i in enumerate(ids)]
        o = [egc[i] * qs[h] + _dot(qk[i], ub[h]) for h, i in enumerate(ids)]
        for h in range(H_C):
            zz = z_ref[rows(ci), DV_C * h:DV_C * (h + 1)]
            y_ref[rows(ci), DV_C * h:DV_C * (h + 1)] = (_rms(o[h], nw_ref[...]) * _silu(zz)).astype(BF16)
    for h in range(H_C):
        s_ref[0, h] = st[h]


def _gdn_prompt(proj, bsz, seq, gp):
    c = _pick(seq, (GDN_CHUNK,))
    tt = _pick(seq, (GDN_TILE,))
    nt = seq // tt
    rows = lambda col: (lambda b, t: (b * nt + t, col))
    const2 = lambda b, t: (0, 0)
    return pl.pallas_call(
        functools.partial(_gdn_prompt_kernel, chunk=c),
        grid=(bsz, nt),
        in_specs=[pl.BlockSpec((tt, D_C), rows(COL_QKV // 4)),
                  pl.BlockSpec((tt, D_C), rows(COL_QKV // 4 + 1)),
                  pl.BlockSpec((tt, D_C), rows(COL_QKV // 4 + 2)),
                  pl.BlockSpec((tt, D_C), rows(COL_Z // 4)),
                  pl.BlockSpec((tt, LANE), rows(COL_SMALL)),
                  pl.BlockSpec((1, LANE), const2),
                  pl.BlockSpec((1, LANE), const2),
                  pl.BlockSpec((GDN_CONV, 3 * D_C), const2),
                  pl.BlockSpec((1, DV_C), const2)],
        out_specs=[pl.BlockSpec((tt, D_C), lambda b, t: (b * nt + t, 0)),
                   pl.BlockSpec((1, H_C, DK_C, DV_C), lambda b, t: (b, 0, 0, 0))],
        out_shape=[jax.ShapeDtypeStruct((bsz * seq, D_C), BF16),
                   jax.ShapeDtypeStruct((bsz, H_C, DK_C, DV_C), F32)],
        scratch_shapes=[pltpu.VMEM((SUBLANE + tt, 3 * D_C), F32)],
        compiler_params=_cparams(("arbitrary", "arbitrary")),
        name="gdn_prompt",
    )(proj, proj, proj, proj, proj, gp["bias"], gp["alog"], gp["conv_w"], gp["norm"])


def _to_col(row, eye):
    return jnp.sum(jnp.where(eye, row, 0.0), axis=1, keepdims=True)


N_SAMPLE_INPUTS = 23


def _sample_mixers_kernel(*refs, fill_layer):
    (proj_ref, sre_ref, sim_ref, c_ref, n_ref, m_ref, gs_ref, gbuf_ref,
     are_ref, aim_ref, ldt_ref, bre_ref, bim_ref, cre_ref, cim_ref, d_ref, wglu_ref,
     mbias_ref, mnorm_ref, gbias_ref, alog_ref, gcw_ref, gnorm_ref) = refs[:N_SAMPLE_INPUTS]
    (y_ref, sre_o, sim_o, c_o, n_o, m_o, gs_o, gbuf_o,
     qkv_s, qn_s, kn_s, beta_s, g_s, li_s, lf_s) = refs[-15:]
    bb = proj_ref.shape[0]
    if fill_layer is not None:
        for l in range(c_o.shape[0]):
            if l != fill_layer:
                c_o[l] = jnp.zeros(c_o.shape[1:], F32)
                gs_o[l] = jnp.zeros(gs_o.shape[1:], F32)
        c_o = c_o.at[fill_layer]
        gs_o = gs_o.at[fill_layer]

    u = proj_ref[:, COL_U * LANE:COL_U * LANE + D_A]
    ub = u.astype(BF16)
    ys = []
    for j in range(S5_NCHUNK):
        sl = slice(S5_CHUNK * j, S5_CHUNK * (j + 1))
        abr, abi, fre, fim = _s5_disc(are_ref[:, sl], aim_ref[:, sl], ldt_ref[:, sl])
        uj = ub[:, LANE * j:LANE * (j + 1)]
        bur = _dot(uj, bre_ref[j])
        bui = _dot(uj, bim_ref[j])
        s0r = sre_ref[:, sl]
        s0i = sim_ref[:, sl]
        xr = fre * bur - fim * bui + abr * s0r - abi * s0i
        xi = fre * bui + fim * bur + abr * s0i + abi * s0r
        sre_o[:, sl] = xr
        sim_o[:, sl] = xi
        ys.append(_dot(xr.astype(BF16), cre_ref[j]) - _dot(xi.astype(BF16), cim_ref[j]))
    y_a = jnp.concatenate(ys, axis=1)
    y_ref[:, Y_A:Y_A + D_A] = _s5_glu(y_a, u, d_ref, wglu_ref)

    sm = proj_ref[:, COL_SMALL * LANE:(COL_SMALL + 1) * LANE]
    smb = sm + mbias_ref[...]
    li_s[...] = smb
    lf_s[...] = _log_sigmoid(smb)
    beta_s[...] = _sigmoid(sm)
    g_s[...] = -jnp.exp(alog_ref[...]) * _softplus(sm + gbias_ref[...])

    xnew = proj_ref[:, COL_QKV * LANE:COL_QKV * LANE + 3 * D_C]
    conv = gcw_ref[GDN_CONV - 1:GDN_CONV, :] * xnew
    for j in range(GDN_CONV - 1):
        conv = conv + gcw_ref[j:j + 1, :] * gbuf_ref[j]
        if j > 0:
            gbuf_o[j - 1] = gbuf_ref[j]
    gbuf_o[GDN_CONV - 2] = xnew
    qkv = _silu(conv)
    qkv_s[...] = qkv
    for h in range(H_C):
        q = qkv[:, DK_C * h:DK_C * (h + 1)]
        k = qkv[:, D_C + DK_C * h:D_C + DK_C * (h + 1)]
        qn_s[:, DK_C * h:DK_C * (h + 1)] = q * lax.rsqrt(jnp.sum(q * q, axis=-1, keepdims=True) + EPS) * (DK_C ** -0.5)
        kn_s[:, DK_C * h:DK_C * (h + 1)] = k * lax.rsqrt(jnp.sum(k * k, axis=-1, keepdims=True) + EPS)

    assert 4 * H_B * bb == LANE and DQK_B == LANE and DK_C == LANE and H_B == H_C
    tiles = ([proj_ref[:, COL_QB * LANE + DQK_B * h:COL_QB * LANE + DQK_B * (h + 1)] * (DQK_B ** -0.5)
              for h in range(H_B)]
             + [proj_ref[:, COL_KB * LANE + DQK_B * h:COL_KB * LANE + DQK_B * (h + 1)] for h in range(H_B)]
             + [qn_s[:, DK_C * h:DK_C * (h + 1)] for h in range(H_C)]
             + [kn_s[:, DK_C * h:DK_C * (h + 1)] for h in range(H_C)])
    stacked_t = jnp.concatenate(tiles, axis=0).T

    def col_of(kind, b, h):
        j = (kind * H_B + h) * bb + b
        return stacked_t[:, j:j + 1]

    lane = lax.broadcasted_iota(jnp.int32, (1, H_B), 1)

    probs = [(b, h) for b in range(bb) for h in range(H_B)]
    rsl = lambda b: slice(b, b + 1)

    q = [proj_ref[rsl(b), COL_QB * LANE + DQK_B * h:COL_QB * LANE + DQK_B * (h + 1)] * (DQK_B ** -0.5)
         for b, h in probs]
    k = [proj_ref[rsl(b), COL_KB * LANE + DQK_B * h:COL_KB * LANE + DQK_B * (h + 1)] for b, h in probs]
    v = [proj_ref[rsl(b), COL_VB * LANE + DV_B * h:COL_VB * LANE + DV_B * (h + 1)] for b, h in probs]
    li = [li_s[rsl(b), SM_I + h:SM_I + h + 1] for b, h in probs]
    inter = [lf_s[rsl(b), SM_F + h:SM_F + h + 1] + m_ref[rsl(b), h:h + 1] for b, h in probs]
    m_t = [jnp.maximum(a, c) for a, c in zip(inter, li)]
    w_intra = [jnp.exp(a - c) for a, c in zip(li, m_t)]
    w_inter = [jnp.exp(a - c) for a, c in zip(inter, m_t)]
    qcol = [col_of(0, b, h) for b, h in probs]
    kcol = [col_of(1, b, h) for b, h in probs]
    s = [jnp.sum(a * c, axis=1, keepdims=True) * w for a, c, w in zip(q, k, w_intra)]
    cst = [c_ref[b, h] for b, h in probs]
    nrow = [n_ref[b, h:h + 1, :] for b, h in probs]
    qc = [jnp.sum(a * c, axis=0, keepdims=True) for a, c in zip(qcol, cst)]
    for i, (b, h) in enumerate(probs):
        c_o[b, h] = w_inter[i] * cst[i] + (w_intra[i] * kcol[i]) * v[i]
        n_o[b, h:h + 1, :] = w_inter[i] * nrow[i] + w_intra[i] * k[i]
    num = [s[i] * v[i] + w_inter[i] * qc[i] for i in range(len(probs))]
    nq = [s[i] + w_inter[i] * jnp.sum(q[i] * nrow[i], axis=1, keepdims=True) for i in range(len(probs))]
    hh = [num[i] / jnp.maximum(jnp.abs(nq[i]), jnp.exp(-m_t[i])) for i in range(len(probs))]
    for i, (b, h) in enumerate(probs):
        og = proj_ref[rsl(b), COL_OB * LANE + DV_B * h:COL_OB * LANE + DV_B * (h + 1)]
        y_ref[rsl(b), Y_B + DV_B * h:Y_B + DV_B * (h + 1)] = _rms(hh[i], mnorm_ref[h:h + 1, :]) * _sigmoid(og)
    for b in range(bb):
        m_out = m_ref[rsl(b), :]
        for h in range(H_B):
            m_out = jnp.where(lane == h, m_t[b * H_B + h], m_out)
        m_o[rsl(b), :] = m_out

    q = [qn_s[rsl(b), DK_C * h:DK_C * (h + 1)] for b, h in probs]
    k = [kn_s[rsl(b), DK_C * h:DK_C * (h + 1)] for b, h in probs]
    v = [qkv_s[rsl(b), 2 * D_C + DV_C * h:2 * D_C + DV_C * (h + 1)] for b, h in probs]
    beta = [beta_s[rsl(b), SM_BETA + h:SM_BETA + h + 1] for b, h in probs]
    eg = [jnp.exp(g_s[rsl(b), SM_A + h:SM_A + h + 1]) for b, h in probs]
    qcol = [col_of(2, b, h) for b, h in probs]
    kcol = [col_of(3, b, h) for b, h in probs]
    st = [gs_ref[b, h] for b, h in probs]
    ks = [jnp.sum(a * c, axis=0, keepdims=True) for a, c in zip(kcol, st)]
    qs = [jnp.sum(a * c, axis=0, keepdims=True) for a, c in zip(qcol, st)]
    un = [beta[i] * (v[i] - eg[i] * ks[i]) for i in range(len(probs))]
    for i, (b, h) in enumerate(probs):
        gs_o[b, h] = eg[i] * st[i] + kcol[i] * un[i]
    o = [eg[i] * qs[i] + jnp.sum(q[i] * k[i], axis=1, keepdims=True) * un[i] for i in range(len(probs))]
    for i, (b, h) in enumerate(probs):
        zz = proj_ref[rsl(b), COL_Z * LANE + DV_C * h:COL_Z * LANE + DV_C * (h + 1)]
        y_ref[rsl(b), Y_C + DV_C * h:Y_C + DV_C * (h + 1)] = _rms(o[i], gnorm_ref[...]) * _silu(zz)


def _sample_mixers(proj, row0, nrows, st, sp, mp, gp, layer, prev):
    bb = SAMPLE_BLOCK
    blk0 = row0 // bb
    depth = st["c"].shape[0]
    const2 = lambda i: (0, 0)
    const3 = lambda i: (0, 0, 0)
    row_spec = pl.BlockSpec((1, S5_LANES), const2)
    c_spec = _layered(layer, (bb, H_B, DQK_B, DV_B), lambda i: (i, 0, 0, 0))
    gs_spec = _layered(layer, (bb, H_C, DK_C, DV_C), lambda i: (i, 0, 0, 0))
    in_specs = [
        pl.BlockSpec((bb, N_PROJ), lambda i: (blk0 + i, 0)),
        pl.BlockSpec((bb, S5_LANES), lambda i: (i, 0)),
        pl.BlockSpec((bb, S5_LANES), lambda i: (i, 0)),
        c_spec,
        pl.BlockSpec((bb, H_B, DQK_B), lambda i: (i, 0, 0)),
        pl.BlockSpec((bb, H_B), lambda i: (i, 0)),
        gs_spec,
        pl.BlockSpec((GDN_CONV - 1, bb, 3 * D_C), lambda i: (0, i, 0)),
        row_spec, row_spec, row_spec,
        pl.BlockSpec((S5_NCHUNK, LANE, S5_CHUNK), const3),
        pl.BlockSpec((S5_NCHUNK, LANE, S5_CHUNK), const3),
        pl.BlockSpec((S5_NCHUNK, S5_CHUNK, LANE), const3),
        pl.BlockSpec((S5_NCHUNK, S5_CHUNK, LANE), const3),
        pl.BlockSpec((1, D_A), const2),
        pl.BlockSpec((D_A, D_A), const2),
        pl.BlockSpec((1, LANE), const2),
        pl.BlockSpec((H_B, DV_B), const2),
        pl.BlockSpec((1, LANE), const2),
        pl.BlockSpec((1, LANE), const2),
        pl.BlockSpec((GDN_CONV, 3 * D_C), const2),
        pl.BlockSpec((1, DV_C), const2),
    ]
    out_specs = [
        pl.BlockSpec((bb, D_MODEL), lambda i: (i, 0)),
        pl.BlockSpec((bb, S5_LANES), lambda i: (i, 0)),
        pl.BlockSpec((bb, S5_LANES), lambda i: (i, 0)),
        c_spec if prev is not None else pl.BlockSpec((depth, bb, H_B, DQK_B, DV_B), lambda i: (0, i, 0, 0, 0)),
        pl.BlockSpec((bb, H_B, DQK_B), lambda i: (i, 0, 0)),
        pl.BlockSpec((bb, H_B), lambda i: (i, 0)),
        gs_spec if prev is not None else pl.BlockSpec((depth, bb, H_C, DK_C, DV_C), lambda i: (0, i, 0, 0, 0)),
        pl.BlockSpec((GDN_CONV - 1, bb, 3 * D_C), lambda i: (0, i, 0)),
    ]
    out_shape = [
        jax.ShapeDtypeStruct((nrows, D_MODEL), F32),
        jax.ShapeDtypeStruct((nrows, S5_LANES), F32),
        jax.ShapeDtypeStruct((nrows, S5_LANES), F32),
        jax.ShapeDtypeStruct((depth, nrows, H_B, DQK_B, DV_B), F32),
        jax.ShapeDtypeStruct((nrows, H_B, DQK_B), F32),
        jax.ShapeDtypeStruct((nrows, H_B), F32),
        jax.ShapeDtypeStruct((depth, nrows, H_C, DK_C, DV_C), F32),
        jax.ShapeDtypeStruct((GDN_CONV - 1, nrows, 3 * D_C), F32),
    ]
    scratch = [pltpu.VMEM((bb, 3 * D_C), F32), pltpu.VMEM((bb, D_C), F32), pltpu.VMEM((bb, D_C), F32),
               pltpu.VMEM((bb, LANE), F32), pltpu.VMEM((bb, LANE), F32),
               pltpu.VMEM((bb, LANE), F32), pltpu.VMEM((bb, LANE), F32)]
    args = [proj, st["sre"], st["sim"], st["c"], st["n"], st["m"], st["gs"], st["gbuf"],
            sp["are"], sp["aim"], sp["ldt"], sp["bre"], sp["bim"], sp["cre"], sp["cim"], sp["d"], sp["wglu"],
            mp["bias"], mp["norm"], gp["bias"], gp["alog"], gp["conv_w"], gp["norm"]]
    assert len(args) == N_SAMPLE_INPUTS
    aliases = {}
    if prev is not None:
        in_specs += [pl.BlockSpec(memory_space=pl.ANY)] * 2
        args += list(prev)
        aliases = {N_SAMPLE_INPUTS: 3, N_SAMPLE_INPUTS + 1: 6}
    return pl.pallas_call(
        functools.partial(_sample_mixers_kernel, fill_layer=layer if prev is None else None),
        grid=(nrows // bb,),
        in_specs=in_specs, out_specs=out_specs, out_shape=out_shape, scratch_shapes=scratch,
        input_output_aliases=aliases,
        compiler_params=_cparams(("arbitrary",)),
        name="sample_mixers",
    )(*args)


def _merge_kernel(h_ref, ya_ref, yb_ref, yc_ref, wg0_ref, wg1_ref, wg2_ref, wa_ref, wb_ref, wc_ref, o_ref):
    h = h_ref[...]
    bf = lambda ref: ref[...].astype(BF16)
    acc = _sigmoid(_dot(h, bf(wg0_ref))) * _dot(bf(ya_ref), bf(wa_ref))
    acc = acc + _sigmoid(_dot(h, bf(wg1_ref))) * _dot(bf(yb_ref), bf(wb_ref))
    acc = acc + _sigmoid(_dot(h, bf(wg2_ref))) * _dot(bf(yc_ref), bf(wc_ref))
    o_ref[...] = acc.astype(BF16)


def _merge(h, ya, yb, yc, wg, wa, wb, wc, layer):
    m, d = h.shape
    tm = _pick(m, (1024, 512, 256, 128))
    tn = 256
    nb = d // tn
    lhs = lambda w, cb: pl.BlockSpec((tm, w), lambda i, j: (i, cb))
    gate = lambda g: _layered(layer, (d, tn), lambda i, j: (0, g * nb + j))
    rhs = lambda w: _layered(layer, (w, tn), lambda i, j: (0, j))
    return pl.pallas_call(
        _merge_kernel,
        grid=(m // tm, nb),
        in_specs=[lhs(d, 0), lhs(D_A, ya[1]), lhs(D_B, yb[1]), lhs(D_C, yc[1]), gate(0), gate(1), gate(2),
                  rhs(D_A), rhs(D_B), rhs(D_C)],
        out_specs=pl.BlockSpec((tm, tn), lambda i, j: (i, j)),
        out_shape=jax.ShapeDtypeStruct((m, d), BF16),
        compiler_params=_cparams(("arbitrary", "arbitrary")),
        name="merge",
    )(h, ya[0], yb[0], yc[0], wg, wg, wg, wa, wb, wc)


def _out_proj_kernel(a_ref, x_ref, w_ref, nw_ref, o_ref):
    o_ref[...] = x_ref[...] + _rms(_dot(a_ref[...], w_ref[...]), nw_ref[...])


def _out_proj(a, x, w, nw, layer):
    m, d = x.shape
    tm = _pick(m, (512, 256, 128))
    return pl.pallas_call(
        _out_proj_kernel,
        grid=(m // tm,),
        in_specs=[pl.BlockSpec((tm, d), lambda i: (i, 0)),
                  pl.BlockSpec((tm, d), lambda i: (i, 0)),
                  _layered(layer, (d, d), lambda i: (0, 0)),
                  _layered(layer, (1, d), lambda i: (0, 0))],
        out_specs=pl.BlockSpec((tm, d), lambda i: (i, 0)),
        out_shape=jax.ShapeDtypeStruct((m, d), F32),
        compiler_params=_cparams(("arbitrary",)),
        name="out_proj",
    )(a, x, w, nw)


def _ffn_body(step, *refs, sample, tiles_per_seq, ft):
    if sample:
        (x_ref, nw_ref, wg_ref, wu_ref, cw_ref, wd_ref, pnw_ref, b0_ref, b1_ref,
         o_ref, g_ref, wg_o, wu_o, wd_o, h2_s) = refs
    else:
        (x_ref, nw_ref, wg_ref, wu_ref, cw_ref, wd_ref, pnw_ref,
         o_ref, g_ref, h2_s, gb_s, carry_s) = refs
    i = step // ft
    j = step % ft
    tm = x_ref.shape[0]

    @pl.when(j == 0)
    def _():
        h2_s[...] = _rms(x_ref[...], nw_ref[...]).astype(BF16)
        o_ref[...] = jnp.zeros_like(o_ref)

    h2 = h2_s[...]
    wg, wu, wd = wg_ref[...], wu_ref[...], wd_ref[...]
    if sample:
        wg, wu, wd = wg.astype(BF16), wu.astype(BF16), wd.astype(BF16)
        wg_o[0] = wg
        wu_o[0] = wu
        wd_o[0] = wd
    g = _dot(h2, wg)
    up = _dot(h2, wu)
    if sample:
        a = cw_ref[0:1, :] * b0_ref[...] + cw_ref[1:2, :] * b1_ref[...] + cw_ref[2:3, :] * g
        g_ref[...] = g
    else:
        prev = jnp.where(i % tiles_per_seq == 0, 0.0, carry_s[j])
        gb_s[0:SUBLANE, :] = prev
        gb_s[SUBLANE:SUBLANE + tm, :] = g
        a = (cw_ref[0:1, :] * gb_s[SUBLANE - 2:SUBLANE - 2 + tm, :]
             + cw_ref[1:2, :] * gb_s[SUBLANE - 1:SUBLANE - 1 + tm, :]
             + cw_ref[2:3, :] * g)
        tail = g[tm - SUBLANE:tm, :]
        carry_s[j] = tail
        g_ref[0] = tail
    act = (_gelu(a) * up).astype(BF16)
    o_ref[...] += _dot(act, wd)

    @pl.when(j == ft - 1)
    def _():
        o_ref[...] = x_ref[...] + _rms(o_ref[...], pnw_ref[...])


def _ffn_part(x, row0, nrows, seq, nw, wg, wu, cw, wd, pnw, layer, bufs=None, steps=None):
    d = x.shape[1]
    f = wg.shape[-1]
    sample = bufs is not None
    tm = nrows if sample else _pick(seq, (512, 256, 128, 64, 32, 16, 8))
    tn = _pick(f, (512, 256, 128) if sample else (1024, 512, 256, 128))
    if sample and steps is not None and f % steps == 0 and (f // steps) % LANE == 0 and f // steps <= tn:
        tn = f // steps
    mt, ft = nrows // tm, f // tn
    blk0 = row0 // tm
    mi = lambda s: s // ft
    nj = lambda s: s % ft
    in_specs = [pl.BlockSpec((tm, d), lambda s: (blk0 + mi(s), 0)),
                _layered(layer, (1, d), lambda s: (0, 0)),
                _layered(layer, (d, tn), lambda s: (0, nj(s))),
                _layered(layer, (d, tn), lambda s: (0, nj(s))),
                _layered(layer, (FFN_CONV, tn), lambda s: (0, nj(s))),
                _layered(layer, (tn, d), lambda s: (nj(s), 0)),
                _layered(layer, (1, d), lambda s: (0, 0))]
    args = [x, nw, wg, wu, cw, wd, pnw]
    scratch = [pltpu.VMEM((tm, d), BF16)]
    if sample:
        in_specs += [pl.BlockSpec((tm, tn), lambda s: (mi(s), nj(s)))] * 2
        args += list(bufs)
        assert mt == 1, "the sample call must visit every weight tile exactly once"
        g_spec = pl.BlockSpec((tm, tn), lambda s: (mi(s), nj(s)))
        g_shape = jax.ShapeDtypeStruct((nrows, f), F32)
        extra_specs = [pl.BlockSpec((1, d, tn), lambda s: (0, 0, nj(s))),
                       pl.BlockSpec((1, d, tn), lambda s: (0, 0, nj(s))),
                       pl.BlockSpec((1, tn, d), lambda s: (0, nj(s), 0))]
        extra_shapes = [jax.ShapeDtypeStruct((1, d, f), BF16), jax.ShapeDtypeStruct((1, d, f), BF16),
                        jax.ShapeDtypeStruct((1, f, d), BF16)]
    else:
        extra_specs, extra_shapes = [], []
        scratch += [pltpu.VMEM((SUBLANE + tm, tn), F32), pltpu.VMEM((ft, SUBLANE, tn), F32)]
        g_spec = pl.BlockSpec((1, SUBLANE, tn), lambda s: (mi(s), 0, nj(s)))
        g_shape = jax.ShapeDtypeStruct((mt, SUBLANE, f), F32)
    return dict(
        n=mt * ft,
        body=functools.partial(_ffn_body, sample=sample, tiles_per_seq=max(seq // tm, 1), ft=ft),
        in_specs=in_specs, args=args,
        out_specs=[pl.BlockSpec((tm, d), lambda s: (mi(s), 0)), g_spec] + extra_specs,
        out_shape=[jax.ShapeDtypeStruct((nrows, d), F32), g_shape] + extra_shapes,
        scratch=scratch)


def _ffn(*args, **kwargs):
    part = _ffn_part(*args, **kwargs)
    return _run_parts("ffn_sample" if kwargs.get("bufs") is not None else "ffn_prompt", [part])[0]


def _ple_kernel(x_ref, p_ref, wg_ref, wp_ref, o_ref):
    x = x_ref[...]
    gate = _sigmoid(_dot(x.astype(BF16), wg_ref[...]))
    o_ref[...] = x + gate * _dot(p_ref[...].astype(BF16), wp_ref[...])


def _ple(x, p, wg, wp, layer):
    m, d = x.shape
    pd = p.shape[-1]
    tm = _pick(m, (512, 256, 128))
    return pl.pallas_call(
        _ple_kernel,
        grid=(m // tm,),
        in_specs=[pl.BlockSpec((tm, d), lambda i: (i, 0)),
                  _layered(layer, (tm, pd), lambda i: (i, 0)),
                  _layered(layer, (d, d), lambda i: (0, 0)),
                  _layered(layer, (pd, d), lambda i: (0, 0))],
        out_specs=pl.BlockSpec((tm, d), lambda i: (i, 0)),
        out_shape=jax.ShapeDtypeStruct((m, d), F32),
        compiler_params=_cparams(("arbitrary",)),
        name="ple",
    )(x, p, wg, wp)


def _split_w_in(w):
    n_if = 2 * H_B
    w_head = w[..., 0:N_HEAD]
    w_mid = w[..., N_HEAD + n_if:N_HEAD + n_if + N_MID]
    gates = jnp.concatenate([w[..., N_HEAD:N_HEAD + n_if], w[..., N_HEAD + n_if + N_MID:]], axis=-1)
    pad = jnp.zeros(w.shape[:-1] + (LANE - gates.shape[-1],), w.dtype)
    return w_head.astype(BF16), w_mid.astype(BF16), jnp.concatenate([gates, pad], axis=-1).astype(BF16)


def _small_row(entries):
    row = jnp.zeros((LANE,), F32)
    for off, val in entries:
        row = row.at[off:off + val.shape[0]].set(val.astype(F32))
    return row.reshape(1, LANE)


def _s5_params(a_re, a_im, log_dt, b_re, b_im, c_re, c_im, d, w_glu):
    gpc = S5_CHUNK // S5_STATE
    eye = jnp.eye(gpc, dtype=F32)

    def bmat(b):
        b4 = b.reshape(S5_NCHUNK, gpc, S5_GROUP, S5_STATE)
        return jnp.einsum('jgcp,gh->jgchp', b4, eye).reshape(S5_NCHUNK, gpc * S5_GROUP, S5_CHUNK).astype(BF16)

    def cmat(c):
        c4 = c.reshape(S5_NCHUNK, gpc, S5_STATE, S5_GROUP)
        return jnp.einsum('jgpc,gh->jgphc', c4, eye).reshape(S5_NCHUNK, S5_CHUNK, gpc * S5_GROUP).astype(BF16)

    return dict(are=a_re.reshape(1, S5_LANES), aim=a_im.reshape(1, S5_LANES),
                ldt=jnp.broadcast_to(log_dt[:, None], (S5_GROUPS, S5_STATE)).reshape(1, S5_LANES),
                bre=bmat(b_re), bim=bmat(b_im), cre=cmat(c_re), cim=cmat(c_im),
                d=d.reshape(1, D_A), wglu=w_glu.astype(BF16))


def _layer(layer, xp, xs, bsz, seq, dense, lw, state, prev):
    np_rows = bsz * seq
    nsamp = xs.shape[0]
    in_w = (dense['norm_mix_pre'], dense['w_in'], dense['w_in_mid'], dense['w_in_small'], layer)
    proj, h_p = _norm_proj(xp, *in_w)
    proj_s, h_s = _norm_proj(xs, *in_w)

    sp = _s5_params(lw['s5_a_re'], lw['s5_a_im'], lw['s5_log_dt'], lw['s5_b_re'], lw['s5_b_im'],
                    lw['s5_c_re'], lw['s5_c_im'], lw['s5_d'], lw['s5_w_glu'])
    mp = dict(bias=_small_row([(SM_I, lw['mlstm_b_i']), (SM_F, lw['mlstm_b_f'])]), norm=lw['mlstm_norm'])
    gp = dict(bias=_small_row([(SM_A, lw['gdn_dt_bias'])]), alog=_small_row([(SM_A, lw['gdn_a_log'])]),
              conv_w=lw['gdn_conv_w'], norm=lw['gdn_norm'].reshape(1, DV_C))

    ssm_re, ssm_im, m_c, m_n, m_m, g_s, g_conv, f_conv = state
    st = dict(sre=ssm_re.reshape(nsamp, S5_LANES), sim=ssm_im.reshape(nsamp, S5_LANES),
              c=m_c, n=m_n, m=m_m, gs=g_s, gbuf=jnp.swapaxes(g_conv, 0, 1))
    y_s, sre_s, sim_s, c_s, n_s, m_s, gs_s, gbuf_s = _sample_mixers(proj_s, 0, nsamp, st, sp, mp, gp, layer, prev)
    merge_w = (dense['w_gate'], dense['w_branch_a'], dense['w_branch_b'], dense['w_branch_c'], layer)
    merged_s = _merge(h_s, (y_s, Y_A // D_A), (y_s, Y_B // D_B), (y_s, Y_C // D_C), *merge_w)
    x1_s = _out_proj(merged_s, xs, dense['w_out'], dense['norm_mix_post'], layer)

    s5_part = _s5_prompt_part(proj, bsz, seq, sp)
    ffn_s_part = _ffn_part(
        x1_s, 0, nsamp, 1, dense['norm_ffn_pre'], dense['ffn_w_gate'], dense['ffn_w_up'], dense['ffn_conv_w'],
        dense['ffn_w_down'], dense['norm_ffn_post'], layer, bufs=(f_conv[:, 0], f_conv[:, 1]), steps=s5_part["n"])
    if s5_part["n"] == ffn_s_part["n"]:
        (ya_p, sre_p, sim_p), (x2_s, g_new, wg_b, wu_b, wd_b) = _run_parts("s5_prompt_ffn_sample",
                                                                             [s5_part, ffn_s_part])
    else:
        ya_p, sre_p, sim_p = _run_parts("s5_prompt", [s5_part])[0]
        x2_s, g_new, wg_b, wu_b, wd_b = _run_parts("ffn_sample", [ffn_s_part])[0]
    yb_p, c_p, n_p, m_p = _mlstm_prompt(proj, bsz, seq, mp)
    yc_p, gs_p = _gdn_prompt(proj, bsz, seq, gp)

    merged_p = _merge(h_p, (ya_p, 0), (yb_p, 0), (yc_p, 0), *merge_w)
    x1_p = _out_proj(merged_p, xp, dense['w_out'], dense['norm_mix_post'], layer)
    cw_l, nw_pre_l, nw_post_l = (dense[k][layer:layer + 1] for k in ('ffn_conv_w', 'norm_ffn_pre', 'norm_ffn_post'))
    x2_p, gtail = _ffn(x1_p, 0, np_rows, seq, nw_pre_l, wg_b, wu_b, cw_l, wd_b, nw_post_l, 0)
    x3_p = _ple(x2_p, dense['p_prompt'], dense['ple_w_gate'], dense['ple_w_proj'], layer)
    x3_s = _ple(x2_s, dense['p_sample'], dense['ple_w_gate'], dense['ple_w_proj'], layer)

    tiles_per_seq = gtail.shape[0] // bsz
    qkv_tail = proj.reshape(bsz, seq, N_PROJ)[:, seq - (GDN_CONV - 1):, COL_QKV * LANE:COL_QKV * LANE + 3 * D_C]
    st_p = (sre_p.reshape(bsz, S5_GROUPS, S5_STATE), sim_p.reshape(bsz, S5_GROUPS, S5_STATE),
            c_p, n_p, m_p[:, 0, :H_B], gs_p, qkv_tail,
            gtail.reshape(bsz, tiles_per_seq, SUBLANE, D_FF)[:, -1, SUBLANE - (FFN_CONV - 1):, :])
    st_s = (sre_s.reshape(nsamp, S5_GROUPS, S5_STATE), sim_s.reshape(nsamp, S5_GROUPS, S5_STATE),
            None, n_s, m_s, None, jnp.swapaxes(gbuf_s, 0, 1),
            jnp.stack([f_conv[:, 1], g_new], axis=1))
    return x3_p, x3_s, st_p, st_s, (c_s, gs_s)


def kernel(x_prompt, x_sample, p_prompt, p_sample, state_ssm_re, state_ssm_im, state_mlstm_c, state_mlstm_n, state_mlstm_m, state_gdn_s, state_gdn_conv, state_ffn_conv, norm_mix_pre, norm_mix_post, norm_ffn_pre, norm_ffn_post, w_in, s5_a_re, s5_a_im, s5_log_dt, s5_b_re, s5_b_im, s5_c_re, s5_c_im, s5_d, s5_w_glu, mlstm_b_i, mlstm_b_f, mlstm_norm, gdn_conv_w, gdn_a_log, gdn_dt_bias, gdn_norm, w_branch_a, w_branch_b, w_branch_c, w_gate, w_out, ffn_w_gate, ffn_w_up, ffn_conv_w, ffn_w_down, ple_w_proj, ple_w_gate):
    bsz, seq, d = x_prompt.shape
    nsamp = x_sample.shape[0]
    depth = w_in.shape[0]
    small = dict(
        s5_a_re=s5_a_re, s5_a_im=s5_a_im, s5_log_dt=s5_log_dt, s5_b_re=s5_b_re, s5_b_im=s5_b_im,
        s5_c_re=s5_c_re, s5_c_im=s5_c_im, s5_d=s5_d, s5_w_glu=s5_w_glu, mlstm_b_i=mlstm_b_i,
        mlstm_b_f=mlstm_b_f, mlstm_norm=mlstm_norm, gdn_conv_w=gdn_conv_w, gdn_a_log=gdn_a_log,
        gdn_dt_bias=gdn_dt_bias, gdn_norm=gdn_norm)
    row = lambda w: w.reshape(depth, 1, -1)
    w_in_head, w_in_mid, w_in_small = _split_w_in(w_in)
    dense = dict(
        norm_mix_pre=row(norm_mix_pre), norm_mix_post=row(norm_mix_post), norm_ffn_pre=row(norm_ffn_pre),
        norm_ffn_post=row(norm_ffn_post), w_in=w_in_head, w_in_mid=w_in_mid, w_in_small=w_in_small,
        w_gate=w_gate, w_branch_a=w_branch_a, w_branch_b=w_branch_b, w_branch_c=w_branch_c,
        w_out=w_out.astype(BF16), ffn_w_gate=ffn_w_gate, ffn_w_up=ffn_w_up, ffn_conv_w=ffn_conv_w,
        ffn_w_down=ffn_w_down,
        ple_w_gate=ple_w_gate.astype(BF16), ple_w_proj=ple_w_proj.astype(BF16),
        p_prompt=p_prompt.reshape(depth, bsz * seq, -1), p_sample=p_sample.reshape(depth, nsamp, -1))
    xp = x_prompt.reshape(bsz * seq, d)
    xs = x_sample.reshape(nsamp, d)
    sp_all, ss_all = [], []
    big = None
    for i in range(depth):
        lw = {k: v[i] for k, v in small.items()}
        state = (state_ssm_re[i], state_ssm_im[i], state_mlstm_c, state_mlstm_n[i], state_mlstm_m[i],
                 state_gdn_s, state_gdn_conv[i], state_ffn_conv[i])
        xp, xs, st_p, st_s, big = _layer(i, xp, xs, bsz, seq, dense, lw, state, big)
        sp_all.append(st_p)
        ss_all.append(st_s)
    stack = lambda sts, j: jnp.stack([s[j] for s in sts], axis=0)
    sample_states = [big[0] if j == 2 else big[1] if j == 5 else stack(ss_all, j) for j in range(8)]
    return ((xp.reshape(bsz, seq, d), xs.reshape(nsamp, 1, d))
            + tuple(stack(sp_all, j) for j in range(8))
            + tuple(sample_states))
```

```python
import functools

import jax
import jax.numpy as jnp
from jax import lax
from jax.experimental import pallas as pl
from jax.experimental.pallas import tpu as pltpu

F32 = jnp.float32
BF16 = jnp.bfloat16

D_MODEL = 2048
DEPTH = 2
D_A = 512
S5_GROUP = 16
S5_GROUPS = 32
S5_STATE = 64
S5_LANES = S5_GROUPS * S5_STATE
S5_CHUNK = 512
S5_NCHUNK = S5_LANES // S5_CHUNK
D_B = 1024
H_B = 4
DV_B = 256
DQK_B = 128
D_C = 512
H_C = 4
DK_C = 128
DV_C = 128
GDN_CONV = 4
D_FF = 8192
FFN_CONV = 3
PLE_DIM = 256
EPS = 1e-6

LANE = 128
SUBLANE = 8
VMEM_LIMIT = 56 * 1024 * 1024

COL_U, COL_QB, COL_KB, COL_VB, COL_OB, COL_QKV, COL_Z, COL_SMALL = 0, 4, 8, 12, 20, 28, 40, 44
N_HEAD = COL_OB * LANE
N_MID = (COL_SMALL - COL_OB) * LANE
PROJ_TILE = 512
N_PROJ = N_HEAD + N_MID + PROJ_TILE
SM_I, SM_F, SM_BETA, SM_A = 0, 4, 8, 12

MLSTM_CHUNK = 128
MLSTM_TILE = 256
GDN_CHUNK = 64
GDN_TILE = 256
S5_TILE = 256
SAMPLE_BLOCK = 8
Y_B, Y_A, Y_C = 0, D_B, D_B + D_A

NT = (((1,), (1,)), ((), ()))
TN = (((0,), (0,)), ((), ()))


def _cparams(sem):
    return pltpu.CompilerParams(dimension_semantics=sem, vmem_limit_bytes=VMEM_LIMIT)


def _dot(a, b):
    return jnp.dot(a, b, preferred_element_type=F32)


def _dot_hi(a, b):
    return jnp.dot(a, b, preferred_element_type=F32, precision=lax.Precision.HIGHEST)


def _gelu(x):
    return 0.5 * x * (1.0 + jnp.tanh(0.7978845608028654 * (x + 0.044715 * (x * x * x))))


def _sigmoid(x):
    return 1.0 / (1.0 + jnp.exp(-x))


def _silu(x):
    return x * _sigmoid(x)


def _softplus(x):
    return jnp.maximum(x, 0.0) + jnp.log1p(jnp.exp(-jnp.abs(x)))


def _log_sigmoid(x):
    return -_softplus(-x)


def _rms(x, w):
    return x * lax.rsqrt(jnp.mean(x * x, axis=-1, keepdims=True) + EPS) * w


def _layered(layer, shape, imap):
    return pl.BlockSpec((None,) + shape, lambda *g: (layer,) + imap(*g))


def _pick(n, cands):
    for c in cands:
        if n % c == 0:
            return c
    return n


def _norm_proj_kernel(x_ref, nw_ref, wa_ref, wb_ref, ws_ref, proj_ref, h_ref, *, na, nb):
    j = pl.program_id(1)

    @pl.when(j == 0)
    def _():
        h_ref[...] = _rms(x_ref[...], nw_ref[...]).astype(BF16)

    @pl.when(j < na)
    def _():
        proj_ref[...] = _dot(h_ref[...], wa_ref[...].astype(BF16))

    @pl.when(jnp.logical_and(j >= na, j < na + nb))
    def _():
        proj_ref[...] = _dot(h_ref[...], wb_ref[...].astype(BF16))

    @pl.when(j >= na + nb)
    def _():
        proj_ref[...] = jnp.zeros_like(proj_ref)
        proj_ref[:, 0:LANE] = _dot(h_ref[...], ws_ref[...].astype(BF16))


def _norm_proj(x, nw, w_head, w_mid, w_small, layer):
    m, d = x.shape
    tm = _pick(m, (1024, 512, 256, 128))
    tn = PROJ_TILE
    na, nb = w_head.shape[1], w_mid.shape[1]
    tile = lambda pick: pl.BlockSpec((None, None, d, tn), lambda i, j: (layer, pick(j), 0, 0))
    return pl.pallas_call(
        functools.partial(_norm_proj_kernel, na=na, nb=nb),
        grid=(m // tm, na + nb + 1),
        in_specs=[pl.BlockSpec((tm, d), lambda i, j: (i, 0)),
                  _layered(layer, (1, d), lambda i, j: (0, 0)),
                  tile(lambda j: jnp.minimum(j, na - 1)),
                  tile(lambda j: jnp.clip(j - na, 0, nb - 1)),
                  _layered(layer, (d, LANE), lambda i, j: (0, 0))],
        out_specs=[pl.BlockSpec((None, tm, tn), lambda i, j: (j, i, 0)),
                   pl.BlockSpec((tm, d), lambda i, j: (i, 0))],
        out_shape=[jax.ShapeDtypeStruct((na + nb + 1, m, tn), F32), jax.ShapeDtypeStruct((m, d), BF16)],
        compiler_params=_cparams(("arbitrary", "arbitrary")),
        name="norm_proj",
    )(x, nw, w_head, w_mid, w_small)


def _proj_cols(rows, width, col, row_block):
    t, off = divmod(col * LANE, PROJ_TILE)
    assert off % width == 0 and off + width <= PROJ_TILE
    return pl.BlockSpec((None, rows, width), lambda *g: (t, row_block(*g), off // width))


class _TiledCols:
    def __init__(self, ref):
        self.ref = ref
        self.shape = (ref.shape[1], ref.shape[0] * ref.shape[2])

    def __getitem__(self, idx):
        rows, cols = idx
        a, b = cols.start, cols.stop
        pieces = []
        while a < b:
            t = a // PROJ_TILE
            e = min(b, (t + 1) * PROJ_TILE)
            pieces.append(self.ref[t, rows, a - t * PROJ_TILE:e - t * PROJ_TILE])
            a = e
        return pieces[0] if len(pieces) == 1 else jnp.concatenate(pieces, axis=1)


def _s5_disc(are, aim, ldt):
    dt = jnp.exp(ldt)
    mag = jnp.exp(dt * are)
    abr = mag * jnp.cos(dt * aim)
    abi = mag * jnp.sin(dt * aim)
    den = are * are + aim * aim
    zr = abr - 1.0
    fre = (zr * are + abi * aim) / den
    fim = (abi * are - zr * aim) / den
    return abr, abi, fre, fim


def _s5_glu(y, u, d_ref, wglu_ref):
    z = _gelu(y + d_ref[...] * u)
    return z * _sigmoid(_dot(z.astype(BF16), wglu_ref[...]))


def _s5_prompt_body(step, u_ref, are_ref, aim_ref, ldt_ref, bre_ref, bim_ref, cre_ref, cim_ref,
                    d_ref, wglu_ref, y_ref, sre_ref, sim_ref, xr_s, xi_s, y_s, car_re, car_im,
                    f_s, tab_s, *, nt):
    @pl.when(step % nt == 0)
    def _():
        car_re[...] = jnp.zeros_like(car_re)
        car_im[...] = jnp.zeros_like(car_im)
        row = lax.broadcasted_iota(jnp.int32, (SUBLANE, S5_LANES), 0)
        abr, abi, fre, fim = _s5_disc(are_ref[...], aim_ref[...], ldt_ref[...])
        f_s[0:1, :] = fre
        f_s[1:2, :] = fim
        pr, pi = abr, abi
        for lvl, s in enumerate((1, 2, 4)):
            tab_s[2 * lvl] = jnp.where(row >= s, pr, 0.0)
            tab_s[2 * lvl + 1] = jnp.where(row >= s, pi, 0.0)
            pr, pi = pr * pr - pi * pi, 2.0 * pr * pi
        cwr = jnp.zeros((SUBLANE, S5_LANES), F32)
        cwi = jnp.zeros((SUBLANE, S5_LANES), F32)
        pr, pi = abr, abi
        for r in range(SUBLANE):
            cwr = jnp.where(row == r, pr, cwr)
            cwi = jnp.where(row == r, pi, cwi)
            pr, pi = pr * abr - pi * abi, pr * abi + pi * abr
        tab_s[6] = cwr
        tab_s[7] = cwi

    tt = u_ref.shape[0]
    u = u_ref[...]
    ub = u.astype(BF16)
    for j in range(S5_NCHUNK):
        sl = slice(S5_CHUNK * j, S5_CHUNK * (j + 1))
        fre, fim = f_s[0:1, sl], f_s[1:2, sl]
        uj = ub[:, LANE * j:LANE * (j + 1)]
        bur = _dot(uj, bre_ref[j])
        bui = _dot(uj, bim_ref[j])
        xr = (fre * bur - fim * bui).reshape(tt // SUBLANE, SUBLANE, S5_CHUNK)
        xi = (fre * bui + fim * bur).reshape(tt // SUBLANE, SUBLANE, S5_CHUNK)
        for lvl, s in enumerate((1, 2, 4)):
            mr = tab_s[2 * lvl, :, sl]
            mi = tab_s[2 * lvl + 1, :, sl]
            sr = pltpu.roll(xr, s, axis=1)
            si = pltpu.roll(xi, s, axis=1)
            xr, xi = xr + mr * sr - mi * si, xi + mr * si + mi * sr
        xr_s[...] = xr.reshape(tt, S5_CHUNK)
        xi_s[...] = xi.reshape(tt, S5_CHUNK)
        cwr = tab_s[6, :, sl]
        cwi = tab_s[7, :, sl]

        def body(g, carry, cwr=cwr, cwi=cwi):
            cr, ci = carry
            r0 = pl.multiple_of(g * SUBLANE, SUBLANE)
            gr = xr_s[pl.ds(r0, SUBLANE), :] + cwr * cr - cwi * ci
            gi = xi_s[pl.ds(r0, SUBLANE), :] + cwr * ci + cwi * cr
            xr_s[pl.ds(r0, SUBLANE), :] = gr
            xi_s[pl.ds(r0, SUBLANE), :] = gi
            return gr[SUBLANE - 1:SUBLANE, :], gi[SUBLANE - 1:SUBLANE, :]

        cr, ci = lax.fori_loop(0, tt // SUBLANE, body, (car_re[:, sl], car_im[:, sl]), unroll=4)
        car_re[:, sl] = cr
        car_im[:, sl] = ci
        y_s[:, LANE * j:LANE * (j + 1)] = (_dot(xr_s[...].astype(BF16), cre_ref[j])
                                           - _dot(xi_s[...].astype(BF16), cim_ref[j]))
    y_ref[...] = _s5_glu(y_s[...], u, d_ref, wglu_ref).astype(BF16)
    sre_ref[0] = car_re[...]
    sim_ref[0] = car_im[...]


def _run_parts(name, parts):
    n = parts[0]["n"]
    assert all(p["n"] == n for p in parts)
    n_in = [len(p["args"]) for p in parts]
    n_out = [len(p["out_shape"]) for p in parts]
    n_scr = [len(p["scratch"]) for p in parts]

    def kern(*refs):
        step = pl.program_id(0)
        ins, outs, scr = refs[:sum(n_in)], refs[sum(n_in):sum(n_in) + sum(n_out)], refs[sum(n_in) + sum(n_out):]
        a = b = c = 0
        for p, na, nb, nc in zip(parts, n_in, n_out, n_scr):
            p["body"](step, *ins[a:a + na], *outs[b:b + nb], *scr[c:c + nc])
            a, b, c = a + na, b + nb, c + nc

    flat = lambda key: [x for p in parts for x in p[key]]
    outs = pl.pallas_call(
        kern, grid=(n,), in_specs=flat("in_specs"), out_specs=flat("out_specs"), out_shape=flat("out_shape"),
        scratch_shapes=flat("scratch"), compiler_params=_cparams(("arbitrary",)), name=name,
    )(*flat("args"))
    res, b = [], 0
    for nb in n_out:
        res.append(list(outs[b:b + nb]))
        b += nb
    return res


def _s5_prompt_part(proj, bsz, seq, sp):
    tt = _pick(seq, (S5_TILE, 128, 64, 32, 16, 8))
    nt = seq // tt
    const2 = lambda i: (0, 0)
    const3 = lambda i: (0, 0, 0)
    row_spec = pl.BlockSpec((1, S5_LANES), const2)
    return dict(
        n=bsz * nt,
        body=functools.partial(_s5_prompt_body, nt=nt),
        in_specs=[_proj_cols(tt, D_A, COL_U, lambda i: i),
                  row_spec, row_spec, row_spec,
                  pl.BlockSpec((S5_NCHUNK, LANE, S5_CHUNK), const3),
                  pl.BlockSpec((S5_NCHUNK, LANE, S5_CHUNK), const3),
                  pl.BlockSpec((S5_NCHUNK, S5_CHUNK, LANE), const3),
                  pl.BlockSpec((S5_NCHUNK, S5_CHUNK, LANE), const3),
                  pl.BlockSpec((1, D_A), const2),
                  pl.BlockSpec((D_A, D_A), const2)],
        args=[proj, sp["are"], sp["aim"], sp["ldt"], sp["bre"], sp["bim"], sp["cre"], sp["cim"],
              sp["d"], sp["wglu"]],
        out_specs=[pl.BlockSpec((tt, D_A), lambda i: (i, 0)),
                   pl.BlockSpec((1, 1, S5_LANES), lambda i: (i // nt, 0, 0)),
                   pl.BlockSpec((1, 1, S5_LANES), lambda i: (i // nt, 0, 0))],
        out_shape=[jax.ShapeDtypeStruct((bsz * seq, D_A), BF16),
                   jax.ShapeDtypeStruct((bsz, 1, S5_LANES), F32),
                   jax.ShapeDtypeStruct((bsz, 1, S5_LANES), F32)],
        scratch=[pltpu.VMEM((tt, S5_CHUNK), F32), pltpu.VMEM((tt, S5_CHUNK), F32),
                 pltpu.VMEM((tt, D_A), F32),
                 pltpu.VMEM((1, S5_LANES), F32), pltpu.VMEM((1, S5_LANES), F32),
                 pltpu.VMEM((2, S5_LANES), F32), pltpu.VMEM((8, SUBLANE, S5_LANES), F32)])


def _s5_prompt(proj, bsz, seq, sp):
    return _run_parts("s5_prompt", [_s5_prompt_part(proj, bsz, seq, sp)])[0]


def _mlstm_prompt_kernel(q_ref, k_ref, v0_ref, v1_ref, o0_ref, o1_ref, sm_ref, bias_ref, nw_ref,
                         y_ref, c_ref, n_ref, m_ref, *, chunk):
    @pl.when(pl.program_id(1) == 0)
    def _():
        c_ref[...] = jnp.zeros_like(c_ref)
        n_ref[...] = jnp.zeros_like(n_ref)
        m_ref[...] = jnp.zeros_like(m_ref)

    tt = q_ref.shape[0]
    c = chunk
    nchunk = tt // c
    smb = sm_ref[...] + bias_ref[...]
    lf_all = _log_sigmoid(smb)
    li_t = smb.T
    lf_t = lf_all.T
    rowi = lax.broadcasted_iota(jnp.int32, (c, c), 0)
    coli = lax.broadcasted_iota(jnp.int32, (c, c), 1)
    causal = rowi >= coli
    lane = lax.broadcasted_iota(jnp.int32, (1, LANE), 1)
    m_row = m_ref[0]
    probs = [(ci, h) for ci in range(nchunk) for h in range(H_B)]
    idx = {p: i for i, p in enumerate(probs)}
    rows = lambda ci: slice(c * ci, c * (ci + 1))
    vo_refs = [(v0_ref, o0_ref) if h < 2 else (v1_ref, o1_ref) for h in range(H_B)]
    q = [q_ref[rows(ci), DQK_B * h:DQK_B * (h + 1)] * (DQK_B ** -0.5) for ci, h in probs]
    k = [k_ref[rows(ci), DQK_B * h:DQK_B * (h + 1)] for ci, h in probs]
    qb = [x.astype(BF16) for x in q]
    kb = [x.astype(BF16) for x in k]
    vb = [vo_refs[h][0][rows(ci), DV_B * (h % 2):DV_B * (h % 2 + 1)].astype(BF16) for ci, h in probs]
    qkt = [lax.dot_general(a, b, NT, preferred_element_type=F32) for a, b in zip(qb, kb)]
    li_c = [smb[rows(ci), SM_I + h:SM_I + h + 1] for ci, h in probs]
    li_r = [li_t[SM_I + h:SM_I + h + 1, rows(ci)] for ci, h in probs]
    bc_c = [jnp.sum(jnp.where(causal, lf_t[SM_F + h:SM_F + h + 1, rows(ci)], 0.0), axis=1, keepdims=True)
            for ci, h in probs]
    bc_r = [jnp.sum(jnp.where(rowi <= coli, lf_all[rows(ci), SM_F + h:SM_F + h + 1], 0.0), axis=0, keepdims=True)
            for ci, h in probs]
    dmat = [jnp.where(causal, a - b + r, -jnp.inf) for a, b, r in zip(bc_c, bc_r, li_r)]
    dmax = [jnp.max(x, axis=1, keepdims=True) for x in dmat]
    b_last = [x[c - 1:c, :] for x in bc_c]
    expo = [bl - a + l for bl, a, l in zip(b_last, bc_c, li_c)]
    emax = [jnp.max(x, axis=0, keepdims=True) for x in expo]
    m_prev, m_new = [None] * len(probs), [None] * len(probs)
    for h in range(H_B):
        m = m_row[:, h:h + 1]
        for ci in range(nchunk):
            i = idx[ci, h]
            m_prev[i] = m
            m = jnp.maximum(b_last[i] + m, emax[i])
            m_new[i] = m
    inter = [a + m for a, m in zip(bc_c, m_prev)]
    m_t = [jnp.maximum(a, b) for a, b in zip(inter, dmax)]
    w_inter = [jnp.exp(a - b) for a, b in zip(inter, m_t)]
    s = [x * jnp.exp(d - m) for x, d, m in zip(qkt, dmat, m_t)]
    sv = [_dot(x.astype(BF16), v) for x, v in zip(s, vb)]
    ssum = [jnp.sum(x, axis=1, keepdims=True) for x in s]
    emt = [jnp.exp(-x) for x in m_t]
    decay = [jnp.exp(bl + mp - mn) for bl, mp, mn in zip(b_last, m_prev, m_new)]
    kw = [jnp.exp(e - mn) * kx for e, mn, kx in zip(expo, m_new, k)]
    kv = [lax.dot_general(x.astype(BF16), v, TN, preferred_element_type=F32) for x, v in zip(kw, vb)]
    ksum = [jnp.sum(x, axis=0, keepdims=True) for x in kw]

    cst = [c_ref[0, h] for h in range(H_B)]
    nrow = [n_ref[0, h:h + 1, :] for h in range(H_B)]
    for ci in range(nchunk):
        ids = [idx[ci, h] for h in range(H_B)]
        qc = [_dot(qb[i], cst[h].astype(BF16)) for h, i in enumerate(ids)]
        num = [sv[i] + w_inter[i] * qc[h] for h, i in enumerate(ids)]
        nq = [ssum[i] + w_inter[i] * jnp.sum(q[i] * nrow[h], axis=1, keepdims=True) for h, i in enumerate(ids)]
        hh = [num[h] / jnp.maximum(jnp.abs(nq[h]), emt[i]) for h, i in enumerate(ids)]
        cst = [decay[i] * cst[h] + kv[i] for h, i in enumerate(ids)]
        nrow = [decay[i] * nrow[h] + ksum[i] for h, i in enumerate(ids)]
        for h in range(H_B):
            og = vo_refs[h][1][rows(ci), DV_B * (h % 2):DV_B * (h % 2 + 1)]
            y_ref[rows(ci), DV_B * h:DV_B * (h + 1)] = (_rms(hh[h], nw_ref[h:h + 1, :]) * _sigmoid(og)).astype(BF16)
    m_out = m_row
    for h in range(H_B):
        c_ref[0, h] = cst[h]
        n_ref[0, h:h + 1, :] = nrow[h]
        m_out = jnp.where(lane == h, m_new[idx[nchunk - 1, h]], m_out)
    m_ref[0] = m_out


def _mlstm_prompt(proj, bsz, seq, mp):
    chunk = _pick(seq, (MLSTM_CHUNK,))
    c = _pick(seq, (MLSTM_TILE, MLSTM_CHUNK))
    nt = seq // c
    row = lambda b, t: b * nt + t
    const2 = lambda b, t: (0, 0)
    return pl.pallas_call(
        functools.partial(_mlstm_prompt_kernel, chunk=chunk),
        grid=(bsz, nt),
        in_specs=[_proj_cols(c, 512, COL_QB, row),
                  _proj_cols(c, 512, COL_KB, row),
                  _proj_cols(c, 512, COL_VB, row),
                  _proj_cols(c, 512, COL_VB + 4, row),
                  _proj_cols(c, 512, COL_OB, row),
                  _proj_cols(c, 512, COL_OB + 4, row),
                  _proj_cols(c, LANE, COL_SMALL, row),
                  pl.BlockSpec((1, LANE), const2),
                  pl.BlockSpec((H_B, DV_B), const2)],
        out_specs=[pl.BlockSpec((c, D_B), lambda b, t: (b * nt + t, 0)),
                   pl.BlockSpec((1, H_B, DQK_B, DV_B), lambda b, t: (b, 0, 0, 0)),
                   pl.BlockSpec((1, H_B, DQK_B), lambda b, t: (b, 0, 0)),
                   pl.BlockSpec((1, 1, LANE), lambda b, t: (b, 0, 0))],
        out_shape=[jax.ShapeDtypeStruct((bsz * seq, D_B), BF16),
                   jax.ShapeDtypeStruct((bsz, H_B, DQK_B, DV_B), F32),
                   jax.ShapeDtypeStruct((bsz, H_B, DQK_B), F32),
                   jax.ShapeDtypeStruct((bsz, 1, LANE), F32)],
        compiler_params=_cparams(("arbitrary", "arbitrary")),
        name="mlstm_prompt",
    )(proj, proj, proj, proj, proj, proj, proj, mp["bias"], mp["norm"])


def _split_bf16(a):
    hi = a.astype(BF16)
    return hi, (a - hi.astype(F32)).astype(BF16)


def _dot_x3(a, b):
    ah, al = _split_bf16(a)
    bh, bl = _split_bf16(b)
    return _dot(ah, bh) + _dot(ah, bl) + _dot(al, bh)


def _unit_lower_inverses(lmats):
    c = lmats[0].shape[0]
    eye = (lax.broadcasted_iota(jnp.int32, (c, c), 0) == lax.broadcasted_iota(jnp.int32, (c, c), 1)).astype(F32)
    hi_half = lax.broadcasted_iota(jnp.int32, (c, 2 * c), 1) >= c
    ms = [jnp.concatenate([-l, eye], axis=1) for l in lmats]
    span = 1
    while span < c:
        split = [_split_bf16(m) for m in ms]
        ms = [_dot(mh[:, 0:c], mh) + _dot(mh[:, 0:c], ml) + _dot(ml[:, 0:c], mh) + jnp.where(hi_half, m, 0.0)
              for m, (mh, ml) in zip(ms, split)]
        span *= 2
    return [m[:, c:2 * c] for m in ms]


def _gdn_prompt_kernel(q_ref, k_ref, v_ref, z_ref, sm_ref, bias_ref, alog_ref, cw_ref, nw_ref,
                       y_ref, s_ref, xb_s, *, chunk):
    tt = q_ref.shape[0]
    c = chunk

    @pl.when(pl.program_id(1) == 0)
    def _():
        s_ref[...] = jnp.zeros_like(s_ref)
        xb_s[0:SUBLANE, :] = jnp.zeros((SUBLANE, 3 * D_C), F32)

    xb_s[SUBLANE:SUBLANE + tt, 0:D_C] = q_ref[...]
    xb_s[SUBLANE:SUBLANE + tt, D_C:2 * D_C] = k_ref[...]
    xb_s[SUBLANE:SUBLANE + tt, 2 * D_C:3 * D_C] = v_ref[...]
    conv = cw_ref[GDN_CONV - 1:GDN_CONV, :] * xb_s[SUBLANE:SUBLANE + tt, :]
    for j in range(GDN_CONV - 1):
        off = SUBLANE - (GDN_CONV - 1) + j
        conv = conv + cw_ref[j:j + 1, :] * xb_s[off:off + tt, :]
    xb_s[0:SUBLANE, :] = xb_s[tt:tt + SUBLANE, :]
    qkv = _silu(conv)

    sm = sm_ref[...]
    beta_all = _sigmoid(sm)
    g_all = -jnp.exp(alog_ref[...]) * _softplus(sm + bias_ref[...])
    g_t = g_all.T
    rowi = lax.broadcasted_iota(jnp.int32, (c, c), 0)
    coli = lax.broadcasted_iota(jnp.int32, (c, c), 1)
    incl = rowi >= coli
    strict = rowi > coli

    nchunk = tt // c
    probs = [(ci, h) for ci in range(nchunk) for h in range(H_C)]
    rows = lambda ci: slice(c * ci, c * (ci + 1))
    l2 = lambda x: x * lax.rsqrt(jnp.sum(x * x, axis=-1, keepdims=True) + EPS)
    q = [l2(qkv[rows(ci), DK_C * h:DK_C * (h + 1)]) * (DK_C ** -0.5) for ci, h in probs]
    k = [l2(qkv[rows(ci), D_C + DK_C * h:D_C + DK_C * (h + 1)]) for ci, h in probs]
    v = [qkv[rows(ci), 2 * D_C + DV_C * h:2 * D_C + DV_C * (h + 1)] for ci, h in probs]
    beta_c = [beta_all[rows(ci), SM_BETA + h:SM_BETA + h + 1] for ci, h in probs]
    gc_c = [jnp.sum(jnp.where(incl, g_t[SM_A + h:SM_A + h + 1, rows(ci)], 0.0), axis=1, keepdims=True)
            for ci, h in probs]
    gc_r = [jnp.sum(jnp.where(rowi <= coli, g_all[rows(ci), SM_A + h:SM_A + h + 1], 0.0), axis=0, keepdims=True)
            for ci, h in probs]
    gam = [jnp.exp(jnp.where(incl, a - b, -jnp.inf)) for a, b in zip(gc_c, gc_r)]
    qb = [x.astype(BF16) for x in q]
    kb = [x.astype(BF16) for x in k]
    kk = [lax.dot_general(x, x, NT, preferred_element_type=F32) for x in kb]
    qk = [(lax.dot_general(a, b, NT, preferred_element_type=F32) * g).astype(BF16) for a, b, g in zip(qb, kb, gam)]
    egc = [jnp.exp(x) for x in gc_c]
    tinv = _unit_lower_inverses([jnp.where(strict, b * g * x, 0.0) for b, g, x in zip(beta_c, gam, kk)])
    tr = [_dot_x3(t, jnp.concatenate([b * vv, (b * e) * kx], axis=1))
          for t, b, vv, e, kx in zip(tinv, beta_c, v, egc, k)]
    u0 = [x[:, 0:DV_C] for x in tr]
    wb = [x[:, DV_C:].astype(BF16) for x in tr]
    g_last = [x[c - 1:c, :] for x in gc_c]
    kw = [(jnp.exp(gl - gc) * kx).astype(BF16) for gl, gc, kx in zip(g_last, gc_c, k)]
    e_last = [jnp.exp(x) for x in g_last]

    st = [s_ref[0, h] for h in range(H_C)]
    for ci in range(nchunk):
        ids = [ci * H_C + h for h in range(H_C)]
        stb = [x.astype(BF16) for x in st]
        ub = [(u0[i] - _dot(wb[i], stb[h])).astype(BF16) for h, i in enumerate(ids)]
        qs = [_dot(qb[i], stb[h]) for h, i in enumerate(ids)]
        st = [e_last[i] * st[h] + lax.dot_general(kw[i], ub[h], TN, preferred_element_type=F32)
              for h, i in enumerate(ids)]
        o = [egc[i] * qs[h] + _dot(qk[i], ub[h]) for h, i in enumerate(ids)]
        for h in range(H_C):
            zz = z_ref[rows(ci), DV_C * h:DV_C * (h + 1)]
            y_ref[rows(ci), DV_C * h:DV_C * (h + 1)] = (_rms(o[h], nw_ref[...]) * _silu(zz)).astype(BF16)
    for h in range(H_C):
        s_ref[0, h] = st[h]


def _gdn_prompt(proj, bsz, seq, gp):
    c = _pick(seq, (GDN_CHUNK,))
    tt = _pick(seq, (GDN_TILE,))
    nt = seq // tt
    row = lambda b, t: b * nt + t
    const2 = lambda b, t: (0, 0)
    return pl.pallas_call(
        functools.partial(_gdn_prompt_kernel, chunk=c),
        grid=(bsz, nt),
        in_specs=[_proj_cols(tt, D_C, COL_QKV, row),
                  _proj_cols(tt, D_C, COL_QKV + 4, row),
                  _proj_cols(tt, D_C, COL_QKV + 8, row),
                  _proj_cols(tt, D_C, COL_Z, row),
                  _proj_cols(tt, LANE, COL_SMALL, row),
                  pl.BlockSpec((1, LANE), const2),
                  pl.BlockSpec((1, LANE), const2),
                  pl.BlockSpec((GDN_CONV, 3 * D_C), const2),
                  pl.BlockSpec((1, DV_C), const2)],
        out_specs=[pl.BlockSpec((tt, D_C), lambda b, t: (b * nt + t, 0)),
                   pl.BlockSpec((1, H_C, DK_C, DV_C), lambda b, t: (b, 0, 0, 0))],
        out_shape=[jax.ShapeDtypeStruct((bsz * seq, D_C), BF16),
                   jax.ShapeDtypeStruct((bsz, H_C, DK_C, DV_C), F32)],
        scratch_shapes=[pltpu.VMEM((SUBLANE + tt, 3 * D_C), F32)],
        compiler_params=_cparams(("arbitrary", "arbitrary")),
        name="gdn_prompt",
    )(proj, proj, proj, proj, proj, gp["bias"], gp["alog"], gp["conv_w"], gp["norm"])


def _to_col(row, eye):
    return jnp.sum(jnp.where(eye, row, 0.0), axis=1, keepdims=True)


N_SAMPLE_INPUTS = 23


def _sample_mixers_kernel(*refs, fill_layer):
    (proj_ref, sre_ref, sim_ref, c_ref, n_ref, m_ref, gs_ref, gbuf_ref,
     are_ref, aim_ref, ldt_ref, bre_ref, bim_ref, cre_ref, cim_ref, d_ref, wglu_ref,
     mbias_ref, mnorm_ref, gbias_ref, alog_ref, gcw_ref, gnorm_ref) = refs[:N_SAMPLE_INPUTS]
    (y_ref, sre_o, sim_o, c_o, n_o, m_o, gs_o, gbuf_o,
     qkv_s, qn_s, kn_s, beta_s, g_s, li_s, lf_s) = refs[-15:]
    proj_ref = _TiledCols(proj_ref)
    bb = proj_ref.shape[0]
    if fill_layer is not None:
        for l in range(c_o.shape[0]):
            if l != fill_layer:
                c_o[l] = jnp.zeros(c_o.shape[1:], F32)
                gs_o[l] = jnp.zeros(gs_o.shape[1:], F32)
        c_o = c_o.at[fill_layer]
        gs_o = gs_o.at[fill_layer]

    u = proj_ref[:, COL_U * LANE:COL_U * LANE + D_A]
    ub = u.astype(BF16)
    ys = []
    for j in range(S5_NCHUNK):
        sl = slice(S5_CHUNK * j, S5_CHUNK * (j + 1))
        abr, abi, fre, fim = _s5_disc(are_ref[:, sl], aim_ref[:, sl], ldt_ref[:, sl])
        uj = ub[:, LANE * j:LANE * (j + 1)]
        bur = _dot(uj, bre_ref[j])
        bui = _dot(uj, bim_ref[j])
        s0r = sre_ref[:, sl]
        s0i = sim_ref[:, sl]
        xr = fre * bur - fim * bui + abr * s0r - abi * s0i
        xi = fre * bui + fim * bur + abr * s0i + abi * s0r
        sre_o[:, sl] = xr
        sim_o[:, sl] = xi
        ys.append(_dot(xr.astype(BF16), cre_ref[j]) - _dot(xi.astype(BF16), cim_ref[j]))
    y_a = jnp.concatenate(ys, axis=1)
    y_ref[:, Y_A:Y_A + D_A] = _s5_glu(y_a, u, d_ref, wglu_ref)

    sm = proj_ref[:, COL_SMALL * LANE:(COL_SMALL + 1) * LANE]
    smb = sm + mbias_ref[...]
    li_s[...] = smb
    lf_s[...] = _log_sigmoid(smb)
    beta_s[...] = _sigmoid(sm)
    g_s[...] = -jnp.exp(alog_ref[...]) * _softplus(sm + gbias_ref[...])

    xnew = proj_ref[:, COL_QKV * LANE:COL_QKV * LANE + 3 * D_C]
    conv = gcw_ref[GDN_CONV - 1:GDN_CONV, :] * xnew
    for j in range(GDN_CONV - 1):
        conv = conv + gcw_ref[j:j + 1, :] * gbuf_ref[j]
        if j > 0:
            gbuf_o[j - 1] = gbuf_ref[j]
    gbuf_o[GDN_CONV - 2] = xnew
    qkv = _silu(conv)
    qkv_s[...] = qkv
    for h in range(H_C):
        q = qkv[:, DK_C * h:DK_C * (h + 1)]
        k = qkv[:, D_C + DK_C * h:D_C + DK_C * (h + 1)]
        qn_s[:, DK_C * h:DK_C * (h + 1)] = q * lax.rsqrt(jnp.sum(q * q, axis=-1, keepdims=True) + EPS) * (DK_C ** -0.5)
        kn_s[:, DK_C * h:DK_C * (h + 1)] = k * lax.rsqrt(jnp.sum(k * k, axis=-1, keepdims=True) + EPS)

    assert 4 * H_B * bb == LANE and DQK_B == LANE and DK_C == LANE and H_B == H_C
    tiles = ([proj_ref[:, COL_QB * LANE + DQK_B * h:COL_QB * LANE + DQK_B * (h + 1)] * (DQK_B ** -0.5)
              for h in range(H_B)]
             + [proj_ref[:, COL_KB * LANE + DQK_B * h:COL_KB * LANE + DQK_B * (h + 1)] for h in range(H_B)]
             + [qn_s[:, DK_C * h:DK_C * (h + 1)] for h in range(H_C)]
             + [kn_s[:, DK_C * h:DK_C * (h + 1)] for h in range(H_C)])
    stacked_t = jnp.concatenate(tiles, axis=0).T

    def col_of(kind, b, h):
        j = (kind * H_B + h) * bb + b
        return stacked_t[:, j:j + 1]

    lane = lax.broadcasted_iota(jnp.int32, (1, H_B), 1)

    probs = [(b, h) for b in range(bb) for h in range(H_B)]
    rsl = lambda b: slice(b, b + 1)

    q = [proj_ref[rsl(b), COL_QB * LANE + DQK_B * h:COL_QB * LANE + DQK_B * (h + 1)] * (DQK_B ** -0.5)
         for b, h in probs]
    k = [proj_ref[rsl(b), COL_KB * LANE + DQK_B * h:COL_KB * LANE + DQK_B * (h + 1)] for b, h in probs]
    v = [proj_ref[rsl(b), COL_VB * LANE + DV_B * h:COL_VB * LANE + DV_B * (h + 1)] for b, h in probs]
    li = [li_s[rsl(b), SM_I + h:SM_I + h + 1] for b, h in probs]
    inter = [lf_s[rsl(b), SM_F + h:SM_F + h + 1] + m_ref[rsl(b), h:h + 1] for b, h in probs]
    m_t = [jnp.maximum(a, c) for a, c in zip(inter, li)]
    w_intra = [jnp.exp(a - c) for a, c in zip(li, m_t)]
    w_inter = [jnp.exp(a - c) for a, c in zip(inter, m_t)]
    qcol = [col_of(0, b, h) for b, h in probs]
    kcol = [col_of(1, b, h) for b, h in probs]
    s = [jnp.sum(a * c, axis=1, keepdims=True) * w for a, c, w in zip(q, k, w_intra)]
    cst = [c_ref[b, h] for b, h in probs]
    nrow = [n_ref[b, h:h + 1, :] for b, h in probs]
    qc = [jnp.sum(a * c, axis=0, keepdims=True) for a, c in zip(qcol, cst)]
    for i, (b, h) in enumerate(probs):
        c_o[b, h] = w_inter[i] * cst[i] + (w_intra[i] * kcol[i]) * v[i]
        n_o[b, h:h + 1, :] = w_inter[i] * nrow[i] + w_intra[i] * k[i]
    num = [s[i] * v[i] + w_inter[i] * qc[i] for i in range(len(probs))]
    nq = [s[i] + w_inter[i] * jnp.sum(q[i] * nrow[i], axis=1, keepdims=True) for i in range(len(probs))]
    hh = [num[i] / jnp.maximum(jnp.abs(nq[i]), jnp.exp(-m_t[i])) for i in range(len(probs))]
    for i, (b, h) in enumerate(probs):
        og = proj_ref[rsl(b), COL_OB * LANE + DV_B * h:COL_OB * LANE + DV_B * (h + 1)]
        y_ref[rsl(b), Y_B + DV_B * h:Y_B + DV_B * (h + 1)] = _rms(hh[i], mnorm_ref[h:h + 1, :]) * _sigmoid(og)
    for b in range(bb):
        m_out = m_ref[rsl(b), :]
        for h in range(H_B):
            m_out = jnp.where(lane == h, m_t[b * H_B + h], m_out)
        m_o[rsl(b), :] = m_out

    q = [qn_s[rsl(b), DK_C * h:DK_C * (h + 1)] for b, h in probs]
    k = [kn_s[rsl(b), DK_C * h:DK_C * (h + 1)] for b, h in probs]
    v = [qkv_s[rsl(b), 2 * D_C + DV_C * h:2 * D_C + DV_C * (h + 1)] for b, h in probs]
    beta = [beta_s[rsl(b), SM_BETA + h:SM_BETA + h + 1] for b, h in probs]
    eg = [jnp.exp(g_s[rsl(b), SM_A + h:SM_A + h + 1]) for b, h in probs]
    qcol = [col_of(2, b, h) for b, h in probs]
    kcol = [col_of(3, b, h) for b, h in probs]
    st = [gs_ref[b, h] for b, h in probs]
    ks = [jnp.sum(a * c, axis=0, keepdims=True) for a, c in zip(kcol, st)]
    qs = [jnp.sum(a * c, axis=0, keepdims=True) for a, c in zip(qcol, st)]
    un = [beta[i] * (v[i] - eg[i] * ks[i]) for i in range(len(probs))]
    for i, (b, h) in enumerate(probs):
        gs_o[b, h] = eg[i] * st[i] + kcol[i] * un[i]
    o = [eg[i] * qs[i] + jnp.sum(q[i] * k[i], axis=1, keepdims=True) * un[i] for i in range(len(probs))]
    for i, (b, h) in enumerate(probs):
        zz = proj_ref[rsl(b), COL_Z * LANE + DV_C * h:COL_Z * LANE + DV_C * (h + 1)]
        y_ref[rsl(b), Y_C + DV_C * h:Y_C + DV_C * (h + 1)] = _rms(o[i], gnorm_ref[...]) * _silu(zz)


def _sample_mixers(proj, row0, nrows, st, sp, mp, gp, layer, prev):
    bb = SAMPLE_BLOCK
    blk0 = row0 // bb
    depth = st["c"].shape[0]
    const2 = lambda i: (0, 0)
    const3 = lambda i: (0, 0, 0)
    row_spec = pl.BlockSpec((1, S5_LANES), const2)
    c_spec = _layered(layer, (bb, H_B, DQK_B, DV_B), lambda i: (i, 0, 0, 0))
    gs_spec = _layered(layer, (bb, H_C, DK_C, DV_C), lambda i: (i, 0, 0, 0))
    in_specs = [
        pl.BlockSpec((N_PROJ // PROJ_TILE, bb, PROJ_TILE), lambda i: (0, blk0 + i, 0)),
        pl.BlockSpec((bb, S5_LANES), lambda i: (i, 0)),
        pl.BlockSpec((bb, S5_LANES), lambda i: (i, 0)),
        c_spec,
        pl.BlockSpec((bb, H_B, DQK_B), lambda i: (i, 0, 0)),
        pl.BlockSpec((bb, H_B), lambda i: (i, 0)),
        gs_spec,
        pl.BlockSpec((GDN_CONV - 1, bb, 3 * D_C), lambda i: (0, i, 0)),
        row_spec, row_spec, row_spec,
        pl.BlockSpec((S5_NCHUNK, LANE, S5_CHUNK), const3),
        pl.BlockSpec((S5_NCHUNK, LANE, S5_CHUNK), const3),
        pl.BlockSpec((S5_NCHUNK, S5_CHUNK, LANE), const3),
        pl.BlockSpec((S5_NCHUNK, S5_CHUNK, LANE), const3),
        pl.BlockSpec((1, D_A), const2),
        pl.BlockSpec((D_A, D_A), const2),
        pl.BlockSpec((1, LANE), const2),
        pl.BlockSpec((H_B, DV_B), const2),
        pl.BlockSpec((1, LANE), const2),
        pl.BlockSpec((1, LANE), const2),
        pl.BlockSpec((GDN_CONV, 3 * D_C), const2),
        pl.BlockSpec((1, DV_C), const2),
    ]
    out_specs = [
        pl.BlockSpec((bb, D_MODEL), lambda i: (i, 0)),
        pl.BlockSpec((bb, S5_LANES), lambda i: (i, 0)),
        pl.BlockSpec((bb, S5_LANES), lambda i: (i, 0)),
        c_spec if prev is not None else pl.BlockSpec((depth, bb, H_B, DQK_B, DV_B), lambda i: (0, i, 0, 0, 0)),
        pl.BlockSpec((bb, H_B, DQK_B), lambda i: (i, 0, 0)),
        pl.BlockSpec((bb, H_B), lambda i: (i, 0)),
        gs_spec if prev is not None else pl.BlockSpec((depth, bb, H_C, DK_C, DV_C), lambda i: (0, i, 0, 0, 0)),
        pl.BlockSpec((GDN_CONV - 1, bb, 3 * D_C), lambda i: (0, i, 0)),
    ]
    out_shape = [
        jax.ShapeDtypeStruct((nrows, D_MODEL), F32),
        jax.ShapeDtypeStruct((nrows, S5_LANES), F32),
        jax.ShapeDtypeStruct((nrows, S5_LANES), F32),
        jax.ShapeDtypeStruct((depth, nrows, H_B, DQK_B, DV_B), F32),
        jax.ShapeDtypeStruct((nrows, H_B, DQK_B), F32),
        jax.ShapeDtypeStruct((nrows, H_B), F32),
        jax.ShapeDtypeStruct((depth, nrows, H_C, DK_C, DV_C), F32),
        jax.ShapeDtypeStruct((GDN_CONV - 1, nrows, 3 * D_C), F32),
    ]
    scratch = [pltpu.VMEM((bb, 3 * D_C), F32), pltpu.VMEM((bb, D_C), F32), pltpu.VMEM((bb, D_C), F32),
               pltpu.VMEM((bb, LANE), F32), pltpu.VMEM((bb, LANE), F32),
               pltpu.VMEM((bb, LANE), F32), pltpu.VMEM((bb, LANE), F32)]
    args = [proj, st["sre"], st["sim"], st["c"], st["n"], st["m"], st["gs"], st["gbuf"],
            sp["are"], sp["aim"], sp["ldt"], sp["bre"], sp["bim"], sp["cre"], sp["cim"], sp["d"], sp["wglu"],
            mp["bias"], mp["norm"], gp["bias"], gp["alog"], gp["conv_w"], gp["norm"]]
    assert len(args) == N_SAMPLE_INPUTS
    aliases = {}
    if prev is not None:
        in_specs += [pl.BlockSpec(memory_space=pl.ANY)] * 2
        args += list(prev)
        aliases = {N_SAMPLE_INPUTS: 3, N_SAMPLE_INPUTS + 1: 6}
    return pl.pallas_call(
        functools.partial(_sample_mixers_kernel, fill_layer=layer if prev is None else None),
        grid=(nrows // bb,),
        in_specs=in_specs, out_specs=out_specs, out_shape=out_shape, scratch_shapes=scratch,
        input_output_aliases=aliases,
        compiler_params=_cparams(("arbitrary",)),
        name="sample_mixers",
    )(*args)


def _merge_kernel(h_ref, ya_ref, yb_ref, yc_ref, wg0_ref, wg1_ref, wg2_ref, wa_ref, wb_ref, wc_ref, o_ref):
    h = h_ref[...]
    bf = lambda ref: ref[...].astype(BF16)
    acc = _sigmoid(_dot(h, bf(wg0_ref))) * _dot(bf(ya_ref), bf(wa_ref))
    acc = acc + _sigmoid(_dot(h, bf(wg1_ref))) * _dot(bf(yb_ref), bf(wb_ref))
    acc = acc + _sigmoid(_dot(h, bf(wg2_ref))) * _dot(bf(yc_ref), bf(wc_ref))
    o_ref[...] = acc.astype(BF16)


def _merge(h, ya, yb, yc, wg, wa, wb, wc, layer):
    m, d = h.shape
    tm = _pick(m, (1024, 512, 256, 128))
    tn = 256
    nb = d // tn
    lhs = lambda w, cb: pl.BlockSpec((tm, w), lambda i, j: (i, cb))
    gate = lambda g: _layered(layer, (d, tn), lambda i, j: (0, g * nb + j))
    rhs = lambda w: _layered(layer, (w, tn), lambda i, j: (0, j))
    return pl.pallas_call(
        _merge_kernel,
        grid=(m // tm, nb),
        in_specs=[lhs(d, 0), lhs(D_A, ya[1]), lhs(D_B, yb[1]), lhs(D_C, yc[1]), gate(0), gate(1), gate(2),
                  rhs(D_A), rhs(D_B), rhs(D_C)],
        out_specs=pl.BlockSpec((tm, tn), lambda i, j: (i, j)),
        out_shape=jax.ShapeDtypeStruct((m, d), BF16),
        compiler_params=_cparams(("arbitrary", "arbitrary")),
        name="merge",
    )(h, ya[0], yb[0], yc[0], wg, wg, wg, wa, wb, wc)


def _out_proj_kernel(a_ref, x_ref, w_ref, nw_ref, o_ref):
    o_ref[...] = x_ref[...] + _rms(_dot(a_ref[...], w_ref[...]), nw_ref[...])


def _out_proj(a, x, w, nw, layer):
    m, d = x.shape
    tm = _pick(m, (512, 256, 128))
    return pl.pallas_call(
        _out_proj_kernel,
        grid=(m // tm,),
        in_specs=[pl.BlockSpec((tm, d), lambda i: (i, 0)),
                  pl.BlockSpec((tm, d), lambda i: (i, 0)),
                  _layered(layer, (d, d), lambda i: (0, 0)),
                  _layered(layer, (1, d), lambda i: (0, 0))],
        out_specs=pl.BlockSpec((tm, d), lambda i: (i, 0)),
        out_shape=jax.ShapeDtypeStruct((m, d), F32),
        compiler_params=_cparams(("arbitrary",)),
        name="out_proj",
    )(a, x, w, nw)


def _ffn_body(step, *refs, sample, tiles_per_seq, ft):
    if sample:
        (x_ref, nw_ref, wg_ref, wu_ref, cw_ref, wd_ref, pnw_ref, b0_ref, b1_ref,
         o_ref, g_ref, wg_o, wu_o, wd_o, h2_s) = refs
    else:
        (x_ref, nw_ref, wg_ref, wu_ref, cw_ref, wd_ref, pnw_ref,
         o_ref, g_ref, h2_s, gb_s, carry_s) = refs
    i = step // ft
    j = step % ft
    tm = x_ref.shape[0]

    @pl.when(j == 0)
    def _():
        h2_s[...] = _rms(x_ref[...], nw_ref[...]).astype(BF16)
        o_ref[...] = jnp.zeros_like(o_ref)

    h2 = h2_s[...]
    wg, wu, wd = wg_ref[...], wu_ref[...], wd_ref[...]
    if sample:
        wg, wu, wd = wg.astype(BF16), wu.astype(BF16), wd.astype(BF16)
        wg_o[0] = wg
        wu_o[0] = wu
        wd_o[0] = wd
    g = _dot(h2, wg)
    up = _dot(h2, wu)
    if sample:
        a = cw_ref[0:1, :] * b0_ref[...] + cw_ref[1:2, :] * b1_ref[...] + cw_ref[2:3, :] * g
        g_ref[...] = g
    else:
        prev = jnp.where(i % tiles_per_seq == 0, 0.0, carry_s[j])
        gb_s[0:SUBLANE, :] = prev
        gb_s[SUBLANE:SUBLANE + tm, :] = g
        a = (cw_ref[0:1, :] * gb_s[SUBLANE - 2:SUBLANE - 2 + tm, :]
             + cw_ref[1:2, :] * gb_s[SUBLANE - 1:SUBLANE - 1 + tm, :]
             + cw_ref[2:3, :] * g)
        tail = g[tm - SUBLANE:tm, :]
        carry_s[j] = tail
        g_ref[0] = tail
    act = (_gelu(a) * up).astype(BF16)
    o_ref[...] += _dot(act, wd)

    @pl.when(j == ft - 1)
    def _():
        o_ref[...] = x_ref[...] + _rms(o_ref[...], pnw_ref[...])


def _ffn_part(x, row0, nrows, seq, nw, wg, wu, cw, wd, pnw, layer, bufs=None, steps=None):
    d = x.shape[1]
    f = wg.shape[-1]
    sample = bufs is not None
    tm = nrows if sample else _pick(seq, (512, 256, 128, 64, 32, 16, 8))
    tn = _pick(f, (512, 256, 128) if sample else (1024, 512, 256, 128))
    if sample and steps is not None and f % steps == 0 and (f // steps) % LANE == 0 and f // steps <= tn:
        tn = f // steps
    mt, ft = nrows // tm, f // tn
    blk0 = row0 // tm
    mi = lambda s: s // ft
    nj = lambda s: s % ft
    in_specs = [pl.BlockSpec((tm, d), lambda s: (blk0 + mi(s), 0)),
                _layered(layer, (1, d), lambda s: (0, 0)),
                _layered(layer, (d, tn), lambda s: (0, nj(s))),
                _layered(layer, (d, tn), lambda s: (0, nj(s))),
                _layered(layer, (FFN_CONV, tn), lambda s: (0, nj(s))),
                _layered(layer, (tn, d), lambda s: (nj(s), 0)),
                _layered(layer, (1, d), lambda s: (0, 0))]
    args = [x, nw, wg, wu, cw, wd, pnw]
    scratch = [pltpu.VMEM((tm, d), BF16)]
    if sample:
        in_specs += [pl.BlockSpec((tm, tn), lambda s: (mi(s), nj(s)))] * 2
        args += list(bufs)
        assert mt == 1, "the sample call must visit every weight tile exactly once"
        g_spec = pl.BlockSpec((tm, tn), lambda s: (mi(s), nj(s)))
        g_shape = jax.ShapeDtypeStruct((nrows, f), F32)
        extra_specs = [pl.BlockSpec((1, d, tn), lambda s: (0, 0, nj(s))),
                       pl.BlockSpec((1, d, tn), lambda s: (0, 0, nj(s))),
                       pl.BlockSpec((1, tn, d), lambda s: (0, nj(s), 0))]
        extra_shapes = [jax.ShapeDtypeStruct((1, d, f), BF16), jax.ShapeDtypeStruct((1, d, f), BF16),
                        jax.ShapeDtypeStruct((1, f, d), BF16)]
    else:
        extra_specs, extra_shapes = [], []
        scratch += [pltpu.VMEM((SUBLANE + tm, tn), F32), pltpu.VMEM((ft, SUBLANE, tn), F32)]
        g_spec = pl.BlockSpec((1, SUBLANE, tn), lambda s: (mi(s), 0, nj(s)))
        g_shape = jax.ShapeDtypeStruct((mt, SUBLANE, f), F32)
    return dict(
        n=mt * ft,
        body=functools.partial(_ffn_body, sample=sample, tiles_per_seq=max(seq // tm, 1), ft=ft),
        in_specs=in_specs, args=args,
        out_specs=[pl.BlockSpec((tm, d), lambda s: (mi(s), 0)), g_spec] + extra_specs,
        out_shape=[jax.ShapeDtypeStruct((nrows, d), F32), g_shape] + extra_shapes,
        scratch=scratch)


def _ffn(*args, **kwargs):
    part = _ffn_part(*args, **kwargs)
    return _run_parts("ffn_sample" if kwargs.get("bufs") is not None else "ffn_prompt", [part])[0]


def _ple_kernel(x_ref, p_ref, wg_ref, wp_ref, o_ref):
    x = x_ref[...]
    gate = _sigmoid(_dot(x.astype(BF16), wg_ref[...]))
    o_ref[...] = x + gate * _dot(p_ref[...].astype(BF16), wp_ref[...])


def _ple(x, p, wg, wp, layer):
    m, d = x.shape
    pd = p.shape[-1]
    tm = _pick(m, (512, 256, 128))
    return pl.pallas_call(
        _ple_kernel,
        grid=(m // tm,),
        in_specs=[pl.BlockSpec((tm, d), lambda i: (i, 0)),
                  _layered(layer, (tm, pd), lambda i: (i, 0)),
                  _layered(layer, (d, d), lambda i: (0, 0)),
                  _layered(layer, (pd, d), lambda i: (0, 0))],
        out_specs=pl.BlockSpec((tm, d), lambda i: (i, 0)),
        out_shape=jax.ShapeDtypeStruct((m, d), F32),
        compiler_params=_cparams(("arbitrary",)),
        name="ple",
    )(x, p, wg, wp)


def _split_w_in(w):
    n_if = 2 * H_B
    depth, d = w.shape[0], w.shape[1]
    tiled = lambda a: jnp.swapaxes(a.astype(BF16).reshape(depth, d, -1, PROJ_TILE), 1, 2)
    w_head = w[..., 0:N_HEAD]
    w_mid = w[..., N_HEAD + n_if:N_HEAD + n_if + N_MID]
    gates = jnp.concatenate([w[..., N_HEAD:N_HEAD + n_if], w[..., N_HEAD + n_if + N_MID:]], axis=-1)
    pad = jnp.zeros(w.shape[:-1] + (LANE - gates.shape[-1],), w.dtype)
    return tiled(w_head), tiled(w_mid), jnp.concatenate([gates, pad], axis=-1).astype(BF16)


def _small_row(entries):
    row = jnp.zeros((LANE,), F32)
    for off, val in entries:
        row = row.at[off:off + val.shape[0]].set(val.astype(F32))
    return row.reshape(1, LANE)


def _s5_params(a_re, a_im, log_dt, b_re, b_im, c_re, c_im, d, w_glu):
    gpc = S5_CHUNK // S5_STATE
    eye = jnp.eye(gpc, dtype=F32)

    def bmat(b):
        b4 = b.reshape(S5_NCHUNK, gpc, S5_GROUP, S5_STATE)
        return jnp.einsum('jgcp,gh->jgchp', b4, eye).reshape(S5_NCHUNK, gpc * S5_GROUP, S5_CHUNK).astype(BF16)

    def cmat(c):
        c4 = c.reshape(S5_NCHUNK, gpc, S5_STATE, S5_GROUP)
        return jnp.einsum('jgpc,gh->jgphc', c4, eye).reshape(S5_NCHUNK, S5_CHUNK, gpc * S5_GROUP).astype(BF16)

    return dict(are=a_re.reshape(1, S5_LANES), aim=a_im.reshape(1, S5_LANES),
                ldt=jnp.broadcast_to(log_dt[:, None], (S5_GROUPS, S5_STATE)).reshape(1, S5_LANES),
                bre=bmat(b_re), bim=bmat(b_im), cre=cmat(c_re), cim=cmat(c_im),
                d=d.reshape(1, D_A), wglu=w_glu.astype(BF16))


def _layer(layer, xp, xs, bsz, seq, dense, lw, state, prev):
    np_rows = bsz * seq
    nsamp = xs.shape[0]
    in_w = (dense['norm_mix_pre'], dense['w_in'], dense['w_in_mid'], dense['w_in_small'], layer)
    proj, h_p = _norm_proj(xp, *in_w)
    proj_s, h_s = _norm_proj(xs, *in_w)

    sp = _s5_params(lw['s5_a_re'], lw['s5_a_im'], lw['s5_log_dt'], lw['s5_b_re'], lw['s5_b_im'],
                    lw['s5_c_re'], lw['s5_c_im'], lw['s5_d'], lw['s5_w_glu'])
    mp = dict(bias=_small_row([(SM_I, lw['mlstm_b_i']), (SM_F, lw['mlstm_b_f'])]), norm=lw['mlstm_norm'])
    gp = dict(bias=_small_row([(SM_A, lw['gdn_dt_bias'])]), alog=_small_row([(SM_A, lw['gdn_a_log'])]),
              conv_w=lw['gdn_conv_w'], norm=lw['gdn_norm'].reshape(1, DV_C))

    ssm_re, ssm_im, m_c, m_n, m_m, g_s, g_conv, f_conv = state
    st = dict(sre=ssm_re.reshape(nsamp, S5_LANES), sim=ssm_im.reshape(nsamp, S5_LANES),
              c=m_c, n=m_n, m=m_m, gs=g_s, gbuf=jnp.swapaxes(g_conv, 0, 1))
    y_s, sre_s, sim_s, c_s, n_s, m_s, gs_s, gbuf_s = _sample_mixers(proj_s, 0, nsamp, st, sp, mp, gp, layer, prev)
    merge_w = (dense['w_gate'], dense['w_branch_a'], dense['w_branch_b'], dense['w_branch_c'], layer)
    merged_s = _merge(h_s, (y_s, Y_A // D_A), (y_s, Y_B // D_B), (y_s, Y_C // D_C), *merge_w)
    x1_s = _out_proj(merged_s, xs, dense['w_out'], dense['norm_mix_post'], layer)

    s5_part = _s5_prompt_part(proj, bsz, seq, sp)
    ffn_s_part = _ffn_part(
        x1_s, 0, nsamp, 1, dense['norm_ffn_pre'], dense['ffn_w_gate'], dense['ffn_w_up'], dense['ffn_conv_w'],
        dense['ffn_w_down'], dense['norm_ffn_post'], layer, bufs=(f_conv[:, 0], f_conv[:, 1]), steps=s5_part["n"])
    if s5_part["n"] == ffn_s_part["n"]:
        (ya_p, sre_p, sim_p), (x2_s, g_new, wg_b, wu_b, wd_b) = _run_parts("s5_prompt_ffn_sample",
                                                                             [s5_part, ffn_s_part])
    else:
        ya_p, sre_p, sim_p = _run_parts("s5_prompt", [s5_part])[0]
        x2_s, g_new, wg_b, wu_b, wd_b = _run_parts("ffn_sample", [ffn_s_part])[0]
    yb_p, c_p, n_p, m_p = _mlstm_prompt(proj, bsz, seq, mp)
    yc_p, gs_p = _gdn_prompt(proj, bsz, seq, gp)

    merged_p = _merge(h_p, (ya_p, 0), (yb_p, 0), (yc_p, 0), *merge_w)
    x1_p = _out_proj(merged_p, xp, dense['w_out'], dense['norm_mix_post'], layer)
    cw_l, nw_pre_l, nw_post_l = (dense[k][layer:layer + 1] for k in ('ffn_conv_w', 'norm_ffn_pre', 'norm_ffn_post'))
    x2_p, gtail = _ffn(x1_p, 0, np_rows, seq, nw_pre_l, wg_b, wu_b, cw_l, wd_b, nw_post_l, 0)
    x3_p = _ple(x2_p, dense['p_prompt'], dense['ple_w_gate'], dense['ple_w_proj'], layer)
    x3_s = _ple(x2_s, dense['p_sample'], dense['ple_w_gate'], dense['ple_w_proj'], layer)

    tiles_per_seq = gtail.shape[0] // bsz
    t0, nq = COL_QKV * LANE // PROJ_TILE, 3 * D_C // PROJ_TILE
    qkv_tail = proj[t0:t0 + nq].reshape(nq, bsz, seq, PROJ_TILE)[:, :, seq - (GDN_CONV - 1):, :]
    qkv_tail = jnp.transpose(qkv_tail, (1, 2, 0, 3)).reshape(bsz, GDN_CONV - 1, 3 * D_C)
    st_p = (sre_p.reshape(bsz, S5_GROUPS, S5_STATE), sim_p.reshape(bsz, S5_GROUPS, S5_STATE),
            c_p, n_p, m_p[:, 0, :H_B], gs_p, qkv_tail,
            gtail.reshape(bsz, tiles_per_seq, SUBLANE, D_FF)[:, -1, SUBLANE - (FFN_CONV - 1):, :])
    st_s = (sre_s.reshape(nsamp, S5_GROUPS, S5_STATE), sim_s.reshape(nsamp, S5_GROUPS, S5_STATE),
            None, n_s, m_s, None, jnp.swapaxes(gbuf_s, 0, 1),
            jnp.stack([f_conv[:, 1], g_new], axis=1))
    return x3_p, x3_s, st_p, st_s, (c_s, gs_s)


def kernel(x_prompt, x_sample, p_prompt, p_sample, state_ssm_re, state_ssm_im, state_mlstm_c, state_mlstm_n, state_mlstm_m, state_gdn_s, state_gdn_conv, state_ffn_conv, norm_mix_pre, norm_mix_post, norm_ffn_pre, norm_ffn_post, w_in, s5_a_re, s5_a_im, s5_log_dt, s5_b_re, s5_b_im, s5_c_re, s5_c_im, s5_d, s5_w_glu, mlstm_b_i, mlstm_b_f, mlstm_norm, gdn_conv_w, gdn_a_log, gdn_dt_bias, gdn_norm, w_branch_a, w_branch_b, w_branch_c, w_gate, w_out, ffn_w_gate, ffn_w_up, ffn_conv_w, ffn_w_down, ple_w_proj, ple_w_gate):
    bsz, seq, d = x_prompt.shape
    nsamp = x_sample.shape[0]
    depth = w_in.shape[0]
    small = dict(
        s5_a_re=s5_a_re, s5_a_im=s5_a_im, s5_log_dt=s5_log_dt, s5_b_re=s5_b_re, s5_b_im=s5_b_im,
        s5_c_re=s5_c_re, s5_c_im=s5_c_im, s5_d=s5_d, s5_w_glu=s5_w_glu, mlstm_b_i=mlstm_b_i,
        mlstm_b_f=mlstm_b_f, mlstm_norm=mlstm_norm, gdn_conv_w=gdn_conv_w, gdn_a_log=gdn_a_log,
        gdn_dt_bias=gdn_dt_bias, gdn_norm=gdn_norm)
    row = lambda w: w.reshape(depth, 1, -1)
    w_in_head, w_in_mid, w_in_small = _split_w_in(w_in)
    dense = dict(
        norm_mix_pre=row(norm_mix_pre), norm_mix_post=row(norm_mix_post), norm_ffn_pre=row(norm_ffn_pre),
        norm_ffn_post=row(norm_ffn_post), w_in=w_in_head, w_in_mid=w_in_mid, w_in_small=w_in_small,
        w_gate=w_gate, w_branch_a=w_branch_a, w_branch_b=w_branch_b, w_branch_c=w_branch_c,
        w_out=w_out.astype(BF16), ffn_w_gate=ffn_w_gate, ffn_w_up=ffn_w_up, ffn_conv_w=ffn_conv_w,
        ffn_w_down=ffn_w_down,
        ple_w_gate=ple_w_gate.astype(BF16), ple_w_proj=ple_w_proj.astype(BF16),
        p_prompt=p_prompt.reshape(depth, bsz * seq, -1), p_sample=p_sample.reshape(depth, nsamp, -1))
    xp = x_prompt.reshape(bsz * seq, d)
    xs = x_sample.reshape(nsamp, d)
    sp_all, ss_all = [], []
    big = None
    for i in range(depth):
        lw = {k: v[i] for k, v in small.items()}
        state = (state_ssm_re[i], state_ssm_im[i], state_mlstm_c, state_mlstm_n[i], state_mlstm_m[i],
                 state_gdn_s, state_gdn_conv[i], state_ffn_conv[i])
        xp, xs, st_p, st_s, big = _layer(i, xp, xs, bsz, seq, dense, lw, state, big)
        sp_all.append(st_p)
        ss_all.append(st_s)
    stack = lambda sts, j: jnp.stack([s[j] for s in sts], axis=0)
    sample_states = [big[0] if j == 2 else big[1] if j == 5 else stack(ss_all, j) for j in range(8)]
    return ((xp.reshape(bsz, seq, d), xs.reshape(nsamp, 1, d))
            + tuple(stack(sp_all, j) for j in range(8))
            + tuple(sample_states))
```

```python
import functools

import jax
import jax.numpy as jnp
from jax import lax
from jax.experimental import pallas as pl
from jax.experimental.pallas import tpu as pltpu

F32 = jnp.float32
BF16 = jnp.bfloat16

D_MODEL = 2048
DEPTH = 2
D_A = 512
S5_GROUP = 16
S5_GROUPS = 32
S5_STATE = 64
S5_LANES = S5_GROUPS * S5_STATE
S5_CHUNK = 512
S5_NCHUNK = S5_LANES // S5_CHUNK
D_B = 1024
H_B = 4
DV_B = 256
DQK_B = 128
D_C = 512
H_C = 4
DK_C = 128
DV_C = 128
GDN_CONV = 4
D_FF = 8192
FFN_CONV = 3
PLE_DIM = 256
EPS = 1e-6

LANE = 128
SUBLANE = 8
VMEM_LIMIT = 56 * 1024 * 1024

COL_U, COL_QB, COL_KB, COL_VB, COL_OB, COL_QKV, COL_Z, COL_SMALL = 0, 4, 8, 12, 20, 28, 40, 44
N_HEAD = COL_OB * LANE
N_MID = (COL_SMALL - COL_OB) * LANE
PROJ_TILE = 512
N_PROJ = N_HEAD + N_MID + PROJ_TILE
SM_I, SM_F, SM_BETA, SM_A = 0, 4, 8, 12

MLSTM_CHUNK = 128
MLSTM_TILE = 512
GDN_CHUNK = 64
GDN_TILE = 256
S5_TILE = 512
SAMPLE_BLOCK = 8
SAMPLE_GROUP = SAMPLE_BLOCK
Y_B, Y_A, Y_C = 0, D_B, D_B + D_A

NT = (((1,), (1,)), ((), ()))
TN = (((0,), (0,)), ((), ()))


def _cparams(sem):
    return pltpu.CompilerParams(dimension_semantics=sem, vmem_limit_bytes=VMEM_LIMIT)


def _dot(a, b):
    return jnp.dot(a, b, preferred_element_type=F32)


def _dot_hi(a, b):
    return jnp.dot(a, b, preferred_element_type=F32, precision=lax.Precision.HIGHEST)


def _gelu(x):
    return 0.5 * x * (1.0 + jnp.tanh(0.7978845608028654 * (x + 0.044715 * (x * x * x))))


def _sigmoid(x):
    return 1.0 / (1.0 + jnp.exp(-x))


def _silu(x):
    return x * _sigmoid(x)


def _softplus(x):
    return jnp.maximum(x, 0.0) + jnp.log1p(jnp.exp(-jnp.abs(x)))


def _log_sigmoid(x):
    return -_softplus(-x)


def _rms(x, w):
    return x * lax.rsqrt(jnp.mean(x * x, axis=-1, keepdims=True) + EPS) * w


def _layered(layer, shape, imap):
    return pl.BlockSpec((None,) + shape, lambda *g: (layer,) + imap(*g))


def _pick(n, cands):
    for c in cands:
        if n % c == 0:
            return c
    return n


def _norm_proj_kernel(x_ref, nw_ref, wa_ref, wb_ref, ws_ref, proj_ref, h_ref, *, na, nb):
    j = pl.program_id(1)

    @pl.when(j == 0)
    def _():
        h_ref[...] = _rms(x_ref[...], nw_ref[...]).astype(BF16)

    @pl.when(j < na)
    def _():
        proj_ref[...] = _dot(h_ref[...], wa_ref[...].astype(BF16))

    @pl.when(jnp.logical_and(j >= na, j < na + nb))
    def _():
        proj_ref[...] = _dot(h_ref[...], wb_ref[...].astype(BF16))

    @pl.when(j >= na + nb)
    def _():
        proj_ref[...] = jnp.zeros_like(proj_ref)
        proj_ref[:, 0:LANE] = _dot(h_ref[...], ws_ref[...].astype(BF16))


def _norm_proj(x, nw, w_head, w_mid, w_small, layer):
    m, d = x.shape
    tm = _pick(m, (1024, 512, 256, 128))
    tn = PROJ_TILE
    na, nb = N_HEAD // tn, N_MID // tn
    return pl.pallas_call(
        functools.partial(_norm_proj_kernel, na=na, nb=nb),
        grid=(m // tm, na + nb + 1),
        in_specs=[pl.BlockSpec((tm, d), lambda i, j: (i, 0)),
                  _layered(layer, (1, d), lambda i, j: (0, 0)),
                  _layered(layer, (d, tn), lambda i, j: (0, jnp.minimum(j, na - 1))),
                  _layered(layer, (d, tn), lambda i, j: (0, jnp.clip(j - na, 0, nb - 1))),
                  _layered(layer, (d, LANE), lambda i, j: (0, 0))],
        out_specs=[pl.BlockSpec((tm, tn), lambda i, j: (i, j)),
                   pl.BlockSpec((tm, d), lambda i, j: (i, 0))],
        out_shape=[jax.ShapeDtypeStruct((m, N_PROJ), F32), jax.ShapeDtypeStruct((m, d), BF16)],
        compiler_params=_cparams(("arbitrary", "arbitrary")),
        name="norm_proj",
    )(x, nw, w_head, w_mid, w_small)


def _s5_disc(are, aim, ldt):
    dt = jnp.exp(ldt)
    mag = jnp.exp(dt * are)
    abr = mag * jnp.cos(dt * aim)
    abi = mag * jnp.sin(dt * aim)
    den = are * are + aim * aim
    zr = abr - 1.0
    fre = (zr * are + abi * aim) / den
    fim = (abi * are - zr * aim) / den
    return abr, abi, fre, fim


def _s5_glu(y, u, d_ref, wglu_ref):
    z = _gelu(y + d_ref[...] * u)
    return z * _sigmoid(_dot(z.astype(BF16), wglu_ref[...]))


def _s5_prompt_body(step, u_ref, are_ref, aim_ref, ldt_ref, bre_ref, bim_ref, cre_ref, cim_ref,
                    d_ref, wglu_ref, y_ref, sre_ref, sim_ref, xr_s, xi_s, y_s, car_re, car_im,
                    f_s, tab_s, *, nt):
    @pl.when(step % nt == 0)
    def _():
        car_re[...] = jnp.zeros_like(car_re)
        car_im[...] = jnp.zeros_like(car_im)
        row = lax.broadcasted_iota(jnp.int32, (SUBLANE, S5_LANES), 0)
        abr, abi, fre, fim = _s5_disc(are_ref[...], aim_ref[...], ldt_ref[...])
        f_s[0:1, :] = fre
        f_s[1:2, :] = fim
        pr, pi = abr, abi
        for lvl, s in enumerate((1, 2, 4)):
            tab_s[2 * lvl] = jnp.where(row >= s, pr, 0.0)
            tab_s[2 * lvl + 1] = jnp.where(row >= s, pi, 0.0)
            pr, pi = pr * pr - pi * pi, 2.0 * pr * pi
        cwr = jnp.zeros((SUBLANE, S5_LANES), F32)
        cwi = jnp.zeros((SUBLANE, S5_LANES), F32)
        pr, pi = abr, abi
        for r in range(SUBLANE):
            cwr = jnp.where(row == r, pr, cwr)
            cwi = jnp.where(row == r, pi, cwi)
            pr, pi = pr * abr - pi * abi, pr * abi + pi * abr
        tab_s[6] = cwr
        tab_s[7] = cwi

    tt = u_ref.shape[0]
    u = u_ref[...]
    ub = u.astype(BF16)
    for j in range(S5_NCHUNK):
        sl = slice(S5_CHUNK * j, S5_CHUNK * (j + 1))
        fre, fim = f_s[0:1, sl], f_s[1:2, sl]
        uj = ub[:, LANE * j:LANE * (j + 1)]
        bur = _dot(uj, bre_ref[j])
        bui = _dot(uj, bim_ref[j])
        xr = (fre * bur - fim * bui).reshape(tt // SUBLANE, SUBLANE, S5_CHUNK)
        xi = (fre * bui + fim * bur).reshape(tt // SUBLANE, SUBLANE, S5_CHUNK)
        for lvl, s in enumerate((1, 2, 4)):
            mr = tab_s[2 * lvl, :, sl]
            mi = tab_s[2 * lvl + 1, :, sl]
            sr = pltpu.roll(xr, s, axis=1)
            si = pltpu.roll(xi, s, axis=1)
            xr, xi = xr + mr * sr - mi * si, xi + mr * si + mi * sr
        xr_s[...] = xr.reshape(tt, S5_CHUNK)
        xi_s[...] = xi.reshape(tt, S5_CHUNK)
        cwr = tab_s[6, :, sl]
        cwi = tab_s[7, :, sl]

        def body(g, carry, cwr=cwr, cwi=cwi):
            cr, ci = carry
            r0 = pl.multiple_of(g * SUBLANE, SUBLANE)
            gr = xr_s[pl.ds(r0, SUBLANE), :] + cwr * cr - cwi * ci
            gi = xi_s[pl.ds(r0, SUBLANE), :] + cwr * ci + cwi * cr
            xr_s[pl.ds(r0, SUBLANE), :] = gr
            xi_s[pl.ds(r0, SUBLANE), :] = gi
            return gr[SUBLANE - 1:SUBLANE, :], gi[SUBLANE - 1:SUBLANE, :]

        cr, ci = lax.fori_loop(0, tt // SUBLANE, body, (car_re[:, sl], car_im[:, sl]), unroll=4)
        car_re[:, sl] = cr
        car_im[:, sl] = ci
        y_s[:, LANE * j:LANE * (j + 1)] = (_dot(xr_s[...].astype(BF16), cre_ref[j])
                                           - _dot(xi_s[...].astype(BF16), cim_ref[j]))
    y_ref[...] = _s5_glu(y_s[...], u, d_ref, wglu_ref).astype(BF16)
    sre_ref[0] = car_re[...]
    sim_ref[0] = car_im[...]


def _run_parts(name, parts):
    n = parts[0]["n"]
    assert all(p["n"] == n for p in parts)
    n_in = [len(p["args"]) for p in parts]
    n_out = [len(p["out_shape"]) for p in parts]
    n_scr = [len(p["scratch"]) for p in parts]

    def kern(*refs):
        step = pl.program_id(0)
        ins, outs, scr = refs[:sum(n_in)], refs[sum(n_in):sum(n_in) + sum(n_out)], refs[sum(n_in) + sum(n_out):]
        a = b = c = 0
        for p, na, nb, nc in zip(parts, n_in, n_out, n_scr):
            p["body"](step, *ins[a:a + na], *outs[b:b + nb], *scr[c:c + nc])
            a, b, c = a + na, b + nb, c + nc

    flat = lambda key: [x for p in parts for x in p[key]]
    outs = pl.pallas_call(
        kern, grid=(n,), in_specs=flat("in_specs"), out_specs=flat("out_specs"), out_shape=flat("out_shape"),
        scratch_shapes=flat("scratch"), compiler_params=_cparams(("arbitrary",)), name=name,
    )(*flat("args"))
    res, b = [], 0
    for nb in n_out:
        res.append(list(outs[b:b + nb]))
        b += nb
    return res


def _s5_prompt_part(proj, bsz, seq, sp):
    tt = _pick(seq, (S5_TILE, 128, 64, 32, 16, 8))
    nt = seq // tt
    const2 = lambda i: (0, 0)
    const3 = lambda i: (0, 0, 0)
    row_spec = pl.BlockSpec((1, S5_LANES), const2)
    return dict(
        n=bsz * nt,
        body=functools.partial(_s5_prompt_body, nt=nt),
        in_specs=[pl.BlockSpec((tt, D_A), lambda i: (i, COL_U)),
                  row_spec, row_spec, row_spec,
                  pl.BlockSpec((S5_NCHUNK, LANE, S5_CHUNK), const3),
                  pl.BlockSpec((S5_NCHUNK, LANE, S5_CHUNK), const3),
                  pl.BlockSpec((S5_NCHUNK, S5_CHUNK, LANE), const3),
                  pl.BlockSpec((S5_NCHUNK, S5_CHUNK, LANE), const3),
                  pl.BlockSpec((1, D_A), const2),
                  pl.BlockSpec((D_A, D_A), const2)],
        args=[proj, sp["are"], sp["aim"], sp["ldt"], sp["bre"], sp["bim"], sp["cre"], sp["cim"],
              sp["d"], sp["wglu"]],
        out_specs=[pl.BlockSpec((tt, D_A), lambda i: (i, 0)),
                   pl.BlockSpec((1, 1, S5_LANES), lambda i: (i // nt, 0, 0)),
                   pl.BlockSpec((1, 1, S5_LANES), lambda i: (i // nt, 0, 0))],
        out_shape=[jax.ShapeDtypeStruct((bsz * seq, D_A), BF16),
                   jax.ShapeDtypeStruct((bsz, 1, S5_LANES), F32),
                   jax.ShapeDtypeStruct((bsz, 1, S5_LANES), F32)],
        scratch=[pltpu.VMEM((tt, S5_CHUNK), F32), pltpu.VMEM((tt, S5_CHUNK), F32),
                 pltpu.VMEM((tt, D_A), F32),
                 pltpu.VMEM((1, S5_LANES), F32), pltpu.VMEM((1, S5_LANES), F32),
                 pltpu.VMEM((2, S5_LANES), F32), pltpu.VMEM((8, SUBLANE, S5_LANES), F32)])


def _s5_prompt(proj, bsz, seq, sp):
    return _run_parts("s5_prompt", [_s5_prompt_part(proj, bsz, seq, sp)])[0]


def _mlstm_prompt_kernel(q_ref, k_ref, v0_ref, v1_ref, o0_ref, o1_ref, sm_ref, bias_ref, nw_ref,
                         y_ref, c_ref, n_ref, m_ref, *, chunk):
    @pl.when(pl.program_id(1) == 0)
    def _():
        c_ref[...] = jnp.zeros_like(c_ref)
        n_ref[...] = jnp.zeros_like(n_ref)
        m_ref[...] = jnp.zeros_like(m_ref)

    tt = q_ref.shape[0]
    c = chunk
    nchunk = tt // c
    smb = sm_ref[...] + bias_ref[...]
    lf_all = _log_sigmoid(smb)
    li_t = smb.T
    lf_t = lf_all.T
    rowi = lax.broadcasted_iota(jnp.int32, (c, c), 0)
    coli = lax.broadcasted_iota(jnp.int32, (c, c), 1)
    causal = rowi >= coli
    lane = lax.broadcasted_iota(jnp.int32, (1, LANE), 1)
    m_row = m_ref[0]
    probs = [(ci, h) for ci in range(nchunk) for h in range(H_B)]
    idx = {p: i for i, p in enumerate(probs)}
    rows = lambda ci: slice(c * ci, c * (ci + 1))
    vo_refs = [(v0_ref, o0_ref) if h < 2 else (v1_ref, o1_ref) for h in range(H_B)]
    q = [q_ref[rows(ci), DQK_B * h:DQK_B * (h + 1)] * (DQK_B ** -0.5) for ci, h in probs]
    k = [k_ref[rows(ci), DQK_B * h:DQK_B * (h + 1)] for ci, h in probs]
    qb = [x.astype(BF16) for x in q]
    kb = [x.astype(BF16) for x in k]
    vb = [vo_refs[h][0][rows(ci), DV_B * (h % 2):DV_B * (h % 2 + 1)].astype(BF16) for ci, h in probs]
    qkt = [lax.dot_general(a, b, NT, preferred_element_type=F32) for a, b in zip(qb, kb)]
    li_c = [smb[rows(ci), SM_I + h:SM_I + h + 1] for ci, h in probs]
    li_r = [li_t[SM_I + h:SM_I + h + 1, rows(ci)] for ci, h in probs]
    bc_c = [jnp.sum(jnp.where(causal, lf_t[SM_F + h:SM_F + h + 1, rows(ci)], 0.0), axis=1, keepdims=True)
            for ci, h in probs]
    bc_r = [jnp.sum(jnp.where(rowi <= coli, lf_all[rows(ci), SM_F + h:SM_F + h + 1], 0.0), axis=0, keepdims=True)
            for ci, h in probs]
    dmat = [jnp.where(causal, a - b + r, -jnp.inf) for a, b, r in zip(bc_c, bc_r, li_r)]
    dmax = [jnp.max(x, axis=1, keepdims=True) for x in dmat]
    b_last = [x[c - 1:c, :] for x in bc_c]
    expo = [bl - a + l for bl, a, l in zip(b_last, bc_c, li_c)]
    emax = [jnp.max(x, axis=0, keepdims=True) for x in expo]
    m_prev, m_new = [None] * len(probs), [None] * len(probs)
    for h in range(H_B):
        m = m_row[:, h:h + 1]
        for ci in range(nchunk):
            i = idx[ci, h]
            m_prev[i] = m
            m = jnp.maximum(b_last[i] + m, emax[i])
            m_new[i] = m
    inter = [a + m for a, m in zip(bc_c, m_prev)]
    m_t = [jnp.maximum(a, b) for a, b in zip(inter, dmax)]
    w_inter = [jnp.exp(a - b) for a, b in zip(inter, m_t)]
    s = [x * jnp.exp(d - m) for x, d, m in zip(qkt, dmat, m_t)]
    sv = [_dot(x.astype(BF16), v) for x, v in zip(s, vb)]
    ssum = [jnp.sum(x, axis=1, keepdims=True) for x in s]
    emt = [jnp.exp(-x) for x in m_t]
    decay = [jnp.exp(bl + mp - mn) for bl, mp, mn in zip(b_last, m_prev, m_new)]
    kw = [jnp.exp(e - mn) * kx for e, mn, kx in zip(expo, m_new, k)]
    kv = [lax.dot_general(x.astype(BF16), v, TN, preferred_element_type=F32) for x, v in zip(kw, vb)]
    ksum = [jnp.sum(x, axis=0, keepdims=True) for x in kw]

    cst = [c_ref[0, h] for h in range(H_B)]
    nrow = [n_ref[0, h:h + 1, :] for h in range(H_B)]
    for ci in range(nchunk):
        ids = [idx[ci, h] for h in range(H_B)]
        qc = [_dot(qb[i], cst[h].astype(BF16)) for h, i in enumerate(ids)]
        num = [sv[i] + w_inter[i] * qc[h] for h, i in enumerate(ids)]
        nq = [ssum[i] + w_inter[i] * jnp.sum(q[i] * nrow[h], axis=1, keepdims=True) for h, i in enumerate(ids)]
        hh = [num[h] / jnp.maximum(jnp.abs(nq[h]), emt[i]) for h, i in enumerate(ids)]
        cst = [decay[i] * cst[h] + kv[i] for h, i in enumerate(ids)]
        nrow = [decay[i] * nrow[h] + ksum[i] for h, i in enumerate(ids)]
        for h in range(H_B):
            og = vo_refs[h][1][rows(ci), DV_B * (h % 2):DV_B * (h % 2 + 1)]
            y_ref[rows(ci), DV_B * h:DV_B * (h + 1)] = (_rms(hh[h], nw_ref[h:h + 1, :]) * _sigmoid(og)).astype(BF16)
    m_out = m_row
    for h in range(H_B):
        c_ref[0, h] = cst[h]
        n_ref[0, h:h + 1, :] = nrow[h]
        m_out = jnp.where(lane == h, m_new[idx[nchunk - 1, h]], m_out)
    m_ref[0] = m_out


def _mlstm_prompt(proj, bsz, seq, mp):
    chunk = _pick(seq, (MLSTM_CHUNK,))
    c = _pick(seq, (MLSTM_TILE, MLSTM_CHUNK))
    nt = seq // c
    rows = lambda col: (lambda b, t: (b * nt + t, col))
    const2 = lambda b, t: (0, 0)
    return pl.pallas_call(
        functools.partial(_mlstm_prompt_kernel, chunk=chunk),
        grid=(bsz, nt),
        in_specs=[pl.BlockSpec((c, 512), rows(COL_QB // 4)),
                  pl.BlockSpec((c, 512), rows(COL_KB // 4)),
                  pl.BlockSpec((c, 512), rows(COL_VB // 4)),
                  pl.BlockSpec((c, 512), rows(COL_VB // 4 + 1)),
                  pl.BlockSpec((c, 512), rows(COL_OB // 4)),
                  pl.BlockSpec((c, 512), rows(COL_OB // 4 + 1)),
                  pl.BlockSpec((c, LANE), rows(COL_SMALL)),
                  pl.BlockSpec((1, LANE), const2),
                  pl.BlockSpec((H_B, DV_B), const2)],
        out_specs=[pl.BlockSpec((c, D_B), lambda b, t: (b * nt + t, 0)),
                   pl.BlockSpec((1, H_B, DQK_B, DV_B), lambda b, t: (b, 0, 0, 0)),
                   pl.BlockSpec((1, H_B, DQK_B), lambda b, t: (b, 0, 0)),
                   pl.BlockSpec((1, 1, LANE), lambda b, t: (b, 0, 0))],
        out_shape=[jax.ShapeDtypeStruct((bsz * seq, D_B), BF16),
                   jax.ShapeDtypeStruct((bsz, H_B, DQK_B, DV_B), F32),
                   jax.ShapeDtypeStruct((bsz, H_B, DQK_B), F32),
                   jax.ShapeDtypeStruct((bsz, 1, LANE), F32)],
        compiler_params=_cparams(("arbitrary", "arbitrary")),
        name="mlstm_prompt",
    )(proj, proj, proj, proj, proj, proj, proj, mp["bias"], mp["norm"])


def _split_bf16(a):
    hi = a.astype(BF16)
    return hi, (a - hi.astype(F32)).astype(BF16)


def _dot_x3(a, b):
    ah, al = _split_bf16(a)
    bh, bl = _split_bf16(b)
    return _dot(ah, bh) + _dot(ah, bl) + _dot(al, bh)


def _unit_lower_inverses(lmats):
    c = lmats[0].shape[0]
    eye = (lax.broadcasted_iota(jnp.int32, (c, c), 0) == lax.broadcasted_iota(jnp.int32, (c, c), 1)).astype(F32)
    hi_half = lax.broadcasted_iota(jnp.int32, (c, 2 * c), 1) >= c
    ms = [jnp.concatenate([-l, eye], axis=1) for l in lmats]
    span = 1
    while span < c:
        split = [_split_bf16(m) for m in ms]
        ms = [_dot(mh[:, 0:c], mh) + _dot(mh[:, 0:c], ml) + _dot(ml[:, 0:c], mh) + jnp.where(hi_half, m, 0.0)
              for m, (mh, ml) in zip(ms, split)]
        span *= 2
    return [m[:, c:2 * c] for m in ms]


def _gdn_prompt_kernel(q_ref, k_ref, v_ref, z_ref, sm_ref, bias_ref, alog_ref, cw_ref, nw_ref,
                       y_ref, s_ref, xb_s, *, chunk):
    tt = q_ref.shape[0]
    c = chunk

    @pl.when(pl.program_id(1) == 0)
    def _():
        s_ref[...] = jnp.zeros_like(s_ref)
        xb_s[0:SUBLANE, :] = jnp.zeros((SUBLANE, 3 * D_C), F32)

    xb_s[SUBLANE:SUBLANE + tt, 0:D_C] = q_ref[...]
    xb_s[SUBLANE:SUBLANE + tt, D_C:2 * D_C] = k_ref[...]
    xb_s[SUBLANE:SUBLANE + tt, 2 * D_C:3 * D_C] = v_ref[...]
    conv = cw_ref[GDN_CONV - 1:GDN_CONV, :] * xb_s[SUBLANE:SUBLANE + tt, :]
    for j in range(GDN_CONV - 1):
        off = SUBLANE - (GDN_CONV - 1) + j
        conv = conv + cw_ref[j:j + 1, :] * xb_s[off:off + tt, :]
    xb_s[0:SUBLANE, :] = xb_s[tt:tt + SUBLANE, :]
    qkv = _silu(conv)

    sm = sm_ref[...]
    beta_all = _sigmoid(sm)
    g_all = -jnp.exp(alog_ref[...]) * _softplus(sm + bias_ref[...])
    g_t = g_all.T
    rowi = lax.broadcasted_iota(jnp.int32, (c, c), 0)
    coli = lax.broadcasted_iota(jnp.int32, (c, c), 1)
    incl = rowi >= coli
    strict = rowi > coli

    nchunk = tt // c
    probs = [(ci, h) for ci in range(nchunk) for h in range(H_C)]
    rows = lambda ci: slice(c * ci, c * (ci + 1))
    l2 = lambda x: x * lax.rsqrt(jnp.sum(x * x, axis=-1, keepdims=True) + EPS)
    q = [l2(qkv[rows(ci), DK_C * h:DK_C * (h + 1)]) * (DK_C ** -0.5) for ci, h in probs]
    k = [l2(qkv[rows(ci), D_C + DK_C * h:D_C + DK_C * (h + 1)]) for ci, h in probs]
    v = [qkv[rows(ci), 2 * D_C + DV_C * h:2 * D_C + DV_C * (h + 1)] for ci, h in probs]
    beta_c = [beta_all[rows(ci), SM_BETA + h:SM_BETA + h + 1] for ci, h in probs]
    gc_c = [jnp.sum(jnp.where(incl, g_t[SM_A + h:SM_A + h + 1, rows(ci)], 0.0), axis=1, keepdims=True)
            for ci, h in probs]
    gc_r = [jnp.sum(jnp.where(rowi <= coli, g_all[rows(ci), SM_A + h:SM_A + h + 1], 0.0), axis=0, keepdims=True)
            for ci, h in probs]
    gam = [jnp.exp(jnp.where(incl, a - b, -jnp.inf)) for a, b in zip(gc_c, gc_r)]
    qb = [x.astype(BF16) for x in q]
    kb = [x.astype(BF16) for x in k]
    kk = [lax.dot_general(x, x, NT, preferred_element_type=F32) for x in kb]
    qk = [(lax.dot_general(a, b, NT, preferred_element_type=F32) * g).astype(BF16) for a, b, g in zip(qb, kb, gam)]
    egc = [jnp.exp(x) for x in gc_c]
    tinv = _unit_lower_inverses([jnp.where(strict, b * g * x, 0.0) for b, g, x in zip(beta_c, gam, kk)])
    tr = [_dot_x3(t, jnp.concatenate([b * vv, (b * e) * kx], axis=1))
          for t, b, vv, e, kx in zip(tinv, beta_c, v, egc, k)]
    u0 = [x[:, 0:DV_C] for x in tr]
    wb = [x[:, DV_C:].astype(BF16) for x in tr]
    g_last = [x[c - 1:c, :] for x in gc_c]
    kw = [(jnp.exp(gl - gc) * kx).astype(BF16) for gl, gc, kx in zip(g_last, gc_c, k)]
    e_last = [jnp.exp(x) for x in g_last]

    st = [s_ref[0, h] for h in range(H_C)]
    for ci in range(nchunk):
        ids = [ci * H_C + h for h in range(H_C)]
        stb = [x.astype(BF16) for x in st]
        ub = [(u0[i] - _dot(wb[i], stb[h])).astype(BF16) for h, i in enumerate(ids)]
        qs = [_dot(qb[i], stb[h]) for h, i in enumerate(ids)]
        st = [e_last[i] * st[h] + lax.dot_general(kw[i], ub[h], TN, preferred_element_type=F32)
              for h, i in enumerate(ids)]
        o = [egc[i] * qs[h] + _dot(qk[i], ub[h]) for h, i in enumerate(ids)]
        for h in range(H_C):
            zz = z_ref[rows(ci), DV_C * h:DV_C * (h + 1)]
            y_ref[rows(ci), DV_C * h:DV_C * (h + 1)] = (_rms(o[h], nw_ref[...]) * _silu(zz)).astype(BF16)
    for h in range(H_C):
        s_ref[0, h] = st[h]


def _gdn_prompt(proj, bsz, seq, gp):
    c = _pick(seq, (GDN_CHUNK,))
    tt = _pick(seq, (GDN_TILE,))
    nt = seq // tt
    rows = lambda col: (lambda b, t: (b * nt + t, col))
    const2 = lambda b, t: (0, 0)
    return pl.pallas_call(
        functools.partial(_gdn_prompt_kernel, chunk=c),
        grid=(bsz, nt),
        in_specs=[pl.BlockSpec((tt, D_C), rows(COL_QKV // 4)),
                  pl.BlockSpec((tt, D_C), rows(COL_QKV // 4 + 1)),
                  pl.BlockSpec((tt, D_C), rows(COL_QKV // 4 + 2)),
                  pl.BlockSpec((tt, D_C), rows(COL_Z // 4)),
                  pl.BlockSpec((tt, LANE), rows(COL_SMALL)),
                  pl.BlockSpec((1, LANE), const2),
                  pl.BlockSpec((1, LANE), const2),
                  pl.BlockSpec((GDN_CONV, 3 * D_C), const2),
                  pl.BlockSpec((1, DV_C), const2)],
        out_specs=[pl.BlockSpec((tt, D_C), lambda b, t: (b * nt + t, 0)),
                   pl.BlockSpec((1, H_C, DK_C, DV_C), lambda b, t: (b, 0, 0, 0))],
        out_shape=[jax.ShapeDtypeStruct((bsz * seq, D_C), BF16),
                   jax.ShapeDtypeStruct((bsz, H_C, DK_C, DV_C), F32)],
        scratch_shapes=[pltpu.VMEM((SUBLANE + tt, 3 * D_C), F32)],
        compiler_params=_cparams(("arbitrary", "arbitrary")),
        name="gdn_prompt",
    )(proj, proj, proj, proj, proj, gp["bias"], gp["alog"], gp["conv_w"], gp["norm"])


def _to_col(row, eye):
    return jnp.sum(jnp.where(eye, row, 0.0), axis=1, keepdims=True)


N_SAMPLE_INPUTS = 23


def _sample_mixers_kernel(*refs, fill_layer):
    (proj_ref, sre_ref, sim_ref, c_ref, n_ref, m_ref, gs_ref, gbuf_ref,
     are_ref, aim_ref, ldt_ref, bre_ref, bim_ref, cre_ref, cim_ref, d_ref, wglu_ref,
     mbias_ref, mnorm_ref, gbias_ref, alog_ref, gcw_ref, gnorm_ref) = refs[:N_SAMPLE_INPUTS]
    (y_ref, sre_o, sim_o, c_o, n_o, m_o, gs_o, gbuf_o,
     qkv_s, qn_s, kn_s, beta_s, g_s, li_s, lf_s) = refs[-15:]
    bb = proj_ref.shape[0]
    if fill_layer is not None:
        for l in range(c_o.shape[0]):
            if l != fill_layer:
                c_o[l] = jnp.zeros(c_o.shape[1:], F32)
                gs_o[l] = jnp.zeros(gs_o.shape[1:], F32)
        c_o = c_o.at[fill_layer]
        gs_o = gs_o.at[fill_layer]

    u = proj_ref[:, COL_U * LANE:COL_U * LANE + D_A]
    ub = u.astype(BF16)
    ys = []
    for j in range(S5_NCHUNK):
        sl = slice(S5_CHUNK * j, S5_CHUNK * (j + 1))
        abr, abi, fre, fim = _s5_disc(are_ref[:, sl], aim_ref[:, sl], ldt_ref[:, sl])
        uj = ub[:, LANE * j:LANE * (j + 1)]
        bur = _dot(uj, bre_ref[j])
        bui = _dot(uj, bim_ref[j])
        s0r = sre_ref[:, sl]
        s0i = sim_ref[:, sl]
        xr = fre * bur - fim * bui + abr * s0r - abi * s0i
        xi = fre * bui + fim * bur + abr * s0i + abi * s0r
        sre_o[:, sl] = xr
        sim_o[:, sl] = xi
        ys.append(_dot(xr.astype(BF16), cre_ref[j]) - _dot(xi.astype(BF16), cim_ref[j]))
    y_a = jnp.concatenate(ys, axis=1)
    y_ref[:, Y_A:Y_A + D_A] = _s5_glu(y_a, u, d_ref, wglu_ref)

    sm = proj_ref[:, COL_SMALL * LANE:(COL_SMALL + 1) * LANE]
    smb = sm + mbias_ref[...]
    li_s[...] = smb
    lf_s[...] = _log_sigmoid(smb)
    beta_s[...] = _sigmoid(sm)
    g_s[...] = -jnp.exp(alog_ref[...]) * _softplus(sm + gbias_ref[...])

    xnew = proj_ref[:, COL_QKV * LANE:COL_QKV * LANE + 3 * D_C]
    conv = gcw_ref[GDN_CONV - 1:GDN_CONV, :] * xnew
    for j in range(GDN_CONV - 1):
        conv = conv + gcw_ref[j:j + 1, :] * gbuf_ref[j]
        if j > 0:
            gbuf_o[j - 1] = gbuf_ref[j]
    gbuf_o[GDN_CONV - 2] = xnew
    qkv = _silu(conv)
    qkv_s[...] = qkv
    for h in range(H_C):
        q = qkv[:, DK_C * h:DK_C * (h + 1)]
        k = qkv[:, D_C + DK_C * h:D_C + DK_C * (h + 1)]
        qn_s[:, DK_C * h:DK_C * (h + 1)] = q * lax.rsqrt(jnp.sum(q * q, axis=-1, keepdims=True) + EPS) * (DK_C ** -0.5)
        kn_s[:, DK_C * h:DK_C * (h + 1)] = k * lax.rsqrt(jnp.sum(k * k, axis=-1, keepdims=True) + EPS)

    assert 4 * H_B * bb == LANE and DQK_B == LANE and DK_C == LANE and H_B == H_C
    tiles = ([proj_ref[:, COL_QB * LANE + DQK_B * h:COL_QB * LANE + DQK_B * (h + 1)] * (DQK_B ** -0.5)
              for h in range(H_B)]
             + [proj_ref[:, COL_KB * LANE + DQK_B * h:COL_KB * LANE + DQK_B * (h + 1)] for h in range(H_B)]
             + [qn_s[:, DK_C * h:DK_C * (h + 1)] for h in range(H_C)]
             + [kn_s[:, DK_C * h:DK_C * (h + 1)] for h in range(H_C)])
    stacked_t = jnp.concatenate(tiles, axis=0).T

    def col_of(kind, b, h):
        j = (kind * H_B + h) * bb + b
        return stacked_t[:, j:j + 1]

    lane = lax.broadcasted_iota(jnp.int32, (1, H_B), 1)

    rsl = lambda b: slice(b, b + 1)
    for g0 in range(0, bb, SAMPLE_GROUP):
        probs = [(b, h) for b in range(g0, g0 + SAMPLE_GROUP) for h in range(H_B)]

        q = [proj_ref[rsl(b), COL_QB * LANE + DQK_B * h:COL_QB * LANE + DQK_B * (h + 1)] * (DQK_B ** -0.5)
             for b, h in probs]
        k = [proj_ref[rsl(b), COL_KB * LANE + DQK_B * h:COL_KB * LANE + DQK_B * (h + 1)] for b, h in probs]
        v = [proj_ref[rsl(b), COL_VB * LANE + DV_B * h:COL_VB * LANE + DV_B * (h + 1)] for b, h in probs]
        li = [li_s[rsl(b), SM_I + h:SM_I + h + 1] for b, h in probs]
        inter = [lf_s[rsl(b), SM_F + h:SM_F + h + 1] + m_ref[rsl(b), h:h + 1] for b, h in probs]
        m_t = [jnp.maximum(a, c) for a, c in zip(inter, li)]
        w_intra = [jnp.exp(a - c) for a, c in zip(li, m_t)]
        w_inter = [jnp.exp(a - c) for a, c in zip(inter, m_t)]
        qcol = [col_of(0, b, h) for b, h in probs]
        kcol = [col_of(1, b, h) for b, h in probs]
        s = [jnp.sum(a * c, axis=1, keepdims=True) * w for a, c, w in zip(q, k, w_intra)]
        cst = [c_ref[b, h] for b, h in probs]
        nrow = [n_ref[b, h:h + 1, :] for b, h in probs]
        qc = [jnp.sum(a * c, axis=0, keepdims=True) for a, c in zip(qcol, cst)]
        for i, (b, h) in enumerate(probs):
            c_o[b, h] = w_inter[i] * cst[i] + (w_intra[i] * kcol[i]) * v[i]
            n_o[b, h:h + 1, :] = w_inter[i] * nrow[i] + w_intra[i] * k[i]
        num = [s[i] * v[i] + w_inter[i] * qc[i] for i in range(len(probs))]
        nq = [s[i] + w_inter[i] * jnp.sum(q[i] * nrow[i], axis=1, keepdims=True) for i in range(len(probs))]
        hh = [num[i] / jnp.maximum(jnp.abs(nq[i]), jnp.exp(-m_t[i])) for i in range(len(probs))]
        for i, (b, h) in enumerate(probs):
            og = proj_ref[rsl(b), COL_OB * LANE + DV_B * h:COL_OB * LANE + DV_B * (h + 1)]
            y_ref[rsl(b), Y_B + DV_B * h:Y_B + DV_B * (h + 1)] = _rms(hh[i], mnorm_ref[h:h + 1, :]) * _sigmoid(og)
        for b in range(g0, g0 + SAMPLE_GROUP):
            m_out = m_ref[rsl(b), :]
            for h in range(H_B):
                m_out = jnp.where(lane == h, m_t[(b - g0) * H_B + h], m_out)
            m_o[rsl(b), :] = m_out

        q = [qn_s[rsl(b), DK_C * h:DK_C * (h + 1)] for b, h in probs]
        k = [kn_s[rsl(b), DK_C * h:DK_C * (h + 1)] for b, h in probs]
        v = [qkv_s[rsl(b), 2 * D_C + DV_C * h:2 * D_C + DV_C * (h + 1)] for b, h in probs]
        beta = [beta_s[rsl(b), SM_BETA + h:SM_BETA + h + 1] for b, h in probs]
        eg = [jnp.exp(g_s[rsl(b), SM_A + h:SM_A + h + 1]) for b, h in probs]
        qcol = [col_of(2, b, h) for b, h in probs]
        kcol = [col_of(3, b, h) for b, h in probs]
        st = [gs_ref[b, h] for b, h in probs]
        ks = [jnp.sum(a * c, axis=0, keepdims=True) for a, c in zip(kcol, st)]
        qs = [jnp.sum(a * c, axis=0, keepdims=True) for a, c in zip(qcol, st)]
        un = [beta[i] * (v[i] - eg[i] * ks[i]) for i in range(len(probs))]
        for i, (b, h) in enumerate(probs):
            gs_o[b, h] = eg[i] * st[i] + kcol[i] * un[i]
        o = [eg[i] * qs[i] + jnp.sum(q[i] * k[i], axis=1, keepdims=True) * un[i] for i in range(len(probs))]
        for i, (b, h) in enumerate(probs):
            zz = proj_ref[rsl(b), COL_Z * LANE + DV_C * h:COL_Z * LANE + DV_C * (h + 1)]
            y_ref[rsl(b), Y_C + DV_C * h:Y_C + DV_C * (h + 1)] = _rms(o[i], gnorm_ref[...]) * _silu(zz)


def _sample_mixers(proj, row0, nrows, st, sp, mp, gp, layer, prev):
    bb = SAMPLE_BLOCK
    blk0 = row0 // bb
    depth = st["c"].shape[0]
    const2 = lambda i: (0, 0)
    const3 = lambda i: (0, 0, 0)
    row_spec = pl.BlockSpec((1, S5_LANES), const2)
    c_spec = _layered(layer, (bb, H_B, DQK_B, DV_B), lambda i: (i, 0, 0, 0))
    gs_spec = _layered(layer, (bb, H_C, DK_C, DV_C), lambda i: (i, 0, 0, 0))
    in_specs = [
        pl.BlockSpec((bb, N_PROJ), lambda i: (blk0 + i, 0)),
        pl.BlockSpec((bb, S5_LANES), lambda i: (i, 0)),
        pl.BlockSpec((bb, S5_LANES), lambda i: (i, 0)),
        c_spec,
        pl.BlockSpec((bb, H_B, DQK_B), lambda i: (i, 0, 0)),
        pl.BlockSpec((bb, H_B), lambda i: (i, 0)),
        gs_spec,
        pl.BlockSpec((GDN_CONV - 1, bb, 3 * D_C), lambda i: (0, i, 0)),
        row_spec, row_spec, row_spec,
        pl.BlockSpec((S5_NCHUNK, LANE, S5_CHUNK), const3),
        pl.BlockSpec((S5_NCHUNK, LANE, S5_CHUNK), const3),
        pl.BlockSpec((S5_NCHUNK, S5_CHUNK, LANE), const3),
        pl.BlockSpec((S5_NCHUNK, S5_CHUNK, LANE), const3),
        pl.BlockSpec((1, D_A), const2),
        pl.BlockSpec((D_A, D_A), const2),
        pl.BlockSpec((1, LANE), const2),
        pl.BlockSpec((H_B, DV_B), const2),
        pl.BlockSpec((1, LANE), const2),
        pl.BlockSpec((1, LANE), const2),
        pl.BlockSpec((GDN_CONV, 3 * D_C), const2),
        pl.BlockSpec((1, DV_C), const2),
    ]
    out_specs = [
        pl.BlockSpec((bb, D_MODEL), lambda i: (i, 0)),
        pl.BlockSpec((bb, S5_LANES), lambda i: (i, 0)),
        pl.BlockSpec((bb, S5_LANES), lambda i: (i, 0)),
        c_spec if prev is not None else pl.BlockSpec((depth, bb, H_B, DQK_B, DV_B), lambda i: (0, i, 0, 0, 0)),
        pl.BlockSpec((bb, H_B, DQK_B), lambda i: (i, 0, 0)),
        pl.BlockSpec((bb, H_B), lambda i: (i, 0)),
        gs_spec if prev is not None else pl.BlockSpec((depth, bb, H_C, DK_C, DV_C), lambda i: (0, i, 0, 0, 0)),
        pl.BlockSpec((GDN_CONV - 1, bb, 3 * D_C), lambda i: (0, i, 0)),
    ]
    out_shape = [
        jax.ShapeDtypeStruct((nrows, D_MODEL), F32),
        jax.ShapeDtypeStruct((nrows, S5_LANES), F32),
        jax.ShapeDtypeStruct((nrows, S5_LANES), F32),
        jax.ShapeDtypeStruct((depth, nrows, H_B, DQK_B, DV_B), F32),
        jax.ShapeDtypeStruct((nrows, H_B, DQK_B), F32),
        jax.ShapeDtypeStruct((nrows, H_B), F32),
        jax.ShapeDtypeStruct((depth, nrows, H_C, DK_C, DV_C), F32),
        jax.ShapeDtypeStruct((GDN_CONV - 1, nrows, 3 * D_C), F32),
    ]
    scratch = [pltpu.VMEM((bb, 3 * D_C), F32), pltpu.VMEM((bb, D_C), F32), pltpu.VMEM((bb, D_C), F32),
               pltpu.VMEM((bb, LANE), F32), pltpu.VMEM((bb, LANE), F32),
               pltpu.VMEM((bb, LANE), F32), pltpu.VMEM((bb, LANE), F32)]
    args = [proj, st["sre"], st["sim"], st["c"], st["n"], st["m"], st["gs"], st["gbuf"],
            sp["are"], sp["aim"], sp["ldt"], sp["bre"], sp["bim"], sp["cre"], sp["cim"], sp["d"], sp["wglu"],
            mp["bias"], mp["norm"], gp["bias"], gp["alog"], gp["conv_w"], gp["norm"]]
    assert len(args) == N_SAMPLE_INPUTS
    aliases = {}
    if prev is not None:
        in_specs += [pl.BlockSpec(memory_space=pl.ANY)] * 2
        args += list(prev)
        aliases = {N_SAMPLE_INPUTS: 3, N_SAMPLE_INPUTS + 1: 6}
    return pl.pallas_call(
        functools.partial(_sample_mixers_kernel, fill_layer=layer if prev is None else None),
        grid=(nrows // bb,),
        in_specs=in_specs, out_specs=out_specs, out_shape=out_shape, scratch_shapes=scratch,
        input_output_aliases=aliases,
        compiler_params=_cparams(("arbitrary",)),
        name="sample_mixers",
    )(*args)


def _merge_kernel(h_ref, ya_ref, yb_ref, yc_ref, wg0_ref, wg1_ref, wg2_ref, wa_ref, wb_ref, wc_ref, o_ref):
    h = h_ref[...]
    bf = lambda ref: ref[...].astype(BF16)
    acc = _sigmoid(_dot(h, bf(wg0_ref))) * _dot(bf(ya_ref), bf(wa_ref))
    acc = acc + _sigmoid(_dot(h, bf(wg1_ref))) * _dot(bf(yb_ref), bf(wb_ref))
    acc = acc + _sigmoid(_dot(h, bf(wg2_ref))) * _dot(bf(yc_ref), bf(wc_ref))
    o_ref[...] = acc.astype(BF16)


def _merge(h, ya, yb, yc, wg, wa, wb, wc, layer):
    m, d = h.shape
    tm = _pick(m, (1024, 512, 256, 128))
    tn = 256
    nb = d // tn
    lhs = lambda w, cb: pl.BlockSpec((tm, w), lambda i, j: (i, cb))
    gate = lambda g: _layered(layer, (d, tn), lambda i, j: (0, g * nb + j))
    rhs = lambda w: _layered(layer, (w, tn), lambda i, j: (0, j))
    return pl.pallas_call(
        _merge_kernel,
        grid=(m // tm, nb),
        in_specs=[lhs(d, 0), lhs(D_A, ya[1]), lhs(D_B, yb[1]), lhs(D_C, yc[1]), gate(0), gate(1), gate(2),
                  rhs(D_A), rhs(D_B), rhs(D_C)],
        out_specs=pl.BlockSpec((tm, tn), lambda i, j: (i, j)),
        out_shape=jax.ShapeDtypeStruct((m, d), BF16),
        compiler_params=_cparams(("arbitrary", "arbitrary")),
        name="merge",
    )(h, ya[0], yb[0], yc[0], wg, wg, wg, wa, wb, wc)


def _out_proj_kernel(a_ref, x_ref, w_ref, nw_ref, o_ref):
    o_ref[...] = x_ref[...] + _rms(_dot(a_ref[...], w_ref[...]), nw_ref[...])


def _out_proj(a, x, w, nw, layer):
    m, d = x.shape
    tm = _pick(m, (512, 256, 128))
    return pl.pallas_call(
        _out_proj_kernel,
        grid=(m // tm,),
        in_specs=[pl.BlockSpec((tm, d), lambda i: (i, 0)),
                  pl.BlockSpec((tm, d), lambda i: (i, 0)),
                  _layered(layer, (d, d), lambda i: (0, 0)),
                  _layered(layer, (1, d), lambda i: (0, 0))],
        out_specs=pl.BlockSpec((tm, d), lambda i: (i, 0)),
        out_shape=jax.ShapeDtypeStruct((m, d), F32),
        compiler_params=_cparams(("arbitrary",)),
        name="out_proj",
    )(a, x, w, nw)


def _ffn_body(step, *refs, sample, tiles_per_seq, ft):
    if sample:
        (x_ref, nw_ref, wg_ref, wu_ref, cw_ref, wd_ref, pnw_ref, b0_ref, b1_ref,
         o_ref, g_ref, wg_o, wu_o, wd_o, h2_s) = refs
    else:
        (x_ref, nw_ref, wg_ref, wu_ref, cw_ref, wd_ref, pnw_ref,
         o_ref, g_ref, h2_s, gb_s, carry_s) = refs
    i = step // ft
    j = step % ft
    tm = x_ref.shape[0]

    @pl.when(j == 0)
    def _():
        h2_s[...] = _rms(x_ref[...], nw_ref[...]).astype(BF16)
        o_ref[...] = jnp.zeros_like(o_ref)

    h2 = h2_s[...]
    wg, wu, wd = wg_ref[...], wu_ref[...], wd_ref[...]
    if sample:
        wg, wu, wd = wg.astype(BF16), wu.astype(BF16), wd.astype(BF16)
        wg_o[0] = wg
        wu_o[0] = wu
        wd_o[0] = wd
    g = _dot(h2, wg)
    up = _dot(h2, wu)
    if sample:
        a = cw_ref[0:1, :] * b0_ref[...] + cw_ref[1:2, :] * b1_ref[...] + cw_ref[2:3, :] * g
        g_ref[...] = g
    else:
        prev = jnp.where(i % tiles_per_seq == 0, 0.0, carry_s[j])
        gb_s[0:SUBLANE, :] = prev
        gb_s[SUBLANE:SUBLANE + tm, :] = g
        a = (cw_ref[0:1, :] * gb_s[SUBLANE - 2:SUBLANE - 2 + tm, :]
             + cw_ref[1:2, :] * gb_s[SUBLANE - 1:SUBLANE - 1 + tm, :]
             + cw_ref[2:3, :] * g)
        tail = g[tm - SUBLANE:tm, :]
        carry_s[j] = tail
        g_ref[0] = tail
    act = (_gelu(a) * up).astype(BF16)
    o_ref[...] += _dot(act, wd)

    @pl.when(j == ft - 1)
    def _():
        o_ref[...] = x_ref[...] + _rms(o_ref[...], pnw_ref[...])


def _ffn_part(x, row0, nrows, seq, nw, wg, wu, cw, wd, pnw, layer, bufs=None, steps=None):
    d = x.shape[1]
    f = wg.shape[-1]
    sample = bufs is not None
    tm = nrows if sample else _pick(seq, (512, 256, 128, 64, 32, 16, 8))
    tn = _pick(f, (512, 256, 128) if sample else (1024, 512, 256, 128))
    if sample and steps is not None and f % steps == 0 and (f // steps) % LANE == 0 and f // steps <= tn:
        tn = f // steps
    mt, ft = nrows // tm, f // tn
    blk0 = row0 // tm
    mi = lambda s: s // ft
    nj = lambda s: s % ft
    in_specs = [pl.BlockSpec((tm, d), lambda s: (blk0 + mi(s), 0)),
                _layered(layer, (1, d), lambda s: (0, 0)),
                _layered(layer, (d, tn), lambda s: (0, nj(s))),
                _layered(layer, (d, tn), lambda s: (0, nj(s))),
                _layered(layer, (FFN_CONV, tn), lambda s: (0, nj(s))),
                _layered(layer, (tn, d), lambda s: (nj(s), 0)),
                _layered(layer, (1, d), lambda s: (0, 0))]
    args = [x, nw, wg, wu, cw, wd, pnw]
    scratch = [pltpu.VMEM((tm, d), BF16)]
    if sample:
        in_specs += [pl.BlockSpec((tm, tn), lambda s: (mi(s), nj(s)))] * 2
        args += list(bufs)
        assert mt == 1, "the sample call must visit every weight tile exactly once"
        g_spec = pl.BlockSpec((tm, tn), lambda s: (mi(s), nj(s)))
        g_shape = jax.ShapeDtypeStruct((nrows, f), F32)
        extra_specs = [pl.BlockSpec((1, d, tn), lambda s: (0, 0, nj(s))),
                       pl.BlockSpec((1, d, tn), lambda s: (0, 0, nj(s))),
                       pl.BlockSpec((1, tn, d), lambda s: (0, nj(s), 0))]
        extra_shapes = [jax.ShapeDtypeStruct((1, d, f), BF16), jax.ShapeDtypeStruct((1, d, f), BF16),
                        jax.ShapeDtypeStruct((1, f, d), BF16)]
    else:
        extra_specs, extra_shapes = [], []
        scratch += [pltpu.VMEM((SUBLANE + tm, tn), F32), pltpu.VMEM((ft, SUBLANE, tn), F32)]
        g_spec = pl.BlockSpec((1, SUBLANE, tn), lambda s: (mi(s), 0, nj(s)))
        g_shape = jax.ShapeDtypeStruct((mt, SUBLANE, f), F32)
    return dict(
        n=mt * ft,
        body=functools.partial(_ffn_body, sample=sample, tiles_per_seq=max(seq // tm, 1), ft=ft),
        in_specs=in_specs, args=args,
        out_specs=[pl.BlockSpec((tm, d), lambda s: (mi(s), 0)), g_spec] + extra_specs,
        out_shape=[jax.ShapeDtypeStruct((nrows, d), F32), g_shape] + extra_shapes,
        scratch=scratch)


def _ffn(*args, **kwargs):
    part = _ffn_part(*args, **kwargs)
    return _run_parts("ffn_sample" if kwargs.get("bufs") is not None else "ffn_prompt", [part])[0]


def _ple_kernel(x_ref, p_ref, wg_ref, wp_ref, o_ref):
    x = x_ref[...]
    gate = _sigmoid(_dot(x.astype(BF16), wg_ref[...]))
    o_ref[...] = x + gate * _dot(p_ref[...].astype(BF16), wp_ref[...])


def _ple(x, p, wg, wp, layer):
    m, d = x.shape
    pd = p.shape[-1]
    tm = _pick(m, (512, 256, 128))
    return pl.pallas_call(
        _ple_kernel,
        grid=(m // tm,),
        in_specs=[pl.BlockSpec((tm, d), lambda i: (i, 0)),
                  _layered(layer, (tm, pd), lambda i: (i, 0)),
                  _layered(layer, (d, d), lambda i: (0, 0)),
                  _layered(layer, (pd, d), lambda i: (0, 0))],
        out_specs=pl.BlockSpec((tm, d), lambda i: (i, 0)),
        out_shape=jax.ShapeDtypeStruct((m, d), F32),
        compiler_params=_cparams(("arbitrary",)),
        name="ple",
    )(x, p, wg, wp)


def _split_w_in(w):
    n_if = 2 * H_B
    w_head = w[..., 0:N_HEAD]
    w_mid = w[..., N_HEAD + n_if:N_HEAD + n_if + N_MID]
    gates = jnp.concatenate([w[..., N_HEAD:N_HEAD + n_if], w[..., N_HEAD + n_if + N_MID:]], axis=-1)
    pad = jnp.zeros(w.shape[:-1] + (LANE - gates.shape[-1],), w.dtype)
    return w_head.astype(BF16), w_mid.astype(BF16), jnp.concatenate([gates, pad], axis=-1).astype(BF16)


def _small_row(entries):
    row = jnp.zeros((LANE,), F32)
    for off, val in entries:
        row = row.at[off:off + val.shape[0]].set(val.astype(F32))
    return row.reshape(1, LANE)


def _s5_params(a_re, a_im, log_dt, b_re, b_im, c_re, c_im, d, w_glu):
    gpc = S5_CHUNK // S5_STATE
    eye = jnp.eye(gpc, dtype=F32)

    def bmat(b):
        b4 = b.reshape(S5_NCHUNK, gpc, S5_GROUP, S5_STATE)
        return jnp.einsum('jgcp,gh->jgchp', b4, eye).reshape(S5_NCHUNK, gpc * S5_GROUP, S5_CHUNK).astype(BF16)

    def cmat(c):
        c4 = c.reshape(S5_NCHUNK, gpc, S5_STATE, S5_GROUP)
        return jnp.einsum('jgpc,gh->jgphc', c4, eye).reshape(S5_NCHUNK, S5_CHUNK, gpc * S5_GROUP).astype(BF16)

    return dict(are=a_re.reshape(1, S5_LANES), aim=a_im.reshape(1, S5_LANES),
                ldt=jnp.broadcast_to(log_dt[:, None], (S5_GROUPS, S5_STATE)).reshape(1, S5_LANES),
                bre=bmat(b_re), bim=bmat(b_im), cre=cmat(c_re), cim=cmat(c_im),
                d=d.reshape(1, D_A), wglu=w_glu.astype(BF16))


def _layer(layer, xp, xs, bsz, seq, dense, lw, state, prev):
    np_rows = bsz * seq
    nsamp = xs.shape[0]
    in_w = (dense['norm_mix_pre'], dense['w_in'], dense['w_in_mid'], dense['w_in_small'], layer)
    proj, h_p = _norm_proj(xp, *in_w)
    proj_s, h_s = _norm_proj(xs, *in_w)

    sp = _s5_params(lw['s5_a_re'], lw['s5_a_im'], lw['s5_log_dt'], lw['s5_b_re'], lw['s5_b_im'],
                    lw['s5_c_re'], lw['s5_c_im'], lw['s5_d'], lw['s5_w_glu'])
    mp = dict(bias=_small_row([(SM_I, lw['mlstm_b_i']), (SM_F, lw['mlstm_b_f'])]), norm=lw['mlstm_norm'])
    gp = dict(bias=_small_row([(SM_A, lw['gdn_dt_bias'])]), alog=_small_row([(SM_A, lw['gdn_a_log'])]),
              conv_w=lw['gdn_conv_w'], norm=lw['gdn_norm'].reshape(1, DV_C))

    ssm_re, ssm_im, m_c, m_n, m_m, g_s, g_conv, f_conv = state
    st = dict(sre=ssm_re.reshape(nsamp, S5_LANES), sim=ssm_im.reshape(nsamp, S5_LANES),
              c=m_c, n=m_n, m=m_m, gs=g_s, gbuf=jnp.swapaxes(g_conv, 0, 1))
    y_s, sre_s, sim_s, c_s, n_s, m_s, gs_s, gbuf_s = _sample_mixers(proj_s, 0, nsamp, st, sp, mp, gp, layer, prev)
    merge_w = (dense['w_gate'], dense['w_branch_a'], dense['w_branch_b'], dense['w_branch_c'], layer)
    merged_s = _merge(h_s, (y_s, Y_A // D_A), (y_s, Y_B // D_B), (y_s, Y_C // D_C), *merge_w)
    x1_s = _out_proj(merged_s, xs, dense['w_out'], dense['norm_mix_post'], layer)

    s5_part = _s5_prompt_part(proj, bsz, seq, sp)
    ffn_s_part = _ffn_part(
        x1_s, 0, nsamp, 1, dense['norm_ffn_pre'], dense['ffn_w_gate'], dense['ffn_w_up'], dense['ffn_conv_w'],
        dense['ffn_w_down'], dense['norm_ffn_post'], layer, bufs=(f_conv[:, 0], f_conv[:, 1]), steps=s5_part["n"])
    if s5_part["n"] == ffn_s_part["n"]:
        (ya_p, sre_p, sim_p), (x2_s, g_new, wg_b, wu_b, wd_b) = _run_parts("s5_prompt_ffn_sample",
                                                                             [s5_part, ffn_s_part])
    else:
        ya_p, sre_p, sim_p = _run_parts("s5_prompt", [s5_part])[0]
        x2_s, g_new, wg_b, wu_b, wd_b = _run_parts("ffn_sample", [ffn_s_part])[0]
    yb_p, c_p, n_p, m_p = _mlstm_prompt(proj, bsz, seq, mp)
    yc_p, gs_p = _gdn_prompt(proj, bsz, seq, gp)

    merged_p = _merge(h_p, (ya_p, 0), (yb_p, 0), (yc_p, 0), *merge_w)
    x1_p = _out_proj(merged_p, xp, dense['w_out'], dense['norm_mix_post'], layer)
    cw_l, nw_pre_l, nw_post_l = (dense[k][layer:layer + 1] for k in ('ffn_conv_w', 'norm_ffn_pre', 'norm_ffn_post'))
    x2_p, gtail = _ffn(x1_p, 0, np_rows, seq, nw_pre_l, wg_b, wu_b, cw_l, wd_b, nw_post_l, 0)
    x3_p = _ple(x2_p, dense['p_prompt'], dense['ple_w_gate'], dense['ple_w_proj'], layer)
    x3_s = _ple(x2_s, dense['p_sample'], dense['ple_w_gate'], dense['ple_w_proj'], layer)

    tiles_per_seq = gtail.shape[0] // bsz
    qkv_tail = proj.reshape(bsz, seq, N_PROJ)[:, seq - (GDN_CONV - 1):, COL_QKV * LANE:COL_QKV * LANE + 3 * D_C]
    st_p = (sre_p.reshape(bsz, S5_GROUPS, S5_STATE), sim_p.reshape(bsz, S5_GROUPS, S5_STATE),
            c_p, n_p, m_p[:, 0, :H_B], gs_p, qkv_tail,
            gtail.reshape(bsz, tiles_per_seq, SUBLANE, D_FF)[:, -1, SUBLANE - (FFN_CONV - 1):, :])
    st_s = (sre_s.reshape(nsamp, S5_GROUPS, S5_STATE), sim_s.reshape(nsamp, S5_GROUPS, S5_STATE),
            None, n_s, m_s, None, jnp.swapaxes(gbuf_s, 0, 1),
            jnp.stack([f_conv[:, 1], g_new], axis=1))
    return x3_p, x3_s, st_p, st_s, (c_s, gs_s)


def kernel(x_prompt, x_sample, p_prompt, p_sample, state_ssm_re, state_ssm_im, state_mlstm_c, state_mlstm_n, state_mlstm_m, state_gdn_s, state_gdn_conv, state_ffn_conv, norm_mix_pre, norm_mix_post, norm_ffn_pre, norm_ffn_post, w_in, s5_a_re, s5_a_im, s5_log_dt, s5_b_re, s5_b_im, s5_c_re, s5_c_im, s5_d, s5_w_glu, mlstm_b_i, mlstm_b_f, mlstm_norm, gdn_conv_w, gdn_a_log, gdn_dt_bias, gdn_norm, w_branch_a, w_branch_b, w_branch_c, w_gate, w_out, ffn_w_gate, ffn_w_up, ffn_conv_w, ffn_w_down, ple_w_proj, ple_w_gate):
    bsz, seq, d = x_prompt.shape
    nsamp = x_sample.shape[0]
    depth = w_in.shape[0]
    small = dict(
        s5_a_re=s5_a_re, s5_a_im=s5_a_im, s5_log_dt=s5_log_dt, s5_b_re=s5_b_re, s5_b_im=s5_b_im,
        s5_c_re=s5_c_re, s5_c_im=s5_c_im, s5_d=s5_d, s5_w_glu=s5_w_glu, mlstm_b_i=mlstm_b_i,
        mlstm_b_f=mlstm_b_f, mlstm_norm=mlstm_norm, gdn_conv_w=gdn_conv_w, gdn_a_log=gdn_a_log,
        gdn_dt_bias=gdn_dt_bias, gdn_norm=gdn_norm)
    row = lambda w: w.reshape(depth, 1, -1)
    w_in_head, w_in_mid, w_in_small = _split_w_in(w_in)
    dense = dict(
        norm_mix_pre=row(norm_mix_pre), norm_mix_post=row(norm_mix_post), norm_ffn_pre=row(norm_ffn_pre),
        norm_ffn_post=row(norm_ffn_post), w_in=w_in_head, w_in_mid=w_in_mid, w_in_small=w_in_small,
        w_gate=w_gate, w_branch_a=w_branch_a, w_branch_b=w_branch_b, w_branch_c=w_branch_c,
        w_out=w_out.astype(BF16), ffn_w_gate=ffn_w_gate, ffn_w_up=ffn_w_up, ffn_conv_w=ffn_conv_w,
        ffn_w_down=ffn_w_down,
        ple_w_gate=ple_w_gate.astype(BF16), ple_w_proj=ple_w_proj.astype(BF16),
        p_prompt=p_prompt.reshape(depth, bsz * seq, -1), p_sample=p_sample.reshape(depth, nsamp, -1))
    xp = x_prompt.reshape(bsz * seq, d)
    xs = x_sample.reshape(nsamp, d)
    sp_all, ss_all = [], []
    big = None
    for i in range(depth):
        lw = {k: v[i] for k, v in small.items()}
        state = (state_ssm_re[i], state_ssm_im[i], state_mlstm_c, state_mlstm_n[i], state_mlstm_m[i],
                 state_gdn_s, state_gdn_conv[i], state_ffn_conv[i])
        xp, xs, st_p, st_s, big = _layer(i, xp, xs, bsz, seq, dense, lw, state, big)
        sp_all.append(st_p)
        ss_all.append(st_s)
    stack = lambda sts, j: jnp.stack([s[j] for s in sts], axis=0)
    sample_states = [big[0] if j == 2 else big[1] if j == 5 else stack(ss_all, j) for j in range(8)]
    return ((xp.reshape(bsz, seq, d), xs.reshape(nsamp, 1, d))
            + tuple(stack(sp_all, j) for j in range(8))
            + tuple(sample_states))
```

```python
import functools

import jax
import jax.numpy as jnp
from jax import lax
from jax.experimental import pallas as pl
from jax.experimental.pallas import tpu as pltpu

F32 = jnp.float32
BF16 = jnp.bfloat16

D_MODEL = 2048
DEPTH = 2
D_A = 512
S5_GROUP = 16
S5_GROUPS = 32
S5_STATE = 64
S5_LANES = S5_GROUPS * S5_STATE
S5_CHUNK = 512
S5_NCHUNK = S5_LANES // S5_CHUNK
D_B = 1024
H_B = 4
DV_B = 256
DQK_B = 128
D_C = 512
H_C = 4
DK_C = 128
DV_C = 128
GDN_CONV = 4
D_FF = 8192
FFN_CONV = 3
PLE_DIM = 256
EPS = 1e-6

LANE = 128
SUBLANE = 8
VMEM_LIMIT = 56 * 1024 * 1024

COL_U, COL_QB, COL_KB, COL_VB, COL_OB, COL_QKV, COL_Z, COL_SMALL = 0, 4, 8, 12, 20, 28, 40, 44
N_HEAD = COL_OB * LANE
N_MID = (COL_SMALL - COL_OB) * LANE
PROJ_TILE = 512
N_PROJ = N_HEAD + N_MID + PROJ_TILE
SM_I, SM_F, SM_BETA, SM_A = 0, 4, 8, 12

MLSTM_CHUNK = 128
MLSTM_TILE = 512
GDN_CHUNK = 64
GDN_TILE = 512
S5_TILE = 512
SAMPLE_BLOCK = 8
SAMPLE_GROUP = SAMPLE_BLOCK
Y_B, Y_A, Y_C = 0, D_B, D_B + D_A

NT = (((1,), (1,)), ((), ()))
TN = (((0,), (0,)), ((), ()))


def _cparams(sem):
    return pltpu.CompilerParams(dimension_semantics=sem, vmem_limit_bytes=VMEM_LIMIT)


def _dot(a, b):
    return jnp.dot(a, b, preferred_element_type=F32)


def _dot_hi(a, b):
    return jnp.dot(a, b, preferred_element_type=F32, precision=lax.Precision.HIGHEST)


def _gelu(x):
    return 0.5 * x * (1.0 + jnp.tanh(0.7978845608028654 * (x + 0.044715 * (x * x * x))))


def _sigmoid(x):
    return 1.0 / (1.0 + jnp.exp(-x))


def _silu(x):
    return x * _sigmoid(x)


def _softplus(x):
    return jnp.maximum(x, 0.0) + jnp.log1p(jnp.exp(-jnp.abs(x)))


def _log_sigmoid(x):
    return -_softplus(-x)


def _rms(x, w):
    return x * lax.rsqrt(jnp.mean(x * x, axis=-1, keepdims=True) + EPS) * w


def _layered(layer, shape, imap):
    return pl.BlockSpec((None,) + shape, lambda *g: (layer,) + imap(*g))


def _pick(n, cands):
    for c in cands:
        if n % c == 0:
            return c
    return n


def _norm_proj_kernel(x_ref, nw_ref, wa_ref, wb_ref, ws_ref, proj_ref, h_ref, *, na, nb):
    j = pl.program_id(1)

    @pl.when(j == 0)
    def _():
        h_ref[...] = _rms(x_ref[...], nw_ref[...]).astype(BF16)

    @pl.when(j < na)
    def _():
        proj_ref[...] = _dot(h_ref[...], wa_ref[...].astype(BF16))

    @pl.when(jnp.logical_and(j >= na, j < na + nb))
    def _():
        proj_ref[...] = _dot(h_ref[...], wb_ref[...].astype(BF16))

    @pl.when(j >= na + nb)
    def _():
        proj_ref[...] = jnp.zeros_like(proj_ref)
        proj_ref[:, 0:LANE] = _dot(h_ref[...], ws_ref[...].astype(BF16))


def _norm_proj(x, nw, w_head, w_mid, w_small, layer):
    m, d = x.shape
    tm = _pick(m, (1024, 512, 256, 128))
    tn = PROJ_TILE
    na, nb = N_HEAD // tn, N_MID // tn
    return pl.pallas_call(
        functools.partial(_norm_proj_kernel, na=na, nb=nb),
        grid=(m // tm, na + nb + 1),
        in_specs=[pl.BlockSpec((tm, d), lambda i, j: (i, 0)),
                  _layered(layer, (1, d), lambda i, j: (0, 0)),
                  _layered(layer, (d, tn), lambda i, j: (0, jnp.minimum(j, na - 1))),
                  _layered(layer, (d, tn), lambda i, j: (0, jnp.clip(j - na, 0, nb - 1))),
                  _layered(layer, (d, LANE), lambda i, j: (0, 0))],
        out_specs=[pl.BlockSpec((tm, tn), lambda i, j: (i, j)),
                   pl.BlockSpec((tm, d), lambda i, j: (i, 0))],
        out_shape=[jax.ShapeDtypeStruct((m, N_PROJ), F32), jax.ShapeDtypeStruct((m, d), BF16)],
        compiler_params=_cparams(("arbitrary", "arbitrary")),
        name="norm_proj",
    )(x, nw, w_head, w_mid, w_small)


def _s5_disc(are, aim, ldt):
    dt = jnp.exp(ldt)
    mag = jnp.exp(dt * are)
    abr = mag * jnp.cos(dt * aim)
    abi = mag * jnp.sin(dt * aim)
    den = are * are + aim * aim
    zr = abr - 1.0
    fre = (zr * are + abi * aim) / den
    fim = (abi * are - zr * aim) / den
    return abr, abi, fre, fim


def _s5_glu(y, u, d_ref, wglu_ref):
    z = _gelu(y + d_ref[...] * u)
    return z * _sigmoid(_dot(z.astype(BF16), wglu_ref[...]))


def _s5_prompt_body(step, u_ref, are_ref, aim_ref, ldt_ref, bre_ref, bim_ref, cre_ref, cim_ref,
                    d_ref, wglu_ref, y_ref, sre_ref, sim_ref, xr_s, xi_s, y_s, car_re, car_im,
                    f_s, tab_s, *, nt):
    @pl.when(step % nt == 0)
    def _():
        car_re[...] = jnp.zeros_like(car_re)
        car_im[...] = jnp.zeros_like(car_im)
        row = lax.broadcasted_iota(jnp.int32, (SUBLANE, S5_LANES), 0)
        abr, abi, fre, fim = _s5_disc(are_ref[...], aim_ref[...], ldt_ref[...])
        f_s[0:1, :] = fre
        f_s[1:2, :] = fim
        pr, pi = abr, abi
        for lvl, s in enumerate((1, 2, 4)):
            tab_s[2 * lvl] = jnp.where(row >= s, pr, 0.0)
            tab_s[2 * lvl + 1] = jnp.where(row >= s, pi, 0.0)
            pr, pi = pr * pr - pi * pi, 2.0 * pr * pi
        cwr = jnp.zeros((SUBLANE, S5_LANES), F32)
        cwi = jnp.zeros((SUBLANE, S5_LANES), F32)
        pr, pi = abr, abi
        for r in range(SUBLANE):
            cwr = jnp.where(row == r, pr, cwr)
            cwi = jnp.where(row == r, pi, cwi)
            pr, pi = pr * abr - pi * abi, pr * abi + pi * abr
        tab_s[6] = cwr
        tab_s[7] = cwi

    tt = u_ref.shape[0]
    u = u_ref[...]
    ub = u.astype(BF16)
    for j in range(S5_NCHUNK):
        sl = slice(S5_CHUNK * j, S5_CHUNK * (j + 1))
        fre, fim = f_s[0:1, sl], f_s[1:2, sl]
        uj = ub[:, LANE * j:LANE * (j + 1)]
        bur = _dot(uj, bre_ref[j])
        bui = _dot(uj, bim_ref[j])
        xr = (fre * bur - fim * bui).reshape(tt // SUBLANE, SUBLANE, S5_CHUNK)
        xi = (fre * bui + fim * bur).reshape(tt // SUBLANE, SUBLANE, S5_CHUNK)
        for lvl, s in enumerate((1, 2, 4)):
            mr = tab_s[2 * lvl, :, sl]
            mi = tab_s[2 * lvl + 1, :, sl]
            sr = pltpu.roll(xr, s, axis=1)
            si = pltpu.roll(xi, s, axis=1)
            xr, xi = xr + mr * sr - mi * si, xi + mr * si + mi * sr
        xr_s[...] = xr.reshape(tt, S5_CHUNK)
        xi_s[...] = xi.reshape(tt, S5_CHUNK)
        cwr = tab_s[6, :, sl]
        cwi = tab_s[7, :, sl]

        def body(g, carry, cwr=cwr, cwi=cwi):
            cr, ci = carry
            r0 = pl.multiple_of(g * SUBLANE, SUBLANE)
            gr = xr_s[pl.ds(r0, SUBLANE), :] + cwr * cr - cwi * ci
            gi = xi_s[pl.ds(r0, SUBLANE), :] + cwr * ci + cwi * cr
            xr_s[pl.ds(r0, SUBLANE), :] = gr
            xi_s[pl.ds(r0, SUBLANE), :] = gi
            return gr[SUBLANE - 1:SUBLANE, :], gi[SUBLANE - 1:SUBLANE, :]

        cr, ci = lax.fori_loop(0, tt // SUBLANE, body, (car_re[:, sl], car_im[:, sl]), unroll=4)
        car_re[:, sl] = cr
        car_im[:, sl] = ci
        y_s[:, LANE * j:LANE * (j + 1)] = (_dot(xr_s[...].astype(BF16), cre_ref[j])
                                           - _dot(xi_s[...].astype(BF16), cim_ref[j]))
    y_ref[...] = _s5_glu(y_s[...], u, d_ref, wglu_ref).astype(BF16)
    sre_ref[0] = car_re[...]
    sim_ref[0] = car_im[...]


def _run_parts(name, parts):
    n = parts[0]["n"]
    assert all(p["n"] == n for p in parts)
    n_in = [len(p["args"]) for p in parts]
    n_out = [len(p["out_shape"]) for p in parts]
    n_scr = [len(p["scratch"]) for p in parts]

    def kern(*refs):
        step = pl.program_id(0)
        ins, outs, scr = refs[:sum(n_in)], refs[sum(n_in):sum(n_in) + sum(n_out)], refs[sum(n_in) + sum(n_out):]
        a = b = c = 0
        for p, na, nb, nc in zip(parts, n_in, n_out, n_scr):
            p["body"](step, *ins[a:a + na], *outs[b:b + nb], *scr[c:c + nc])
            a, b, c = a + na, b + nb, c + nc

    flat = lambda key: [x for p in parts for x in p[key]]
    outs = pl.pallas_call(
        kern, grid=(n,), in_specs=flat("in_specs"), out_specs=flat("out_specs"), out_shape=flat("out_shape"),
        scratch_shapes=flat("scratch"), compiler_params=_cparams(("arbitrary",)), name=name,
    )(*flat("args"))
    res, b = [], 0
    for nb in n_out:
        res.append(list(outs[b:b + nb]))
        b += nb
    return res


def _s5_prompt_part(proj, bsz, seq, sp):
    tt = _pick(seq, (S5_TILE, 128, 64, 32, 16, 8))
    nt = seq // tt
    const2 = lambda i: (0, 0)
    const3 = lambda i: (0, 0, 0)
    row_spec = pl.BlockSpec((1, S5_LANES), const2)
    return dict(
        n=bsz * nt,
        body=functools.partial(_s5_prompt_body, nt=nt),
        in_specs=[pl.BlockSpec((tt, D_A), lambda i: (i, COL_U)),
                  row_spec, row_spec, row_spec,
                  pl.BlockSpec((S5_NCHUNK, LANE, S5_CHUNK), const3),
                  pl.BlockSpec((S5_NCHUNK, LANE, S5_CHUNK), const3),
                  pl.BlockSpec((S5_NCHUNK, S5_CHUNK, LANE), const3),
                  pl.BlockSpec((S5_NCHUNK, S5_CHUNK, LANE), const3),
                  pl.BlockSpec((1, D_A), const2),
                  pl.BlockSpec((D_A, D_A), const2)],
        args=[proj, sp["are"], sp["aim"], sp["ldt"], sp["bre"], sp["bim"], sp["cre"], sp["cim"],
              sp["d"], sp["wglu"]],
        out_specs=[pl.BlockSpec((tt, D_A), lambda i: (i, 0)),
                   pl.BlockSpec((1, 1, S5_LANES), lambda i: (i // nt, 0, 0)),
                   pl.BlockSpec((1, 1, S5_LANES), lambda i: (i // nt, 0, 0))],
        out_shape=[jax.ShapeDtypeStruct((bsz * seq, D_A), BF16),
                   jax.ShapeDtypeStruct((bsz, 1, S5_LANES), F32),
                   jax.ShapeDtypeStruct((bsz, 1, S5_LANES), F32)],
        scratch=[pltpu.VMEM((tt, S5_CHUNK), F32), pltpu.VMEM((tt, S5_CHUNK), F32),
                 pltpu.VMEM((tt, D_A), F32),
                 pltpu.VMEM((1, S5_LANES), F32), pltpu.VMEM((1, S5_LANES), F32),
                 pltpu.VMEM((2, S5_LANES), F32), pltpu.VMEM((8, SUBLANE, S5_LANES), F32)])


def _s5_prompt(proj, bsz, seq, sp):
    return _run_parts("s5_prompt", [_s5_prompt_part(proj, bsz, seq, sp)])[0]


def _mlstm_prompt_kernel(q_ref, k_ref, v0_ref, v1_ref, o0_ref, o1_ref, sm_ref, bias_ref, nw_ref,
                         y_ref, c_ref, n_ref, m_ref, *, chunk):
    @pl.when(pl.program_id(1) == 0)
    def _():
        c_ref[...] = jnp.zeros_like(c_ref)
        n_ref[...] = jnp.zeros_like(n_ref)
        m_ref[...] = jnp.zeros_like(m_ref)

    tt = q_ref.shape[0]
    c = chunk
    nchunk = tt // c
    smb = sm_ref[...] + bias_ref[...]
    lf_all = _log_sigmoid(smb)
    li_t = smb.T
    lf_t = lf_all.T
    rowi = lax.broadcasted_iota(jnp.int32, (c, c), 0)
    coli = lax.broadcasted_iota(jnp.int32, (c, c), 1)
    causal = rowi >= coli
    lane = lax.broadcasted_iota(jnp.int32, (1, LANE), 1)
    m_row = m_ref[0]
    probs = [(ci, h) for ci in range(nchunk) for h in range(H_B)]
    idx = {p: i for i, p in enumerate(probs)}
    rows = lambda ci: slice(c * ci, c * (ci + 1))
    vo_refs = [(v0_ref, o0_ref) if h < 2 else (v1_ref, o1_ref) for h in range(H_B)]
    q = [q_ref[rows(ci), DQK_B * h:DQK_B * (h + 1)] * (DQK_B ** -0.5) for ci, h in probs]
    k = [k_ref[rows(ci), DQK_B * h:DQK_B * (h + 1)] for ci, h in probs]
    qb = [x.astype(BF16) for x in q]
    kb = [x.astype(BF16) for x in k]
    vb = [vo_refs[h][0][rows(ci), DV_B * (h % 2):DV_B * (h % 2 + 1)].astype(BF16) for ci, h in probs]
    qkt = [lax.dot_general(a, b, NT, preferred_element_type=F32) for a, b in zip(qb, kb)]
    li_c = [smb[rows(ci), SM_I + h:SM_I + h + 1] for ci, h in probs]
    li_r = [li_t[SM_I + h:SM_I + h + 1, rows(ci)] for ci, h in probs]
    bc_c = [jnp.sum(jnp.where(causal, lf_t[SM_F + h:SM_F + h + 1, rows(ci)], 0.0), axis=1, keepdims=True)
            for ci, h in probs]
    bc_r = [jnp.sum(jnp.where(rowi <= coli, lf_all[rows(ci), SM_F + h:SM_F + h + 1], 0.0), axis=0, keepdims=True)
            for ci, h in probs]
    dmat = [jnp.where(causal, a - b + r, -jnp.inf) for a, b, r in zip(bc_c, bc_r, li_r)]
    dmax = [jnp.max(x, axis=1, keepdims=True) for x in dmat]
    b_last = [x[c - 1:c, :] for x in bc_c]
    expo = [bl - a + l for bl, a, l in zip(b_last, bc_c, li_c)]
    emax = [jnp.max(x, axis=0, keepdims=True) for x in expo]
    m_prev, m_new = [None] * len(probs), [None] * len(probs)
    for h in range(H_B):
        m = m_row[:, h:h + 1]
        for ci in range(nchunk):
            i = idx[ci, h]
            m_prev[i] = m
            m = jnp.maximum(b_last[i] + m, emax[i])
            m_new[i] = m
    inter = [a + m for a, m in zip(bc_c, m_prev)]
    m_t = [jnp.maximum(a, b) for a, b in zip(inter, dmax)]
    w_inter = [jnp.exp(a - b) for a, b in zip(inter, m_t)]
    s = [x * jnp.exp(d - m) for x, d, m in zip(qkt, dmat, m_t)]
    sv = [_dot(x.astype(BF16), v) for x, v in zip(s, vb)]
    ssum = [jnp.sum(x, axis=1, keepdims=True) for x in s]
    emt = [jnp.exp(-x) for x in m_t]
    decay = [jnp.exp(bl + mp - mn) for bl, mp, mn in zip(b_last, m_prev, m_new)]
    kw = [jnp.exp(e - mn) * kx for e, mn, kx in zip(expo, m_new, k)]
    kv = [lax.dot_general(x.astype(BF16), v, TN, preferred_element_type=F32) for x, v in zip(kw, vb)]
    ksum = [jnp.sum(x, axis=0, keepdims=True) for x in kw]

    cst = [c_ref[0, h] for h in range(H_B)]
    nrow = [n_ref[0, h:h + 1, :] for h in range(H_B)]
    for ci in range(nchunk):
        ids = [idx[ci, h] for h in range(H_B)]
        qc = [_dot(qb[i], cst[h].astype(BF16)) for h, i in enumerate(ids)]
        num = [sv[i] + w_inter[i] * qc[h] for h, i in enumerate(ids)]
        nq = [ssum[i] + w_inter[i] * jnp.sum(q[i] * nrow[h], axis=1, keepdims=True) for h, i in enumerate(ids)]
        hh = [num[h] / jnp.maximum(jnp.abs(nq[h]), emt[i]) for h, i in enumerate(ids)]
        cst = [decay[i] * cst[h] + kv[i] for h, i in enumerate(ids)]
        nrow = [decay[i] * nrow[h] + ksum[i] for h, i in enumerate(ids)]
        for h in range(H_B):
            og = vo_refs[h][1][rows(ci), DV_B * (h % 2):DV_B * (h % 2 + 1)]
            y_ref[rows(ci), DV_B * h:DV_B * (h + 1)] = (_rms(hh[h], nw_ref[h:h + 1, :]) * _sigmoid(og)).astype(BF16)
    m_out = m_row
    for h in range(H_B):
        c_ref[0, h] = cst[h]
        n_ref[0, h:h + 1, :] = nrow[h]
        m_out = jnp.where(lane == h, m_new[idx[nchunk - 1, h]], m_out)
    m_ref[0] = m_out


def _mlstm_prompt(proj, bsz, seq, mp):
    chunk = _pick(seq, (MLSTM_CHUNK,))
    c = _pick(seq, (MLSTM_TILE, MLSTM_CHUNK))
    nt = seq // c
    rows = lambda col: (lambda b, t: (b * nt + t, col))
    const2 = lambda b, t: (0, 0)
    return pl.pallas_call(
        functools.partial(_mlstm_prompt_kernel, chunk=chunk),
        grid=(bsz, nt),
        in_specs=[pl.BlockSpec((c, 512), rows(COL_QB // 4)),
                  pl.BlockSpec((c, 512), rows(COL_KB // 4)),
                  pl.BlockSpec((c, 512), rows(COL_VB // 4)),
                  pl.BlockSpec((c, 512), rows(COL_VB // 4 + 1)),
                  pl.BlockSpec((c, 512), rows(COL_OB // 4)),
                  pl.BlockSpec((c, 512), rows(COL_OB // 4 + 1)),
                  pl.BlockSpec((c, LANE), rows(COL_SMALL)),
                  pl.BlockSpec((1, LANE), const2),
                  pl.BlockSpec((H_B, DV_B), const2)],
        out_specs=[pl.BlockSpec((c, D_B), lambda b, t: (b * nt + t, 0)),
                   pl.BlockSpec((1, H_B, DQK_B, DV_B), lambda b, t: (b, 0, 0, 0)),
                   pl.BlockSpec((1, H_B, DQK_B), lambda b, t: (b, 0, 0)),
                   pl.BlockSpec((1, 1, LANE), lambda b, t: (b, 0, 0))],
        out_shape=[jax.ShapeDtypeStruct((bsz * seq, D_B), BF16),
                   jax.ShapeDtypeStruct((bsz, H_B, DQK_B, DV_B), F32),
                   jax.ShapeDtypeStruct((bsz, H_B, DQK_B), F32),
                   jax.ShapeDtypeStruct((bsz, 1, LANE), F32)],
        compiler_params=_cparams(("arbitrary", "arbitrary")),
        name="mlstm_prompt",
    )(proj, proj, proj, proj, proj, proj, proj, mp["bias"], mp["norm"])


def _split_bf16(a):
    hi = a.astype(BF16)
    return hi, (a - hi.astype(F32)).astype(BF16)


def _dot_x3(a, b):
    ah, al = _split_bf16(a)
    bh, bl = _split_bf16(b)
    return _dot(ah, bh) + _dot(ah, bl) + _dot(al, bh)


def _unit_lower_inverses(lmats):
    c = lmats[0].shape[0]
    eye = (lax.broadcasted_iota(jnp.int32, (c, c), 0) == lax.broadcasted_iota(jnp.int32, (c, c), 1)).astype(F32)
    hi_half = lax.broadcasted_iota(jnp.int32, (c, 2 * c), 1) >= c
    ms = [jnp.concatenate([-l, eye], axis=1) for l in lmats]
    span = 1
    while span < c:
        split = [_split_bf16(m) for m in ms]
        ms = [_dot(mh[:, 0:c], mh) + _dot(mh[:, 0:c], ml) + _dot(ml[:, 0:c], mh) + jnp.where(hi_half, m, 0.0)
              for m, (mh, ml) in zip(ms, split)]
        span *= 2
    return [m[:, c:2 * c] for m in ms]


def _gdn_prompt_kernel(q_ref, k_ref, v_ref, z_ref, sm_ref, bias_ref, alog_ref, cw_ref, nw_ref,
                       y_ref, s_ref, xb_s, *, chunk):
    tt = q_ref.shape[0]
    c = chunk

    @pl.when(pl.program_id(1) == 0)
    def _():
        s_ref[...] = jnp.zeros_like(s_ref)
        xb_s[0:SUBLANE, :] = jnp.zeros((SUBLANE, 3 * D_C), F32)

    xb_s[SUBLANE:SUBLANE + tt, 0:D_C] = q_ref[...]
    xb_s[SUBLANE:SUBLANE + tt, D_C:2 * D_C] = k_ref[...]
    xb_s[SUBLANE:SUBLANE + tt, 2 * D_C:3 * D_C] = v_ref[...]
    conv = cw_ref[GDN_CONV - 1:GDN_CONV, :] * xb_s[SUBLANE:SUBLANE + tt, :]
    for j in range(GDN_CONV - 1):
        off = SUBLANE - (GDN_CONV - 1) + j
        conv = conv + cw_ref[j:j + 1, :] * xb_s[off:off + tt, :]
    xb_s[0:SUBLANE, :] = xb_s[tt:tt + SUBLANE, :]
    qkv = _silu(conv)

    sm = sm_ref[...]
    beta_all = _sigmoid(sm)
    g_all = -jnp.exp(alog_ref[...]) * _softplus(sm + bias_ref[...])
    g_t = g_all.T
    rowi = lax.broadcasted_iota(jnp.int32, (c, c), 0)
    coli = lax.broadcasted_iota(jnp.int32, (c, c), 1)
    incl = rowi >= coli
    strict = rowi > coli

    nchunk = tt // c
    probs = [(ci, h) for ci in range(nchunk) for h in range(H_C)]
    rows = lambda ci: slice(c * ci, c * (ci + 1))
    l2 = lambda x: x * lax.rsqrt(jnp.sum(x * x, axis=-1, keepdims=True) + EPS)
    q = [l2(qkv[rows(ci), DK_C * h:DK_C * (h + 1)]) * (DK_C ** -0.5) for ci, h in probs]
    k = [l2(qkv[rows(ci), D_C + DK_C * h:D_C + DK_C * (h + 1)]) for ci, h in probs]
    v = [qkv[rows(ci), 2 * D_C + DV_C * h:2 * D_C + DV_C * (h + 1)] for ci, h in probs]
    beta_c = [beta_all[rows(ci), SM_BETA + h:SM_BETA + h + 1] for ci, h in probs]
    gc_c = [jnp.sum(jnp.where(incl, g_t[SM_A + h:SM_A + h + 1, rows(ci)], 0.0), axis=1, keepdims=True)
            for ci, h in probs]
    gc_r = [jnp.sum(jnp.where(rowi <= coli, g_all[rows(ci), SM_A + h:SM_A + h + 1], 0.0), axis=0, keepdims=True)
            for ci, h in probs]
    gam = [jnp.exp(jnp.where(incl, a - b, -jnp.inf)) for a, b in zip(gc_c, gc_r)]
    qb = [x.astype(BF16) for x in q]
    kb = [x.astype(BF16) for x in k]
    kk = [lax.dot_general(x, x, NT, preferred_element_type=F32) for x in kb]
    qk = [(lax.dot_general(a, b, NT, preferred_element_type=F32) * g).astype(BF16) for a, b, g in zip(qb, kb, gam)]
    egc = [jnp.exp(x) for x in gc_c]
    tinv = _unit_lower_inverses([jnp.where(strict, b * g * x, 0.0) for b, g, x in zip(beta_c, gam, kk)])
    tr = [_dot_x3(t, jnp.concatenate([b * vv, (b * e) * kx], axis=1))
          for t, b, vv, e, kx in zip(tinv, beta_c, v, egc, k)]
    u0 = [x[:, 0:DV_C] for x in tr]
    wb = [x[:, DV_C:].astype(BF16) for x in tr]
    g_last = [x[c - 1:c, :] for x in gc_c]
    kw = [(jnp.exp(gl - gc) * kx).astype(BF16) for gl, gc, kx in zip(g_last, gc_c, k)]
    e_last = [jnp.exp(x) for x in g_last]

    st = [s_ref[0, h] for h in range(H_C)]
    for ci in range(nchunk):
        ids = [ci * H_C + h for h in range(H_C)]
        stb = [x.astype(BF16) for x in st]
        ub = [(u0[i] - _dot(wb[i], stb[h])).astype(BF16) for h, i in enumerate(ids)]
        qs = [_dot(qb[i], stb[h]) for h, i in enumerate(ids)]
        st = [e_last[i] * st[h] + lax.dot_general(kw[i], ub[h], TN, preferred_element_type=F32)
              for h, i in enumerate(ids)]
        o = [egc[i] * qs[h] + _dot(qk[i], ub[h]) for h, i in enumerate(ids)]
        for h in range(H_C):
            zz = z_ref[rows(ci), DV_C * h:DV_C * (h + 1)]
            y_ref[rows(ci), DV_C * h:DV_C * (h + 1)] = (_rms(o[h], nw_ref[...]) * _silu(zz)).astype(BF16)
    for h in range(H_C):
        s_ref[0, h] = st[h]


def _gdn_prompt(proj, bsz, seq, gp):
    c = _pick(seq, (GDN_CHUNK,))
    tt = _pick(seq, (GDN_TILE,))
    nt = seq // tt
    rows = lambda col: (lambda b, t: (b * nt + t, col))
    const2 = lambda b, t: (0, 0)
    return pl.pallas_call(
        functools.partial(_gdn_prompt_kernel, chunk=c),
        grid=(bsz, nt),
        in_specs=[pl.BlockSpec((tt, D_C), rows(COL_QKV // 4)),
                  pl.BlockSpec((tt, D_C), rows(COL_QKV // 4 + 1)),
                  pl.BlockSpec((tt, D_C), rows(COL_QKV // 4 + 2)),
                  pl.BlockSpec((tt, D_C), rows(COL_Z // 4)),
                  pl.BlockSpec((tt, LANE), rows(COL_SMALL)),
                  pl.BlockSpec((1, LANE), const2),
                  pl.BlockSpec((1, LANE), const2),
                  pl.BlockSpec((GDN_CONV, 3 * D_C), const2),
                  pl.BlockSpec((1, DV_C), const2)],
        out_specs=[pl.BlockSpec((tt, D_C), lambda b, t: (b * nt + t, 0)),
                   pl.BlockSpec((1, H_C, DK_C, DV_C), lambda b, t: (b, 0, 0, 0))],
        out_shape=[jax.ShapeDtypeStruct((bsz * seq, D_C), BF16),
                   jax.ShapeDtypeStruct((bsz, H_C, DK_C, DV_C), F32)],
        scratch_shapes=[pltpu.VMEM((SUBLANE + tt, 3 * D_C), F32)],
        compiler_params=_cparams(("arbitrary", "arbitrary")),
        name="gdn_prompt",
    )(proj, proj, proj, proj, proj, gp["bias"], gp["alog"], gp["conv_w"], gp["norm"])


def _to_col(row, eye):
    return jnp.sum(jnp.where(eye, row, 0.0), axis=1, keepdims=True)


N_SAMPLE_INPUTS = 23


def _sample_mixers_kernel(*refs, fill_layer):
    (proj_ref, sre_ref, sim_ref, c_ref, n_ref, m_ref, gs_ref, gbuf_ref,
     are_ref, aim_ref, ldt_ref, bre_ref, bim_ref, cre_ref, cim_ref, d_ref, wglu_ref,
     mbias_ref, mnorm_ref, gbias_ref, alog_ref, gcw_ref, gnorm_ref) = refs[:N_SAMPLE_INPUTS]
    (y_ref, sre_o, sim_o, c_o, n_o, m_o, gs_o, gbuf_o,
     qkv_s, qn_s, kn_s, beta_s, g_s, li_s, lf_s) = refs[-15:]
    bb = proj_ref.shape[0]
    if fill_layer is not None:
        for l in range(c_o.shape[0]):
            if l != fill_layer:
                c_o[l] = jnp.zeros(c_o.shape[1:], F32)
                gs_o[l] = jnp.zeros(gs_o.shape[1:], F32)
        c_o = c_o.at[fill_layer]
        gs_o = gs_o.at[fill_layer]

    u = proj_ref[:, COL_U * LANE:COL_U * LANE + D_A]
    ub = u.astype(BF16)
    ys = []
    for j in range(S5_NCHUNK):
        sl = slice(S5_CHUNK * j, S5_CHUNK * (j + 1))
        abr, abi, fre, fim = _s5_disc(are_ref[:, sl], aim_ref[:, sl], ldt_ref[:, sl])
        uj = ub[:, LANE * j:LANE * (j + 1)]
        bur = _dot(uj, bre_ref[j])
        bui = _dot(uj, bim_ref[j])
        s0r = sre_ref[:, sl]
        s0i = sim_ref[:, sl]
        xr = fre * bur - fim * bui + abr * s0r - abi * s0i
        xi = fre * bui + fim * bur + abr * s0i + abi * s0r
        sre_o[:, sl] = xr
        sim_o[:, sl] = xi
        ys.append(_dot(xr.astype(BF16), cre_ref[j]) - _dot(xi.astype(BF16), cim_ref[j]))
    y_a = jnp.concatenate(ys, axis=1)
    y_ref[:, Y_A:Y_A + D_A] = _s5_glu(y_a, u, d_ref, wglu_ref)

    sm = proj_ref[:, COL_SMALL * LANE:(COL_SMALL + 1) * LANE]
    smb = sm + mbias_ref[...]
    li_s[...] = smb
    lf_s[...] = _log_sigmoid(smb)
    beta_s[...] = _sigmoid(sm)
    g_s[...] = -jnp.exp(alog_ref[...]) * _softplus(sm + gbias_ref[...])

    xnew = proj_ref[:, COL_QKV * LANE:COL_QKV * LANE + 3 * D_C]
    conv = gcw_ref[GDN_CONV - 1:GDN_CONV, :] * xnew
    for j in range(GDN_CONV - 1):
        conv = conv + gcw_ref[j:j + 1, :] * gbuf_ref[j]
        if j > 0:
            gbuf_o[j - 1] = gbuf_ref[j]
    gbuf_o[GDN_CONV - 2] = xnew
    qkv = _silu(conv)
    qkv_s[...] = qkv
    for h in range(H_C):
        q = qkv[:, DK_C * h:DK_C * (h + 1)]
        k = qkv[:, D_C + DK_C * h:D_C + DK_C * (h + 1)]
        qn_s[:, DK_C * h:DK_C * (h + 1)] = q * lax.rsqrt(jnp.sum(q * q, axis=-1, keepdims=True) + EPS) * (DK_C ** -0.5)
        kn_s[:, DK_C * h:DK_C * (h + 1)] = k * lax.rsqrt(jnp.sum(k * k, axis=-1, keepdims=True) + EPS)

    assert 4 * H_B * bb == LANE and DQK_B == LANE and DK_C == LANE and H_B == H_C
    tiles = ([proj_ref[:, COL_QB * LANE + DQK_B * h:COL_QB * LANE + DQK_B * (h + 1)] * (DQK_B ** -0.5)
              for h in range(H_B)]
             + [proj_ref[:, COL_KB * LANE + DQK_B * h:COL_KB * LANE + DQK_B * (h + 1)] for h in range(H_B)]
             + [qn_s[:, DK_C * h:DK_C * (h + 1)] for h in range(H_C)]
             + [kn_s[:, DK_C * h:DK_C * (h + 1)] for h in range(H_C)])
    stacked_t = jnp.concatenate(tiles, axis=0).T

    def col_of(kind, b, h):
        j = (kind * H_B + h) * bb + b
        return stacked_t[:, j:j + 1]

    lane = lax.broadcasted_iota(jnp.int32, (1, H_B), 1)

    rsl = lambda b: slice(b, b + 1)
    for g0 in range(0, bb, SAMPLE_GROUP):
        probs = [(b, h) for b in range(g0, g0 + SAMPLE_GROUP) for h in range(H_B)]

        q = [proj_ref[rsl(b), COL_QB * LANE + DQK_B * h:COL_QB * LANE + DQK_B * (h + 1)] * (DQK_B ** -0.5)
             for b, h in probs]
        k = [proj_ref[rsl(b), COL_KB * LANE + DQK_B * h:COL_KB * LANE + DQK_B * (h + 1)] for b, h in probs]
        v = [proj_ref[rsl(b), COL_VB * LANE + DV_B * h:COL_VB * LANE + DV_B * (h + 1)] for b, h in probs]
        li = [li_s[rsl(b), SM_I + h:SM_I + h + 1] for b, h in probs]
        inter = [lf_s[rsl(b), SM_F + h:SM_F + h + 1] + m_ref[rsl(b), h:h + 1] for b, h in probs]
        m_t = [jnp.maximum(a, c) for a, c in zip(inter, li)]
        w_intra = [jnp.exp(a - c) for a, c in zip(li, m_t)]
        w_inter = [jnp.exp(a - c) for a, c in zip(inter, m_t)]
        qcol = [col_of(0, b, h) for b, h in probs]
        kcol = [col_of(1, b, h) for b, h in probs]
        s = [jnp.sum(a * c, axis=1, keepdims=True) * w for a, c, w in zip(q, k, w_intra)]
        cst = [c_ref[b, h] for b, h in probs]
        nrow = [n_ref[b, h:h + 1, :] for b, h in probs]
        qc = [jnp.sum(a * c, axis=0, keepdims=True) for a, c in zip(qcol, cst)]
        for i, (b, h) in enumerate(probs):
            c_o[b, h] = w_inter[i] * cst[i] + (w_intra[i] * kcol[i]) * v[i]
            n_o[b, h:h + 1, :] = w_inter[i] * nrow[i] + w_intra[i] * k[i]
        num = [s[i] * v[i] + w_inter[i] * qc[i] for i in range(len(probs))]
        nq = [s[i] + w_inter[i] * jnp.sum(q[i] * nrow[i], axis=1, keepdims=True) for i in range(len(probs))]
        hh = [num[i] / jnp.maximum(jnp.abs(nq[i]), jnp.exp(-m_t[i])) for i in range(len(probs))]
        for i, (b, h) in enumerate(probs):
            og = proj_ref[rsl(b), COL_OB * LANE + DV_B * h:COL_OB * LANE + DV_B * (h + 1)]
            y_ref[rsl(b), Y_B + DV_B * h:Y_B + DV_B * (h + 1)] = _rms(hh[i], mnorm_ref[h:h + 1, :]) * _sigmoid(og)
        for b in range(g0, g0 + SAMPLE_GROUP):
            m_out = m_ref[rsl(b), :]
            for h in range(H_B):
                m_out = jnp.where(lane == h, m_t[(b - g0) * H_B + h], m_out)
            m_o[rsl(b), :] = m_out

        q = [qn_s[rsl(b), DK_C * h:DK_C * (h + 1)] for b, h in probs]
        k = [kn_s[rsl(b), DK_C * h:DK_C * (h + 1)] for b, h in probs]
        v = [qkv_s[rsl(b), 2 * D_C + DV_C * h:2 * D_C + DV_C * (h + 1)] for b, h in probs]
        beta = [beta_s[rsl(b), SM_BETA + h:SM_BETA + h + 1] for b, h in probs]
        eg = [jnp.exp(g_s[rsl(b), SM_A + h:SM_A + h + 1]) for b, h in probs]
        qcol = [col_of(2, b, h) for b, h in probs]
        kcol = [col_of(3, b, h) for b, h in probs]
        st = [gs_ref[b, h] for b, h in probs]
        ks = [jnp.sum(a * c, axis=0, keepdims=True) for a, c in zip(kcol, st)]
        qs = [jnp.sum(a * c, axis=0, keepdims=True) for a, c in zip(qcol, st)]
        un = [beta[i] * (v[i] - eg[i] * ks[i]) for i in range(len(probs))]
        for i, (b, h) in enumerate(probs):
            gs_o[b, h] = eg[i] * st[i] + kcol[i] * un[i]
        o = [eg[i] * qs[i] + jnp.sum(q[i] * k[i], axis=1, keepdims=True) * un[i] for i in range(len(probs))]
        for i, (b, h) in enumerate(probs):
            zz = proj_ref[rsl(b), COL_Z * LANE + DV_C * h:COL_Z * LANE + DV_C * (h + 1)]
            y_ref[rsl(b), Y_C + DV_C * h:Y_C + DV_C * (h + 1)] = _rms(o[i], gnorm_ref[...]) * _silu(zz)


def _sample_mixers(proj, row0, nrows, st, sp, mp, gp, layer, prev):
    bb = SAMPLE_BLOCK
    blk0 = row0 // bb
    depth = st["c"].shape[0]
    const2 = lambda i: (0, 0)
    const3 = lambda i: (0, 0, 0)
    row_spec = pl.BlockSpec((1, S5_LANES), const2)
    c_spec = _layered(layer, (bb, H_B, DQK_B, DV_B), lambda i: (i, 0, 0, 0))
    gs_spec = _layered(layer, (bb, H_C, DK_C, DV_C), lambda i: (i, 0, 0, 0))
    in_specs = [
        pl.BlockSpec((bb, N_PROJ), lambda i: (blk0 + i, 0)),
        pl.BlockSpec((bb, S5_LANES), lambda i: (i, 0)),
        pl.BlockSpec((bb, S5_LANES), lambda i: (i, 0)),
        c_spec,
        pl.BlockSpec((bb, H_B, DQK_B), lambda i: (i, 0, 0)),
        pl.BlockSpec((bb, H_B), lambda i: (i, 0)),
        gs_spec,
        pl.BlockSpec((GDN_CONV - 1, bb, 3 * D_C), lambda i: (0, i, 0)),
        row_spec, row_spec, row_spec,
        pl.BlockSpec((S5_NCHUNK, LANE, S5_CHUNK), const3),
        pl.BlockSpec((S5_NCHUNK, LANE, S5_CHUNK), const3),
        pl.BlockSpec((S5_NCHUNK, S5_CHUNK, LANE), const3),
        pl.BlockSpec((S5_NCHUNK, S5_CHUNK, LANE), const3),
        pl.BlockSpec((1, D_A), const2),
        pl.BlockSpec((D_A, D_A), const2),
        pl.BlockSpec((1, LANE), const2),
        pl.BlockSpec((H_B, DV_B), const2),
        pl.BlockSpec((1, LANE), const2),
        pl.BlockSpec((1, LANE), const2),
        pl.BlockSpec((GDN_CONV, 3 * D_C), const2),
        pl.BlockSpec((1, DV_C), const2),
    ]
    out_specs = [
        pl.BlockSpec((bb, D_MODEL), lambda i: (i, 0)),
        pl.BlockSpec((bb, S5_LANES), lambda i: (i, 0)),
        pl.BlockSpec((bb, S5_LANES), lambda i: (i, 0)),
        c_spec if prev is not None else pl.BlockSpec((depth, bb, H_B, DQK_B, DV_B), lambda i: (0, i, 0, 0, 0)),
        pl.BlockSpec((bb, H_B, DQK_B), lambda i: (i, 0, 0)),
        pl.BlockSpec((bb, H_B), lambda i: (i, 0)),
        gs_spec if prev is not None else pl.BlockSpec((depth, bb, H_C, DK_C, DV_C), lambda i: (0, i, 0, 0, 0)),
        pl.BlockSpec((GDN_CONV - 1, bb, 3 * D_C), lambda i: (0, i, 0)),
    ]
    out_shape = [
        jax.ShapeDtypeStruct((nrows, D_MODEL), F32),
        jax.ShapeDtypeStruct((nrows, S5_LANES), F32),
        jax.ShapeDtypeStruct((nrows, S5_LANES), F32),
        jax.ShapeDtypeStruct((depth, nrows, H_B, DQK_B, DV_B), F32),
        jax.ShapeDtypeStruct((nrows, H_B, DQK_B), F32),
        jax.ShapeDtypeStruct((nrows, H_B), F32),
        jax.ShapeDtypeStruct((depth, nrows, H_C, DK_C, DV_C), F32),
        jax.ShapeDtypeStruct((GDN_CONV - 1, nrows, 3 * D_C), F32),
    ]
    scratch = [pltpu.VMEM((bb, 3 * D_C), F32), pltpu.VMEM((bb, D_C), F32), pltpu.VMEM((bb, D_C), F32),
               pltpu.VMEM((bb, LANE), F32), pltpu.VMEM((bb, LANE), F32),
               pltpu.VMEM((bb, LANE), F32), pltpu.VMEM((bb, LANE), F32)]
    args = [proj, st["sre"], st["sim"], st["c"], st["n"], st["m"], st["gs"], st["gbuf"],
            sp["are"], sp["aim"], sp["ldt"], sp["bre"], sp["bim"], sp["cre"], sp["cim"], sp["d"], sp["wglu"],
            mp["bias"], mp["norm"], gp["bias"], gp["alog"], gp["conv_w"], gp["norm"]]
    assert len(args) == N_SAMPLE_INPUTS
    aliases = {}
    if prev is not None:
        in_specs += [pl.BlockSpec(memory_space=pl.ANY)] * 2
        args += list(prev)
        aliases = {N_SAMPLE_INPUTS: 3, N_SAMPLE_INPUTS + 1: 6}
    return pl.pallas_call(
        functools.partial(_sample_mixers_kernel, fill_layer=layer if prev is None else None),
        grid=(nrows // bb,),
        in_specs=in_specs, out_specs=out_specs, out_shape=out_shape, scratch_shapes=scratch,
        input_output_aliases=aliases,
        compiler_params=_cparams(("arbitrary",)),
        name="sample_mixers",
    )(*args)


def _merge_kernel(h_ref, ya_ref, yb_ref, yc_ref, wg0_ref, wg1_ref, wg2_ref, wa_ref, wb_ref, wc_ref, o_ref):
    h = h_ref[...]
    bf = lambda ref: ref[...].astype(BF16)
    acc = _sigmoid(_dot(h, bf(wg0_ref))) * _dot(bf(ya_ref), bf(wa_ref))
    acc = acc + _sigmoid(_dot(h, bf(wg1_ref))) * _dot(bf(yb_ref), bf(wb_ref))
    acc = acc + _sigmoid(_dot(h, bf(wg2_ref))) * _dot(bf(yc_ref), bf(wc_ref))
    o_ref[...] = acc.astype(BF16)


def _merge(h, ya, yb, yc, wg, wa, wb, wc, layer):
    m, d = h.shape
    tm = _pick(m, (1024, 512, 256, 128))
    tn = 256
    nb = d // tn
    lhs = lambda w, cb: pl.BlockSpec((tm, w), lambda i, j: (i, cb))
    gate = lambda g: _layered(layer, (d, tn), lambda i, j: (0, g * nb + j))
    rhs = lambda w: _layered(layer, (w, tn), lambda i, j: (0, j))
    return pl.pallas_call(
        _merge_kernel,
        grid=(m // tm, nb),
        in_specs=[lhs(d, 0), lhs(D_A, ya[1]), lhs(D_B, yb[1]), lhs(D_C, yc[1]), gate(0), gate(1), gate(2),
                  rhs(D_A), rhs(D_B), rhs(D_C)],
        out_specs=pl.BlockSpec((tm, tn), lambda i, j: (i, j)),
        out_shape=jax.ShapeDtypeStruct((m, d), BF16),
        compiler_params=_cparams(("arbitrary", "arbitrary")),
        name="merge",
    )(h, ya[0], yb[0], yc[0], wg, wg, wg, wa, wb, wc)


def _out_proj_kernel(a_ref, x_ref, w_ref, nw_ref, o_ref):
    o_ref[...] = x_ref[...] + _rms(_dot(a_ref[...], w_ref[...]), nw_ref[...])


def _out_proj(a, x, w, nw, layer):
    m, d = x.shape
    tm = _pick(m, (512, 256, 128))
    return pl.pallas_call(
        _out_proj_kernel,
        grid=(m // tm,),
        in_specs=[pl.BlockSpec((tm, d), lambda i: (i, 0)),
                  pl.BlockSpec((tm, d), lambda i: (i, 0)),
                  _layered(layer, (d, d), lambda i: (0, 0)),
                  _layered(layer, (1, d), lambda i: (0, 0))],
        out_specs=pl.BlockSpec((tm, d), lambda i: (i, 0)),
        out_shape=jax.ShapeDtypeStruct((m, d), F32),
        compiler_params=_cparams(("arbitrary",)),
        name="out_proj",
    )(a, x, w, nw)


def _ffn_body(step, *refs, sample, tiles_per_seq, ft):
    if sample:
        (x_ref, nw_ref, wg_ref, wu_ref, cw_ref, wd_ref, pnw_ref, b0_ref, b1_ref,
         o_ref, g_ref, wg_o, wu_o, wd_o, h2_s) = refs
    else:
        (x_ref, nw_ref, wg_ref, wu_ref, cw_ref, wd_ref, pnw_ref,
         o_ref, g_ref, h2_s, gb_s, carry_s) = refs
    i = step // ft
    j = step % ft
    tm = x_ref.shape[0]

    @pl.when(j == 0)
    def _():
        h2_s[...] = _rms(x_ref[...], nw_ref[...]).astype(BF16)
        o_ref[...] = jnp.zeros_like(o_ref)

    h2 = h2_s[...]
    wg, wu, wd = wg_ref[...], wu_ref[...], wd_ref[...]
    if sample:
        wg, wu, wd = wg.astype(BF16), wu.astype(BF16), wd.astype(BF16)
        wg_o[0] = wg
        wu_o[0] = wu
        wd_o[0] = wd
    g = _dot(h2, wg)
    up = _dot(h2, wu)
    if sample:
        a = cw_ref[0:1, :] * b0_ref[...] + cw_ref[1:2, :] * b1_ref[...] + cw_ref[2:3, :] * g
        g_ref[...] = g
    else:
        prev = jnp.where(i % tiles_per_seq == 0, 0.0, carry_s[j])
        gb_s[0:SUBLANE, :] = prev
        gb_s[SUBLANE:SUBLANE + tm, :] = g
        a = (cw_ref[0:1, :] * gb_s[SUBLANE - 2:SUBLANE - 2 + tm, :]
             + cw_ref[1:2, :] * gb_s[SUBLANE - 1:SUBLANE - 1 + tm, :]
             + cw_ref[2:3, :] * g)
        tail = g[tm - SUBLANE:tm, :]
        carry_s[j] = tail
        g_ref[0] = tail
    act = (_gelu(a) * up).astype(BF16)
    o_ref[...] += _dot(act, wd)

    @pl.when(j == ft - 1)
    def _():
        o_ref[...] = x_ref[...] + _rms(o_ref[...], pnw_ref[...])


def _ffn_part(x, row0, nrows, seq, nw, wg, wu, cw, wd, pnw, layer, bufs=None, steps=None):
    d = x.shape[1]
    f = wg.shape[-1]
    sample = bufs is not None
    tm = nrows if sample else _pick(seq, (512, 256, 128, 64, 32, 16, 8))
    tn = _pick(f, (512, 256, 128) if sample else (1024, 512, 256, 128))
    if sample and steps is not None and f % steps == 0 and (f // steps) % LANE == 0 and f // steps <= tn:
        tn = f // steps
    mt, ft = nrows // tm, f // tn
    blk0 = row0 // tm
    mi = lambda s: s // ft
    nj = lambda s: s % ft
    in_specs = [pl.BlockSpec((tm, d), lambda s: (blk0 + mi(s), 0)),
                _layered(layer, (1, d), lambda s: (0, 0)),
                _layered(layer, (d, tn), lambda s: (0, nj(s))),
                _layered(layer, (d, tn), lambda s: (0, nj(s))),
                _layered(layer, (FFN_CONV, tn), lambda s: (0, nj(s))),
                _layered(layer, (tn, d), lambda s: (nj(s), 0)),
                _layered(layer, (1, d), lambda s: (0, 0))]
    args = [x, nw, wg, wu, cw, wd, pnw]
    scratch = [pltpu.VMEM((tm, d), BF16)]
    if sample:
        in_specs += [pl.BlockSpec((tm, tn), lambda s: (mi(s), nj(s)))] * 2
        args += list(bufs)
        assert mt == 1, "the sample call must visit every weight tile exactly once"
        g_spec = pl.BlockSpec((tm, tn), lambda s: (mi(s), nj(s)))
        g_shape = jax.ShapeDtypeStruct((nrows, f), F32)
        extra_specs = [pl.BlockSpec((1, d, tn), lambda s: (0, 0, nj(s))),
                       pl.BlockSpec((1, d, tn), lambda s: (0, 0, nj(s))),
                       pl.BlockSpec((1, tn, d), lambda s: (0, nj(s), 0))]
        extra_shapes = [jax.ShapeDtypeStruct((1, d, f), BF16), jax.ShapeDtypeStruct((1, d, f), BF16),
                        jax.ShapeDtypeStruct((1, f, d), BF16)]
    else:
        extra_specs, extra_shapes = [], []
        scratch += [pltpu.VMEM((SUBLANE + tm, tn), F32), pltpu.VMEM((ft, SUBLANE, tn), F32)]
        g_spec = pl.BlockSpec((1, SUBLANE, tn), lambda s: (mi(s), 0, nj(s)))
        g_shape = jax.ShapeDtypeStruct((mt, SUBLANE, f), F32)
    return dict(
        n=mt * ft,
        body=functools.partial(_ffn_body, sample=sample, tiles_per_seq=max(seq // tm, 1), ft=ft),
        in_specs=in_specs, args=args,
        out_specs=[pl.BlockSpec((tm, d), lambda s: (mi(s), 0)), g_spec] + extra_specs,
        out_shape=[jax.ShapeDtypeStruct((nrows, d), F32), g_shape] + extra_shapes,
        scratch=scratch)


def _ffn(*args, **kwargs):
    part = _ffn_part(*args, **kwargs)
    return _run_parts("ffn_sample" if kwargs.get("bufs") is not None else "ffn_prompt", [part])[0]


def _ple_kernel(x_ref, p_ref, wg_ref, wp_ref, o_ref):
    x = x_ref[...]
    gate = _sigmoid(_dot(x.astype(BF16), wg_ref[...]))
    o_ref[...] = x + gate * _dot(p_ref[...].astype(BF16), wp_ref[...])


def _ple(x, p, wg, wp, layer):
    m, d = x.shape
    pd = p.shape[-1]
    tm = _pick(m, (512, 256, 128))
    return pl.pallas_call(
        _ple_kernel,
        grid=(m // tm,),
        in_specs=[pl.BlockSpec((tm, d), lambda i: (i, 0)),
                  _layered(layer, (tm, pd), lambda i: (i, 0)),
                  _layered(layer, (d, d), lambda i: (0, 0)),
                  _layered(layer, (pd, d), lambda i: (0, 0))],
        out_specs=pl.BlockSpec((tm, d), lambda i: (i, 0)),
        out_shape=jax.ShapeDtypeStruct((m, d), F32),
        compiler_params=_cparams(("arbitrary",)),
        name="ple",
    )(x, p, wg, wp)


def _split_w_in(w):
    n_if = 2 * H_B
    w_head = w[..., 0:N_HEAD]
    w_mid = w[..., N_HEAD + n_if:N_HEAD + n_if + N_MID]
    gates = jnp.concatenate([w[..., N_HEAD:N_HEAD + n_if], w[..., N_HEAD + n_if + N_MID:]], axis=-1)
    pad = jnp.zeros(w.shape[:-1] + (LANE - gates.shape[-1],), w.dtype)
    return w_head.astype(BF16), w_mid.astype(BF16), jnp.concatenate([gates, pad], axis=-1).astype(BF16)


def _small_row(entries):
    row = jnp.zeros((LANE,), F32)
    for off, val in entries:
        row = row.at[off:off + val.shape[0]].set(val.astype(F32))
    return row.reshape(1, LANE)


def _s5_params(a_re, a_im, log_dt, b_re, b_im, c_re, c_im, d, w_glu):
    gpc = S5_CHUNK // S5_STATE
    eye = jnp.eye(gpc, dtype=F32)

    def bmat(b):
        b4 = b.reshape(S5_NCHUNK, gpc, S5_GROUP, S5_STATE)
        return jnp.einsum('jgcp,gh->jgchp', b4, eye).reshape(S5_NCHUNK, gpc * S5_GROUP, S5_CHUNK).astype(BF16)

    def cmat(c):
        c4 = c.reshape(S5_NCHUNK, gpc, S5_STATE, S5_GROUP)
        return jnp.einsum('jgpc,gh->jgphc', c4, eye).reshape(S5_NCHUNK, S5_CHUNK, gpc * S5_GROUP).astype(BF16)

    return dict(are=a_re.reshape(1, S5_LANES), aim=a_im.reshape(1, S5_LANES),
                ldt=jnp.broadcast_to(log_dt[:, None], (S5_GROUPS, S5_STATE)).reshape(1, S5_LANES),
                bre=bmat(b_re), bim=bmat(b_im), cre=cmat(c_re), cim=cmat(c_im),
                d=d.reshape(1, D_A), wglu=w_glu.astype(BF16))


def _layer(layer, xp, xs, bsz, seq, dense, lw, state, prev):
    np_rows = bsz * seq
    nsamp = xs.shape[0]
    in_w = (dense['norm_mix_pre'], dense['w_in'], dense['w_in_mid'], dense['w_in_small'], layer)
    proj, h_p = _norm_proj(xp, *in_w)
    proj_s, h_s = _norm_proj(xs, *in_w)

    sp = _s5_params(lw['s5_a_re'], lw['s5_a_im'], lw['s5_log_dt'], lw['s5_b_re'], lw['s5_b_im'],
                    lw['s5_c_re'], lw['s5_c_im'], lw['s5_d'], lw['s5_w_glu'])
    mp = dict(bias=_small_row([(SM_I, lw['mlstm_b_i']), (SM_F, lw['mlstm_b_f'])]), norm=lw['mlstm_norm'])
    gp = dict(bias=_small_row([(SM_A, lw['gdn_dt_bias'])]), alog=_small_row([(SM_A, lw['gdn_a_log'])]),
              conv_w=lw['gdn_conv_w'], norm=lw['gdn_norm'].reshape(1, DV_C))

    ssm_re, ssm_im, m_c, m_n, m_m, g_s, g_conv, f_conv = state
    st = dict(sre=ssm_re.reshape(nsamp, S5_LANES), sim=ssm_im.reshape(nsamp, S5_LANES),
              c=m_c, n=m_n, m=m_m, gs=g_s, gbuf=jnp.swapaxes(g_conv, 0, 1))
    y_s, sre_s, sim_s, c_s, n_s, m_s, gs_s, gbuf_s = _sample_mixers(proj_s, 0, nsamp, st, sp, mp, gp, layer, prev)
    merge_w = (dense['w_gate'], dense['w_branch_a'], dense['w_branch_b'], dense['w_branch_c'], layer)
    merged_s = _merge(h_s, (y_s, Y_A // D_A), (y_s, Y_B // D_B), (y_s, Y_C // D_C), *merge_w)
    x1_s = _out_proj(merged_s, xs, dense['w_out'], dense['norm_mix_post'], layer)

    s5_part = _s5_prompt_part(proj, bsz, seq, sp)
    ffn_s_part = _ffn_part(
        x1_s, 0, nsamp, 1, dense['norm_ffn_pre'], dense['ffn_w_gate'], dense['ffn_w_up'], dense['ffn_conv_w'],
        dense['ffn_w_down'], dense['norm_ffn_post'], layer, bufs=(f_conv[:, 0], f_conv[:, 1]), steps=s5_part["n"])
    if s5_part["n"] == ffn_s_part["n"]:
        (ya_p, sre_p, sim_p), (x2_s, g_new, wg_b, wu_b, wd_b) = _run_parts("s5_prompt_ffn_sample",
                                                                             [s5_part, ffn_s_part])
    else:
        ya_p, sre_p, sim_p = _run_parts("s5_prompt", [s5_part])[0]
        x2_s, g_new, wg_b, wu_b, wd_b = _run_parts("ffn_sample", [ffn_s_part])[0]
    yb_p, c_p, n_p, m_p = _mlstm_prompt(proj, bsz, seq, mp)
    yc_p, gs_p = _gdn_prompt(proj, bsz, seq, gp)

    merged_p = _merge(h_p, (ya_p, 0), (yb_p, 0), (yc_p, 0), *merge_w)
    x1_p = _out_proj(merged_p, xp, dense['w_out'], dense['norm_mix_post'], layer)
    cw_l, nw_pre_l, nw_post_l = (dense[k][layer:layer + 1] for k in ('ffn_conv_w', 'norm_ffn_pre', 'norm_ffn_post'))
    x2_p, gtail = _ffn(x1_p, 0, np_rows, seq, nw_pre_l, wg_b, wu_b, cw_l, wd_b, nw_post_l, 0)
    x3_p = _ple(x2_p, dense['p_prompt'], dense['ple_w_gate'], dense['ple_w_proj'], layer)
    x3_s = _ple(x2_s, dense['p_sample'], dense['ple_w_gate'], dense['ple_w_proj'], layer)

    tiles_per_seq = gtail.shape[0] // bsz
    qkv_tail = proj.reshape(bsz, seq, N_PROJ)[:, seq - (GDN_CONV - 1):, COL_QKV * LANE:COL_QKV * LANE + 3 * D_C]
    st_p = (sre_p.reshape(bsz, S5_GROUPS, S5_STATE), sim_p.reshape(bsz, S5_GROUPS, S5_STATE),
            c_p, n_p, m_p[:, 0, :H_B], gs_p, qkv_tail,
            gtail.reshape(bsz, tiles_per_seq, SUBLANE, D_FF)[:, -1, SUBLANE - (FFN_CONV - 1):, :])
    st_s = (sre_s.reshape(nsamp, S5_GROUPS, S5_STATE), sim_s.reshape(nsamp, S5_GROUPS, S5_STATE),
            None, n_s, m_s, None, jnp.swapaxes(gbuf_s, 0, 1),
            jnp.stack([f_conv[:, 1], g_new], axis=1))
    return x3_p, x3_s, st_p, st_s, (c_s, gs_s)


def kernel(x_prompt, x_sample, p_prompt, p_sample, state_ssm_re, state_ssm_im, state_mlstm_c, state_mlstm_n, state_mlstm_m, state_gdn_s, state_gdn_conv, state_ffn_conv, norm_mix_pre, norm_mix_post, norm_ffn_pre, norm_ffn_post, w_in, s5_a_re, s5_a_im, s5_log_dt, s5_b_re, s5_b_im, s5_c_re, s5_c_im, s5_d, s5_w_glu, mlstm_b_i, mlstm_b_f, mlstm_norm, gdn_conv_w, gdn_a_log, gdn_dt_bias, gdn_norm, w_branch_a, w_branch_b, w_branch_c, w_gate, w_out, ffn_w_gate, ffn_w_up, ffn_conv_w, ffn_w_down, ple_w_proj, ple_w_gate):
    bsz, seq, d = x_prompt.shape
    nsamp = x_sample.shape[0]
    depth = w_in.shape[0]
    small = dict(
        s5_a_re=s5_a_re, s5_a_im=s5_a_im, s5_log_dt=s5_log_dt, s5_b_re=s5_b_re, s5_b_im=s5_b_im,
        s5_c_re=s5_c_re, s5_c_im=s5_c_im, s5_d=s5_d, s5_w_glu=s5_w_glu, mlstm_b_i=mlstm_b_i,
        mlstm_b_f=mlstm_b_f, mlstm_norm=mlstm_norm, gdn_conv_w=gdn_conv_w, gdn_a_log=gdn_a_log,
        gdn_dt_bias=gdn_dt_bias, gdn_norm=gdn_norm)
    row = lambda w: w.reshape(depth, 1, -1)
    w_in_head, w_in_mid, w_in_small = _split_w_in(w_in)
    dense = dict(
        norm_mix_pre=row(norm_mix_pre), norm_mix_post=row(norm_mix_post), norm_ffn_pre=row(norm_ffn_pre),
        norm_ffn_post=row(norm_ffn_post), w_in=w_in_head, w_in_mid=w_in_mid, w_in_small=w_in_small,
        w_gate=w_gate, w_branch_a=w_branch_a, w_branch_b=w_branch_b, w_branch_c=w_branch_c,
        w_out=w_out.astype(BF16), ffn_w_gate=ffn_w_gate, ffn_w_up=ffn_w_up, ffn_conv_w=ffn_conv_w,
        ffn_w_down=ffn_w_down,
        ple_w_gate=ple_w_gate.astype(BF16), ple_w_proj=ple_w_proj.astype(BF16),
        p_prompt=p_prompt.reshape(depth, bsz * seq, -1), p_sample=p_sample.reshape(depth, nsamp, -1))
    xp = x_prompt.reshape(bsz * seq, d)
    xs = x_sample.reshape(nsamp, d)
    sp_all, ss_all = [], []
    big = None
    for i in range(depth):
        lw = {k: v[i] for k, v in small.items()}
        state = (state_ssm_re[i], state_ssm_im[i], state_mlstm_c, state_mlstm_n[i], state_mlstm_m[i],
                 state_gdn_s, state_gdn_conv[i], state_ffn_conv[i])
        xp, xs, st_p, st_s, big = _layer(i, xp, xs, bsz, seq, dense, lw, state, big)
        sp_all.append(st_p)
        ss_all.append(st_s)
    stack = lambda sts, j: jnp.stack([s[j] for s in sts], axis=0)
    sample_states = [big[0] if j == 2 else big[1] if j == 5 else stack(ss_all, j) for j in range(8)]
    return ((xp.reshape(bsz, seq, d), xs.reshape(nsamp, 1, d))
            + tuple(stack(sp_all, j) for j in range(8))
            + tuple(sample_states))
```

```python
import functools

import jax
import jax.numpy as jnp
from jax import lax
from jax.experimental import pallas as pl
from jax.experimental.pallas import tpu as pltpu

F32 = jnp.float32
BF16 = jnp.bfloat16

D_MODEL = 2048
DEPTH = 2
D_A = 512
S5_GROUP = 16
S5_GROUPS = 32
S5_STATE = 64
S5_LANES = S5_GROUPS * S5_STATE
S5_CHUNK = 512
S5_NCHUNK = S5_LANES // S5_CHUNK
D_B = 1024
H_B = 4
DV_B = 256
DQK_B = 128
D_C = 512
H_C = 4
DK_C = 128
DV_C = 128
GDN_CONV = 4
D_FF = 8192
FFN_CONV = 3
PLE_DIM = 256
EPS = 1e-6

LANE = 128
SUBLANE = 8
VMEM_LIMIT = 56 * 1024 * 1024

COL_U, COL_QB, COL_KB, COL_VB, COL_SMALL, COL_OB, COL_QKV, COL_Z = 0, 4, 8, 12, 20, 24, 32, 44
N_HEAD = COL_SMALL * LANE
N_FRONT = COL_OB * LANE
N_MID = D_B + 3 * D_C + D_C
PROJ_TILE = 1024
N_PROJ = N_FRONT + N_MID
SM_I, SM_F, SM_BETA, SM_A = 0, 4, 8, 12

MLSTM_CHUNK = 128
MLSTM_TILE = 512
GDN_CHUNK = 64
GDN_TILE = 512
S5_TILE = 512
SAMPLE_BLOCK = 8
SAMPLE_GROUP = SAMPLE_BLOCK
Y_B, Y_A, Y_C = 0, D_B, D_B + D_A

NT = (((1,), (1,)), ((), ()))
TN = (((0,), (0,)), ((), ()))


def _cparams(sem):
    return pltpu.CompilerParams(dimension_semantics=sem, vmem_limit_bytes=VMEM_LIMIT)


def _dot(a, b):
    return jnp.dot(a, b, preferred_element_type=F32)


def _dot_hi(a, b):
    return jnp.dot(a, b, preferred_element_type=F32, precision=lax.Precision.HIGHEST)


def _gelu(x):
    return 0.5 * x * (1.0 + jnp.tanh(0.7978845608028654 * (x + 0.044715 * (x * x * x))))


def _sigmoid(x):
    return 1.0 / (1.0 + jnp.exp(-x))


def _silu(x):
    return x * _sigmoid(x)


def _softplus(x):
    return jnp.maximum(x, 0.0) + jnp.log1p(jnp.exp(-jnp.abs(x)))


def _log_sigmoid(x):
    return -_softplus(-x)


def _rms(x, w):
    return x * lax.rsqrt(jnp.mean(x * x, axis=-1, keepdims=True) + EPS) * w


def _layered(layer, shape, imap):
    return pl.BlockSpec((None,) + shape, lambda *g: (layer,) + imap(*g))


def _pick(n, cands):
    for c in cands:
        if n % c == 0:
            return c
    return n


def _norm_proj_kernel(x_ref, nw_ref, wa_ref, wb_ref, proj_ref, h_ref, *, na):
    j = pl.program_id(1)

    @pl.when(j == 0)
    def _():
        h_ref[...] = _rms(x_ref[...], nw_ref[...]).astype(BF16)

    @pl.when(j < na)
    def _():
        proj_ref[...] = _dot(h_ref[...], wa_ref[...])

    @pl.when(j >= na)
    def _():
        proj_ref[...] = _dot(h_ref[...], wb_ref[...])


def _norm_proj(x, nw, w_front, w_mid, layer):
    m, d = x.shape
    tm = _pick(m, (1024, 512, 256, 128))
    tn = PROJ_TILE
    na, nb = N_FRONT // tn, N_MID // tn
    return pl.pallas_call(
        functools.partial(_norm_proj_kernel, na=na),
        grid=(m // tm, na + nb),
        in_specs=[pl.BlockSpec((tm, d), lambda i, j: (i, 0)),
                  _layered(layer, (1, d), lambda i, j: (0, 0)),
                  _layered(layer, (d, tn), lambda i, j: (0, jnp.minimum(j, na - 1))),
                  _layered(layer, (d, tn), lambda i, j: (0, jnp.clip(j - na, 0, nb - 1)))],
        out_specs=[pl.BlockSpec((tm, tn), lambda i, j: (i, j)),
                   pl.BlockSpec((tm, d), lambda i, j: (i, 0))],
        out_shape=[jax.ShapeDtypeStruct((m, N_PROJ), F32), jax.ShapeDtypeStruct((m, d), BF16)],
        compiler_params=_cparams(("arbitrary", "arbitrary")),
        name="norm_proj",
    )(x, nw, w_front, w_mid)


def _s5_disc(are, aim, ldt):
    dt = jnp.exp(ldt)
    mag = jnp.exp(dt * are)
    abr = mag * jnp.cos(dt * aim)
    abi = mag * jnp.sin(dt * aim)
    den = are * are + aim * aim
    zr = abr - 1.0
    fre = (zr * are + abi * aim) / den
    fim = (abi * are - zr * aim) / den
    return abr, abi, fre, fim


def _s5_glu(y, u, d_ref, wglu_ref):
    z = _gelu(y + d_ref[...] * u)
    return z * _sigmoid(_dot(z.astype(BF16), wglu_ref[...]))


def _s5_prompt_body(step, u_ref, are_ref, aim_ref, ldt_ref, bre_ref, bim_ref, cre_ref, cim_ref,
                    d_ref, wglu_ref, y_ref, sre_ref, sim_ref, xr_s, xi_s, y_s, car_re, car_im,
                    f_s, tab_s, *, nt):
    @pl.when(step % nt == 0)
    def _():
        car_re[...] = jnp.zeros_like(car_re)
        car_im[...] = jnp.zeros_like(car_im)
        row = lax.broadcasted_iota(jnp.int32, (SUBLANE, S5_LANES), 0)
        abr, abi, fre, fim = _s5_disc(are_ref[...], aim_ref[...], ldt_ref[...])
        f_s[0:1, :] = fre
        f_s[1:2, :] = fim
        pr, pi = abr, abi
        for lvl, s in enumerate((1, 2, 4)):
            tab_s[2 * lvl] = jnp.where(row >= s, pr, 0.0)
            tab_s[2 * lvl + 1] = jnp.where(row >= s, pi, 0.0)
            pr, pi = pr * pr - pi * pi, 2.0 * pr * pi
        cwr = jnp.zeros((SUBLANE, S5_LANES), F32)
        cwi = jnp.zeros((SUBLANE, S5_LANES), F32)
        pr, pi = abr, abi
        for r in range(SUBLANE):
            cwr = jnp.where(row == r, pr, cwr)
            cwi = jnp.where(row == r, pi, cwi)
            pr, pi = pr * abr - pi * abi, pr * abi + pi * abr
        tab_s[6] = cwr
        tab_s[7] = cwi

    tt = u_ref.shape[0]
    u = u_ref[...]
    ub = u.astype(BF16)
    for j in range(S5_NCHUNK):
        sl = slice(S5_CHUNK * j, S5_CHUNK * (j + 1))
        fre, fim = f_s[0:1, sl], f_s[1:2, sl]
        uj = ub[:, LANE * j:LANE * (j + 1)]
        bur = _dot(uj, bre_ref[j])
        bui = _dot(uj, bim_ref[j])
        xr = (fre * bur - fim * bui).reshape(tt // SUBLANE, SUBLANE, S5_CHUNK)
        xi = (fre * bui + fim * bur).reshape(tt // SUBLANE, SUBLANE, S5_CHUNK)
        for lvl, s in enumerate((1, 2, 4)):
            mr = tab_s[2 * lvl, :, sl]
            mi = tab_s[2 * lvl + 1, :, sl]
            sr = pltpu.roll(xr, s, axis=1)
            si = pltpu.roll(xi, s, axis=1)
            xr, xi = xr + mr * sr - mi * si, xi + mr * si + mi * sr
        xr_s[...] = xr.reshape(tt, S5_CHUNK)
        xi_s[...] = xi.reshape(tt, S5_CHUNK)
        cwr = tab_s[6, :, sl]
        cwi = tab_s[7, :, sl]

        def body(g, carry, cwr=cwr, cwi=cwi):
            cr, ci = carry
            r0 = pl.multiple_of(g * SUBLANE, SUBLANE)
            gr = xr_s[pl.ds(r0, SUBLANE), :] + cwr * cr - cwi * ci
            gi = xi_s[pl.ds(r0, SUBLANE), :] + cwr * ci + cwi * cr
            xr_s[pl.ds(r0, SUBLANE), :] = gr
            xi_s[pl.ds(r0, SUBLANE), :] = gi
            return gr[SUBLANE - 1:SUBLANE, :], gi[SUBLANE - 1:SUBLANE, :]

        cr, ci = lax.fori_loop(0, tt // SUBLANE, body, (car_re[:, sl], car_im[:, sl]), unroll=4)
        car_re[:, sl] = cr
        car_im[:, sl] = ci
        y_s[:, LANE * j:LANE * (j + 1)] = (_dot(xr_s[...].astype(BF16), cre_ref[j])
                                           - _dot(xi_s[...].astype(BF16), cim_ref[j]))
    y_ref[...] = _s5_glu(y_s[...], u, d_ref, wglu_ref).astype(BF16)
    sre_ref[0] = car_re[...]
    sim_ref[0] = car_im[...]


def _run_parts(name, parts):
    n = parts[0]["n"]
    assert all(p["n"] == n for p in parts)
    n_in = [len(p["args"]) for p in parts]
    n_out = [len(p["out_shape"]) for p in parts]
    n_scr = [len(p["scratch"]) for p in parts]

    def kern(*refs):
        step = pl.program_id(0)
        ins, outs, scr = refs[:sum(n_in)], refs[sum(n_in):sum(n_in) + sum(n_out)], refs[sum(n_in) + sum(n_out):]
        a = b = c = 0
        for p, na, nb, nc in zip(parts, n_in, n_out, n_scr):
            p["body"](step, *ins[a:a + na], *outs[b:b + nb], *scr[c:c + nc])
            a, b, c = a + na, b + nb, c + nc

    flat = lambda key: [x for p in parts for x in p[key]]
    outs = pl.pallas_call(
        kern, grid=(n,), in_specs=flat("in_specs"), out_specs=flat("out_specs"), out_shape=flat("out_shape"),
        scratch_shapes=flat("scratch"), compiler_params=_cparams(("arbitrary",)), name=name,
    )(*flat("args"))
    res, b = [], 0
    for nb in n_out:
        res.append(list(outs[b:b + nb]))
        b += nb
    return res


def _s5_prompt_part(proj, bsz, seq, sp):
    tt = _pick(seq, (S5_TILE, 128, 64, 32, 16, 8))
    nt = seq // tt
    const2 = lambda i: (0, 0)
    const3 = lambda i: (0, 0, 0)
    row_spec = pl.BlockSpec((1, S5_LANES), const2)
    return dict(
        n=bsz * nt,
        body=functools.partial(_s5_prompt_body, nt=nt),
        in_specs=[pl.BlockSpec((tt, D_A), lambda i: (i, COL_U)),
                  row_spec, row_spec, row_spec,
                  pl.BlockSpec((S5_NCHUNK, LANE, S5_CHUNK), const3),
                  pl.BlockSpec((S5_NCHUNK, LANE, S5_CHUNK), const3),
                  pl.BlockSpec((S5_NCHUNK, S5_CHUNK, LANE), const3),
                  pl.BlockSpec((S5_NCHUNK, S5_CHUNK, LANE), const3),
                  pl.BlockSpec((1, D_A), const2),
                  pl.BlockSpec((D_A, D_A), const2)],
        args=[proj, sp["are"], sp["aim"], sp["ldt"], sp["bre"], sp["bim"], sp["cre"], sp["cim"],
              sp["d"], sp["wglu"]],
        out_specs=[pl.BlockSpec((tt, D_A), lambda i: (i, 0)),
                   pl.BlockSpec((1, 1, S5_LANES), lambda i: (i // nt, 0, 0)),
                   pl.BlockSpec((1, 1, S5_LANES), lambda i: (i // nt, 0, 0))],
        out_shape=[jax.ShapeDtypeStruct((bsz * seq, D_A), BF16),
                   jax.ShapeDtypeStruct((bsz, 1, S5_LANES), F32),
                   jax.ShapeDtypeStruct((bsz, 1, S5_LANES), F32)],
        scratch=[pltpu.VMEM((tt, S5_CHUNK), F32), pltpu.VMEM((tt, S5_CHUNK), F32),
                 pltpu.VMEM((tt, D_A), F32),
                 pltpu.VMEM((1, S5_LANES), F32), pltpu.VMEM((1, S5_LANES), F32),
                 pltpu.VMEM((2, S5_LANES), F32), pltpu.VMEM((8, SUBLANE, S5_LANES), F32)])


def _s5_prompt(proj, bsz, seq, sp):
    return _run_parts("s5_prompt", [_s5_prompt_part(proj, bsz, seq, sp)])[0]


def _mlstm_prompt_kernel(q_ref, k_ref, v0_ref, v1_ref, o0_ref, o1_ref, sm_ref, bias_ref, nw_ref,
                         y_ref, c_ref, n_ref, m_ref, *, chunk):
    @pl.when(pl.program_id(1) == 0)
    def _():
        c_ref[...] = jnp.zeros_like(c_ref)
        n_ref[...] = jnp.zeros_like(n_ref)
        m_ref[...] = jnp.zeros_like(m_ref)

    tt = q_ref.shape[0]
    c = chunk
    nchunk = tt // c
    smb = sm_ref[...] + bias_ref[...]
    lf_all = _log_sigmoid(smb)
    li_t = smb.T
    lf_t = lf_all.T
    rowi = lax.broadcasted_iota(jnp.int32, (c, c), 0)
    coli = lax.broadcasted_iota(jnp.int32, (c, c), 1)
    causal = rowi >= coli
    lane = lax.broadcasted_iota(jnp.int32, (1, LANE), 1)
    m_row = m_ref[0]
    probs = [(ci, h) for ci in range(nchunk) for h in range(H_B)]
    idx = {p: i for i, p in enumerate(probs)}
    rows = lambda ci: slice(c * ci, c * (ci + 1))
    vo_refs = [(v0_ref, o0_ref) if h < 2 else (v1_ref, o1_ref) for h in range(H_B)]
    q = [q_ref[rows(ci), DQK_B * h:DQK_B * (h + 1)] * (DQK_B ** -0.5) for ci, h in probs]
    k = [k_ref[rows(ci), DQK_B * h:DQK_B * (h + 1)] for ci, h in probs]
    qb = [x.astype(BF16) for x in q]
    kb = [x.astype(BF16) for x in k]
    vb = [vo_refs[h][0][rows(ci), DV_B * (h % 2):DV_B * (h % 2 + 1)].astype(BF16) for ci, h in probs]
    qkt = [lax.dot_general(a, b, NT, preferred_element_type=F32) for a, b in zip(qb, kb)]
    li_c = [smb[rows(ci), SM_I + h:SM_I + h + 1] for ci, h in probs]
    li_r = [li_t[SM_I + h:SM_I + h + 1, rows(ci)] for ci, h in probs]
    bc_c = [jnp.sum(jnp.where(causal, lf_t[SM_F + h:SM_F + h + 1, rows(ci)], 0.0), axis=1, keepdims=True)
            for ci, h in probs]
    bc_r = [jnp.sum(jnp.where(rowi <= coli, lf_all[rows(ci), SM_F + h:SM_F + h + 1], 0.0), axis=0, keepdims=True)
            for ci, h in probs]
    dmat = [jnp.where(causal, a - b + r, -jnp.inf) for a, b, r in zip(bc_c, bc_r, li_r)]
    dmax = [jnp.max(x, axis=1, keepdims=True) for x in dmat]
    b_last = [x[c - 1:c, :] for x in bc_c]
    expo = [bl - a + l for bl, a, l in zip(b_last, bc_c, li_c)]
    emax = [jnp.max(x, axis=0, keepdims=True) for x in expo]
    m_prev, m_new = [None] * len(probs), [None] * len(probs)
    for h in range(H_B):
        m = m_row[:, h:h + 1]
        for ci in range(nchunk):
            i = idx[ci, h]
            m_prev[i] = m
            m = jnp.maximum(b_last[i] + m, emax[i])
            m_new[i] = m
    inter = [a + m for a, m in zip(bc_c, m_prev)]
    m_t = [jnp.maximum(a, b) for a, b in zip(inter, dmax)]
    w_inter = [jnp.exp(a - b) for a, b in zip(inter, m_t)]
    s = [x * jnp.exp(d - m) for x, d, m in zip(qkt, dmat, m_t)]
    sv = [_dot(x.astype(BF16), v) for x, v in zip(s, vb)]
    ssum = [jnp.sum(x, axis=1, keepdims=True) for x in s]
    emt = [jnp.exp(-x) for x in m_t]
    decay = [jnp.exp(bl + mp - mn) for bl, mp, mn in zip(b_last, m_prev, m_new)]
    kw = [jnp.exp(e - mn) * kx for e, mn, kx in zip(expo, m_new, k)]
    kv = [lax.dot_general(x.astype(BF16), v, TN, preferred_element_type=F32) for x, v in zip(kw, vb)]
    ksum = [jnp.sum(x, axis=0, keepdims=True) for x in kw]

    cst = [c_ref[0, h] for h in range(H_B)]
    nrow = [n_ref[0, h:h + 1, :] for h in range(H_B)]
    for ci in range(nchunk):
        ids = [idx[ci, h] for h in range(H_B)]
        qc = [_dot(qb[i], cst[h].astype(BF16)) for h, i in enumerate(ids)]
        num = [sv[i] + w_inter[i] * qc[h] for h, i in enumerate(ids)]
        nq = [ssum[i] + w_inter[i] * jnp.sum(q[i] * nrow[h], axis=1, keepdims=True) for h, i in enumerate(ids)]
        hh = [num[h] / jnp.maximum(jnp.abs(nq[h]), emt[i]) for h, i in enumerate(ids)]
        cst = [decay[i] * cst[h] + kv[i] for h, i in enumerate(ids)]
        nrow = [decay[i] * nrow[h] + ksum[i] for h, i in enumerate(ids)]
        for h in range(H_B):
            og = vo_refs[h][1][rows(ci), DV_B * (h % 2):DV_B * (h % 2 + 1)]
            y_ref[rows(ci), DV_B * h:DV_B * (h + 1)] = (_rms(hh[h], nw_ref[h:h + 1, :]) * _sigmoid(og)).astype(BF16)
    m_out = m_row
    for h in range(H_B):
        c_ref[0, h] = cst[h]
        n_ref[0, h:h + 1, :] = nrow[h]
        m_out = jnp.where(lane == h, m_new[idx[nchunk - 1, h]], m_out)
    m_ref[0] = m_out


def _mlstm_prompt(proj, bsz, seq, mp):
    chunk = _pick(seq, (MLSTM_CHUNK,))
    c = _pick(seq, (MLSTM_TILE, MLSTM_CHUNK))
    nt = seq // c
    rows = lambda col: (lambda b, t: (b * nt + t, col))
    const2 = lambda b, t: (0, 0)
    return pl.pallas_call(
        functools.partial(_mlstm_prompt_kernel, chunk=chunk),
        grid=(bsz, nt),
        in_specs=[pl.BlockSpec((c, 512), rows(COL_QB // 4)),
                  pl.BlockSpec((c, 512), rows(COL_KB // 4)),
                  pl.BlockSpec((c, 512), rows(COL_VB // 4)),
                  pl.BlockSpec((c, 512), rows(COL_VB // 4 + 1)),
                  pl.BlockSpec((c, 512), rows(COL_OB // 4)),
                  pl.BlockSpec((c, 512), rows(COL_OB // 4 + 1)),
                  pl.BlockSpec((c, LANE), rows(COL_SMALL)),
                  pl.BlockSpec((1, LANE), const2),
                  pl.BlockSpec((H_B, DV_B), const2)],
        out_specs=[pl.BlockSpec((c, D_B), lambda b, t: (b * nt + t, 0)),
                   pl.BlockSpec((1, H_B, DQK_B, DV_B), lambda b, t: (b, 0, 0, 0)),
                   pl.BlockSpec((1, H_B, DQK_B), lambda b, t: (b, 0, 0)),
                   pl.BlockSpec((1, 1, LANE), lambda b, t: (b, 0, 0))],
        out_shape=[jax.ShapeDtypeStruct((bsz * seq, D_B), BF16),
                   jax.ShapeDtypeStruct((bsz, H_B, DQK_B, DV_B), F32),
                   jax.ShapeDtypeStruct((bsz, H_B, DQK_B), F32),
                   jax.ShapeDtypeStruct((bsz, 1, LANE), F32)],
        compiler_params=_cparams(("arbitrary", "arbitrary")),
        name="mlstm_prompt",
    )(proj, proj, proj, proj, proj, proj, proj, mp["bias"], mp["norm"])


def _split_bf16(a):
    hi = a.astype(BF16)
    return hi, (a - hi.astype(F32)).astype(BF16)


def _dot_x3(a, b):
    ah, al = _split_bf16(a)
    bh, bl = _split_bf16(b)
    return _dot(ah, bh) + _dot(ah, bl) + _dot(al, bh)


def _unit_lower_inverses(lmats):
    c = lmats[0].shape[0]
    eye = (lax.broadcasted_iota(jnp.int32, (c, c), 0) == lax.broadcasted_iota(jnp.int32, (c, c), 1)).astype(F32)
    hi_half = lax.broadcasted_iota(jnp.int32, (c, 2 * c), 1) >= c
    ms = [jnp.concatenate([-l, eye], axis=1) for l in lmats]
    span = 1
    while span < c:
        split = [_split_bf16(m) for m in ms]
        ms = [_dot(mh[:, 0:c], mh) + _dot(mh[:, 0:c], ml) + _dot(ml[:, 0:c], mh) + jnp.where(hi_half, m, 0.0)
              for m, (mh, ml) in zip(ms, split)]
        span *= 2
    return [m[:, c:2 * c] for m in ms]


def _gdn_prompt_kernel(q_ref, k_ref, v_ref, z_ref, sm_ref, bias_ref, alog_ref, cw_ref, nw_ref,
                       y_ref, s_ref, xb_s, *, chunk):
    tt = q_ref.shape[0]
    c = chunk

    @pl.when(pl.program_id(1) == 0)
    def _():
        s_ref[...] = jnp.zeros_like(s_ref)
        xb_s[0:SUBLANE, :] = jnp.zeros((SUBLANE, 3 * D_C), F32)

    xb_s[SUBLANE:SUBLANE + tt, 0:D_C] = q_ref[...]
    xb_s[SUBLANE:SUBLANE + tt, D_C:2 * D_C] = k_ref[...]
    xb_s[SUBLANE:SUBLANE + tt, 2 * D_C:3 * D_C] = v_ref[...]
    conv = cw_ref[GDN_CONV - 1:GDN_CONV, :] * xb_s[SUBLANE:SUBLANE + tt, :]
    for j in range(GDN_CONV - 1):
        off = SUBLANE - (GDN_CONV - 1) + j
        conv = conv + cw_ref[j:j + 1, :] * xb_s[off:off + tt, :]
    xb_s[0:SUBLANE, :] = xb_s[tt:tt + SUBLANE, :]
    qkv = _silu(conv)

    sm = sm_ref[...]
    beta_all = _sigmoid(sm)
    g_all = -jnp.exp(alog_ref[...]) * _softplus(sm + bias_ref[...])
    g_t = g_all.T
    rowi = lax.broadcasted_iota(jnp.int32, (c, c), 0)
    coli = lax.broadcasted_iota(jnp.int32, (c, c), 1)
    incl = rowi >= coli
    strict = rowi > coli

    nchunk = tt // c
    probs = [(ci, h) for ci in range(nchunk) for h in range(H_C)]
    rows = lambda ci: slice(c * ci, c * (ci + 1))
    l2 = lambda x: x * lax.rsqrt(jnp.sum(x * x, axis=-1, keepdims=True) + EPS)
    q = [l2(qkv[rows(ci), DK_C * h:DK_C * (h + 1)]) * (DK_C ** -0.5) for ci, h in probs]
    k = [l2(qkv[rows(ci), D_C + DK_C * h:D_C + DK_C * (h + 1)]) for ci, h in probs]
    v = [qkv[rows(ci), 2 * D_C + DV_C * h:2 * D_C + DV_C * (h + 1)] for ci, h in probs]
    beta_c = [beta_all[rows(ci), SM_BETA + h:SM_BETA + h + 1] for ci, h in probs]
    gc_c = [jnp.sum(jnp.where(incl, g_t[SM_A + h:SM_A + h + 1, rows(ci)], 0.0), axis=1, keepdims=True)
            for ci, h in probs]
    gc_r = [jnp.sum(jnp.where(rowi <= coli, g_all[rows(ci), SM_A + h:SM_A + h + 1], 0.0), axis=0, keepdims=True)
            for ci, h in probs]
    gam = [jnp.exp(jnp.where(incl, a - b, -jnp.inf)) for a, b in zip(gc_c, gc_r)]
    qb = [x.astype(BF16) for x in q]
    kb = [x.astype(BF16) for x in k]
    kk = [lax.dot_general(x, x, NT, preferred_element_type=F32) for x in kb]
    qk = [(lax.dot_general(a, b, NT, preferred_element_type=F32) * g).astype(BF16) for a, b, g in zip(qb, kb, gam)]
    egc = [jnp.exp(x) for x in gc_c]
    tinv = _unit_lower_inverses([jnp.where(strict, b * g * x, 0.0) for b, g, x in zip(beta_c, gam, kk)])
    tr = [_dot_x3(t, jnp.concatenate([b * vv, (b * e) * kx], axis=1))
          for t, b, vv, e, kx in zip(tinv, beta_c, v, egc, k)]
    u0 = [x[:, 0:DV_C] for x in tr]
    wb = [x[:, DV_C:].astype(BF16) for x in tr]
    g_last = [x[c - 1:c, :] for x in gc_c]
    kw = [(jnp.exp(gl - gc) * kx).astype(BF16) for gl, gc, kx in zip(g_last, gc_c, k)]
    e_last = [jnp.exp(x) for x in g_last]

    st = [s_ref[0, h] for h in range(H_C)]
    for ci in range(nchunk):
        ids = [ci * H_C + h for h in range(H_C)]
        stb = [x.astype(BF16) for x in st]
        ub = [(u0[i] - _dot(wb[i], stb[h])).astype(BF16) for h, i in enumerate(ids)]
        qs = [_dot(qb[i], stb[h]) for h, i in enumerate(ids)]
        st = [e_last[i] * st[h] + lax.dot_general(kw[i], ub[h], TN, preferred_element_type=F32)
              for h, i in enumerate(ids)]
        o = [egc[i] * qs[h] + _dot(qk[i], ub[h]) for h, i in enumerate(ids)]
        for h in range(H_C):
            zz = z_ref[rows(ci), DV_C * h:DV_C * (h + 1)]
            y_ref[rows(ci), DV_C * h:DV_C * (h + 1)] = (_rms(o[h], nw_ref[...]) * _silu(zz)).astype(BF16)
    for h in range(H_C):
        s_ref[0, h] = st[h]


def _gdn_prompt(proj, bsz, seq, gp):
    c = _pick(seq, (GDN_CHUNK,))
    tt = _pick(seq, (GDN_TILE,))
    nt = seq // tt
    rows = lambda col: (lambda b, t: (b * nt + t, col))
    const2 = lambda b, t: (0, 0)
    return pl.pallas_call(
        functools.partial(_gdn_prompt_kernel, chunk=c),
        grid=(bsz, nt),
        in_specs=[pl.BlockSpec((tt, D_C), rows(COL_QKV // 4)),
                  pl.BlockSpec((tt, D_C), rows(COL_QKV // 4 + 1)),
                  pl.BlockSpec((tt, D_C), rows(COL_QKV // 4 + 2)),
                  pl.BlockSpec((tt, D_C), rows(COL_Z // 4)),
                  pl.BlockSpec((tt, LANE), rows(COL_SMALL)),
                  pl.BlockSpec((1, LANE), const2),
                  pl.BlockSpec((1, LANE), const2),
                  pl.BlockSpec((GDN_CONV, 3 * D_C), const2),
                  pl.BlockSpec((1, DV_C), const2)],
        out_specs=[pl.BlockSpec((tt, D_C), lambda b, t: (b * nt + t, 0)),
                   pl.BlockSpec((1, H_C, DK_C, DV_C), lambda b, t: (b, 0, 0, 0))],
        out_shape=[jax.ShapeDtypeStruct((bsz * seq, D_C), BF16),
                   jax.ShapeDtypeStruct((bsz, H_C, DK_C, DV_C), F32)],
        scratch_shapes=[pltpu.VMEM((SUBLANE + tt, 3 * D_C), F32)],
        compiler_params=_cparams(("arbitrary", "arbitrary")),
        name="gdn_prompt",
    )(proj, proj, proj, proj, proj, gp["bias"], gp["alog"], gp["conv_w"], gp["norm"])


def _to_col(row, eye):
    return jnp.sum(jnp.where(eye, row, 0.0), axis=1, keepdims=True)


N_SAMPLE_INPUTS = 23


def _sample_mixers_kernel(*refs, fill_layer):
    (proj_ref, sre_ref, sim_ref, c_ref, n_ref, m_ref, gs_ref, gbuf_ref,
     are_ref, aim_ref, ldt_ref, bre_ref, bim_ref, cre_ref, cim_ref, d_ref, wglu_ref,
     mbias_ref, mnorm_ref, gbias_ref, alog_ref, gcw_ref, gnorm_ref) = refs[:N_SAMPLE_INPUTS]
    (y_ref, sre_o, sim_o, c_o, n_o, m_o, gs_o, gbuf_o,
     qkv_s, qn_s, kn_s, beta_s, g_s, li_s, lf_s) = refs[-15:]
    bb = proj_ref.shape[0]
    if fill_layer is not None:
        for l in range(c_o.shape[0]):
            if l != fill_layer:
                c_o[l] = jnp.zeros(c_o.shape[1:], F32)
                gs_o[l] = jnp.zeros(gs_o.shape[1:], F32)
        c_o = c_o.at[fill_layer]
        gs_o = gs_o.at[fill_layer]

    u = proj_ref[:, COL_U * LANE:COL_U * LANE + D_A]
    ub = u.astype(BF16)
    ys = []
    for j in range(S5_NCHUNK):
        sl = slice(S5_CHUNK * j, S5_CHUNK * (j + 1))
        abr, abi, fre, fim = _s5_disc(are_ref[:, sl], aim_ref[:, sl], ldt_ref[:, sl])
        uj = ub[:, LANE * j:LANE * (j + 1)]
        bur = _dot(uj, bre_ref[j])
        bui = _dot(uj, bim_ref[j])
        s0r = sre_ref[:, sl]
        s0i = sim_ref[:, sl]
        xr = fre * bur - fim * bui + abr * s0r - abi * s0i
        xi = fre * bui + fim * bur + abr * s0i + abi * s0r
        sre_o[:, sl] = xr
        sim_o[:, sl] = xi
        ys.append(_dot(xr.astype(BF16), cre_ref[j]) - _dot(xi.astype(BF16), cim_ref[j]))
    y_a = jnp.concatenate(ys, axis=1)
    y_ref[:, Y_A:Y_A + D_A] = _s5_glu(y_a, u, d_ref, wglu_ref)

    sm = proj_ref[:, COL_SMALL * LANE:(COL_SMALL + 1) * LANE]
    smb = sm + mbias_ref[...]
    li_s[...] = smb
    lf_s[...] = _log_sigmoid(smb)
    beta_s[...] = _sigmoid(sm)
    g_s[...] = -jnp.exp(alog_ref[...]) * _softplus(sm + gbias_ref[...])

    xnew = proj_ref[:, COL_QKV * LANE:COL_QKV * LANE + 3 * D_C]
    conv = gcw_ref[GDN_CONV - 1:GDN_CONV, :] * xnew
    for j in range(GDN_CONV - 1):
        conv = conv + gcw_ref[j:j + 1, :] * gbuf_ref[j]
        if j > 0:
            gbuf_o[j - 1] = gbuf_ref[j]
    gbuf_o[GDN_CONV - 2] = xnew
    qkv = _silu(conv)
    qkv_s[...] = qkv
    for h in range(H_C):
        q = qkv[:, DK_C * h:DK_C * (h + 1)]
        k = qkv[:, D_C + DK_C * h:D_C + DK_C * (h + 1)]
        qn_s[:, DK_C * h:DK_C * (h + 1)] = q * lax.rsqrt(jnp.sum(q * q, axis=-1, keepdims=True) + EPS) * (DK_C ** -0.5)
        kn_s[:, DK_C * h:DK_C * (h + 1)] = k * lax.rsqrt(jnp.sum(k * k, axis=-1, keepdims=True) + EPS)

    assert 4 * H_B * bb == LANE and DQK_B == LANE and DK_C == LANE and H_B == H_C
    tiles = ([proj_ref[:, COL_QB * LANE + DQK_B * h:COL_QB * LANE + DQK_B * (h + 1)] * (DQK_B ** -0.5)
              for h in range(H_B)]
             + [proj_ref[:, COL_KB * LANE + DQK_B * h:COL_KB * LANE + DQK_B * (h + 1)] for h in range(H_B)]
             + [qn_s[:, DK_C * h:DK_C * (h + 1)] for h in range(H_C)]
             + [kn_s[:, DK_C * h:DK_C * (h + 1)] for h in range(H_C)])
    stacked_t = jnp.concatenate(tiles, axis=0).T

    def col_of(kind, b, h):
        j = (kind * H_B + h) * bb + b
        return stacked_t[:, j:j + 1]

    lane = lax.broadcasted_iota(jnp.int32, (1, H_B), 1)

    rsl = lambda b: slice(b, b + 1)
    for g0 in range(0, bb, SAMPLE_GROUP):
        probs = [(b, h) for b in range(g0, g0 + SAMPLE_GROUP) for h in range(H_B)]

        q = [proj_ref[rsl(b), COL_QB * LANE + DQK_B * h:COL_QB * LANE + DQK_B * (h + 1)] * (DQK_B ** -0.5)
             for b, h in probs]
        k = [proj_ref[rsl(b), COL_KB * LANE + DQK_B * h:COL_KB * LANE + DQK_B * (h + 1)] for b, h in probs]
        v = [proj_ref[rsl(b), COL_VB * LANE + DV_B * h:COL_VB * LANE + DV_B * (h + 1)] for b, h in probs]
        li = [li_s[rsl(b), SM_I + h:SM_I + h + 1] for b, h in probs]
        inter = [lf_s[rsl(b), SM_F + h:SM_F + h + 1] + m_ref[rsl(b), h:h + 1] for b, h in probs]
        m_t = [jnp.maximum(a, c) for a, c in zip(inter, li)]
        w_intra = [jnp.exp(a - c) for a, c in zip(li, m_t)]
        w_inter = [jnp.exp(a - c) for a, c in zip(inter, m_t)]
        qcol = [col_of(0, b, h) for b, h in probs]
        kcol = [col_of(1, b, h) for b, h in probs]
        s = [jnp.sum(a * c, axis=1, keepdims=True) * w for a, c, w in zip(q, k, w_intra)]
        cst = [c_ref[b, h] for b, h in probs]
        nrow = [n_ref[b, h:h + 1, :] for b, h in probs]
        qc = [jnp.sum(a * c, axis=0, keepdims=True) for a, c in zip(qcol, cst)]
        for i, (b, h) in enumerate(probs):
            c_o[b, h] = w_inter[i] * cst[i] + (w_intra[i] * kcol[i]) * v[i]
            n_o[b, h:h + 1, :] = w_inter[i] * nrow[i] + w_intra[i] * k[i]
        num = [s[i] * v[i] + w_inter[i] * qc[i] for i in range(len(probs))]
        nq = [s[i] + w_inter[i] * jnp.sum(q[i] * nrow[i], axis=1, keepdims=True) for i in range(len(probs))]
        hh = [num[i] / jnp.maximum(jnp.abs(nq[i]), jnp.exp(-m_t[i])) for i in range(len(probs))]
        for i, (b, h) in enumerate(probs):
            og = proj_ref[rsl(b), COL_OB * LANE + DV_B * h:COL_OB * LANE + DV_B * (h + 1)]
            y_ref[rsl(b), Y_B + DV_B * h:Y_B + DV_B * (h + 1)] = _rms(hh[i], mnorm_ref[h:h + 1, :]) * _sigmoid(og)
        for b in range(g0, g0 + SAMPLE_GROUP):
            m_out = m_ref[rsl(b), :]
            for h in range(H_B):
                m_out = jnp.where(lane == h, m_t[(b - g0) * H_B + h], m_out)
            m_o[rsl(b), :] = m_out

        q = [qn_s[rsl(b), DK_C * h:DK_C * (h + 1)] for b, h in probs]
        k = [kn_s[rsl(b), DK_C * h:DK_C * (h + 1)] for b, h in probs]
        v = [qkv_s[rsl(b), 2 * D_C + DV_C * h:2 * D_C + DV_C * (h + 1)] for b, h in probs]
        beta = [beta_s[rsl(b), SM_BETA + h:SM_BETA + h + 1] for b, h in probs]
        eg = [jnp.exp(g_s[rsl(b), SM_A + h:SM_A + h + 1]) for b, h in probs]
        qcol = [col_of(2, b, h) for b, h in probs]
        kcol = [col_of(3, b, h) for b, h in probs]
        st = [gs_ref[b, h] for b, h in probs]
        ks = [jnp.sum(a * c, axis=0, keepdims=True) for a, c in zip(kcol, st)]
        qs = [jnp.sum(a * c, axis=0, keepdims=True) for a, c in zip(qcol, st)]
        un = [beta[i] * (v[i] - eg[i] * ks[i]) for i in range(len(probs))]
        for i, (b, h) in enumerate(probs):
            gs_o[b, h] = eg[i] * st[i] + kcol[i] * un[i]
        o = [eg[i] * qs[i] + jnp.sum(q[i] * k[i], axis=1, keepdims=True) * un[i] for i in range(len(probs))]
        for i, (b, h) in enumerate(probs):
            zz = proj_ref[rsl(b), COL_Z * LANE + DV_C * h:COL_Z * LANE + DV_C * (h + 1)]
            y_ref[rsl(b), Y_C + DV_C * h:Y_C + DV_C * (h + 1)] = _rms(o[i], gnorm_ref[...]) * _silu(zz)


def _sample_mixers(proj, row0, nrows, st, sp, mp, gp, layer, prev):
    bb = SAMPLE_BLOCK
    blk0 = row0 // bb
    depth = st["c"].shape[0]
    const2 = lambda i: (0, 0)
    const3 = lambda i: (0, 0, 0)
    row_spec = pl.BlockSpec((1, S5_LANES), const2)
    c_spec = _layered(layer, (bb, H_B, DQK_B, DV_B), lambda i: (i, 0, 0, 0))
    gs_spec = _layered(layer, (bb, H_C, DK_C, DV_C), lambda i: (i, 0, 0, 0))
    in_specs = [
        pl.BlockSpec((bb, N_PROJ), lambda i: (blk0 + i, 0)),
        pl.BlockSpec((bb, S5_LANES), lambda i: (i, 0)),
        pl.BlockSpec((bb, S5_LANES), lambda i: (i, 0)),
        c_spec,
        pl.BlockSpec((bb, H_B, DQK_B), lambda i: (i, 0, 0)),
        pl.BlockSpec((bb, H_B), lambda i: (i, 0)),
        gs_spec,
        pl.BlockSpec((GDN_CONV - 1, bb, 3 * D_C), lambda i: (0, i, 0)),
        row_spec, row_spec, row_spec,
        pl.BlockSpec((S5_NCHUNK, LANE, S5_CHUNK), const3),
        pl.BlockSpec((S5_NCHUNK, LANE, S5_CHUNK), const3),
        pl.BlockSpec((S5_NCHUNK, S5_CHUNK, LANE), const3),
        pl.BlockSpec((S5_NCHUNK, S5_CHUNK, LANE), const3),
        pl.BlockSpec((1, D_A), const2),
        pl.BlockSpec((D_A, D_A), const2),
        pl.BlockSpec((1, LANE), const2),
        pl.BlockSpec((H_B, DV_B), const2),
        pl.BlockSpec((1, LANE), const2),
        pl.BlockSpec((1, LANE), const2),
        pl.BlockSpec((GDN_CONV, 3 * D_C), const2),
        pl.BlockSpec((1, DV_C), const2),
    ]
    out_specs = [
        pl.BlockSpec((bb, D_MODEL), lambda i: (i, 0)),
        pl.BlockSpec((bb, S5_LANES), lambda i: (i, 0)),
        pl.BlockSpec((bb, S5_LANES), lambda i: (i, 0)),
        c_spec if prev is not None else pl.BlockSpec((depth, bb, H_B, DQK_B, DV_B), lambda i: (0, i, 0, 0, 0)),
        pl.BlockSpec((bb, H_B, DQK_B), lambda i: (i, 0, 0)),
        pl.BlockSpec((bb, H_B), lambda i: (i, 0)),
        gs_spec if prev is not None else pl.BlockSpec((depth, bb, H_C, DK_C, DV_C), lambda i: (0, i, 0, 0, 0)),
        pl.BlockSpec((GDN_CONV - 1, bb, 3 * D_C), lambda i: (0, i, 0)),
    ]
    out_shape = [
        jax.ShapeDtypeStruct((nrows, D_MODEL), F32),
        jax.ShapeDtypeStruct((nrows, S5_LANES), F32),
        jax.ShapeDtypeStruct((nrows, S5_LANES), F32),
        jax.ShapeDtypeStruct((depth, nrows, H_B, DQK_B, DV_B), F32),
        jax.ShapeDtypeStruct((nrows, H_B, DQK_B), F32),
        jax.ShapeDtypeStruct((nrows, H_B), F32),
        jax.ShapeDtypeStruct((depth, nrows, H_C, DK_C, DV_C), F32),
        jax.ShapeDtypeStruct((GDN_CONV - 1, nrows, 3 * D_C), F32),
    ]
    scratch = [pltpu.VMEM((bb, 3 * D_C), F32), pltpu.VMEM((bb, D_C), F32), pltpu.VMEM((bb, D_C), F32),
               pltpu.VMEM((bb, LANE), F32), pltpu.VMEM((bb, LANE), F32),
               pltpu.VMEM((bb, LANE), F32), pltpu.VMEM((bb, LANE), F32)]
    args = [proj, st["sre"], st["sim"], st["c"], st["n"], st["m"], st["gs"], st["gbuf"],
            sp["are"], sp["aim"], sp["ldt"], sp["bre"], sp["bim"], sp["cre"], sp["cim"], sp["d"], sp["wglu"],
            mp["bias"], mp["norm"], gp["bias"], gp["alog"], gp["conv_w"], gp["norm"]]
    assert len(args) == N_SAMPLE_INPUTS
    aliases = {}
    if prev is not None:
        in_specs += [pl.BlockSpec(memory_space=pl.ANY)] * 2
        args += list(prev)
        aliases = {N_SAMPLE_INPUTS: 3, N_SAMPLE_INPUTS + 1: 6}
    return pl.pallas_call(
        functools.partial(_sample_mixers_kernel, fill_layer=layer if prev is None else None),
        grid=(nrows // bb,),
        in_specs=in_specs, out_specs=out_specs, out_shape=out_shape, scratch_shapes=scratch,
        input_output_aliases=aliases,
        compiler_params=_cparams(("arbitrary",)),
        name="sample_mixers",
    )(*args)


def _merge_kernel(h_ref, ya_ref, yb_ref, yc_ref, wg0_ref, wg1_ref, wg2_ref, wa_ref, wb_ref, wc_ref, o_ref):
    h = h_ref[...]
    bf = lambda ref: ref[...].astype(BF16)
    acc = _sigmoid(_dot(h, bf(wg0_ref))) * _dot(bf(ya_ref), bf(wa_ref))
    acc = acc + _sigmoid(_dot(h, bf(wg1_ref))) * _dot(bf(yb_ref), bf(wb_ref))
    acc = acc + _sigmoid(_dot(h, bf(wg2_ref))) * _dot(bf(yc_ref), bf(wc_ref))
    o_ref[...] = acc.astype(BF16)


def _merge(h, ya, yb, yc, wg, wa, wb, wc, layer):
    m, d = h.shape
    tm = _pick(m, (1024, 512, 256, 128))
    tn = 256
    nb = d // tn
    lhs = lambda w, cb: pl.BlockSpec((tm, w), lambda i, j: (i, cb))
    gate = lambda g: _layered(layer, (d, tn), lambda i, j: (0, g * nb + j))
    rhs = lambda w: _layered(layer, (w, tn), lambda i, j: (0, j))
    return pl.pallas_call(
        _merge_kernel,
        grid=(m // tm, nb),
        in_specs=[lhs(d, 0), lhs(D_A, ya[1]), lhs(D_B, yb[1]), lhs(D_C, yc[1]), gate(0), gate(1), gate(2),
                  rhs(D_A), rhs(D_B), rhs(D_C)],
        out_specs=pl.BlockSpec((tm, tn), lambda i, j: (i, j)),
        out_shape=jax.ShapeDtypeStruct((m, d), BF16),
        compiler_params=_cparams(("arbitrary", "arbitrary")),
        name="merge",
    )(h, ya[0], yb[0], yc[0], wg, wg, wg, wa, wb, wc)


def _out_proj_kernel(a_ref, x_ref, w_ref, nw_ref, o_ref):
    o_ref[...] = x_ref[...] + _rms(_dot(a_ref[...], w_ref[...]), nw_ref[...])


def _out_proj(a, x, w, nw, layer):
    m, d = x.shape
    tm = _pick(m, (512, 256, 128))
    return pl.pallas_call(
        _out_proj_kernel,
        grid=(m // tm,),
        in_specs=[pl.BlockSpec((tm, d), lambda i: (i, 0)),
                  pl.BlockSpec((tm, d), lambda i: (i, 0)),
                  _layered(layer, (d, d), lambda i: (0, 0)),
                  _layered(layer, (1, d), lambda i: (0, 0))],
        out_specs=pl.BlockSpec((tm, d), lambda i: (i, 0)),
        out_shape=jax.ShapeDtypeStruct((m, d), F32),
        compiler_params=_cparams(("arbitrary",)),
        name="out_proj",
    )(a, x, w, nw)


def _ffn_body(step, *refs, sample, tiles_per_seq, ft):
    if sample:
        (x_ref, nw_ref, wg_ref, wu_ref, cw_ref, wd_ref, pnw_ref, b0_ref, b1_ref,
         o_ref, g_ref, wg_o, wu_o, wd_o, h2_s) = refs
    else:
        (x_ref, nw_ref, wg_ref, wu_ref, cw_ref, wd_ref, pnw_ref,
         o_ref, g_ref, h2_s, gb_s, carry_s) = refs
    i = step // ft
    j = step % ft
    tm = x_ref.shape[0]

    @pl.when(j == 0)
    def _():
        h2_s[...] = _rms(x_ref[...], nw_ref[...]).astype(BF16)
        o_ref[...] = jnp.zeros_like(o_ref)

    h2 = h2_s[...]
    wg, wu, wd = wg_ref[...], wu_ref[...], wd_ref[...]
    if sample:
        wg, wu, wd = wg.astype(BF16), wu.astype(BF16), wd.astype(BF16)
        wg_o[0] = wg
        wu_o[0] = wu
        wd_o[0] = wd
    g = _dot(h2, wg)
    up = _dot(h2, wu)
    if sample:
        a = cw_ref[0:1, :] * b0_ref[...] + cw_ref[1:2, :] * b1_ref[...] + cw_ref[2:3, :] * g
        g_ref[...] = g
    else:
        prev = jnp.where(i % tiles_per_seq == 0, 0.0, carry_s[j])
        gb_s[0:SUBLANE, :] = prev
        gb_s[SUBLANE:SUBLANE + tm, :] = g
        a = (cw_ref[0:1, :] * gb_s[SUBLANE - 2:SUBLANE - 2 + tm, :]
             + cw_ref[1:2, :] * gb_s[SUBLANE - 1:SUBLANE - 1 + tm, :]
             + cw_ref[2:3, :] * g)
        tail = g[tm - SUBLANE:tm, :]
        carry_s[j] = tail
        g_ref[0] = tail
    act = (_gelu(a) * up).astype(BF16)
    o_ref[...] += _dot(act, wd)

    @pl.when(j == ft - 1)
    def _():
        o_ref[...] = x_ref[...] + _rms(o_ref[...], pnw_ref[...])


def _ffn_part(x, row0, nrows, seq, nw, wg, wu, cw, wd, pnw, layer, bufs=None, steps=None):
    d = x.shape[1]
    f = wg.shape[-1]
    sample = bufs is not None
    tm = nrows if sample else _pick(seq, (512, 256, 128, 64, 32, 16, 8))
    tn = _pick(f, (512, 256, 128) if sample else (1024, 512, 256, 128))
    if sample and steps is not None and f % steps == 0 and (f // steps) % LANE == 0 and f // steps <= tn:
        tn = f // steps
    mt, ft = nrows // tm, f // tn
    blk0 = row0 // tm
    mi = lambda s: s // ft
    nj = lambda s: s % ft
    in_specs = [pl.BlockSpec((tm, d), lambda s: (blk0 + mi(s), 0)),
                _layered(layer, (1, d), lambda s: (0, 0)),
                _layered(layer, (d, tn), lambda s: (0, nj(s))),
                _layered(layer, (d, tn), lambda s: (0, nj(s))),
                _layered(layer, (FFN_CONV, tn), lambda s: (0, nj(s))),
                _layered(layer, (tn, d), lambda s: (nj(s), 0)),
                _layered(layer, (1, d), lambda s: (0, 0))]
    args = [x, nw, wg, wu, cw, wd, pnw]
    scratch = [pltpu.VMEM((tm, d), BF16)]
    if sample:
        in_specs += [pl.BlockSpec((tm, tn), lambda s: (mi(s), nj(s)))] * 2
        args += list(bufs)
        assert mt == 1, "the sample call must visit every weight tile exactly once"
        g_spec = pl.BlockSpec((tm, tn), lambda s: (mi(s), nj(s)))
        g_shape = jax.ShapeDtypeStruct((nrows, f), F32)
        extra_specs = [pl.BlockSpec((1, d, tn), lambda s: (0, 0, nj(s))),
                       pl.BlockSpec((1, d, tn), lambda s: (0, 0, nj(s))),
                       pl.BlockSpec((1, tn, d), lambda s: (0, nj(s), 0))]
        extra_shapes = [jax.ShapeDtypeStruct((1, d, f), BF16), jax.ShapeDtypeStruct((1, d, f), BF16),
                        jax.ShapeDtypeStruct((1, f, d), BF16)]
    else:
        extra_specs, extra_shapes = [], []
        scratch += [pltpu.VMEM((SUBLANE + tm, tn), F32), pltpu.VMEM((ft, SUBLANE, tn), F32)]
        g_spec = pl.BlockSpec((1, SUBLANE, tn), lambda s: (mi(s), 0, nj(s)))
        g_shape = jax.ShapeDtypeStruct((mt, SUBLANE, f), F32)
    return dict(
        n=mt * ft,
        body=functools.partial(_ffn_body, sample=sample, tiles_per_seq=max(seq // tm, 1), ft=ft),
        in_specs=in_specs, args=args,
        out_specs=[pl.BlockSpec((tm, d), lambda s: (mi(s), 0)), g_spec] + extra_specs,
        out_shape=[jax.ShapeDtypeStruct((nrows, d), F32), g_shape] + extra_shapes,
        scratch=scratch)


def _ffn(*args, **kwargs):
    part = _ffn_part(*args, **kwargs)
    return _run_parts("ffn_sample" if kwargs.get("bufs") is not None else "ffn_prompt", [part])[0]


def _ple_kernel(x_ref, p_ref, wg_ref, wp_ref, o_ref):
    x = x_ref[...]
    gate = _sigmoid(_dot(x.astype(BF16), wg_ref[...]))
    o_ref[...] = x + gate * _dot(p_ref[...].astype(BF16), wp_ref[...])


def _ple(x, p, wg, wp, layer):
    m, d = x.shape
    pd = p.shape[-1]
    tm = _pick(m, (512, 256, 128))
    return pl.pallas_call(
        _ple_kernel,
        grid=(m // tm,),
        in_specs=[pl.BlockSpec((tm, d), lambda i: (i, 0)),
                  _layered(layer, (tm, pd), lambda i: (i, 0)),
                  _layered(layer, (d, d), lambda i: (0, 0)),
                  _layered(layer, (pd, d), lambda i: (0, 0))],
        out_specs=pl.BlockSpec((tm, d), lambda i: (i, 0)),
        out_shape=jax.ShapeDtypeStruct((m, d), F32),
        compiler_params=_cparams(("arbitrary",)),
        name="ple",
    )(x, p, wg, wp)


def _split_w_in(w):
    n_if = 2 * H_B
    w_head = w[..., 0:N_HEAD]
    w_mid = w[..., N_HEAD + n_if:N_HEAD + n_if + N_MID]
    gates = jnp.concatenate([w[..., N_HEAD:N_HEAD + n_if], w[..., N_HEAD + n_if + N_MID:]], axis=-1)
    pad = jnp.zeros(w.shape[:-1] + (N_FRONT - N_HEAD - gates.shape[-1],), w.dtype)
    return jnp.concatenate([w_head, gates, pad], axis=-1).astype(BF16), w_mid.astype(BF16)


def _small_row(entries):
    row = jnp.zeros((LANE,), F32)
    for off, val in entries:
        row = row.at[off:off + val.shape[0]].set(val.astype(F32))
    return row.reshape(1, LANE)


def _s5_params(a_re, a_im, log_dt, b_re, b_im, c_re, c_im, d, w_glu):
    gpc = S5_CHUNK // S5_STATE
    eye = jnp.eye(gpc, dtype=F32)

    def bmat(b):
        b4 = b.reshape(S5_NCHUNK, gpc, S5_GROUP, S5_STATE)
        return jnp.einsum('jgcp,gh->jgchp', b4, eye).reshape(S5_NCHUNK, gpc * S5_GROUP, S5_CHUNK).astype(BF16)

    def cmat(c):
        c4 = c.reshape(S5_NCHUNK, gpc, S5_STATE, S5_GROUP)
        return jnp.einsum('jgpc,gh->jgphc', c4, eye).reshape(S5_NCHUNK, S5_CHUNK, gpc * S5_GROUP).astype(BF16)

    return dict(are=a_re.reshape(1, S5_LANES), aim=a_im.reshape(1, S5_LANES),
                ldt=jnp.broadcast_to(log_dt[:, None], (S5_GROUPS, S5_STATE)).reshape(1, S5_LANES),
                bre=bmat(b_re), bim=bmat(b_im), cre=cmat(c_re), cim=cmat(c_im),
                d=d.reshape(1, D_A), wglu=w_glu.astype(BF16))


def _layer(layer, xp, xs, bsz, seq, dense, lw, state, prev):
    np_rows = bsz * seq
    nsamp = xs.shape[0]
    in_w = (dense['norm_mix_pre'], dense['w_in'], dense['w_in_mid'], layer)
    proj, h_p = _norm_proj(xp, *in_w)
    proj_s, h_s = _norm_proj(xs, *in_w)

    sp = _s5_params(lw['s5_a_re'], lw['s5_a_im'], lw['s5_log_dt'], lw['s5_b_re'], lw['s5_b_im'],
                    lw['s5_c_re'], lw['s5_c_im'], lw['s5_d'], lw['s5_w_glu'])
    mp = dict(bias=_small_row([(SM_I, lw['mlstm_b_i']), (SM_F, lw['mlstm_b_f'])]), norm=lw['mlstm_norm'])
    gp = dict(bias=_small_row([(SM_A, lw['gdn_dt_bias'])]), alog=_small_row([(SM_A, lw['gdn_a_log'])]),
              conv_w=lw['gdn_conv_w'], norm=lw['gdn_norm'].reshape(1, DV_C))

    ssm_re, ssm_im, m_c, m_n, m_m, g_s, g_conv, f_conv = state
    st = dict(sre=ssm_re.reshape(nsamp, S5_LANES), sim=ssm_im.reshape(nsamp, S5_LANES),
              c=m_c, n=m_n, m=m_m, gs=g_s, gbuf=jnp.swapaxes(g_conv, 0, 1))
    y_s, sre_s, sim_s, c_s, n_s, m_s, gs_s, gbuf_s = _sample_mixers(proj_s, 0, nsamp, st, sp, mp, gp, layer, prev)
    merge_w = (dense['w_gate'], dense['w_branch_a'], dense['w_branch_b'], dense['w_branch_c'], layer)
    merged_s = _merge(h_s, (y_s, Y_A // D_A), (y_s, Y_B // D_B), (y_s, Y_C // D_C), *merge_w)
    x1_s = _out_proj(merged_s, xs, dense['w_out'], dense['norm_mix_post'], layer)

    s5_part = _s5_prompt_part(proj, bsz, seq, sp)
    ffn_s_part = _ffn_part(
        x1_s, 0, nsamp, 1, dense['norm_ffn_pre'], dense['ffn_w_gate'], dense['ffn_w_up'], dense['ffn_conv_w'],
        dense['ffn_w_down'], dense['norm_ffn_post'], layer, bufs=(f_conv[:, 0], f_conv[:, 1]), steps=s5_part["n"])
    if s5_part["n"] == ffn_s_part["n"]:
        (ya_p, sre_p, sim_p), (x2_s, g_new, wg_b, wu_b, wd_b) = _run_parts("s5_prompt_ffn_sample",
                                                                             [s5_part, ffn_s_part])
    else:
        ya_p, sre_p, sim_p = _run_parts("s5_prompt", [s5_part])[0]
        x2_s, g_new, wg_b, wu_b, wd_b = _run_parts("ffn_sample", [ffn_s_part])[0]
    yb_p, c_p, n_p, m_p = _mlstm_prompt(proj, bsz, seq, mp)
    yc_p, gs_p = _gdn_prompt(proj, bsz, seq, gp)

    merged_p = _merge(h_p, (ya_p, 0), (yb_p, 0), (yc_p, 0), *merge_w)
    x1_p = _out_proj(merged_p, xp, dense['w_out'], dense['norm_mix_post'], layer)
    cw_l, nw_pre_l, nw_post_l = (dense[k][layer:layer + 1] for k in ('ffn_conv_w', 'norm_ffn_pre', 'norm_ffn_post'))
    x2_p, gtail = _ffn(x1_p, 0, np_rows, seq, nw_pre_l, wg_b, wu_b, cw_l, wd_b, nw_post_l, 0)
    x3_p = _ple(x2_p, dense['p_prompt'], dense['ple_w_gate'], dense['ple_w_proj'], layer)
    x3_s = _ple(x2_s, dense['p_sample'], dense['ple_w_gate'], dense['ple_w_proj'], layer)

    tiles_per_seq = gtail.shape[0] // bsz
    qkv_tail = proj.reshape(bsz, seq, N_PROJ)[:, seq - (GDN_CONV - 1):, COL_QKV * LANE:COL_QKV * LANE + 3 * D_C]
    st_p = (sre_p.reshape(bsz, S5_GROUPS, S5_STATE), sim_p.reshape(bsz, S5_GROUPS, S5_STATE),
            c_p, n_p, m_p[:, 0, :H_B], gs_p, qkv_tail,
            gtail.reshape(bsz, tiles_per_seq, SUBLANE, D_FF)[:, -1, SUBLANE - (FFN_CONV - 1):, :])
    st_s = (sre_s.reshape(nsamp, S5_GROUPS, S5_STATE), sim_s.reshape(nsamp, S5_GROUPS, S5_STATE),
            None, n_s, m_s, None, jnp.swapaxes(gbuf_s, 0, 1),
            jnp.stack([f_conv[:, 1], g_new], axis=1))
    return x3_p, x3_s, st_p, st_s, (c_s, gs_s)


def kernel(x_prompt, x_sample, p_prompt, p_sample, state_ssm_re, state_ssm_im, state_mlstm_c, state_mlstm_n, state_mlstm_m, state_gdn_s, state_gdn_conv, state_ffn_conv, norm_mix_pre, norm_mix_post, norm_ffn_pre, norm_ffn_post, w_in, s5_a_re, s5_a_im, s5_log_dt, s5_b_re, s5_b_im, s5_c_re, s5_c_im, s5_d, s5_w_glu, mlstm_b_i, mlstm_b_f, mlstm_norm, gdn_conv_w, gdn_a_log, gdn_dt_bias, gdn_norm, w_branch_a, w_branch_b, w_branch_c, w_gate, w_out, ffn_w_gate, ffn_w_up, ffn_conv_w, ffn_w_down, ple_w_proj, ple_w_gate):
    bsz, seq, d = x_prompt.shape
    nsamp = x_sample.shape[0]
    depth = w_in.shape[0]
    small = dict(
        s5_a_re=s5_a_re, s5_a_im=s5_a_im, s5_log_dt=s5_log_dt, s5_b_re=s5_b_re, s5_b_im=s5_b_im,
        s5_c_re=s5_c_re, s5_c_im=s5_c_im, s5_d=s5_d, s5_w_glu=s5_w_glu, mlstm_b_i=mlstm_b_i,
        mlstm_b_f=mlstm_b_f, mlstm_norm=mlstm_norm, gdn_conv_w=gdn_conv_w, gdn_a_log=gdn_a_log,
        gdn_dt_bias=gdn_dt_bias, gdn_norm=gdn_norm)
    row = lambda w: w.reshape(depth, 1, -1)
    w_in_front, w_in_mid = _split_w_in(w_in)
    dense = dict(
        norm_mix_pre=row(norm_mix_pre), norm_mix_post=row(norm_mix_post), norm_ffn_pre=row(norm_ffn_pre),
        norm_ffn_post=row(norm_ffn_post), w_in=w_in_front, w_in_mid=w_in_mid,
        w_gate=w_gate, w_branch_a=w_branch_a, w_branch_b=w_branch_b, w_branch_c=w_branch_c,
        w_out=w_out.astype(BF16), ffn_w_gate=ffn_w_gate, ffn_w_up=ffn_w_up, ffn_conv_w=ffn_conv_w,
        ffn_w_down=ffn_w_down,
        ple_w_gate=ple_w_gate.astype(BF16), ple_w_proj=ple_w_proj.astype(BF16),
        p_prompt=p_prompt.reshape(depth, bsz * seq, -1), p_sample=p_sample.reshape(depth, nsamp, -1))
    xp = x_prompt.reshape(bsz * seq, d)
    xs = x_sample.reshape(nsamp, d)
    sp_all, ss_all = [], []
    big = None
    for i in range(depth):
        lw = {k: v[i] for k, v in small.items()}
        state = (state_ssm_re[i], state_ssm_im[i], state_mlstm_c, state_mlstm_n[i], state_mlstm_m[i],
                 state_gdn_s, state_gdn_conv[i], state_ffn_conv[i])
        xp, xs, st_p, st_s, big = _layer(i, xp, xs, bsz, seq, dense, lw, state, big)
        sp_all.append(st_p)
        ss_all.append(st_s)
    stack = lambda sts, j: jnp.stack([s[j] for s in sts], axis=0)
    sample_states = [big[0] if j == 2 else big[1] if j == 5 else stack(ss_all, j) for j in range(8)]
    return ((xp.reshape(bsz, seq, d), xs.reshape(nsamp, 1, d))
            + tuple(stack(sp_all, j) for j in range(8))
            + tuple(sample_states))
```

```python
import functools

import jax
import jax.numpy as jnp
from jax import lax
from jax.experimental import pallas as pl
from jax.experimental.pallas import tpu as pltpu

F32 = jnp.float32
BF16 = jnp.bfloat16

D_MODEL = 2048
DEPTH = 2
D_A = 512
S5_GROUP = 16
S5_GROUPS = 32
S5_STATE = 64
S5_LANES = S5_GROUPS * S5_STATE
S5_CHUNK = 512
S5_NCHUNK = S5_LANES // S5_CHUNK
D_B = 1024
H_B = 4
DV_B = 256
DQK_B = 128
D_C = 512
H_C = 4
DK_C = 128
DV_C = 128
GDN_CONV = 4
D_FF = 8192
FFN_CONV = 3
PLE_DIM = 256
EPS = 1e-6

LANE = 128
SUBLANE = 8
VMEM_LIMIT = 56 * 1024 * 1024

COL_U, COL_QB, COL_KB, COL_VB, COL_SMALL, COL_OB, COL_QKV, COL_Z = 0, 4, 8, 12, 20, 24, 32, 44
N_HEAD = COL_SMALL * LANE
N_FRONT = COL_OB * LANE
N_MID = D_B + 3 * D_C + D_C
PROJ_TILE = 1024
N_PROJ = N_FRONT + N_MID
SM_I, SM_F, SM_BETA, SM_A = 0, 4, 8, 12

MLSTM_CHUNK = 128
MLSTM_TILE = 512
GDN_CHUNK = 64
GDN_TILE = 512
S5_TILE = 512
SAMPLE_BLOCK = 8
SAMPLE_GROUP = SAMPLE_BLOCK
Y_B, Y_A, Y_C = 0, D_B, D_B + D_A

NT = (((1,), (1,)), ((), ()))
TN = (((0,), (0,)), ((), ()))


def _cparams(sem):
    return pltpu.CompilerParams(dimension_semantics=sem, vmem_limit_bytes=VMEM_LIMIT)


def _dot(a, b):
    return jnp.dot(a, b, preferred_element_type=F32)


def _gelu(x):
    return 0.5 * x * (1.0 + jnp.tanh(0.7978845608028654 * (x + 0.044715 * (x * x * x))))


def _sigmoid(x):
    return 1.0 / (1.0 + jnp.exp(-x))


def _silu(x):
    return x * _sigmoid(x)


def _softplus(x):
    return jnp.maximum(x, 0.0) + jnp.log1p(jnp.exp(-jnp.abs(x)))


def _log_sigmoid(x):
    return -_softplus(-x)


def _rms(x, w):
    return x * lax.rsqrt(jnp.mean(x * x, axis=-1, keepdims=True) + EPS) * w


def _layered(layer, shape, imap):
    return pl.BlockSpec((None,) + shape, lambda *g: (layer,) + imap(*g))


def _pick(n, cands):
    for c in cands:
        if n % c == 0:
            return c
    return n


def _norm_proj_kernel(x_ref, nw_ref, wa_ref, wb_ref, proj_ref, h_ref, *, na):
    j = pl.program_id(1)

    @pl.when(j == 0)
    def _():
        h_ref[...] = _rms(x_ref[...], nw_ref[...]).astype(BF16)

    @pl.when(j < na)
    def _():
        proj_ref[...] = _dot(h_ref[...], wa_ref[...])

    @pl.when(j >= na)
    def _():
        proj_ref[...] = _dot(h_ref[...], wb_ref[...])


def _norm_proj(x, nw, w_front, w_mid, layer):
    m, d = x.shape
    tm = _pick(m, (1024, 512, 256, 128))
    tn = PROJ_TILE
    na, nb = N_FRONT // tn, N_MID // tn
    return pl.pallas_call(
        functools.partial(_norm_proj_kernel, na=na),
        grid=(m // tm, na + nb),
        in_specs=[pl.BlockSpec((tm, d), lambda i, j: (i, 0)),
                  _layered(layer, (1, d), lambda i, j: (0, 0)),
                  _layered(layer, (d, tn), lambda i, j: (0, jnp.minimum(j, na - 1))),
                  _layered(layer, (d, tn), lambda i, j: (0, jnp.clip(j - na, 0, nb - 1)))],
        out_specs=[pl.BlockSpec((tm, tn), lambda i, j: (i, j)),
                   pl.BlockSpec((tm, d), lambda i, j: (i, 0))],
        out_shape=[jax.ShapeDtypeStruct((m, N_PROJ), F32), jax.ShapeDtypeStruct((m, d), BF16)],
        compiler_params=_cparams(("arbitrary", "arbitrary")),
        name="norm_proj",
    )(x, nw, w_front, w_mid)


def _s5_disc(are, aim, ldt):
    dt = jnp.exp(ldt)
    mag = jnp.exp(dt * are)
    abr = mag * jnp.cos(dt * aim)
    abi = mag * jnp.sin(dt * aim)
    den = are * are + aim * aim
    zr = abr - 1.0
    fre = (zr * are + abi * aim) / den
    fim = (abi * are - zr * aim) / den
    return abr, abi, fre, fim


def _s5_glu(y, u, d_ref, wglu_ref):
    z = _gelu(y + d_ref[...] * u)
    return z * _sigmoid(_dot(z.astype(BF16), wglu_ref[...]))


def _s5_prompt_body(step, u_ref, are_ref, aim_ref, ldt_ref, bre_ref, bim_ref, cre_ref, cim_ref,
                    d_ref, wglu_ref, y_ref, sre_ref, sim_ref, xr_s, xi_s, y_s, car_re, car_im,
                    f_s, tab_s, *, nt):
    @pl.when(step % nt == 0)
    def _():
        car_re[...] = jnp.zeros_like(car_re)
        car_im[...] = jnp.zeros_like(car_im)
        row = lax.broadcasted_iota(jnp.int32, (SUBLANE, S5_LANES), 0)
        abr, abi, fre, fim = _s5_disc(are_ref[...], aim_ref[...], ldt_ref[...])
        f_s[0:1, :] = fre
        f_s[1:2, :] = fim
        pr, pi = abr, abi
        for lvl, s in enumerate((1, 2, 4)):
            tab_s[2 * lvl] = jnp.where(row >= s, pr, 0.0)
            tab_s[2 * lvl + 1] = jnp.where(row >= s, pi, 0.0)
            pr, pi = pr * pr - pi * pi, 2.0 * pr * pi
        cwr = jnp.zeros((SUBLANE, S5_LANES), F32)
        cwi = jnp.zeros((SUBLANE, S5_LANES), F32)
        pr, pi = abr, abi
        for r in range(SUBLANE):
            cwr = jnp.where(row == r, pr, cwr)
            cwi = jnp.where(row == r, pi, cwi)
            pr, pi = pr * abr - pi * abi, pr * abi + pi * abr
        tab_s[6] = cwr
        tab_s[7] = cwi

    tt = u_ref.shape[0]
    u = u_ref[...]
    ub = u.astype(BF16)
    for j in range(S5_NCHUNK):
        sl = slice(S5_CHUNK * j, S5_CHUNK * (j + 1))
        fre, fim = f_s[0:1, sl], f_s[1:2, sl]
        uj = ub[:, LANE * j:LANE * (j + 1)]
        bur = _dot(uj, bre_ref[j])
        bui = _dot(uj, bim_ref[j])
        xr = (fre * bur - fim * bui).reshape(tt // SUBLANE, SUBLANE, S5_CHUNK)
        xi = (fre * bui + fim * bur).reshape(tt // SUBLANE, SUBLANE, S5_CHUNK)
        for lvl, s in enumerate((1, 2, 4)):
            mr = tab_s[2 * lvl, :, sl]
            mi = tab_s[2 * lvl + 1, :, sl]
            sr = pltpu.roll(xr, s, axis=1)
            si = pltpu.roll(xi, s, axis=1)
            xr, xi = xr + mr * sr - mi * si, xi + mr * si + mi * sr
        xr_s[...] = xr.reshape(tt, S5_CHUNK)
        xi_s[...] = xi.reshape(tt, S5_CHUNK)
        cwr = tab_s[6, :, sl]
        cwi = tab_s[7, :, sl]

        def body(g, carry, cwr=cwr, cwi=cwi):
            cr, ci = carry
            r0 = pl.multiple_of(g * SUBLANE, SUBLANE)
            gr = xr_s[pl.ds(r0, SUBLANE), :] + cwr * cr - cwi * ci
            gi = xi_s[pl.ds(r0, SUBLANE), :] + cwr * ci + cwi * cr
            xr_s[pl.ds(r0, SUBLANE), :] = gr
            xi_s[pl.ds(r0, SUBLANE), :] = gi
            return gr[SUBLANE - 1:SUBLANE, :], gi[SUBLANE - 1:SUBLANE, :]

        cr, ci = lax.fori_loop(0, tt // SUBLANE, body, (car_re[:, sl], car_im[:, sl]), unroll=4)
        car_re[:, sl] = cr
        car_im[:, sl] = ci
        y_s[:, LANE * j:LANE * (j + 1)] = (_dot(xr_s[...].astype(BF16), cre_ref[j])
                                           - _dot(xi_s[...].astype(BF16), cim_ref[j]))
    y_ref[...] = _s5_glu(y_s[...], u, d_ref, wglu_ref).astype(BF16)
    sre_ref[0] = car_re[...]
    sim_ref[0] = car_im[...]


def _run_parts(name, parts):
    n = parts[0]["n"]
    assert all(p["n"] == n for p in parts)
    n_in = [len(p["args"]) for p in parts]
    n_out = [len(p["out_shape"]) for p in parts]
    n_scr = [len(p["scratch"]) for p in parts]

    def kern(*refs):
        step = pl.program_id(0)
        ins, outs, scr = refs[:sum(n_in)], refs[sum(n_in):sum(n_in) + sum(n_out)], refs[sum(n_in) + sum(n_out):]
        a = b = c = 0
        for p, na, nb, nc in zip(parts, n_in, n_out, n_scr):
            p["body"](step, *ins[a:a + na], *outs[b:b + nb], *scr[c:c + nc])
            a, b, c = a + na, b + nb, c + nc

    flat = lambda key: [x for p in parts for x in p[key]]
    outs = pl.pallas_call(
        kern, grid=(n,), in_specs=flat("in_specs"), out_specs=flat("out_specs"), out_shape=flat("out_shape"),
        scratch_shapes=flat("scratch"), compiler_params=_cparams(("arbitrary",)), name=name,
    )(*flat("args"))
    res, b = [], 0
    for nb in n_out:
        res.append(list(outs[b:b + nb]))
        b += nb
    return res


def _s5_prompt_part(proj, bsz, seq, sp):
    tt = _pick(seq, (S5_TILE, 128, 64, 32, 16, 8))
    nt = seq // tt
    const2 = lambda i: (0, 0)
    const3 = lambda i: (0, 0, 0)
    row_spec = pl.BlockSpec((1, S5_LANES), const2)
    return dict(
        n=bsz * nt,
        body=functools.partial(_s5_prompt_body, nt=nt),
        in_specs=[pl.BlockSpec((tt, D_A), lambda i: (i, COL_U)),
                  row_spec, row_spec, row_spec,
                  pl.BlockSpec((S5_NCHUNK, LANE, S5_CHUNK), const3),
                  pl.BlockSpec((S5_NCHUNK, LANE, S5_CHUNK), const3),
                  pl.BlockSpec((S5_NCHUNK, S5_CHUNK, LANE), const3),
                  pl.BlockSpec((S5_NCHUNK, S5_CHUNK, LANE), const3),
                  pl.BlockSpec((1, D_A), const2),
                  pl.BlockSpec((D_A, D_A), const2)],
        args=[proj, sp["are"], sp["aim"], sp["ldt"], sp["bre"], sp["bim"], sp["cre"], sp["cim"],
              sp["d"], sp["wglu"]],
        out_specs=[pl.BlockSpec((tt, D_A), lambda i: (i, 0)),
                   pl.BlockSpec((1, 1, S5_LANES), lambda i: (i // nt, 0, 0)),
                   pl.BlockSpec((1, 1, S5_LANES), lambda i: (i // nt, 0, 0))],
        out_shape=[jax.ShapeDtypeStruct((bsz * seq, D_A), BF16),
                   jax.ShapeDtypeStruct((bsz, 1, S5_LANES), F32),
                   jax.ShapeDtypeStruct((bsz, 1, S5_LANES), F32)],
        scratch=[pltpu.VMEM((tt, S5_CHUNK), F32), pltpu.VMEM((tt, S5_CHUNK), F32),
                 pltpu.VMEM((tt, D_A), F32),
                 pltpu.VMEM((1, S5_LANES), F32), pltpu.VMEM((1, S5_LANES), F32),
                 pltpu.VMEM((2, S5_LANES), F32), pltpu.VMEM((8, SUBLANE, S5_LANES), F32)])


def _s5_prompt(proj, bsz, seq, sp):
    return _run_parts("s5_prompt", [_s5_prompt_part(proj, bsz, seq, sp)])[0]


def _mlstm_prompt_kernel(q_ref, k_ref, v0_ref, v1_ref, o0_ref, o1_ref, sm_ref, bias_ref, nw_ref,
                         y_ref, c_ref, n_ref, m_ref, *, chunk):
    @pl.when(pl.program_id(1) == 0)
    def _():
        c_ref[...] = jnp.zeros_like(c_ref)
        n_ref[...] = jnp.zeros_like(n_ref)
        m_ref[...] = jnp.zeros_like(m_ref)

    tt = q_ref.shape[0]
    c = chunk
    nchunk = tt // c
    smb = sm_ref[...] + bias_ref[...]
    lf_all = _log_sigmoid(smb)
    li_t = smb.T
    lf_t = lf_all.T
    rowi = lax.broadcasted_iota(jnp.int32, (c, c), 0)
    coli = lax.broadcasted_iota(jnp.int32, (c, c), 1)
    causal = rowi >= coli
    lane = lax.broadcasted_iota(jnp.int32, (1, LANE), 1)
    m_row = m_ref[0]
    probs = [(ci, h) for ci in range(nchunk) for h in range(H_B)]
    idx = {p: i for i, p in enumerate(probs)}
    rows = lambda ci: slice(c * ci, c * (ci + 1))
    vo_refs = [(v0_ref, o0_ref) if h < 2 else (v1_ref, o1_ref) for h in range(H_B)]
    q = [q_ref[rows(ci), DQK_B * h:DQK_B * (h + 1)] * (DQK_B ** -0.5) for ci, h in probs]
    k = [k_ref[rows(ci), DQK_B * h:DQK_B * (h + 1)] for ci, h in probs]
    qb = [x.astype(BF16) for x in q]
    kb = [x.astype(BF16) for x in k]
    vb = [vo_refs[h][0][rows(ci), DV_B * (h % 2):DV_B * (h % 2 + 1)].astype(BF16) for ci, h in probs]
    qkt = [lax.dot_general(a, b, NT, preferred_element_type=F32) for a, b in zip(qb, kb)]
    li_c = [smb[rows(ci), SM_I + h:SM_I + h + 1] for ci, h in probs]
    li_r = [li_t[SM_I + h:SM_I + h + 1, rows(ci)] for ci, h in probs]
    bc_c = [jnp.sum(jnp.where(causal, lf_t[SM_F + h:SM_F + h + 1, rows(ci)], 0.0), axis=1, keepdims=True)
            for ci, h in probs]
    bc_r = [jnp.sum(jnp.where(rowi <= coli, lf_all[rows(ci), SM_F + h:SM_F + h + 1], 0.0), axis=0, keepdims=True)
            for ci, h in probs]
    dmat = [jnp.where(causal, a - b + r, -jnp.inf) for a, b, r in zip(bc_c, bc_r, li_r)]
    dmax = [jnp.max(x, axis=1, keepdims=True) for x in dmat]
    b_last = [x[c - 1:c, :] for x in bc_c]
    expo = [bl - a + l for bl, a, l in zip(b_last, bc_c, li_c)]
    emax = [jnp.max(x, axis=0, keepdims=True) for x in expo]
    m_prev, m_new = [None] * len(probs), [None] * len(probs)
    for h in range(H_B):
        m = m_row[:, h:h + 1]
        for ci in range(nchunk):
            i = idx[ci, h]
            m_prev[i] = m
            m = jnp.maximum(b_last[i] + m, emax[i])
            m_new[i] = m
    inter = [a + m for a, m in zip(bc_c, m_prev)]
    m_t = [jnp.maximum(a, b) for a, b in zip(inter, dmax)]
    w_inter = [jnp.exp(a - b) for a, b in zip(inter, m_t)]
    s = [x * jnp.exp(d - m) for x, d, m in zip(qkt, dmat, m_t)]
    sv = [_dot(x.astype(BF16), v) for x, v in zip(s, vb)]
    ssum = [jnp.sum(x, axis=1, keepdims=True) for x in s]
    emt = [jnp.exp(-x) for x in m_t]
    decay = [jnp.exp(bl + mp - mn) for bl, mp, mn in zip(b_last, m_prev, m_new)]
    kw = [jnp.exp(e - mn) * kx for e, mn, kx in zip(expo, m_new, k)]
    kv = [lax.dot_general(x.astype(BF16), v, TN, preferred_element_type=F32) for x, v in zip(kw, vb)]
    ksum = [jnp.sum(x, axis=0, keepdims=True) for x in kw]

    cst = [c_ref[0, h] for h in range(H_B)]
    nrow = [n_ref[0, h:h + 1, :] for h in range(H_B)]
    for ci in range(nchunk):
        ids = [idx[ci, h] for h in range(H_B)]
        qc = [_dot(qb[i], cst[h].astype(BF16)) for h, i in enumerate(ids)]
        num = [sv[i] + w_inter[i] * qc[h] for h, i in enumerate(ids)]
        nq = [ssum[i] + w_inter[i] * jnp.sum(q[i] * nrow[h], axis=1, keepdims=True) for h, i in enumerate(ids)]
        hh = [num[h] / jnp.maximum(jnp.abs(nq[h]), emt[i]) for h, i in enumerate(ids)]
        cst = [decay[i] * cst[h] + kv[i] for h, i in enumerate(ids)]
        nrow = [decay[i] * nrow[h] + ksum[i] for h, i in enumerate(ids)]
        for h in range(H_B):
            og = vo_refs[h][1][rows(ci), DV_B * (h % 2):DV_B * (h % 2 + 1)]
            y_ref[rows(ci), DV_B * h:DV_B * (h + 1)] = (_rms(hh[h], nw_ref[h:h + 1, :]) * _sigmoid(og)).astype(BF16)
    m_out = m_row
    for h in range(H_B):
        c_ref[0, h] = cst[h]
        n_ref[0, h:h + 1, :] = nrow[h]
        m_out = jnp.where(lane == h, m_new[idx[nchunk - 1, h]], m_out)
    m_ref[0] = m_out


def _mlstm_prompt(proj, bsz, seq, mp):
    chunk = _pick(seq, (MLSTM_CHUNK,))
    c = _pick(seq, (MLSTM_TILE, MLSTM_CHUNK))
    nt = seq // c
    rows = lambda col: (lambda b, t: (b * nt + t, col))
    const2 = lambda b, t: (0, 0)
    return pl.pallas_call(
        functools.partial(_mlstm_prompt_kernel, chunk=chunk),
        grid=(bsz, nt),
        in_specs=[pl.BlockSpec((c, 512), rows(COL_QB // 4)),
                  pl.BlockSpec((c, 512), rows(COL_KB // 4)),
                  pl.BlockSpec((c, 512), rows(COL_VB // 4)),
                  pl.BlockSpec((c, 512), rows(COL_VB // 4 + 1)),
                  pl.BlockSpec((c, 512), rows(COL_OB // 4)),
                  pl.BlockSpec((c, 512), rows(COL_OB // 4 + 1)),
                  pl.BlockSpec((c, LANE), rows(COL_SMALL)),
                  pl.BlockSpec((1, LANE), const2),
                  pl.BlockSpec((H_B, DV_B), const2)],
        out_specs=[pl.BlockSpec((c, D_B), lambda b, t: (b * nt + t, 0)),
                   pl.BlockSpec((1, H_B, DQK_B, DV_B), lambda b, t: (b, 0, 0, 0)),
                   pl.BlockSpec((1, H_B, DQK_B), lambda b, t: (b, 0, 0)),
                   pl.BlockSpec((1, 1, LANE), lambda b, t: (b, 0, 0))],
        out_shape=[jax.ShapeDtypeStruct((bsz * seq, D_B), BF16),
                   jax.ShapeDtypeStruct((bsz, H_B, DQK_B, DV_B), F32),
                   jax.ShapeDtypeStruct((bsz, H_B, DQK_B), F32),
                   jax.ShapeDtypeStruct((bsz, 1, LANE), F32)],
        compiler_params=_cparams(("arbitrary", "arbitrary")),
        name="mlstm_prompt",
    )(proj, proj, proj, proj, proj, proj, proj, mp["bias"], mp["norm"])


def _split_bf16(a):
    hi = a.astype(BF16)
    return hi, (a - hi.astype(F32)).astype(BF16)


def _dot_x3(a, b):
    ah, al = _split_bf16(a)
    bh, bl = _split_bf16(b)
    return _dot(ah, bh) + _dot(ah, bl) + _dot(al, bh)


def _unit_lower_inverses(lmats):
    c = lmats[0].shape[0]
    eye = (lax.broadcasted_iota(jnp.int32, (c, c), 0) == lax.broadcasted_iota(jnp.int32, (c, c), 1)).astype(F32)
    hi_half = lax.broadcasted_iota(jnp.int32, (c, 2 * c), 1) >= c
    ms = [jnp.concatenate([-l, eye], axis=1) for l in lmats]
    span = 1
    while span < c:
        split = [_split_bf16(m) for m in ms]
        ms = [_dot(mh[:, 0:c], mh) + _dot(mh[:, 0:c], ml) + _dot(ml[:, 0:c], mh) + jnp.where(hi_half, m, 0.0)
              for m, (mh, ml) in zip(ms, split)]
        span *= 2
    return [m[:, c:2 * c] for m in ms]


def _gdn_prompt_kernel(q_ref, k_ref, v_ref, z_ref, sm_ref, bias_ref, alog_ref, cw_ref, nw_ref,
                       y_ref, s_ref, xb_s, *, chunk):
    tt = q_ref.shape[0]
    c = chunk

    @pl.when(pl.program_id(1) == 0)
    def _():
        s_ref[...] = jnp.zeros_like(s_ref)
        xb_s[0:SUBLANE, :] = jnp.zeros((SUBLANE, 3 * D_C), F32)

    xb_s[SUBLANE:SUBLANE + tt, 0:D_C] = q_ref[...]
    xb_s[SUBLANE:SUBLANE + tt, D_C:2 * D_C] = k_ref[...]
    xb_s[SUBLANE:SUBLANE + tt, 2 * D_C:3 * D_C] = v_ref[...]
    conv = cw_ref[GDN_CONV - 1:GDN_CONV, :] * xb_s[SUBLANE:SUBLANE + tt, :]
    for j in range(GDN_CONV - 1):
        off = SUBLANE - (GDN_CONV - 1) + j
        conv = conv + cw_ref[j:j + 1, :] * xb_s[off:off + tt, :]
    xb_s[0:SUBLANE, :] = xb_s[tt:tt + SUBLANE, :]
    qkv = _silu(conv)

    sm = sm_ref[...]
    beta_all = _sigmoid(sm)
    g_all = -jnp.exp(alog_ref[...]) * _softplus(sm + bias_ref[...])
    g_t = g_all.T
    rowi = lax.broadcasted_iota(jnp.int32, (c, c), 0)
    coli = lax.broadcasted_iota(jnp.int32, (c, c), 1)
    incl = rowi >= coli
    strict = rowi > coli

    nchunk = tt // c
    probs = [(ci, h) for ci in range(nchunk) for h in range(H_C)]
    rows = lambda ci: slice(c * ci, c * (ci + 1))
    l2 = lambda x: x * lax.rsqrt(jnp.sum(x * x, axis=-1, keepdims=True) + EPS)
    q = [l2(qkv[rows(ci), DK_C * h:DK_C * (h + 1)]) * (DK_C ** -0.5) for ci, h in probs]
    k = [l2(qkv[rows(ci), D_C + DK_C * h:D_C + DK_C * (h + 1)]) for ci, h in probs]
    v = [qkv[rows(ci), 2 * D_C + DV_C * h:2 * D_C + DV_C * (h + 1)] for ci, h in probs]
    beta_c = [beta_all[rows(ci), SM_BETA + h:SM_BETA + h + 1] for ci, h in probs]
    gc_c = [jnp.sum(jnp.where(incl, g_t[SM_A + h:SM_A + h + 1, rows(ci)], 0.0), axis=1, keepdims=True)
            for ci, h in probs]
    gc_r = [jnp.sum(jnp.where(rowi <= coli, g_all[rows(ci), SM_A + h:SM_A + h + 1], 0.0), axis=0, keepdims=True)
            for ci, h in probs]
    gam = [jnp.exp(jnp.where(incl, a - b, -jnp.inf)) for a, b in zip(gc_c, gc_r)]
    qb = [x.astype(BF16) for x in q]
    kb = [x.astype(BF16) for x in k]
    kk = [lax.dot_general(x, x, NT, preferred_element_type=F32) for x in kb]
    qk = [(lax.dot_general(a, b, NT, preferred_element_type=F32) * g).astype(BF16) for a, b, g in zip(qb, kb, gam)]
    egc = [jnp.exp(x) for x in gc_c]
    tinv = _unit_lower_inverses([jnp.where(strict, b * g * x, 0.0) for b, g, x in zip(beta_c, gam, kk)])
    tr = [_dot_x3(t, jnp.concatenate([b * vv, (b * e) * kx], axis=1))
          for t, b, vv, e, kx in zip(tinv, beta_c, v, egc, k)]
    u0 = [x[:, 0:DV_C] for x in tr]
    wb = [x[:, DV_C:].astype(BF16) for x in tr]
    g_last = [x[c - 1:c, :] for x in gc_c]
    kw = [(jnp.exp(gl - gc) * kx).astype(BF16) for gl, gc, kx in zip(g_last, gc_c, k)]
    e_last = [jnp.exp(x) for x in g_last]

    st = [s_ref[0, h] for h in range(H_C)]
    for ci in range(nchunk):
        ids = [ci * H_C + h for h in range(H_C)]
        stb = [x.astype(BF16) for x in st]
        ub = [(u0[i] - _dot(wb[i], stb[h])).astype(BF16) for h, i in enumerate(ids)]
        qs = [_dot(qb[i], stb[h]) for h, i in enumerate(ids)]
        st = [e_last[i] * st[h] + lax.dot_general(kw[i], ub[h], TN, preferred_element_type=F32)
              for h, i in enumerate(ids)]
        o = [egc[i] * qs[h] + _dot(qk[i], ub[h]) for h, i in enumerate(ids)]
        for h in range(H_C):
            zz = z_ref[rows(ci), DV_C * h:DV_C * (h + 1)]
            y_ref[rows(ci), DV_C * h:DV_C * (h + 1)] = (_rms(o[h], nw_ref[...]) * _silu(zz)).astype(BF16)
    for h in range(H_C):
        s_ref[0, h] = st[h]


def _gdn_prompt(proj, bsz, seq, gp):
    c = _pick(seq, (GDN_CHUNK,))
    tt = _pick(seq, (GDN_TILE,))
    nt = seq // tt
    rows = lambda col: (lambda b, t: (b * nt + t, col))
    const2 = lambda b, t: (0, 0)
    return pl.pallas_call(
        functools.partial(_gdn_prompt_kernel, chunk=c),
        grid=(bsz, nt),
        in_specs=[pl.BlockSpec((tt, D_C), rows(COL_QKV // 4)),
                  pl.BlockSpec((tt, D_C), rows(COL_QKV // 4 + 1)),
                  pl.BlockSpec((tt, D_C), rows(COL_QKV // 4 + 2)),
                  pl.BlockSpec((tt, D_C), rows(COL_Z // 4)),
                  pl.BlockSpec((tt, LANE), rows(COL_SMALL)),
                  pl.BlockSpec((1, LANE), const2),
                  pl.BlockSpec((1, LANE), const2),
                  pl.BlockSpec((GDN_CONV, 3 * D_C), const2),
                  pl.BlockSpec((1, DV_C), const2)],
        out_specs=[pl.BlockSpec((tt, D_C), lambda b, t: (b * nt + t, 0)),
                   pl.BlockSpec((1, H_C, DK_C, DV_C), lambda b, t: (b, 0, 0, 0))],
        out_shape=[jax.ShapeDtypeStruct((bsz * seq, D_C), BF16),
                   jax.ShapeDtypeStruct((bsz, H_C, DK_C, DV_C), F32)],
        scratch_shapes=[pltpu.VMEM((SUBLANE + tt, 3 * D_C), F32)],
        compiler_params=_cparams(("arbitrary", "arbitrary")),
        name="gdn_prompt",
    )(proj, proj, proj, proj, proj, gp["bias"], gp["alog"], gp["conv_w"], gp["norm"])


N_SAMPLE_INPUTS = 23


def _sample_mixers_kernel(*refs, fill_layer):
    (proj_ref, sre_ref, sim_ref, c_ref, n_ref, m_ref, gs_ref, gbuf_ref,
     are_ref, aim_ref, ldt_ref, bre_ref, bim_ref, cre_ref, cim_ref, d_ref, wglu_ref,
     mbias_ref, mnorm_ref, gbias_ref, alog_ref, gcw_ref, gnorm_ref) = refs[:N_SAMPLE_INPUTS]
    (y_ref, sre_o, sim_o, c_o, n_o, m_o, gs_o, gbuf_o,
     qkv_s, qn_s, kn_s, beta_s, g_s, li_s, lf_s) = refs[-15:]
    bb = proj_ref.shape[0]
    if fill_layer is not None:
        for l in range(c_o.shape[0]):
            if l != fill_layer:
                c_o[l] = jnp.zeros(c_o.shape[1:], F32)
                gs_o[l] = jnp.zeros(gs_o.shape[1:], F32)
        c_o = c_o.at[fill_layer]
        gs_o = gs_o.at[fill_layer]

    u = proj_ref[:, COL_U * LANE:COL_U * LANE + D_A]
    ub = u.astype(BF16)
    ys = []
    for j in range(S5_NCHUNK):
        sl = slice(S5_CHUNK * j, S5_CHUNK * (j + 1))
        abr, abi, fre, fim = _s5_disc(are_ref[:, sl], aim_ref[:, sl], ldt_ref[:, sl])
        uj = ub[:, LANE * j:LANE * (j + 1)]
        bur = _dot(uj, bre_ref[j])
        bui = _dot(uj, bim_ref[j])
        s0r = sre_ref[:, sl]
        s0i = sim_ref[:, sl]
        xr = fre * bur - fim * bui + abr * s0r - abi * s0i
        xi = fre * bui + fim * bur + abr * s0i + abi * s0r
        sre_o[:, sl] = xr
        sim_o[:, sl] = xi
        ys.append(_dot(xr.astype(BF16), cre_ref[j]) - _dot(xi.astype(BF16), cim_ref[j]))
    y_a = jnp.concatenate(ys, axis=1)
    y_ref[:, Y_A:Y_A + D_A] = _s5_glu(y_a, u, d_ref, wglu_ref)

    sm = proj_ref[:, COL_SMALL * LANE:(COL_SMALL + 1) * LANE]
    smb = sm + mbias_ref[...]
    li_s[...] = smb
    lf_s[...] = _log_sigmoid(smb)
    beta_s[...] = _sigmoid(sm)
    g_s[...] = -jnp.exp(alog_ref[...]) * _softplus(sm + gbias_ref[...])

    xnew = proj_ref[:, COL_QKV * LANE:COL_QKV * LANE + 3 * D_C]
    conv = gcw_ref[GDN_CONV - 1:GDN_CONV, :] * xnew
    for j in range(GDN_CONV - 1):
        conv = conv + gcw_ref[j:j + 1, :] * gbuf_ref[j]
        if j > 0:
            gbuf_o[j - 1] = gbuf_ref[j]
    gbuf_o[GDN_CONV - 2] = xnew
    qkv = _silu(conv)
    qkv_s[...] = qkv
    for h in range(H_C):
        q = qkv[:, DK_C * h:DK_C * (h + 1)]
        k = qkv[:, D_C + DK_C * h:D_C + DK_C * (h + 1)]
        qn_s[:, DK_C * h:DK_C * (h + 1)] = q * lax.rsqrt(jnp.sum(q * q, axis=-1, keepdims=True) + EPS) * (DK_C ** -0.5)
        kn_s[:, DK_C * h:DK_C * (h + 1)] = k * lax.rsqrt(jnp.sum(k * k, axis=-1, keepdims=True) + EPS)

    assert 4 * H_B * bb == LANE and DQK_B == LANE and DK_C == LANE and H_B == H_C
    tiles = ([proj_ref[:, COL_QB * LANE + DQK_B * h:COL_QB * LANE + DQK_B * (h + 1)] * (DQK_B ** -0.5)
              for h in range(H_B)]
             + [proj_ref[:, COL_KB * LANE + DQK_B * h:COL_KB * LANE + DQK_B * (h + 1)] for h in range(H_B)]
             + [qn_s[:, DK_C * h:DK_C * (h + 1)] for h in range(H_C)]
             + [kn_s[:, DK_C * h:DK_C * (h + 1)] for h in range(H_C)])
    stacked_t = jnp.concatenate(tiles, axis=0).T

    def col_of(kind, b, h):
        j = (kind * H_B + h) * bb + b
        return stacked_t[:, j:j + 1]

    lane = lax.broadcasted_iota(jnp.int32, (1, H_B), 1)

    rsl = lambda b: slice(b, b + 1)
    for g0 in range(0, bb, SAMPLE_GROUP):
        probs = [(b, h) for b in range(g0, g0 + SAMPLE_GROUP) for h in range(H_B)]

        q = [proj_ref[rsl(b), COL_QB * LANE + DQK_B * h:COL_QB * LANE + DQK_B * (h + 1)] * (DQK_B ** -0.5)
             for b, h in probs]
        k = [proj_ref[rsl(b), COL_KB * LANE + DQK_B * h:COL_KB * LANE + DQK_B * (h + 1)] for b, h in probs]
        v = [proj_ref[rsl(b), COL_VB * LANE + DV_B * h:COL_VB * LANE + DV_B * (h + 1)] for b, h in probs]
        li = [li_s[rsl(b), SM_I + h:SM_I + h + 1] for b, h in probs]
        inter = [lf_s[rsl(b), SM_F + h:SM_F + h + 1] + m_ref[rsl(b), h:h + 1] for b, h in probs]
        m_t = [jnp.maximum(a, c) for a, c in zip(inter, li)]
        w_intra = [jnp.exp(a - c) for a, c in zip(li, m_t)]
        w_inter = [jnp.exp(a - c) for a, c in zip(inter, m_t)]
        qcol = [col_of(0, b, h) for b, h in probs]
        kcol = [col_of(1, b, h) for b, h in probs]
        s = [jnp.sum(a * c, axis=1, keepdims=True) * w for a, c, w in zip(q, k, w_intra)]
        cst = [c_ref[b, h] for b, h in probs]
        nrow = [n_ref[b, h:h + 1, :] for b, h in probs]
        qc = [jnp.sum(a * c, axis=0, keepdims=True) for a, c in zip(qcol, cst)]
        for i, (b, h) in enumerate(probs):
            c_o[b, h] = w_inter[i] * cst[i] + (w_intra[i] * kcol[i]) * v[i]
            n_o[b, h:h + 1, :] = w_inter[i] * nrow[i] + w_intra[i] * k[i]
        num = [s[i] * v[i] + w_inter[i] * qc[i] for i in range(len(probs))]
        nq = [s[i] + w_inter[i] * jnp.sum(q[i] * nrow[i], axis=1, keepdims=True) for i in range(len(probs))]
        hh = [num[i] / jnp.maximum(jnp.abs(nq[i]), jnp.exp(-m_t[i])) for i in range(len(probs))]
        for i, (b, h) in enumerate(probs):
            og = proj_ref[rsl(b), COL_OB * LANE + DV_B * h:COL_OB * LANE + DV_B * (h + 1)]
            y_ref[rsl(b), Y_B + DV_B * h:Y_B + DV_B * (h + 1)] = _rms(hh[i], mnorm_ref[h:h + 1, :]) * _sigmoid(og)
        for b in range(g0, g0 + SAMPLE_GROUP):
            m_out = m_ref[rsl(b), :]
            for h in range(H_B):
                m_out = jnp.where(lane == h, m_t[(b - g0) * H_B + h], m_out)
            m_o[rsl(b), :] = m_out

        q = [qn_s[rsl(b), DK_C * h:DK_C * (h + 1)] for b, h in probs]
        k = [kn_s[rsl(b), DK_C * h:DK_C * (h + 1)] for b, h in probs]
        v = [qkv_s[rsl(b), 2 * D_C + DV_C * h:2 * D_C + DV_C * (h + 1)] for b, h in probs]
        beta = [beta_s[rsl(b), SM_BETA + h:SM_BETA + h + 1] for b, h in probs]
        eg = [jnp.exp(g_s[rsl(b), SM_A + h:SM_A + h + 1]) for b, h in probs]
        qcol = [col_of(2, b, h) for b, h in probs]
        kcol = [col_of(3, b, h) for b, h in probs]
        st = [gs_ref[b, h] for b, h in probs]
        ks = [jnp.sum(a * c, axis=0, keepdims=True) for a, c in zip(kcol, st)]
        qs = [jnp.sum(a * c, axis=0, keepdims=True) for a, c in zip(qcol, st)]
        un = [beta[i] * (v[i] - eg[i] * ks[i]) for i in range(len(probs))]
        for i, (b, h) in enumerate(probs):
            gs_o[b, h] = eg[i] * st[i] + kcol[i] * un[i]
        o = [eg[i] * qs[i] + jnp.sum(q[i] * k[i], axis=1, keepdims=True) * un[i] for i in range(len(probs))]
        for i, (b, h) in enumerate(probs):
            zz = proj_ref[rsl(b), COL_Z * LANE + DV_C * h:COL_Z * LANE + DV_C * (h + 1)]
            y_ref[rsl(b), Y_C + DV_C * h:Y_C + DV_C * (h + 1)] = _rms(o[i], gnorm_ref[...]) * _silu(zz)


def _sample_mixers(proj, row0, nrows, st, sp, mp, gp, layer, prev):
    bb = SAMPLE_BLOCK
    blk0 = row0 // bb
    depth = st["c"].shape[0]
    const2 = lambda i: (0, 0)
    const3 = lambda i: (0, 0, 0)
    row_spec = pl.BlockSpec((1, S5_LANES), const2)
    c_spec = _layered(layer, (bb, H_B, DQK_B, DV_B), lambda i: (i, 0, 0, 0))
    gs_spec = _layered(layer, (bb, H_C, DK_C, DV_C), lambda i: (i, 0, 0, 0))
    in_specs = [
        pl.BlockSpec((bb, N_PROJ), lambda i: (blk0 + i, 0)),
        pl.BlockSpec((bb, S5_LANES), lambda i: (i, 0)),
        pl.BlockSpec((bb, S5_LANES), lambda i: (i, 0)),
        c_spec,
        pl.BlockSpec((bb, H_B, DQK_B), lambda i: (i, 0, 0)),
        pl.BlockSpec((bb, H_B), lambda i: (i, 0)),
        gs_spec,
        pl.BlockSpec((GDN_CONV - 1, bb, 3 * D_C), lambda i: (0, i, 0)),
        row_spec, row_spec, row_spec,
        pl.BlockSpec((S5_NCHUNK, LANE, S5_CHUNK), const3),
        pl.BlockSpec((S5_NCHUNK, LANE, S5_CHUNK), const3),
        pl.BlockSpec((S5_NCHUNK, S5_CHUNK, LANE), const3),
        pl.BlockSpec((S5_NCHUNK, S5_CHUNK, LANE), const3),
        pl.BlockSpec((1, D_A), const2),
        pl.BlockSpec((D_A, D_A), const2),
        pl.BlockSpec((1, LANE), const2),
        pl.BlockSpec((H_B, DV_B), const2),
        pl.BlockSpec((1, LANE), const2),
        pl.BlockSpec((1, LANE), const2),
        pl.BlockSpec((GDN_CONV, 3 * D_C), const2),
        pl.BlockSpec((1, DV_C), const2),
    ]
    out_specs = [
        pl.BlockSpec((bb, D_MODEL), lambda i: (i, 0)),
        pl.BlockSpec((bb, S5_LANES), lambda i: (i, 0)),
        pl.BlockSpec((bb, S5_LANES), lambda i: (i, 0)),
        c_spec if prev is not None else pl.BlockSpec((depth, bb, H_B, DQK_B, DV_B), lambda i: (0, i, 0, 0, 0)),
        pl.BlockSpec((bb, H_B, DQK_B), lambda i: (i, 0, 0)),
        pl.BlockSpec((bb, H_B), lambda i: (i, 0)),
        gs_spec if prev is not None else pl.BlockSpec((depth, bb, H_C, DK_C, DV_C), lambda i: (0, i, 0, 0, 0)),
        pl.BlockSpec((GDN_CONV - 1, bb, 3 * D_C), lambda i: (0, i, 0)),
    ]
    out_shape = [
        jax.ShapeDtypeStruct((nrows, D_MODEL), F32),
        jax.ShapeDtypeStruct((nrows, S5_LANES), F32),
        jax.ShapeDtypeStruct((nrows, S5_LANES), F32),
        jax.ShapeDtypeStruct((depth, nrows, H_B, DQK_B, DV_B), F32),
        jax.ShapeDtypeStruct((nrows, H_B, DQK_B), F32),
        jax.ShapeDtypeStruct((nrows, H_B), F32),
        jax.ShapeDtypeStruct((depth, nrows, H_C, DK_C, DV_C), F32),
        jax.ShapeDtypeStruct((GDN_CONV - 1, nrows, 3 * D_C), F32),
    ]
    scratch = [pltpu.VMEM((bb, 3 * D_C), F32), pltpu.VMEM((bb, D_C), F32), pltpu.VMEM((bb, D_C), F32),
               pltpu.VMEM((bb, LANE), F32), pltpu.VMEM((bb, LANE), F32),
               pltpu.VMEM((bb, LANE), F32), pltpu.VMEM((bb, LANE), F32)]
    args = [proj, st["sre"], st["sim"], st["c"], st["n"], st["m"], st["gs"], st["gbuf"],
            sp["are"], sp["aim"], sp["ldt"], sp["bre"], sp["bim"], sp["cre"], sp["cim"], sp["d"], sp["wglu"],
            mp["bias"], mp["norm"], gp["bias"], gp["alog"], gp["conv_w"], gp["norm"]]
    assert len(args) == N_SAMPLE_INPUTS
    aliases = {}
    if prev is not None:
        in_specs += [pl.BlockSpec(memory_space=pl.ANY)] * 2
        args += list(prev)
        aliases = {N_SAMPLE_INPUTS: 3, N_SAMPLE_INPUTS + 1: 6}
    return pl.pallas_call(
        functools.partial(_sample_mixers_kernel, fill_layer=layer if prev is None else None),
        grid=(nrows // bb,),
        in_specs=in_specs, out_specs=out_specs, out_shape=out_shape, scratch_shapes=scratch,
        input_output_aliases=aliases,
        compiler_params=_cparams(("arbitrary",)),
        name="sample_mixers",
    )(*args)


def _merge_kernel(h_ref, ya_ref, yb_ref, yc_ref, wg0_ref, wg1_ref, wg2_ref, wa_ref, wb_ref, wc_ref, o_ref):
    h = h_ref[...]
    bf = lambda ref: ref[...].astype(BF16)
    acc = _sigmoid(_dot(h, bf(wg0_ref))) * _dot(bf(ya_ref), bf(wa_ref))
    acc = acc + _sigmoid(_dot(h, bf(wg1_ref))) * _dot(bf(yb_ref), bf(wb_ref))
    acc = acc + _sigmoid(_dot(h, bf(wg2_ref))) * _dot(bf(yc_ref), bf(wc_ref))
    o_ref[...] = acc.astype(BF16)


def _merge(h, ya, yb, yc, wg, wa, wb, wc, layer):
    m, d = h.shape
    tm = _pick(m, (1024, 512, 256, 128))
    tn = 256
    nb = d // tn
    lhs = lambda w, cb: pl.BlockSpec((tm, w), lambda i, j: (i, cb))
    gate = lambda g: _layered(layer, (d, tn), lambda i, j: (0, g * nb + j))
    rhs = lambda w: _layered(layer, (w, tn), lambda i, j: (0, j))
    return pl.pallas_call(
        _merge_kernel,
        grid=(m // tm, nb),
        in_specs=[lhs(d, 0), lhs(D_A, ya[1]), lhs(D_B, yb[1]), lhs(D_C, yc[1]), gate(0), gate(1), gate(2),
                  rhs(D_A), rhs(D_B), rhs(D_C)],
        out_specs=pl.BlockSpec((tm, tn), lambda i, j: (i, j)),
        out_shape=jax.ShapeDtypeStruct((m, d), BF16),
        compiler_params=_cparams(("arbitrary", "arbitrary")),
        name="merge",
    )(h, ya[0], yb[0], yc[0], wg, wg, wg, wa, wb, wc)


def _out_proj_kernel(a_ref, x_ref, w_ref, nw_ref, o_ref):
    o_ref[...] = x_ref[...] + _rms(_dot(a_ref[...], w_ref[...]), nw_ref[...])


def _out_proj(a, x, w, nw, layer):
    m, d = x.shape
    tm = _pick(m, (512, 256, 128))
    return pl.pallas_call(
        _out_proj_kernel,
        grid=(m // tm,),
        in_specs=[pl.BlockSpec((tm, d), lambda i: (i, 0)),
                  pl.BlockSpec((tm, d), lambda i: (i, 0)),
                  _layered(layer, (d, d), lambda i: (0, 0)),
                  _layered(layer, (1, d), lambda i: (0, 0))],
        out_specs=pl.BlockSpec((tm, d), lambda i: (i, 0)),
        out_shape=jax.ShapeDtypeStruct((m, d), F32),
        compiler_params=_cparams(("arbitrary",)),
        name="out_proj",
    )(a, x, w, nw)


def _ffn_body(step, *refs, sample, tiles_per_seq, ft):
    if sample:
        (x_ref, nw_ref, wg_ref, wu_ref, cw_ref, wd_ref, pnw_ref, b0_ref, b1_ref,
         o_ref, g_ref, wg_o, wu_o, wd_o, h2_s) = refs
    else:
        (x_ref, nw_ref, wg_ref, wu_ref, cw_ref, wd_ref, pnw_ref,
         o_ref, g_ref, h2_s, gb_s, carry_s) = refs
    i = step // ft
    j = step % ft
    tm = x_ref.shape[0]

    @pl.when(j == 0)
    def _():
        h2_s[...] = _rms(x_ref[...], nw_ref[...]).astype(BF16)
        o_ref[...] = jnp.zeros_like(o_ref)

    h2 = h2_s[...]
    wg, wu, wd = wg_ref[...], wu_ref[...], wd_ref[...]
    if sample:
        wg, wu, wd = wg.astype(BF16), wu.astype(BF16), wd.astype(BF16)
        wg_o[0] = wg
        wu_o[0] = wu
        wd_o[0] = wd
    g = _dot(h2, wg)
    up = _dot(h2, wu)
    if sample:
        a = cw_ref[0:1, :] * b0_ref[...] + cw_ref[1:2, :] * b1_ref[...] + cw_ref[2:3, :] * g
        g_ref[...] = g
    else:
        prev = jnp.where(i % tiles_per_seq == 0, 0.0, carry_s[j])
        gb_s[0:SUBLANE, :] = prev
        gb_s[SUBLANE:SUBLANE + tm, :] = g
        a = (cw_ref[0:1, :] * gb_s[SUBLANE - 2:SUBLANE - 2 + tm, :]
             + cw_ref[1:2, :] * gb_s[SUBLANE - 1:SUBLANE - 1 + tm, :]
             + cw_ref[2:3, :] * g)
        tail = g[tm - SUBLANE:tm, :]
        carry_s[j] = tail
        g_ref[0] = tail
    act = (_gelu(a) * up).astype(BF16)
    o_ref[...] += _dot(act, wd)

    @pl.when(j == ft - 1)
    def _():
        o_ref[...] = x_ref[...] + _rms(o_ref[...], pnw_ref[...])


def _ffn_part(x, row0, nrows, seq, nw, wg, wu, cw, wd, pnw, layer, bufs=None, steps=None):
    d = x.shape[1]
    f = wg.shape[-1]
    sample = bufs is not None
    tm = nrows if sample else _pick(seq, (512, 256, 128, 64, 32, 16, 8))
    tn = _pick(f, (512, 256, 128) if sample else (1024, 512, 256, 128))
    if sample and steps is not None and f % steps == 0 and (f // steps) % LANE == 0 and f // steps <= tn:
        tn = f // steps
    mt, ft = nrows // tm, f // tn
    blk0 = row0 // tm
    mi = lambda s: s // ft
    nj = lambda s: s % ft
    in_specs = [pl.BlockSpec((tm, d), lambda s: (blk0 + mi(s), 0)),
                _layered(layer, (1, d), lambda s: (0, 0)),
                _layered(layer, (d, tn), lambda s: (0, nj(s))),
                _layered(layer, (d, tn), lambda s: (0, nj(s))),
                _layered(layer, (FFN_CONV, tn), lambda s: (0, nj(s))),
                _layered(layer, (tn, d), lambda s: (nj(s), 0)),
                _layered(layer, (1, d), lambda s: (0, 0))]
    args = [x, nw, wg, wu, cw, wd, pnw]
    scratch = [pltpu.VMEM((tm, d), BF16)]
    if sample:
        in_specs += [pl.BlockSpec((tm, tn), lambda s: (mi(s), nj(s)))] * 2
        args += list(bufs)
        assert mt == 1, "the sample call must visit every weight tile exactly once"
        g_spec = pl.BlockSpec((tm, tn), lambda s: (mi(s), nj(s)))
        g_shape = jax.ShapeDtypeStruct((nrows, f), F32)
        extra_specs = [pl.BlockSpec((1, d, tn), lambda s: (0, 0, nj(s))),
                       pl.BlockSpec((1, d, tn), lambda s: (0, 0, nj(s))),
                       pl.BlockSpec((1, tn, d), lambda s: (0, nj(s), 0))]
        extra_shapes = [jax.ShapeDtypeStruct((1, d, f), BF16), jax.ShapeDtypeStruct((1, d, f), BF16),
                        jax.ShapeDtypeStruct((1, f, d), BF16)]
    else:
        extra_specs, extra_shapes = [], []
        scratch += [pltpu.VMEM((SUBLANE + tm, tn), F32), pltpu.VMEM((ft, SUBLANE, tn), F32)]
        g_spec = pl.BlockSpec((1, SUBLANE, tn), lambda s: (mi(s), 0, nj(s)))
        g_shape = jax.ShapeDtypeStruct((mt, SUBLANE, f), F32)
    return dict(
        n=mt * ft,
        body=functools.partial(_ffn_body, sample=sample, tiles_per_seq=max(seq // tm, 1), ft=ft),
        in_specs=in_specs, args=args,
        out_specs=[pl.BlockSpec((tm, d), lambda s: (mi(s), 0)), g_spec] + extra_specs,
        out_shape=[jax.ShapeDtypeStruct((nrows, d), F32), g_shape] + extra_shapes,
        scratch=scratch)


def _ffn(*args, **kwargs):
    part = _ffn_part(*args, **kwargs)
    return _run_parts("ffn_sample" if kwargs.get("bufs") is not None else "ffn_prompt", [part])[0]


def _ple_kernel(x_ref, p_ref, wg_ref, wp_ref, o_ref):
    x = x_ref[...]
    gate = _sigmoid(_dot(x.astype(BF16), wg_ref[...]))
    o_ref[...] = x + gate * _dot(p_ref[...].astype(BF16), wp_ref[...])


def _ple(x, p, wg, wp, layer):
    m, d = x.shape
    pd = p.shape[-1]
    tm = _pick(m, (512, 256, 128))
    return pl.pallas_call(
        _ple_kernel,
        grid=(m // tm,),
        in_specs=[pl.BlockSpec((tm, d), lambda i: (i, 0)),
                  _layered(layer, (tm, pd), lambda i: (i, 0)),
                  _layered(layer, (d, d), lambda i: (0, 0)),
                  _layered(layer, (pd, d), lambda i: (0, 0))],
        out_specs=pl.BlockSpec((tm, d), lambda i: (i, 0)),
        out_shape=jax.ShapeDtypeStruct((m, d), F32),
        compiler_params=_cparams(("arbitrary",)),
        name="ple",
    )(x, p, wg, wp)


def _split_w_in(w):
    n_if = 2 * H_B
    w_head = w[..., 0:N_HEAD]
    w_mid = w[..., N_HEAD + n_if:N_HEAD + n_if + N_MID]
    gates = jnp.concatenate([w[..., N_HEAD:N_HEAD + n_if], w[..., N_HEAD + n_if + N_MID:]], axis=-1)
    pad = jnp.zeros(w.shape[:-1] + (N_FRONT - N_HEAD - gates.shape[-1],), w.dtype)
    return jnp.concatenate([w_head, gates, pad], axis=-1).astype(BF16), w_mid.astype(BF16)


def _small_row(entries):
    row = jnp.zeros((LANE,), F32)
    for off, val in entries:
        row = row.at[off:off + val.shape[0]].set(val.astype(F32))
    return row.reshape(1, LANE)


def _s5_params(a_re, a_im, log_dt, b_re, b_im, c_re, c_im, d, w_glu):
    gpc = S5_CHUNK // S5_STATE
    eye = jnp.eye(gpc, dtype=F32)

    def bmat(b):
        b4 = b.reshape(S5_NCHUNK, gpc, S5_GROUP, S5_STATE)
        return jnp.einsum('jgcp,gh->jgchp', b4, eye).reshape(S5_NCHUNK, gpc * S5_GROUP, S5_CHUNK).astype(BF16)

    def cmat(c):
        c4 = c.reshape(S5_NCHUNK, gpc, S5_STATE, S5_GROUP)
        return jnp.einsum('jgpc,gh->jgphc', c4, eye).reshape(S5_NCHUNK, S5_CHUNK, gpc * S5_GROUP).astype(BF16)

    return dict(are=a_re.reshape(1, S5_LANES), aim=a_im.reshape(1, S5_LANES),
                ldt=jnp.broadcast_to(log_dt[:, None], (S5_GROUPS, S5_STATE)).reshape(1, S5_LANES),
                bre=bmat(b_re), bim=bmat(b_im), cre=cmat(c_re), cim=cmat(c_im),
                d=d.reshape(1, D_A), wglu=w_glu.astype(BF16))


def _layer(layer, xp, xs, bsz, seq, dense, lw, state, prev):
    np_rows = bsz * seq
    nsamp = xs.shape[0]
    in_w = (dense['norm_mix_pre'], dense['w_in'], dense['w_in_mid'], layer)
    proj, h_p = _norm_proj(xp, *in_w)
    proj_s, h_s = _norm_proj(xs, *in_w)

    sp = _s5_params(lw['s5_a_re'], lw['s5_a_im'], lw['s5_log_dt'], lw['s5_b_re'], lw['s5_b_im'],
                    lw['s5_c_re'], lw['s5_c_im'], lw['s5_d'], lw['s5_w_glu'])
    mp = dict(bias=_small_row([(SM_I, lw['mlstm_b_i']), (SM_F, lw['mlstm_b_f'])]), norm=lw['mlstm_norm'])
    gp = dict(bias=_small_row([(SM_A, lw['gdn_dt_bias'])]), alog=_small_row([(SM_A, lw['gdn_a_log'])]),
              conv_w=lw['gdn_conv_w'], norm=lw['gdn_norm'].reshape(1, DV_C))

    ssm_re, ssm_im, m_c, m_n, m_m, g_s, g_conv, f_conv = state
    st = dict(sre=ssm_re.reshape(nsamp, S5_LANES), sim=ssm_im.reshape(nsamp, S5_LANES),
              c=m_c, n=m_n, m=m_m, gs=g_s, gbuf=jnp.swapaxes(g_conv, 0, 1))
    y_s, sre_s, sim_s, c_s, n_s, m_s, gs_s, gbuf_s = _sample_mixers(proj_s, 0, nsamp, st, sp, mp, gp, layer, prev)
    merge_w = (dense['w_gate'], dense['w_branch_a'], dense['w_branch_b'], dense['w_branch_c'], layer)
    merged_s = _merge(h_s, (y_s, Y_A // D_A), (y_s, Y_B // D_B), (y_s, Y_C // D_C), *merge_w)
    x1_s = _out_proj(merged_s, xs, dense['w_out'], dense['norm_mix_post'], layer)

    s5_part = _s5_prompt_part(proj, bsz, seq, sp)
    ffn_s_part = _ffn_part(
        x1_s, 0, nsamp, 1, dense['norm_ffn_pre'], dense['ffn_w_gate'], dense['ffn_w_up'], dense['ffn_conv_w'],
        dense['ffn_w_down'], dense['norm_ffn_post'], layer, bufs=(f_conv[:, 0], f_conv[:, 1]), steps=s5_part["n"])
    if s5_part["n"] == ffn_s_part["n"]:
        (ya_p, sre_p, sim_p), (x2_s, g_new, wg_b, wu_b, wd_b) = _run_parts("s5_prompt_ffn_sample",
                                                                             [s5_part, ffn_s_part])
    else:
        ya_p, sre_p, sim_p = _run_parts("s5_prompt", [s5_part])[0]
        x2_s, g_new, wg_b, wu_b, wd_b = _run_parts("ffn_sample", [ffn_s_part])[0]
    yb_p, c_p, n_p, m_p = _mlstm_prompt(proj, bsz, seq, mp)
    yc_p, gs_p = _gdn_prompt(proj, bsz, seq, gp)

    merged_p = _merge(h_p, (ya_p, 0), (yb_p, 0), (yc_p, 0), *merge_w)
    x1_p = _out_proj(merged_p, xp, dense['w_out'], dense['norm_mix_post'], layer)
    cw_l, nw_pre_l, nw_post_l = (dense[k][layer:layer + 1] for k in ('ffn_conv_w', 'norm_ffn_pre', 'norm_ffn_post'))
    x2_p, gtail = _ffn(x1_p, 0, np_rows, seq, nw_pre_l, wg_b, wu_b, cw_l, wd_b, nw_post_l, 0)
    x3_p = _ple(x2_p, dense['p_prompt'], dense['ple_w_gate'], dense['ple_w_proj'], layer)
    x3_s = _ple(x2_s, dense['p_sample'], dense['ple_w_gate'], dense['ple_w_proj'], layer)

    tiles_per_seq = gtail.shape[0] // bsz
    qkv_tail = proj.reshape(bsz, seq, N_PROJ)[:, seq - (GDN_CONV - 1):, COL_QKV * LANE:COL_QKV * LANE + 3 * D_C]
    st_p = (sre_p.reshape(bsz, S5_GROUPS, S5_STATE), sim_p.reshape(bsz, S5_GROUPS, S5_STATE),
            c_p, n_p, m_p[:, 0, :H_B], gs_p, qkv_tail,
            gtail.reshape(bsz, tiles_per_seq, SUBLANE, D_FF)[:, -1, SUBLANE - (FFN_CONV - 1):, :])
    st_s = (sre_s.reshape(nsamp, S5_GROUPS, S5_STATE), sim_s.reshape(nsamp, S5_GROUPS, S5_STATE),
            None, n_s, m_s, None, jnp.swapaxes(gbuf_s, 0, 1),
            jnp.stack([f_conv[:, 1], g_new], axis=1))
    return x3_p, x3_s, st_p, st_s, (c_s, gs_s)


def kernel(x_prompt, x_sample, p_prompt, p_sample, state_ssm_re, state_ssm_im, state_mlstm_c, state_mlstm_n, state_mlstm_m, state_gdn_s, state_gdn_conv, state_ffn_conv, norm_mix_pre, norm_mix_post, norm_ffn_pre, norm_ffn_post, w_in, s5_a_re, s5_a_im, s5_log_dt, s5_b_re, s5_b_im, s5_c_re, s5_c_im, s5_d, s5_w_glu, mlstm_b_i, mlstm_b_f, mlstm_norm, gdn_conv_w, gdn_a_log, gdn_dt_bias, gdn_norm, w_branch_a, w_branch_b, w_branch_c, w_gate, w_out, ffn_w_gate, ffn_w_up, ffn_conv_w, ffn_w_down, ple_w_proj, ple_w_gate):
    bsz, seq, d = x_prompt.shape
    nsamp = x_sample.shape[0]
    depth = w_in.shape[0]
    small = dict(
        s5_a_re=s5_a_re, s5_a_im=s5_a_im, s5_log_dt=s5_log_dt, s5_b_re=s5_b_re, s5_b_im=s5_b_im,
        s5_c_re=s5_c_re, s5_c_im=s5_c_im, s5_d=s5_d, s5_w_glu=s5_w_glu, mlstm_b_i=mlstm_b_i,
        mlstm_b_f=mlstm_b_f, mlstm_norm=mlstm_norm, gdn_conv_w=gdn_conv_w, gdn_a_log=gdn_a_log,
        gdn_dt_bias=gdn_dt_bias, gdn_norm=gdn_norm)
    row = lambda w: w.reshape(depth, 1, -1)
    w_in_front, w_in_mid = _split_w_in(w_in)
    dense = dict(
        norm_mix_pre=row(norm_mix_pre), norm_mix_post=row(norm_mix_post), norm_ffn_pre=row(norm_ffn_pre),
        norm_ffn_post=row(norm_ffn_post), w_in=w_in_front, w_in_mid=w_in_mid,
        w_gate=w_gate, w_branch_a=w_branch_a, w_branch_b=w_branch_b, w_branch_c=w_branch_c,
        w_out=w_out.astype(BF16), ffn_w_gate=ffn_w_gate, ffn_w_up=ffn_w_up, ffn_conv_w=ffn_conv_w,
        ffn_w_down=ffn_w_down,
        ple_w_gate=ple_w_gate.astype(BF16), ple_w_proj=ple_w_proj.astype(BF16),
        p_prompt=p_prompt.reshape(depth, bsz * seq, -1), p_sample=p_sample.reshape(depth, nsamp, -1))
    xp = x_prompt.reshape(bsz * seq, d)
    xs = x_sample.reshape(nsamp, d)
    sp_all, ss_all = [], []
    big = None
    for i in range(depth):
        lw = {k: v[i] for k, v in small.items()}
        state = (state_ssm_re[i], state_ssm_im[i], state_mlstm_c, state_mlstm_n[i], state_mlstm_m[i],
                 state_gdn_s, state_gdn_conv[i], state_ffn_conv[i])
        xp, xs, st_p, st_s, big = _layer(i, xp, xs, bsz, seq, dense, lw, state, big)
        sp_all.append(st_p)
        ss_all.append(st_s)
    stack = lambda sts, j: jnp.stack([s[j] for s in sts], axis=0)
    sample_states = [big[0] if j == 2 else big[1] if j == 5 else stack(ss_all, j) for j in range(8)]
    return ((xp.reshape(bsz, seq, d), xs.reshape(nsamp, 1, d))
            + tuple(stack(sp_all, j) for j in range(8))
            + tuple(sample_states))
```
